```python
import jax, jax.numpy as jnp
from jax import lax
import numpy as np

D_MODEL = 1024
BATCH = 8
SEQ = 2048
DEPTH = 1

HGRN_HEADS = 4
HGRN_DK = 128
HGRN_DV = 128
HGRN_WIDTH = HGRN_HEADS * HGRN_DK
CHUNK = 64
ATTN_HEADS = 8
ATTN_KV_HEADS = 2
ATTN_GROUP = ATTN_HEADS // ATTN_KV_HEADS
ATTN_HD = 64
ATTN_WIDTH = ATTN_HEADS * ATTN_HD
KV_WIDTH = ATTN_KV_HEADS * ATTN_HD
WINDOW = 128
BLOCK = 128
MIX_WIDTH = HGRN_HEADS * HGRN_DV + ATTN_WIDTH
PROJ_WIDTH = 4 * HGRN_WIDTH + ATTN_WIDTH + 2 * KV_WIDTH
SPLITS = [HGRN_WIDTH, 2 * HGRN_WIDTH, 3 * HGRN_WIDTH, 4 * HGRN_WIDTH,
          4 * HGRN_WIDTH + ATTN_WIDTH, 4 * HGRN_WIDTH + ATTN_WIDTH + KV_WIDTH]
D_FF = -(-8 * D_MODEL // (3 * 256)) * 256
EPS = 1e-6
NEG_INF = -1e30

kernel_name = "hymba_hgrn2_swa_sink_block"


def rms_norm(x, gain):
    xf = x.astype(jnp.float32)
    y = xf * lax.rsqrt(jnp.mean(xf * xf, axis=-1, keepdims=True) + EPS)
    return (y * gain.astype(jnp.float32)).astype(x.dtype)


def hgrn2_mixer(q, f_logit, i, g, lb, out_gain):
    B, S, _ = q.shape
    f32 = jnp.float32
    lb = lb.astype(f32)
    q = jax.nn.silu(q.astype(f32)) * (HGRN_DK ** -0.5)
    f = lb + (1.0 - lb) * jax.nn.sigmoid(f_logit.astype(f32))
    k = 1.0 - f
    logf = jnp.log(f)
    v = i.astype(f32)
    nc = S // CHUNK

    def to_chunks(t, d):
        return t.reshape(B, nc, CHUNK, HGRN_HEADS, d).transpose(1, 0, 3, 2, 4)

    xs = (to_chunks(q, HGRN_DK), to_chunks(k, HGRN_DK), to_chunks(v, HGRN_DV), to_chunks(logf, HGRN_DK))
    causal = jnp.tril(jnp.ones((CHUNK, CHUNK), dtype=bool))[:, :, None]

    def step(state, inp):
        qb, kb, vb, gb = inp
        bcum = jnp.cumsum(gb, axis=2)
        diff = bcum[:, :, :, None, :] - bcum[:, :, None, :, :]
        decay = jnp.exp(jnp.where(causal, diff, NEG_INF))
        scores = jnp.einsum('bhtc,bhsc,bhtsc->bhts', qb, kb, decay)
        o_intra = jnp.einsum('bhts,bhsv->bhtv', scores, vb)
        o_inter = jnp.einsum('bhtc,bhcv->bhtv', qb * jnp.exp(bcum), state)
        btot = bcum[:, :, -1:, :]
        k_dec = kb * jnp.exp(btot - bcum)
        new_state = jnp.exp(btot[:, :, 0, :])[..., None] * state + jnp.einsum('bhsc,bhsv->bhcv', k_dec, vb)
        return new_state, o_intra + o_inter

    s0 = jnp.zeros((B, HGRN_HEADS, HGRN_DK, HGRN_DV), f32)
    _, o = lax.scan(step, s0, xs)
    o = o.transpose(1, 0, 3, 2, 4).reshape(B, S, HGRN_HEADS, HGRN_DV)
    o = rms_norm(o, out_gain).reshape(B, S, HGRN_HEADS * HGRN_DV)
    return o * jax.nn.silu(g.astype(f32))


def swa_sink_mixer(q, k, v, q_gain, k_gain, sinks):
    B, S, _ = q.shape
    f32 = jnp.float32
    q = rms_norm(q.astype(f32).reshape(B, S, ATTN_KV_HEADS, ATTN_GROUP, ATTN_HD), q_gain) * (ATTN_HD ** -0.5)
    k = rms_norm(k.astype(f32).reshape(B, S, ATTN_KV_HEADS, ATTN_HD), k_gain)
    v = v.astype(f32).reshape(B, S, ATTN_KV_HEADS, ATTN_HD)
    nb = S // BLOCK

    def band(t):
        prev = jnp.pad(t, ((0, 0), (BLOCK, 0), (0, 0), (0, 0)))[:, :S]
        cur_b = t.reshape(B, nb, BLOCK, ATTN_KV_HEADS, ATTN_HD)
        prev_b = prev.reshape(B, nb, BLOCK, ATTN_KV_HEADS, ATTN_HD)
        return jnp.concatenate([prev_b, cur_b], axis=2)

    kb, vb = band(k), band(v)
    qb = q.reshape(B, nb, BLOCK, ATTN_KV_HEADS, ATTN_GROUP, ATTN_HD)
    scores = jnp.einsum('bnqhgd,bnkhd->bnhgqk', qb, kb)
    n_idx = jnp.arange(nb)[:, None, None]
    qpos = n_idx * BLOCK + jnp.arange(BLOCK)[None, :, None]
    kpos = n_idx * BLOCK - BLOCK + jnp.arange(2 * BLOCK)[None, None, :]
    rel = qpos - kpos
    mask = (rel >= 0) & (rel < WINDOW) & (kpos >= 0)
    scores = jnp.where(mask[None, :, None, None], scores, NEG_INF)
    s = sinks.astype(f32).reshape(ATTN_KV_HEADS, ATTN_GROUP)[None, None, :, :, None, None]
    m = jnp.maximum(jnp.max(scores, axis=-1, keepdims=True), s)
    p = jnp.exp(scores - m)
    denom = jnp.sum(p, axis=-1, keepdims=True) + jnp.exp(s - m)
    out = jnp.einsum('bnhgqk,bnkhd->bnqhgd', p / denom, vb)
    return out.reshape(B, S, ATTN_WIDTH)


def setup_inputs(seed: int = 0) -> dict:
    key = jax.random.key(seed)
    ks = jax.random.split(key, 14)
    f32 = jnp.float32
    nrm = lambda k, shape, scale: jax.random.normal(k, shape, f32) * scale
    return {
        "x": jax.random.normal(ks[0], (BATCH, SEQ, D_MODEL), f32),
        "norm1_gain": 1.0 + nrm(ks[1], (DEPTH, D_MODEL), 0.02),
        "w_in": nrm(ks[2], (DEPTH, D_MODEL, PROJ_WIDTH), D_MODEL ** -0.5),
        "hgrn_lb_logits": nrm(ks[3], (DEPTH + 1, HGRN_WIDTH), 0.5),
        "hgrn_out_gain": 1.0 + nrm(ks[4], (DEPTH, HGRN_DV), 0.02),
        "q_norm_gain": 1.0 + nrm(ks[5], (DEPTH, ATTN_HD), 0.02),
        "k_norm_gain": 1.0 + nrm(ks[6], (DEPTH, ATTN_HD), 0.02),
        "attn_sinks": nrm(ks[7], (DEPTH, ATTN_HEADS), 0.5),
        "w_out": nrm(ks[8], (DEPTH, MIX_WIDTH, D_MODEL), MIX_WIDTH ** -0.5),
        "norm2_gain": 1.0 + nrm(ks[9], (DEPTH, D_MODEL), 0.02),
        "w_ffn_gate": nrm(ks[10], (DEPTH, D_MODEL, D_FF), D_MODEL ** -0.5),
        "w_ffn_up": nrm(ks[11], (DEPTH, D_MODEL, D_FF), D_MODEL ** -0.5),
        "w_ffn_down": nrm(ks[12], (DEPTH, D_FF, D_MODEL), D_FF ** -0.5),
    }


def reference(x, norm1_gain, w_in, hgrn_lb_logits, hgrn_out_gain, q_norm_gain, k_norm_gain,
              attn_sinks, w_out, norm2_gain, w_ffn_gate, w_ffn_up, w_ffn_down):
    h = x
    lb_all = jnp.cumsum(jax.nn.softmax(hgrn_lb_logits.astype(jnp.float32), axis=0), axis=0)
    for l in range(DEPTH):
        u = rms_norm(h, norm1_gain[l])
        proj = u @ w_in[l]
        hq, hf, hi, hg, aq, ak, av = jnp.split(proj, SPLITS, axis=-1)
        o_rec = hgrn2_mixer(hq, hf, hi, hg, lb_all[l], hgrn_out_gain[l])
        o_att = swa_sink_mixer(aq, ak, av, q_norm_gain[l], k_norm_gain[l], attn_sinks[l])
        mixed = jnp.concatenate([o_rec, o_att], axis=-1).astype(h.dtype)
        h = h + mixed @ w_out[l]
        u = rms_norm(h, norm2_gain[l])
        h = h + (jax.nn.silu(u @ w_ffn_gate[l]) * (u @ w_ffn_up[l])) @ w_ffn_down[l]
    return h
```

```python
import functools

import jax
import jax.numpy as jnp
from jax import lax
from jax.experimental import pallas as pl
from jax.experimental.pallas import tpu as pltpu

D_MODEL = 1024
HGRN_HEADS = 4
HGRN_DK = 128
HGRN_DV = 128
HGRN_WIDTH = HGRN_HEADS * HGRN_DK
ATTN_HEADS = 8
ATTN_KV_HEADS = 2
ATTN_GROUP = ATTN_HEADS // ATTN_KV_HEADS
ATTN_HD = 64
ATTN_WIDTH = ATTN_HEADS * ATTN_HD
KV_WIDTH = ATTN_KV_HEADS * ATTN_HD
WINDOW = 128
PROJ_WIDTH = 4 * HGRN_WIDTH + ATTN_WIDTH + 2 * KV_WIDTH
D_FF = 2816
EPS = 1e-6
NEG_INF = -1e30

F32 = jnp.float32
BF16 = jnp.bfloat16

VMEM_LIMIT_BYTES = 56 * 1024 * 1024

HGRN_CHUNK = 64
HGRN_LEVELS = 6
HGRN_BLOCK = 256
ATTN_BLOCK = 128
PROJ_ROWS = 512
FFN_ROWS = 512

_NT = (((1,), (1,)), ((), ()))


def _sigmoid(x):
    return 1.0 / (1.0 + jnp.exp(-x))


def _inproj_kernel(x_ref, gain_ref, w_ref, o_ref):
    x = x_ref[...]
    ms = jnp.mean(x * x, axis=-1, keepdims=True)
    u = (x * lax.rsqrt(ms + EPS) * gain_ref[...]).astype(BF16)
    o_ref[...] = jnp.dot(u, w_ref[...], preferred_element_type=F32)


def _inproj(x2, gain, w_bf16):
    t = x2.shape[0]
    return pl.pallas_call(
        _inproj_kernel,
        grid=(t // PROJ_ROWS,),
        in_specs=[
            pl.BlockSpec((PROJ_ROWS, D_MODEL), lambda i: (i, 0)),
            pl.BlockSpec((1, D_MODEL), lambda i: (0, 0)),
            pl.BlockSpec((D_MODEL, PROJ_WIDTH), lambda i: (0, 0)),
        ],
        out_specs=pl.BlockSpec((PROJ_ROWS, PROJ_WIDTH), lambda i: (i, 0)),
        out_shape=jax.ShapeDtypeStruct((t, PROJ_WIDTH), F32),
        compiler_params=pltpu.CompilerParams(
            dimension_semantics=("arbitrary",), vmem_limit_bytes=VMEM_LIMIT_BYTES),
        name="inproj",
    )(x2, gain, w_bf16)


def _hgrn_kernel(q_ref, f_ref, i_ref, g_ref, lbl_ref, og_ref, o_ref, st_ref):
    c_rows = HGRN_CHUNK
    w = HGRN_WIDTH

    @pl.when(pl.program_id(1) == 0)
    def _():
        st_ref[...] = jnp.zeros_like(st_ref)

    l0 = lbl_ref[0:1, :]
    l1 = lbl_ref[1:2, :]
    lmax = jnp.maximum(l0, l1)
    e0 = jnp.exp(l0 - lmax)
    e1 = jnp.exp(l1 - lmax)
    lb = e0 / (e0 + e1)

    og = jnp.concatenate([og_ref[...]] * HGRN_HEADS, axis=1)

    ri = lax.broadcasted_iota(jnp.int32, (c_rows, c_rows), 0)
    ci = lax.broadcasted_iota(jnp.int32, (c_rows, c_rows), 1)
    xor = ri ^ ci
    lvl = jnp.zeros((c_rows, c_rows), jnp.int32)
    for k in range(HGRN_LEVELS):
        lvl = lvl + (xor >= (1 << k)).astype(jnp.int32)
    level = jnp.where(ri >= ci, lvl, -1)

    row = lax.broadcasted_iota(jnp.int32, (c_rows, w), 0)

    def chunk(c, carry):
        r0 = pl.multiple_of(c * c_rows, c_rows)
        rows = pl.ds(r0, c_rows)
        xq = q_ref[0, rows, :]
        xf = f_ref[0, rows, :]
        v = i_ref[0, rows, :]
        xg = g_ref[0, rows, :]

        f = lb + (1.0 - lb) * _sigmoid(xf)
        lf = jnp.log(f)
        kk = 1.0 - f
        qq = xq * _sigmoid(xq) * (HGRN_DK ** -0.5)
        vb = v.astype(BF16)

        qb = qq.astype(BF16)
        kb = kk.astype(BF16)
        scores = []
        for h in range(HGRN_HEADS):
            sl = slice(h * HGRN_DK, (h + 1) * HGRN_DK)
            s0 = lax.dot_general(qb[:, sl], kb[:, sl], _NT, preferred_element_type=F32)
            scores.append(jnp.where(level == 0, s0, 0.0))

        inc = lf
        for ml in range(HGRN_LEVELS):
            m = 1 << ml
            second = (row & m) != 0
            if m == 1:
                e = jnp.where(second, inc, 0.0)
                inc_new = jnp.where(second, inc + pltpu.roll(inc, 1, 0), inc)
            elif m == 2:
                r4 = row & 3
                up1 = pltpu.roll(lf, c_rows - 1, 0)
                e = jnp.where(second, inc, jnp.where(r4 == 0, up1, 0.0))
                tb = jnp.where(r4 == 2, pltpu.roll(inc, 1, 0), pltpu.roll(inc, 2, 0))
                inc_new = jnp.where(second, inc + tb, inc)
            else:
                blk = 2 * m
                x3 = inc.reshape(c_rows // blk, blk, w)
                tb = jnp.broadcast_to(x3[:, m - 1:m, :], x3.shape).reshape(c_rows, w)
                e = jnp.where(second, inc, tb - inc)
                inc_new = jnp.where(second, inc + tb, inc)
            xl = (jnp.where(second, qq, kk) * jnp.exp(e)).astype(BF16)
            for h in range(HGRN_HEADS):
                sl = slice(h * HGRN_DK, (h + 1) * HGRN_DK)
                s = lax.dot_general(xl[:, sl], xl[:, sl], _NT, preferred_element_type=F32)
                scores[h] = jnp.where(level == ml + 1, s, scores[h])
            inc = inc_new

        btot = inc[c_rows - 1:c_rows, :]
        qs = (qq * jnp.exp(inc)).astype(BF16)
        kd = (kk * jnp.exp(btot - inc)).astype(BF16)
        dec = jnp.exp(btot)

        outs = []
        for h in range(HGRN_HEADS):
            sl = slice(h * HGRN_DK, (h + 1) * HGRN_DK)
            st = st_ref[h]
            o_h = jnp.dot(scores[h].astype(BF16), vb[:, sl], preferred_element_type=F32)
            o_h = o_h + lax.dot_general(qs[:, sl], st.astype(BF16), _NT,
                                        preferred_element_type=F32)
            upd = lax.dot_general(vb[:, sl], kd[:, sl], (((0,), (0,)), ((), ())),
                                  preferred_element_type=F32)
            st_ref[h] = st * dec[:, sl] + upd
            ms = jnp.mean(o_h * o_h, axis=-1, keepdims=True)
            outs.append(o_h * lax.rsqrt(ms + EPS))
        o = jnp.concatenate(outs, axis=1) * og
        o_ref[0, rows, :] = (o * (xg * _sigmoid(xg))).astype(o_ref.dtype)
        return carry

    lax.fori_loop(0, HGRN_BLOCK // c_rows, chunk, 0)


def _hgrn(proj3, lb_logits, out_gain):
    b, s, _ = proj3.shape

    def col(k):
        return pl.BlockSpec((1, HGRN_BLOCK, HGRN_WIDTH), lambda bi, j, k=k: (bi, j, k))

    return pl.pallas_call(
        _hgrn_kernel,
        grid=(b, s // HGRN_BLOCK),
        in_specs=[
            col(0), col(1), col(2), col(3),
            pl.BlockSpec((2, HGRN_WIDTH), lambda bi, j: (0, 0)),
            pl.BlockSpec((1, HGRN_DV), lambda bi, j: (0, 0)),
        ],
        out_specs=pl.BlockSpec((1, HGRN_BLOCK, HGRN_WIDTH), lambda bi, j: (bi, j, 0)),
        out_shape=jax.ShapeDtypeStruct((b, s, HGRN_WIDTH), BF16),
        scratch_shapes=[pltpu.VMEM((HGRN_HEADS, HGRN_DV, HGRN_DK), F32)],
        compiler_params=pltpu.CompilerParams(
            dimension_semantics=("arbitrary", "arbitrary"),
            vmem_limit_bytes=VMEM_LIMIT_BYTES),
        name="hgrn2",
    )(proj3, proj3, proj3, proj3, lb_logits, out_gain)


def _head_rms(x, width):
    n = x.shape[-1]
    gi = lax.broadcasted_iota(jnp.int32, (n, n), 0) // width
    gj = lax.broadcasted_iota(jnp.int32, (n, n), 1) // width
    ones_bd = jnp.where(gi == gj, 1.0, 0.0).astype(BF16)
    sq = x * x
    hi = sq.astype(BF16)
    lo = (sq - hi.astype(F32)).astype(BF16)
    ss = (jnp.dot(hi, ones_bd, preferred_element_type=F32)
          + jnp.dot(lo, ones_bd, preferred_element_type=F32))
    return x * lax.rsqrt(ss * (1.0 / width) + EPS)


def _swa_kernel(q_ref, kc_ref, vc_ref, kp_ref, vp_ref, qg_ref, kg_ref, sink_ref, o_ref):
    n = pl.program_id(1)
    blk = ATTN_BLOCK
    qgain = jnp.concatenate([qg_ref[...]] * ATTN_HEADS, axis=1)
    kgain = jnp.concatenate([kg_ref[...]] * ATTN_KV_HEADS, axis=1)

    q = _head_rms(q_ref[0], ATTN_HD) * qgain * (ATTN_HD ** -0.5)
    kc = _head_rms(kc_ref[0], ATTN_HD) * kgain
    kp = _head_rms(kp_ref[0], ATTN_HD) * kgain
    k_all = jnp.concatenate([kp, kc], axis=0).astype(BF16)
    v_all = jnp.concatenate([vp_ref[0], vc_ref[0]], axis=0).astype(BF16)
    qb = q.astype(BF16)

    qi = lax.broadcasted_iota(jnp.int32, (blk, 2 * blk), 0)
    kj = lax.broadcasted_iota(jnp.int32, (blk, 2 * blk), 1)
    rel = qi + blk - kj
    mask = (rel >= 0) & (rel < WINDOW) & ((kj >= blk) | (n > 0))

    outs = []
    for h in range(ATTN_HEADS):
        g = h // ATTN_GROUP
        qh = qb[:, h * ATTN_HD:(h + 1) * ATTN_HD]
        kh = k_all[:, g * ATTN_HD:(g + 1) * ATTN_HD]
        vh = v_all[:, g * ATTN_HD:(g + 1) * ATTN_HD]
        s = lax.dot_general(qh, kh, _NT, preferred_element_type=F32)
        s = jnp.where(mask, s, NEG_INF)
        sink = sink_ref[0:1, h:h + 1]
        mx = jnp.maximum(jnp.max(s, axis=-1, keepdims=True), sink)
        p = jnp.exp(s - mx)
        denom = jnp.sum(p, axis=-1, keepdims=True) + jnp.exp(sink - mx)
        o = jnp.dot(p.astype(BF16), vh, preferred_element_type=F32)
        outs.append(o / denom)
    o_ref[0] = jnp.concatenate(outs, axis=1).astype(o_ref.dtype)


def _swa(proj3, q_gain, k_gain, sinks):
    b, s, _ = proj3.shape
    q_col = 4 * HGRN_WIDTH // ATTN_WIDTH
    k_col = (4 * HGRN_WIDTH + ATTN_WIDTH) // KV_WIDTH
    v_col = k_col + 1

    def cur(width, colblk):
        return pl.BlockSpec((1, ATTN_BLOCK, width), lambda bi, n, c=colblk: (bi, n, c))

    def prev(width, colblk):
        return pl.BlockSpec((1, ATTN_BLOCK, width),
                            lambda bi, n, c=colblk: (bi, jnp.maximum(n - 1, 0), c))

    return pl.pallas_call(
        _swa_kernel,
        grid=(b, s // ATTN_BLOCK),
        in_specs=[
            cur(ATTN_WIDTH, q_col),
            cur(KV_WIDTH, k_col), cur(KV_WIDTH, v_col),
            prev(KV_WIDTH, k_col), prev(KV_WIDTH, v_col),
            pl.BlockSpec((1, ATTN_HD), lambda bi, n: (0, 0)),
            pl.BlockSpec((1, ATTN_HD), lambda bi, n: (0, 0)),
            pl.BlockSpec((1, ATTN_HEADS), lambda bi, n: (0, 0)),
        ],
        out_specs=pl.BlockSpec((1, ATTN_BLOCK, ATTN_WIDTH), lambda bi, n: (bi, n, 0)),
        out_shape=jax.ShapeDtypeStruct((b, s, ATTN_WIDTH), BF16),
        compiler_params=pltpu.CompilerParams(
            dimension_semantics=("arbitrary", "arbitrary"),
            vmem_limit_bytes=VMEM_LIMIT_BYTES),
        name="swa",
    )(proj3, proj3, proj3, proj3, proj3, q_gain, k_gain, sinks)


def _ffn_kernel(x_ref, rec_ref, att_ref, wo_ref, gain_ref, wg_ref, wu_ref, wd_ref, o_ref):
    mixed = jnp.concatenate([rec_ref[...], att_ref[...]], axis=1)
    h = x_ref[...] + jnp.dot(mixed, wo_ref[...], preferred_element_type=F32)
    ms = jnp.mean(h * h, axis=-1, keepdims=True)
    u = (h * lax.rsqrt(ms + EPS) * gain_ref[...]).astype(BF16)
    gate = jnp.dot(u, wg_ref[...], preferred_element_type=F32)
    up = jnp.dot(u, wu_ref[...], preferred_element_type=F32)
    act = (gate * _sigmoid(gate) * up).astype(BF16)
    o_ref[...] = h + jnp.dot(act, wd_ref[...], preferred_element_type=F32)


def _ffn(x2, rec2, att2, wo, gain, wg, wu, wd):
    t = x2.shape[0]

    def resident(shape):
        return pl.BlockSpec(shape, lambda i: (0, 0), pipeline_mode=pl.Buffered(1))

    return pl.pallas_call(
        _ffn_kernel,
        grid=(t // FFN_ROWS,),
        in_specs=[
            pl.BlockSpec((FFN_ROWS, D_MODEL), lambda i: (i, 0)),
            pl.BlockSpec((FFN_ROWS, HGRN_WIDTH), lambda i: (i, 0)),
            pl.BlockSpec((FFN_ROWS, ATTN_WIDTH), lambda i: (i, 0)),
            resident((D_MODEL, D_MODEL)),
            resident((1, D_MODEL)),
            resident((D_MODEL, D_FF)),
            resident((D_MODEL, D_FF)),
            resident((D_FF, D_MODEL)),
        ],
        out_specs=pl.BlockSpec((FFN_ROWS, D_MODEL), lambda i: (i, 0)),
        out_shape=jax.ShapeDtypeStruct((t, D_MODEL), F32),
        compiler_params=pltpu.CompilerParams(
            dimension_semantics=("arbitrary",), vmem_limit_bytes=VMEM_LIMIT_BYTES),
        name="outproj_ffn",
    )(x2, rec2, att2, wo, gain, wg, wu, wd)


def kernel(x, norm1_gain, w_in, hgrn_lb_logits, hgrn_out_gain, q_norm_gain, k_norm_gain,
           attn_sinks, w_out, norm2_gain, w_ffn_gate, w_ffn_up, w_ffn_down):
    b, s, d = x.shape
    assert (d, w_in.shape[0]) == (D_MODEL, 1), "single-layer kernel"
    assert s % HGRN_BLOCK == 0 and s % ATTN_BLOCK == 0 and (b * s) % PROJ_ROWS == 0
    t = b * s
    x2 = x.reshape(t, d)

    proj = _inproj(x2, norm1_gain[0][None, :], w_in[0].astype(BF16))
    proj3 = proj.reshape(b, s, PROJ_WIDTH)
    rec = _hgrn(proj3, hgrn_lb_logits, hgrn_out_gain[0][None, :])
    att = _swa(proj3, q_norm_gain[0][None, :], k_norm_gain[0][None, :],
               attn_sinks[0][None, :])
    out = _ffn(x2, rec.reshape(t, HGRN_WIDTH), att.reshape(t, ATTN_WIDTH),
               w_out[0].astype(BF16), norm2_gain[0][None, :],
               w_ffn_gate[0].astype(BF16), w_ffn_up[0].astype(BF16),
               w_ffn_down[0].astype(BF16))
    return out.reshape(b, s, d)
```

```python
import functools

import jax
import jax.numpy as jnp
from jax import lax
from jax.experimental import pallas as pl
from jax.experimental.pallas import tpu as pltpu

D_MODEL = 1024
HGRN_HEADS = 4
HGRN_DK = 128
HGRN_DV = 128
HGRN_WIDTH = HGRN_HEADS * HGRN_DK
ATTN_HEADS = 8
ATTN_KV_HEADS = 2
ATTN_GROUP = ATTN_HEADS // ATTN_KV_HEADS
ATTN_HD = 64
ATTN_WIDTH = ATTN_HEADS * ATTN_HD
KV_WIDTH = ATTN_KV_HEADS * ATTN_HD
WINDOW = 128
PROJ_WIDTH = 4 * HGRN_WIDTH + ATTN_WIDTH + 2 * KV_WIDTH
D_FF = 2816
EPS = 1e-6
NEG_INF = -1e30

F32 = jnp.float32
BF16 = jnp.bfloat16

VMEM_LIMIT_BYTES = 56 * 1024 * 1024

HGRN_CHUNK = 64
HGRN_LEVELS = 6
HGRN_BLOCK = 256
ATTN_BLOCK = 128
ATTN_STEP = 512
PROJ_ROWS = 512
FFN_ROWS = 512

_NT = (((1,), (1,)), ((), ()))


def _sigmoid(x):
    return 1.0 / (1.0 + jnp.exp(-x))


def _inproj_kernel(x_ref, gain_ref, w_ref, o_ref):
    x = x_ref[...]
    ms = jnp.mean(x * x, axis=-1, keepdims=True)
    u = (x * lax.rsqrt(ms + EPS) * gain_ref[...]).astype(BF16)
    o_ref[...] = jnp.dot(u, w_ref[...], preferred_element_type=F32)


def _inproj(x2, gain, w_bf16):
    t = x2.shape[0]
    return pl.pallas_call(
        _inproj_kernel,
        grid=(t // PROJ_ROWS,),
        in_specs=[
            pl.BlockSpec((PROJ_ROWS, D_MODEL), lambda i: (i, 0)),
            pl.BlockSpec((1, D_MODEL), lambda i: (0, 0)),
            pl.BlockSpec((D_MODEL, PROJ_WIDTH), lambda i: (0, 0)),
        ],
        out_specs=pl.BlockSpec((PROJ_ROWS, PROJ_WIDTH), lambda i: (i, 0)),
        out_shape=jax.ShapeDtypeStruct((t, PROJ_WIDTH), F32),
        compiler_params=pltpu.CompilerParams(
            dimension_semantics=("arbitrary",), vmem_limit_bytes=VMEM_LIMIT_BYTES),
        name="inproj",
    )(x2, gain, w_bf16)


def _hgrn_kernel(q_ref, f_ref, i_ref, g_ref, lbl_ref, og_ref, o_ref, st_ref):
    c_rows = HGRN_CHUNK
    w = HGRN_WIDTH

    @pl.when(pl.program_id(1) == 0)
    def _():
        st_ref[...] = jnp.zeros_like(st_ref)

    l0 = lbl_ref[0:1, :]
    l1 = lbl_ref[1:2, :]
    lmax = jnp.maximum(l0, l1)
    e0 = jnp.exp(l0 - lmax)
    e1 = jnp.exp(l1 - lmax)
    lb = e0 / (e0 + e1)

    og = jnp.concatenate([og_ref[...]] * HGRN_HEADS, axis=1)

    ri = lax.broadcasted_iota(jnp.int32, (c_rows, c_rows), 0)
    ci = lax.broadcasted_iota(jnp.int32, (c_rows, c_rows), 1)
    xor = ri ^ ci
    lvl = jnp.zeros((c_rows, c_rows), jnp.int32)
    for k in range(HGRN_LEVELS):
        lvl = lvl + (xor >= (1 << k)).astype(jnp.int32)
    level = jnp.where(ri >= ci, lvl, -1)

    row = lax.broadcasted_iota(jnp.int32, (c_rows, w), 0)

    def chunk(c, carry):
        r0 = pl.multiple_of(c * c_rows, c_rows)
        rows = pl.ds(r0, c_rows)
        xq = q_ref[0, rows, :]
        xf = f_ref[0, rows, :]
        v = i_ref[0, rows, :]
        xg = g_ref[0, rows, :]

        f = lb + (1.0 - lb) * _sigmoid(xf)
        lf = jnp.log(f)
        kk = 1.0 - f
        qq = xq * _sigmoid(xq) * (HGRN_DK ** -0.5)
        vb = v.astype(BF16)

        qb = qq.astype(BF16)
        kb = kk.astype(BF16)
        scores = []
        for h in range(HGRN_HEADS):
            sl = slice(h * HGRN_DK, (h + 1) * HGRN_DK)
            s0 = lax.dot_general(qb[:, sl], kb[:, sl], _NT, preferred_element_type=F32)
            scores.append(jnp.where(level == 0, s0, 0.0))

        inc = lf
        for ml in range(HGRN_LEVELS):
            m = 1 << ml
            second = (row & m) != 0
            if m == 1:
                e = jnp.where(second, inc, 0.0)
                inc_new = jnp.where(second, inc + pltpu.roll(inc, 1, 0), inc)
            elif m == 2:
                r4 = row & 3
                up1 = pltpu.roll(lf, c_rows - 1, 0)
                e = jnp.where(second, inc, jnp.where(r4 == 0, up1, 0.0))
                tb = jnp.where(r4 == 2, pltpu.roll(inc, 1, 0), pltpu.roll(inc, 2, 0))
                inc_new = jnp.where(second, inc + tb, inc)
            else:
                blk = 2 * m
                x3 = inc.reshape(c_rows // blk, blk, w)
                tb = jnp.broadcast_to(x3[:, m - 1:m, :], x3.shape).reshape(c_rows, w)
                e = jnp.where(second, inc, tb - inc)
                inc_new = jnp.where(second, inc + tb, inc)
            xl = (jnp.where(second, qq, kk) * jnp.exp(e)).astype(BF16)
            for h in range(HGRN_HEADS):
                sl = slice(h * HGRN_DK, (h + 1) * HGRN_DK)
                s = lax.dot_general(xl[:, sl], xl[:, sl], _NT, preferred_element_type=F32)
                scores[h] = jnp.where(level == ml + 1, s, scores[h])
            inc = inc_new

        btot = inc[c_rows - 1:c_rows, :]
        qs = (qq * jnp.exp(inc)).astype(BF16)
        kd = (kk * jnp.exp(btot - inc)).astype(BF16)
        dec = jnp.exp(btot)

        outs = []
        for h in range(HGRN_HEADS):
            sl = slice(h * HGRN_DK, (h + 1) * HGRN_DK)
            st = st_ref[h]
            o_h = jnp.dot(scores[h].astype(BF16), vb[:, sl], preferred_element_type=F32)
            o_h = o_h + lax.dot_general(qs[:, sl], st.astype(BF16), _NT,
                                        preferred_element_type=F32)
            upd = lax.dot_general(vb[:, sl], kd[:, sl], (((0,), (0,)), ((), ())),
                                  preferred_element_type=F32)
            st_ref[h] = st * dec[:, sl] + upd
            ms = jnp.mean(o_h * o_h, axis=-1, keepdims=True)
            outs.append(o_h * lax.rsqrt(ms + EPS))
        o = jnp.concatenate(outs, axis=1) * og
        o_ref[0, rows, :] = (o * (xg * _sigmoid(xg))).astype(o_ref.dtype)
        return carry

    lax.fori_loop(0, HGRN_BLOCK // c_rows, chunk, 0)


def _hgrn(proj3, lb_logits, out_gain):
    b, s, _ = proj3.shape

    def col(k):
        return pl.BlockSpec((1, HGRN_BLOCK, HGRN_WIDTH), lambda bi, j, k=k: (bi, j, k))

    return pl.pallas_call(
        _hgrn_kernel,
        grid=(b, s // HGRN_BLOCK),
        in_specs=[
            col(0), col(1), col(2), col(3),
            pl.BlockSpec((2, HGRN_WIDTH), lambda bi, j: (0, 0)),
            pl.BlockSpec((1, HGRN_DV), lambda bi, j: (0, 0)),
        ],
        out_specs=pl.BlockSpec((1, HGRN_BLOCK, HGRN_WIDTH), lambda bi, j: (bi, j, 0)),
        out_shape=jax.ShapeDtypeStruct((b, s, HGRN_WIDTH), BF16),
        scratch_shapes=[pltpu.VMEM((HGRN_HEADS, HGRN_DV, HGRN_DK), F32)],
        compiler_params=pltpu.CompilerParams(
            dimension_semantics=("arbitrary", "arbitrary"),
            vmem_limit_bytes=VMEM_LIMIT_BYTES),
        name="hgrn2",
    )(proj3, proj3, proj3, proj3, lb_logits, out_gain)


def _pair_rms(x, lo_half):
    sq = x * x
    s_lo = jnp.sum(jnp.where(lo_half, sq, 0.0), axis=-1, keepdims=True)
    s_hi = jnp.sum(jnp.where(lo_half, 0.0, sq), axis=-1, keepdims=True)
    ss = jnp.where(lo_half, s_lo, s_hi)
    return x * lax.rsqrt(ss * (1.0 / ATTN_HD) + EPS)


def _swa_kernel(q_ref, kc_ref, vc_ref, kp_ref, vp_ref, qg_ref, kg_ref, sink_ref, o_ref):
    blk = ATTN_BLOCK
    nblk = ATTN_STEP // blk
    first_step = pl.program_id(1) == 0

    lo128 = lax.broadcasted_iota(jnp.int32, (1, 2 * ATTN_HD), 1) < ATTN_HD
    pair = lambda r: jnp.concatenate([r, r], axis=1)
    qgain = pair(qg_ref[...]) * (ATTN_HD ** -0.5)
    kgain = pair(kg_ref[...])

    k_all = jnp.concatenate([kp_ref[0], kc_ref[0]], axis=0)
    v_all = jnp.concatenate([vp_ref[0], vc_ref[0]], axis=0)
    kn = _pair_rms(k_all, lo128) * kgain
    kn_sw = pltpu.roll(kn, ATTN_HD, 1)
    v_sw = pltpu.roll(v_all, ATTN_HD, 1)
    zero = jnp.zeros_like(kn)
    k_pad = [[jnp.where(lo128, kn, zero).astype(BF16), jnp.where(lo128, zero, kn_sw).astype(BF16)],
             [jnp.where(lo128, kn_sw, zero).astype(BF16), jnp.where(lo128, zero, kn).astype(BF16)]]
    v_pad = [[jnp.where(lo128, v_all, zero).astype(BF16), jnp.where(lo128, zero, v_sw).astype(BF16)],
             [jnp.where(lo128, v_sw, zero).astype(BF16), jnp.where(lo128, zero, v_all).astype(BF16)]]
    ones_lo = jnp.broadcast_to(jnp.where(lo128, 1.0, 0.0), (2 * blk, 2 * ATTN_HD))
    ones_pad = jnp.concatenate([ones_lo, 1.0 - ones_lo], axis=0).astype(BF16)

    qi = lax.broadcasted_iota(jnp.int32, (blk, 2 * blk), 0)
    kj = lax.broadcasted_iota(jnp.int32, (blk, 2 * blk), 1)
    rel = qi + blk - kj
    in_window = (rel >= 0) & (rel < WINDOW)
    mask_any = jnp.concatenate([in_window, in_window], axis=1)
    has_prev = jnp.logical_not(first_step)
    in_window0 = in_window & ((kj >= blk) | has_prev)
    mask_first = jnp.concatenate([in_window0, in_window0], axis=1)

    n_slab = ATTN_WIDTH // (2 * ATTN_HD)
    for c in range(n_slab):
        g = (2 * c) // ATTN_GROUP
        lanes = slice(c * 2 * ATTN_HD, (c + 1) * 2 * ATTN_HD)
        qs = (_pair_rms(q_ref[0, :, lanes], lo128) * qgain).astype(BF16)
        sink_a = sink_ref[0:1, 2 * c:2 * c + 1]
        sink_b = sink_ref[0:1, 2 * c + 1:2 * c + 2]
        for n in range(nblk):
            keys = slice(n * blk, (n + 2) * blk)
            k_cat = jnp.concatenate([k_pad[g][0][keys], k_pad[g][1][keys]], axis=0)
            s = lax.dot_general(qs[n * blk:(n + 1) * blk], k_cat, _NT,
                                preferred_element_type=F32)
            s = jnp.where(mask_first if n == 0 else mask_any, s, NEG_INF)
            m_a = jnp.maximum(jnp.max(s[:, :2 * blk], axis=-1, keepdims=True), sink_a)
            m_b = jnp.maximum(jnp.max(s[:, 2 * blk:], axis=-1, keepdims=True), sink_b)
            p = jnp.concatenate([jnp.exp(s[:, :2 * blk] - m_a),
                                 jnp.exp(s[:, 2 * blk:] - m_b)], axis=1).astype(BF16)
            v_cat = jnp.concatenate([v_pad[g][0][keys], v_pad[g][1][keys]], axis=0)
            rhs = jnp.concatenate([v_cat, ones_pad], axis=1)
            nd = jnp.dot(p, rhs, preferred_element_type=F32)
            den = nd[:, 2 * ATTN_HD:] + jnp.where(lo128, jnp.exp(sink_a - m_a),
                                                  jnp.exp(sink_b - m_b))
            o_ref[0, n * blk:(n + 1) * blk, lanes] = (nd[:, :2 * ATTN_HD] / den).astype(o_ref.dtype)


def _swa(proj3, q_gain, k_gain, sinks):
    b, s, _ = proj3.shape
    q_col = 4 * HGRN_WIDTH // ATTN_WIDTH
    k_col = (4 * HGRN_WIDTH + ATTN_WIDTH) // KV_WIDTH
    v_col = k_col + 1
    ratio = ATTN_STEP // ATTN_BLOCK

    def cur(width, colblk):
        return pl.BlockSpec((1, ATTN_STEP, width), lambda bi, n, c=colblk: (bi, n, c))

    def prev(width, colblk):
        return pl.BlockSpec((1, ATTN_BLOCK, width),
                            lambda bi, n, c=colblk: (bi, jnp.maximum(n * ratio - 1, 0), c))

    return pl.pallas_call(
        _swa_kernel,
        grid=(b, s // ATTN_STEP),
        in_specs=[
            cur(ATTN_WIDTH, q_col),
            cur(KV_WIDTH, k_col), cur(KV_WIDTH, v_col),
            prev(KV_WIDTH, k_col), prev(KV_WIDTH, v_col),
            pl.BlockSpec((1, ATTN_HD), lambda bi, n: (0, 0)),
            pl.BlockSpec((1, ATTN_HD), lambda bi, n: (0, 0)),
            pl.BlockSpec((1, ATTN_HEADS), lambda bi, n: (0, 0)),
        ],
        out_specs=pl.BlockSpec((1, ATTN_STEP, ATTN_WIDTH), lambda bi, n: (bi, n, 0)),
        out_shape=jax.ShapeDtypeStruct((b, s, ATTN_WIDTH), BF16),
        compiler_params=pltpu.CompilerParams(
            dimension_semantics=("arbitrary", "arbitrary"),
            vmem_limit_bytes=VMEM_LIMIT_BYTES),
        name="swa",
    )(proj3, proj3, proj3, proj3, proj3, q_gain, k_gain, sinks)


def _ffn_kernel(x_ref, rec_ref, att_ref, wo_ref, gain_ref, wg_ref, wu_ref, wd_ref, o_ref):
    mixed = jnp.concatenate([rec_ref[...], att_ref[...]], axis=1)
    h = x_ref[...] + jnp.dot(mixed, wo_ref[...], preferred_element_type=F32)
    ms = jnp.mean(h * h, axis=-1, keepdims=True)
    u = (h * lax.rsqrt(ms + EPS) * gain_ref[...]).astype(BF16)
    gate = jnp.dot(u, wg_ref[...], preferred_element_type=F32)
    up = jnp.dot(u, wu_ref[...], preferred_element_type=F32)
    act = (gate * _sigmoid(gate) * up).astype(BF16)
    o_ref[...] = h + jnp.dot(act, wd_ref[...], preferred_element_type=F32)


def _ffn(x2, rec2, att2, wo, gain, wg, wu, wd):
    t = x2.shape[0]

    def resident(shape):
        return pl.BlockSpec(shape, lambda i: (0, 0), pipeline_mode=pl.Buffered(1))

    return pl.pallas_call(
        _ffn_kernel,
        grid=(t // FFN_ROWS,),
        in_specs=[
            pl.BlockSpec((FFN_ROWS, D_MODEL), lambda i: (i, 0)),
            pl.BlockSpec((FFN_ROWS, HGRN_WIDTH), lambda i: (i, 0)),
            pl.BlockSpec((FFN_ROWS, ATTN_WIDTH), lambda i: (i, 0)),
            resident((D_MODEL, D_MODEL)),
            resident((1, D_MODEL)),
            resident((D_MODEL, D_FF)),
            resident((D_MODEL, D_FF)),
            resident((D_FF, D_MODEL)),
        ],
        out_specs=pl.BlockSpec((FFN_ROWS, D_MODEL), lambda i: (i, 0)),
        out_shape=jax.ShapeDtypeStruct((t, D_MODEL), F32),
        compiler_params=pltpu.CompilerParams(
            dimension_semantics=("arbitrary",), vmem_limit_bytes=VMEM_LIMIT_BYTES),
        name="outproj_ffn",
    )(x2, rec2, att2, wo, gain, wg, wu, wd)


def kernel(x, norm1_gain, w_in, hgrn_lb_logits, hgrn_out_gain, q_norm_gain, k_norm_gain,
           attn_sinks, w_out, norm2_gain, w_ffn_gate, w_ffn_up, w_ffn_down):
    b, s, d = x.shape
    assert (d, w_in.shape[0]) == (D_MODEL, 1), "single-layer kernel"
    assert s % HGRN_BLOCK == 0 and s % ATTN_STEP == 0 and (b * s) % PROJ_ROWS == 0
    t = b * s
    x2 = x.reshape(t, d)

    proj = _inproj(x2, norm1_gain[0][None, :], w_in[0].astype(BF16))
    proj3 = proj.reshape(b, s, PROJ_WIDTH)
    rec = _hgrn(proj3, hgrn_lb_logits, hgrn_out_gain[0][None, :])
    att = _swa(proj3, q_norm_gain[0][None, :], k_norm_gain[0][None, :],
               attn_sinks[0][None, :])
    out = _ffn(x2, rec.reshape(t, HGRN_WIDTH), att.reshape(t, ATTN_WIDTH),
               w_out[0].astype(BF16), norm2_gain[0][None, :],
               w_ffn_gate[0].astype(BF16), w_ffn_up[0].astype(BF16),
               w_ffn_down[0].astype(BF16))
    return out.reshape(b, s, d)
```

```python
import functools

import jax
import jax.numpy as jnp
from jax import lax
from jax.experimental import pallas as pl
from jax.experimental.pallas import tpu as pltpu

D_MODEL = 1024
HGRN_HEADS = 4
HGRN_DK = 128
HGRN_DV = 128
HGRN_WIDTH = HGRN_HEADS * HGRN_DK
ATTN_HEADS = 8
ATTN_KV_HEADS = 2
ATTN_GROUP = ATTN_HEADS // ATTN_KV_HEADS
ATTN_HD = 64
ATTN_WIDTH = ATTN_HEADS * ATTN_HD
KV_WIDTH = ATTN_KV_HEADS * ATTN_HD
WINDOW = 128
PROJ_WIDTH = 4 * HGRN_WIDTH + ATTN_WIDTH + 2 * KV_WIDTH
D_FF = 2816
EPS = 1e-6
NEG_INF = -1e30

F32 = jnp.float32
BF16 = jnp.bfloat16

VMEM_LIMIT_BYTES = 56 * 1024 * 1024

HGRN_CHUNK = 64
HGRN_LEVELS = 6
HGRN_BLOCK = 256
HGRN_UNROLL = 2
ATTN_BLOCK = 128
ATTN_STEP = 512
PROJ_ROWS = 512
FFN_ROWS = 512

_NT = (((1,), (1,)), ((), ()))


def _sigmoid(x):
    return 1.0 / (1.0 + jnp.exp(-x))


def _inproj_kernel(x_ref, gain_ref, w_ref, o_ref):
    x = x_ref[...]
    ms = jnp.mean(x * x, axis=-1, keepdims=True)
    u = (x * lax.rsqrt(ms + EPS) * gain_ref[...]).astype(BF16)
    o_ref[...] = jnp.dot(u, w_ref[...], preferred_element_type=F32)


def _inproj(x2, gain, w_bf16):
    t = x2.shape[0]
    return pl.pallas_call(
        _inproj_kernel,
        grid=(t // PROJ_ROWS,),
        in_specs=[
            pl.BlockSpec((PROJ_ROWS, D_MODEL), lambda i: (i, 0)),
            pl.BlockSpec((1, D_MODEL), lambda i: (0, 0)),
            pl.BlockSpec((D_MODEL, PROJ_WIDTH), lambda i: (0, 0)),
        ],
        out_specs=pl.BlockSpec((PROJ_ROWS, PROJ_WIDTH), lambda i: (i, 0)),
        out_shape=jax.ShapeDtypeStruct((t, PROJ_WIDTH), F32),
        compiler_params=pltpu.CompilerParams(
            dimension_semantics=("arbitrary",), vmem_limit_bytes=VMEM_LIMIT_BYTES),
        name="inproj",
    )(x2, gain, w_bf16)


def _pair_level(ri, ci):
    xor = ri ^ ci
    lvl = jnp.zeros(ri.shape, jnp.int32)
    for k in range(HGRN_LEVELS):
        lvl = lvl + (xor >= (1 << k)).astype(jnp.int32)
    return jnp.where(ri >= ci, lvl, -1)


def _mul1(a, b):
    if a is None:
        return b
    if b is None:
        return a
    return a * b


def _hgrn_kernel(*refs):
    nh = HGRN_HEADS
    q_refs, f_refs, i_refs, g_refs = refs[0:nh], refs[nh:2 * nh], refs[2 * nh:3 * nh], refs[3 * nh:4 * nh]
    lbl_ref, og_ref, o_ref, st_ref, cv_ref = refs[4 * nh:]
    c_rows = HGRN_CHUNK
    nv = c_rows // 8
    assert nv == 8 and HGRN_LEVELS == 6

    @pl.when(pl.program_id(1) == 0)
    def _():
        st_ref[...] = jnp.zeros_like(st_ref)

    l0 = lbl_ref[0:1, :]
    l1 = lbl_ref[1:2, :]
    lmax = jnp.maximum(l0, l1)
    e0 = jnp.exp(l0 - lmax)
    e1 = jnp.exp(l1 - lmax)
    lb = e0 / (e0 + e1)
    c0 = 0.5 * (1.0 + lb)
    c1 = 0.5 * (1.0 - lb)
    og = og_ref[...]

    ri = lax.broadcasted_iota(jnp.int32, (c_rows, c_rows), 0)
    ci = lax.broadcasted_iota(jnp.int32, (c_rows, c_rows), 1)
    level_nat = _pair_level(ri, ci)
    tok = lambda r: ((r & 7) << 3) | (r >> 3)
    level_tr = _pair_level(tok(ri), tok(ci))

    def bc(t, r):
        return jnp.broadcast_to(t[r:r + 1, :], (8, HGRN_DK))

    def xxt(pieces):
        x = jnp.concatenate(pieces, axis=0).astype(BF16)
        return lax.dot_general(x, x, _NT, preferred_element_type=F32)

    def chunk(c, carry):
        m_tr = [level_tr == l for l in range(4)]
        m_nat = [level_nat == l for l in range(4, 7)]

        heads = range(nh * HGRN_UNROLL)
        hd = [u % nh for u in heads]
        r0 = [pl.multiple_of((c * HGRN_UNROLL + u // nh) * c_rows, c_rows) for u in heads]
        rows = [pl.ds(r0[u], c_rows) for u in heads]
        lanes = [slice(hd[u] * HGRN_DK, (hd[u] + 1) * HGRN_DK) for u in heads]
        vrange = range(nv)

        fp, kp, qp, vtr = [], [], [], []
        for h in heads:
            c0h, c1h = c0[:, lanes[h]], c1[:, lanes[h]]
            fh, kh, qh, vh = [], [], [], []
            for b in vrange:
                srows = pl.ds(r0[h] + b, 8, stride=8)
                ct = c1h * jnp.tanh(0.5 * f_refs[hd[h]][0, srows, :])
                fh.append(c0h + ct)
                kh.append(c1h - ct)
                xq = q_refs[hd[h]][0, srows, :]
                xs = xq * (0.5 * HGRN_DK ** -0.5)
                qh.append(xs * jnp.tanh(0.5 * xq) + xs)
                vh.append(i_refs[hd[h]][0, srows, :])
            fp.append(fh)
            kp.append(kh)
            qp.append(qh)
            vtr.append(jnp.concatenate(vh, axis=0).astype(BF16))

        s_tr = [jnp.where(m_tr[0],
                          lax.dot_general(jnp.concatenate(qp[h], axis=0).astype(BF16),
                                          jnp.concatenate(kp[h], axis=0).astype(BF16), _NT,
                                          preferred_element_type=F32), 0.0) for h in heads]
        s_tr = [jnp.where(m_tr[1], xxt([qp[h][b] * fp[h][b] if b & 1 else kp[h][b] for b in vrange]),
                          s_tr[h]) for h in heads]
        p2 = [[fp[h][b] * fp[h][b - 1] if b & 1 else fp[h][b] for b in vrange] for h in heads]
        x2 = [[None if b & 1 else fp[h][b + 1] for b in vrange] for h in heads]
        s_tr = [jnp.where(m_tr[2], xxt([qp[h][b] * p2[h][b] if b & 2 else _mul1(kp[h][b], x2[h][b])
                                        for b in vrange]), s_tr[h]) for h in heads]
        p4 = [[p2[h][b] * p2[h][(b & ~3) + 1] if b & 2 else p2[h][b] for b in vrange] for h in heads]
        x4 = [[x2[h][b] if b & 2 else _mul1(x2[h][b], p2[h][(b & ~3) + 3]) for b in vrange]
              for h in heads]
        s_tr = [jnp.where(m_tr[3], xxt([qp[h][b] * p4[h][b] if b & 4 else _mul1(kp[h][b], x4[h][b])
                                        for b in vrange]), s_tr[h]) for h in heads]
        p8 = [[p4[h][b] * p4[h][3] if b & 4 else p4[h][b] for b in vrange] for h in heads]
        x8 = [[x4[h][b] if b & 4 else _mul1(x4[h][b], p4[h][7]) for b in vrange] for h in heads]
        tot8 = [p8[h][7] for h in heads]

        for h in heads:
            for b in vrange:
                srows = pl.ds(b, 8, stride=8)
                cv_ref[h, 0, srows, :] = qp[h][b] * p8[h][b]
                cv_ref[h, 1, srows, :] = _mul1(kp[h][b], x8[h][b])
        o_tr = [jnp.dot(s_tr[h].astype(BF16), vtr[h], preferred_element_type=F32) for h in heads]
        an = [[cv_ref[h, 0, 8 * a:8 * a + 8, :] for a in vrange] for h in heads]
        cn = [[cv_ref[h, 1, 8 * a:8 * a + 8, :] for a in vrange] for h in heads]

        s_nat = [jnp.where(m_nat[0], xxt([an[h][a] if a & 1 else cn[h][a] for a in vrange]), 0.0)
                 for h in heads]
        an = [[an[h][a] * bc(tot8[h], a - 1) if a & 1 else an[h][a] for a in vrange] for h in heads]
        cn = [[cn[h][a] if a & 1 else cn[h][a] * bc(tot8[h], a + 1) for a in vrange] for h in heads]
        t16 = [tot8[h] * pltpu.roll(tot8[h], 1, 0) for h in heads]
        s_nat = [jnp.where(m_nat[1], xxt([an[h][a] if a & 2 else cn[h][a] for a in vrange]), s_nat[h])
                 for h in heads]
        an = [[an[h][a] * bc(t16[h], (a & ~3) + 1) if a & 2 else an[h][a] for a in vrange]
              for h in heads]
        cn = [[cn[h][a] if a & 2 else cn[h][a] * bc(t16[h], (a & ~3) + 3) for a in vrange]
              for h in heads]
        t32 = [t16[h] * pltpu.roll(t16[h], 2, 0) for h in heads]
        s_nat = [jnp.where(m_nat[2], xxt([an[h][a] if a & 4 else cn[h][a] for a in vrange]), s_nat[h])
                 for h in heads]
        an = [[an[h][a] * bc(t32[h], 3) if a & 4 else an[h][a] for a in vrange] for h in heads]
        cn = [[cn[h][a] if a & 4 else cn[h][a] * bc(t32[h], 7) for a in vrange] for h in heads]

        for h in heads:
            for b in vrange:
                cv_ref[h, 2, pl.ds(b, 8, stride=8), :] = o_tr[h][8 * b:8 * b + 8, :]

        vb = [i_refs[hd[h]][0, rows[h], :].astype(BF16) for h in heads]
        o_nat = [jnp.dot(s_nat[h].astype(BF16), vb[h], preferred_element_type=F32) for h in heads]
        upd = [lax.dot_general(vb[h], jnp.concatenate(cn[h], axis=0).astype(BF16),
                               (((0,), (0,)), ((), ())), preferred_element_type=F32)
               for h in heads]
        for hh in range(nh):
            st = st_ref[hh]
            for h in range(hh, len(heads), nh):
                qs = jnp.concatenate(an[h], axis=0).astype(BF16)
                o_nat[h] = o_nat[h] + lax.dot_general(qs, st.astype(BF16), _NT,
                                                      preferred_element_type=F32)
                dec = t32[h][3:4, :] * t32[h][7:8, :]
                st = st * dec + upd[h]
            st_ref[hh] = st

        for h in heads:
            o_h = o_nat[h] + cv_ref[h, 2]
            ms = jnp.mean(o_h * o_h, axis=-1, keepdims=True)
            xgs = 0.5 * g_refs[hd[h]][0, rows[h], :]
            gate = xgs * jnp.tanh(xgs) + xgs
            o_ref[0, rows[h], lanes[h]] = (o_h * lax.rsqrt(ms + EPS) * og * gate).astype(o_ref.dtype)
        return carry

    lax.fori_loop(0, HGRN_BLOCK // (c_rows * HGRN_UNROLL), chunk, 0)


def _hgrn(proj3, lb_logits, out_gain):
    b, s, _ = proj3.shape

    def col(k):
        return pl.BlockSpec((1, HGRN_BLOCK, HGRN_DK), lambda bi, j, k=k: (bi, j, k))

    head_cols = [col(part * HGRN_HEADS + h) for part in range(4) for h in range(HGRN_HEADS)]
    return pl.pallas_call(
        _hgrn_kernel,
        grid=(b, s // HGRN_BLOCK),
        in_specs=head_cols + [
            pl.BlockSpec((2, HGRN_WIDTH), lambda bi, j: (0, 0)),
            pl.BlockSpec((1, HGRN_DV), lambda bi, j: (0, 0)),
        ],
        out_specs=pl.BlockSpec((1, HGRN_BLOCK, HGRN_WIDTH), lambda bi, j: (bi, j, 0)),
        out_shape=jax.ShapeDtypeStruct((b, s, HGRN_WIDTH), BF16),
        scratch_shapes=[pltpu.VMEM((HGRN_HEADS, HGRN_DV, HGRN_DK), F32),
                        pltpu.VMEM((HGRN_HEADS * HGRN_UNROLL, 3, HGRN_CHUNK, HGRN_DK), F32)],
        compiler_params=pltpu.CompilerParams(
            dimension_semantics=("arbitrary", "arbitrary"),
            vmem_limit_bytes=VMEM_LIMIT_BYTES),
        name="hgrn2",
    )(*([proj3] * (4 * HGRN_HEADS)), lb_logits, out_gain)


def _pair_rms(x, lo_half):
    sq = x * x
    s_lo = jnp.sum(jnp.where(lo_half, sq, 0.0), axis=-1, keepdims=True)
    s_hi = jnp.sum(jnp.where(lo_half, 0.0, sq), axis=-1, keepdims=True)
    ss = jnp.where(lo_half, s_lo, s_hi)
    return x * lax.rsqrt(ss * (1.0 / ATTN_HD) + EPS)


def _swa_kernel(q_ref, kc_ref, vc_ref, kp_ref, vp_ref, qg_ref, kg_ref, sink_ref, o_ref):
    blk = ATTN_BLOCK
    nblk = ATTN_STEP // blk
    first_step = pl.program_id(1) == 0

    lo128 = lax.broadcasted_iota(jnp.int32, (1, 2 * ATTN_HD), 1) < ATTN_HD
    pair = lambda r: jnp.concatenate([r, r], axis=1)
    qgain = pair(qg_ref[...]) * (ATTN_HD ** -0.5)
    kgain = pair(kg_ref[...])

    k_all = jnp.concatenate([kp_ref[0], kc_ref[0]], axis=0)
    v_all = jnp.concatenate([vp_ref[0], vc_ref[0]], axis=0)
    kn = _pair_rms(k_all, lo128) * kgain
    kn_sw = pltpu.roll(kn, ATTN_HD, 1)
    v_sw = pltpu.roll(v_all, ATTN_HD, 1)
    zero = jnp.zeros_like(kn)
    k_pad = [[jnp.where(lo128, kn, zero).astype(BF16), jnp.where(lo128, zero, kn_sw).astype(BF16)],
             [jnp.where(lo128, kn_sw, zero).astype(BF16), jnp.where(lo128, zero, kn).astype(BF16)]]
    v_pad = [[jnp.where(lo128, v_all, zero).astype(BF16), jnp.where(lo128, zero, v_sw).astype(BF16)],
             [jnp.where(lo128, v_sw, zero).astype(BF16), jnp.where(lo128, zero, v_all).astype(BF16)]]
    ones_lo = jnp.broadcast_to(jnp.where(lo128, 1.0, 0.0), (2 * blk, 2 * ATTN_HD))
    ones_pad = jnp.concatenate([ones_lo, 1.0 - ones_lo], axis=0).astype(BF16)

    qi = lax.broadcasted_iota(jnp.int32, (blk, 2 * blk), 0)
    kj = lax.broadcasted_iota(jnp.int32, (blk, 2 * blk), 1)
    rel = qi + blk - kj
    in_window = (rel >= 0) & (rel < WINDOW)
    mask_any = jnp.concatenate([in_window, in_window], axis=1)
    has_prev = jnp.logical_not(first_step)
    in_window0 = in_window & ((kj >= blk) | has_prev)
    mask_first = jnp.concatenate([in_window0, in_window0], axis=1)

    n_slab = ATTN_WIDTH // (2 * ATTN_HD)
    for c in range(n_slab):
        g = (2 * c) // ATTN_GROUP
        lanes = slice(c * 2 * ATTN_HD, (c + 1) * 2 * ATTN_HD)
        qs = (_pair_rms(q_ref[0, :, lanes], lo128) * qgain).astype(BF16)
        sink_a = sink_ref[0:1, 2 * c:2 * c + 1]
        sink_b = sink_ref[0:1, 2 * c + 1:2 * c + 2]
        for n in range(nblk):
            keys = slice(n * blk, (n + 2) * blk)
            k_cat = jnp.concatenate([k_pad[g][0][keys], k_pad[g][1][keys]], axis=0)
            s = lax.dot_general(qs[n * blk:(n + 1) * blk], k_cat, _NT,
                                preferred_element_type=F32)
            s = jnp.where(mask_first if n == 0 else mask_any, s, NEG_INF)
            m_a = jnp.maximum(jnp.max(s[:, :2 * blk], axis=-1, keepdims=True), sink_a)
            m_b = jnp.maximum(jnp.max(s[:, 2 * blk:], axis=-1, keepdims=True), sink_b)
            p = jnp.concatenate([jnp.exp(s[:, :2 * blk] - m_a),
                                 jnp.exp(s[:, 2 * blk:] - m_b)], axis=1).astype(BF16)
            v_cat = jnp.concatenate([v_pad[g][0][keys], v_pad[g][1][keys]], axis=0)
            rhs = jnp.concatenate([v_cat, ones_pad], axis=1)
            nd = jnp.dot(p, rhs, preferred_element_type=F32)
            den = nd[:, 2 * ATTN_HD:] + jnp.where(lo128, jnp.exp(sink_a - m_a),
                                                  jnp.exp(sink_b - m_b))
            o_ref[0, n * blk:(n + 1) * blk, lanes] = (nd[:, :2 * ATTN_HD] / den).astype(o_ref.dtype)


def _swa(proj3, q_gain, k_gain, sinks):
    b, s, _ = proj3.shape
    q_col = 4 * HGRN_WIDTH // ATTN_WIDTH
    k_col = (4 * HGRN_WIDTH + ATTN_WIDTH) // KV_WIDTH
    v_col = k_col + 1
    ratio = ATTN_STEP // ATTN_BLOCK

    def cur(width, colblk):
        return pl.BlockSpec((1, ATTN_STEP, width), lambda bi, n, c=colblk: (bi, n, c))

    def prev(width, colblk):
        return pl.BlockSpec((1, ATTN_BLOCK, width),
                            lambda bi, n, c=colblk: (bi, jnp.maximum(n * ratio - 1, 0), c))

    return pl.pallas_call(
        _swa_kernel,
        grid=(b, s // ATTN_STEP),
        in_specs=[
            cur(ATTN_WIDTH, q_col),
            cur(KV_WIDTH, k_col), cur(KV_WIDTH, v_col),
            prev(KV_WIDTH, k_col), prev(KV_WIDTH, v_col),
            pl.BlockSpec((1, ATTN_HD), lambda bi, n: (0, 0)),
            pl.BlockSpec((1, ATTN_HD), lambda bi, n: (0, 0)),
            pl.BlockSpec((1, ATTN_HEADS), lambda bi, n: (0, 0)),
        ],
        out_specs=pl.BlockSpec((1, ATTN_STEP, ATTN_WIDTH), lambda bi, n: (bi, n, 0)),
        out_shape=jax.ShapeDtypeStruct((b, s, ATTN_WIDTH), BF16),
        compiler_params=pltpu.CompilerParams(
            dimension_semantics=("arbitrary", "arbitrary"),
            vmem_limit_bytes=VMEM_LIMIT_BYTES),
        name="swa",
    )(proj3, proj3, proj3, proj3, proj3, q_gain, k_gain, sinks)


def _ffn_kernel(x_ref, rec_ref, att_ref, wo_ref, gain_ref, wg_ref, wu_ref, wd_ref, o_ref):
    mixed = jnp.concatenate([rec_ref[...], att_ref[...]], axis=1)
    h = x_ref[...] + jnp.dot(mixed, wo_ref[...], preferred_element_type=F32)
    ms = jnp.mean(h * h, axis=-1, keepdims=True)
    u = (h * lax.rsqrt(ms + EPS) * gain_ref[...]).astype(BF16)
    gate = jnp.dot(u, wg_ref[...], preferred_element_type=F32)
    up = jnp.dot(u, wu_ref[...], preferred_element_type=F32)
    act = (gate * _sigmoid(gate) * up).astype(BF16)
    o_ref[...] = h + jnp.dot(act, wd_ref[...], preferred_element_type=F32)


def _ffn(x2, rec2, att2, wo, gain, wg, wu, wd):
    t = x2.shape[0]

    def resident(shape):
        return pl.BlockSpec(shape, lambda i: (0, 0), pipeline_mode=pl.Buffered(1))

    return pl.pallas_call(
        _ffn_kernel,
        grid=(t // FFN_ROWS,),
        in_specs=[
            pl.BlockSpec((FFN_ROWS, D_MODEL), lambda i: (i, 0)),
            pl.BlockSpec((FFN_ROWS, HGRN_WIDTH), lambda i: (i, 0)),
            pl.BlockSpec((FFN_ROWS, ATTN_WIDTH), lambda i: (i, 0)),
            resident((D_MODEL, D_MODEL)),
            resident((1, D_MODEL)),
            resident((D_MODEL, D_FF)),
            resident((D_MODEL, D_FF)),
            resident((D_FF, D_MODEL)),
        ],
        out_specs=pl.BlockSpec((FFN_ROWS, D_MODEL), lambda i: (i, 0)),
        out_shape=jax.ShapeDtypeStruct((t, D_MODEL), F32),
        compiler_params=pltpu.CompilerParams(
            dimension_semantics=("arbitrary",), vmem_limit_bytes=VMEM_LIMIT_BYTES),
        name="outproj_ffn",
    )(x2, rec2, att2, wo, gain, wg, wu, wd)


def kernel(x, norm1_gain, w_in, hgrn_lb_logits, hgrn_out_gain, q_norm_gain, k_norm_gain,
           attn_sinks, w_out, norm2_gain, w_ffn_gate, w_ffn_up, w_ffn_down):
    b, s, d = x.shape
    assert (d, w_in.shape[0]) == (D_MODEL, 1), "single-layer kernel"
    assert s % HGRN_BLOCK == 0 and s % ATTN_STEP == 0 and (b * s) % PROJ_ROWS == 0
    t = b * s
    x2 = x.reshape(t, d)

    proj = _inproj(x2, norm1_gain[0][None, :], w_in[0].astype(BF16))
    proj3 = proj.reshape(b, s, PROJ_WIDTH)
    rec = _hgrn(proj3, hgrn_lb_logits, hgrn_out_gain[0][None, :])
    att = _swa(proj3, q_norm_gain[0][None, :], k_norm_gain[0][None, :],
               attn_sinks[0][None, :])
    out = _ffn(x2, rec.reshape(t, HGRN_WIDTH), att.reshape(t, ATTN_WIDTH),
               w_out[0].astype(BF16), norm2_gain[0][None, :],
               w_ffn_gate[0].astype(BF16), w_ffn_up[0].astype(BF16),
               w_ffn_down[0].astype(BF16))
    return out.reshape(b, s, d)
```

```python
import functools

import jax
import jax.numpy as jnp
from jax import lax
from jax.experimental import pallas as pl
from jax.experimental.pallas import tpu as pltpu

D_MODEL = 1024
HGRN_HEADS = 4
HGRN_DK = 128
HGRN_DV = 128
HGRN_WIDTH = HGRN_HEADS * HGRN_DK
ATTN_HEADS = 8
ATTN_KV_HEADS = 2
ATTN_GROUP = ATTN_HEADS // ATTN_KV_HEADS
ATTN_HD = 64
ATTN_WIDTH = ATTN_HEADS * ATTN_HD
KV_WIDTH = ATTN_KV_HEADS * ATTN_HD
WINDOW = 128
PROJ_WIDTH = 4 * HGRN_WIDTH + ATTN_WIDTH + 2 * KV_WIDTH
D_FF = 2816
EPS = 1e-6
NEG_INF = -1e30
LOG2E = 1.4426950408889634

F32 = jnp.float32
BF16 = jnp.bfloat16

VMEM_LIMIT_BYTES = 56 * 1024 * 1024

HGRN_CHUNK = 64
HGRN_LEVELS = 6
HGRN_BLOCK = 256
HGRN_UNROLL = 2
ATTN_BLOCK = 128
ATTN_STEP = 512
PROJ_ROWS = 512
FFN_ROWS = 512

_NT = (((1,), (1,)), ((), ()))


def _sigmoid(x):
    return 1.0 / (1.0 + jnp.exp(-x))


def _inproj_kernel(x_ref, gain_ref, w_ref, o_ref):
    x = x_ref[...]
    ms = jnp.mean(x * x, axis=-1, keepdims=True)
    u = (x * lax.rsqrt(ms + EPS) * gain_ref[...]).astype(BF16)
    o_ref[...] = jnp.dot(u, w_ref[...], preferred_element_type=F32)


def _inproj(x2, gain, w_bf16):
    t = x2.shape[0]
    return pl.pallas_call(
        _inproj_kernel,
        grid=(t // PROJ_ROWS,),
        in_specs=[
            pl.BlockSpec((PROJ_ROWS, D_MODEL), lambda i: (i, 0)),
            pl.BlockSpec((1, D_MODEL), lambda i: (0, 0)),
            pl.BlockSpec((D_MODEL, PROJ_WIDTH), lambda i: (0, 0)),
        ],
        out_specs=pl.BlockSpec((PROJ_ROWS, PROJ_WIDTH), lambda i: (i, 0)),
        out_shape=jax.ShapeDtypeStruct((t, PROJ_WIDTH), F32),
        compiler_params=pltpu.CompilerParams(
            dimension_semantics=("arbitrary",), vmem_limit_bytes=VMEM_LIMIT_BYTES),
        name="inproj",
    )(x2, gain, w_bf16)


def _pair_level(ri, ci):
    xor = ri ^ ci
    lvl = jnp.zeros(ri.shape, jnp.int32)
    for k in range(HGRN_LEVELS):
        lvl = lvl + (xor >= (1 << k)).astype(jnp.int32)
    return jnp.where(ri >= ci, lvl, -1)


def _mul1(a, b):
    if a is None:
        return b
    if b is None:
        return a
    return a * b


def _hgrn_kernel(*refs):
    nh = HGRN_HEADS
    q_refs, f_refs, i_refs, g_refs = refs[0:nh], refs[nh:2 * nh], refs[2 * nh:3 * nh], refs[3 * nh:4 * nh]
    lbl_ref, og_ref, o_ref, st_ref, cv_ref = refs[4 * nh:]
    c_rows = HGRN_CHUNK
    nv = c_rows // 8
    assert nv == 8 and HGRN_LEVELS == 6

    @pl.when(pl.program_id(1) == 0)
    def _():
        st_ref[...] = jnp.zeros_like(st_ref)

    l0 = lbl_ref[0:1, :]
    l1 = lbl_ref[1:2, :]
    lmax = jnp.maximum(l0, l1)
    e0 = jnp.exp(l0 - lmax)
    e1 = jnp.exp(l1 - lmax)
    lb = e0 / (e0 + e1)
    c0 = 0.5 * (1.0 + lb)
    c1 = 0.5 * (1.0 - lb)
    og = og_ref[...]

    ri = lax.broadcasted_iota(jnp.int32, (c_rows, c_rows), 0)
    ci = lax.broadcasted_iota(jnp.int32, (c_rows, c_rows), 1)
    level_nat = _pair_level(ri, ci)
    tok = lambda r: ((r & 7) << 3) | (r >> 3)
    level_tr = _pair_level(tok(ri), tok(ci))

    def bc(t, r):
        return jnp.broadcast_to(t[r:r + 1, :], (8, HGRN_DK))

    def xxt(pieces):
        x = jnp.concatenate(pieces, axis=0).astype(BF16)
        return lax.dot_general(x, x, _NT, preferred_element_type=F32)

    def chunk(c, carry):
        m_tr = [level_tr == l for l in range(4)]
        m_nat = [level_nat == l for l in range(4, 7)]

        heads = range(nh * HGRN_UNROLL)
        hd = [u % nh for u in heads]
        r0 = [pl.multiple_of((c * HGRN_UNROLL + u // nh) * c_rows, c_rows) for u in heads]
        rows = [pl.ds(r0[u], c_rows) for u in heads]
        lanes = [slice(hd[u] * HGRN_DK, (hd[u] + 1) * HGRN_DK) for u in heads]
        vrange = range(nv)

        fp, kp, qp, vtr = [], [], [], []
        for h in heads:
            c0h, c1h = c0[:, lanes[h]], c1[:, lanes[h]]
            fh, kh, qh, vh = [], [], [], []
            for b in vrange:
                srows = pl.ds(r0[h] + b, 8, stride=8)
                ct = c1h * jnp.tanh(0.5 * f_refs[hd[h]][0, srows, :])
                fh.append(c0h + ct)
                kh.append(c1h - ct)
                xq = q_refs[hd[h]][0, srows, :]
                xs = xq * (0.5 * HGRN_DK ** -0.5)
                qh.append(xs * jnp.tanh(0.5 * xq) + xs)
                vh.append(i_refs[hd[h]][0, srows, :])
            fp.append(fh)
            kp.append(kh)
            qp.append(qh)
            vtr.append(jnp.concatenate(vh, axis=0).astype(BF16))

        s_tr = [jnp.where(m_tr[0],
                          lax.dot_general(jnp.concatenate(qp[h], axis=0).astype(BF16),
                                          jnp.concatenate(kp[h], axis=0).astype(BF16), _NT,
                                          preferred_element_type=F32), 0.0) for h in heads]
        s_tr = [jnp.where(m_tr[1], xxt([qp[h][b] * fp[h][b] if b & 1 else kp[h][b] for b in vrange]),
                          s_tr[h]) for h in heads]
        p2 = [[fp[h][b] * fp[h][b - 1] if b & 1 else fp[h][b] for b in vrange] for h in heads]
        x2 = [[None if b & 1 else fp[h][b + 1] for b in vrange] for h in heads]
        s_tr = [jnp.where(m_tr[2], xxt([qp[h][b] * p2[h][b] if b & 2 else _mul1(kp[h][b], x2[h][b])
                                        for b in vrange]), s_tr[h]) for h in heads]
        p4 = [[p2[h][b] * p2[h][(b & ~3) + 1] if b & 2 else p2[h][b] for b in vrange] for h in heads]
        x4 = [[x2[h][b] if b & 2 else _mul1(x2[h][b], p2[h][(b & ~3) + 3]) for b in vrange]
              for h in heads]
        s_tr = [jnp.where(m_tr[3], xxt([qp[h][b] * p4[h][b] if b & 4 else _mul1(kp[h][b], x4[h][b])
                                        for b in vrange]), s_tr[h]) for h in heads]
        p8 = [[p4[h][b] * p4[h][3] if b & 4 else p4[h][b] for b in vrange] for h in heads]
        x8 = [[x4[h][b] if b & 4 else _mul1(x4[h][b], p4[h][7]) for b in vrange] for h in heads]
        tot8 = [p8[h][7] for h in heads]

        for h in heads:
            for b in vrange:
                srows = pl.ds(b, 8, stride=8)
                cv_ref[h, 0, srows, :] = qp[h][b] * p8[h][b]
                cv_ref[h, 1, srows, :] = _mul1(kp[h][b], x8[h][b])
        o_tr = [jnp.dot(s_tr[h].astype(BF16), vtr[h], preferred_element_type=F32) for h in heads]
        an = [[cv_ref[h, 0, 8 * a:8 * a + 8, :] for a in vrange] for h in heads]
        cn = [[cv_ref[h, 1, 8 * a:8 * a + 8, :] for a in vrange] for h in heads]

        s_nat = [jnp.where(m_nat[0], xxt([an[h][a] if a & 1 else cn[h][a] for a in vrange]), 0.0)
                 for h in heads]
        an = [[an[h][a] * bc(tot8[h], a - 1) if a & 1 else an[h][a] for a in vrange] for h in heads]
        cn = [[cn[h][a] if a & 1 else cn[h][a] * bc(tot8[h], a + 1) for a in vrange] for h in heads]
        t16 = [tot8[h] * pltpu.roll(tot8[h], 1, 0) for h in heads]
        s_nat = [jnp.where(m_nat[1], xxt([an[h][a] if a & 2 else cn[h][a] for a in vrange]), s_nat[h])
                 for h in heads]
        an = [[an[h][a] * bc(t16[h], (a & ~3) + 1) if a & 2 else an[h][a] for a in vrange]
              for h in heads]
        cn = [[cn[h][a] if a & 2 else cn[h][a] * bc(t16[h], (a & ~3) + 3) for a in vrange]
              for h in heads]
        t32 = [t16[h] * pltpu.roll(t16[h], 2, 0) for h in heads]
        s_nat = [jnp.where(m_nat[2], xxt([an[h][a] if a & 4 else cn[h][a] for a in vrange]), s_nat[h])
                 for h in heads]
        an = [[an[h][a] * bc(t32[h], 3) if a & 4 else an[h][a] for a in vrange] for h in heads]
        cn = [[cn[h][a] if a & 4 else cn[h][a] * bc(t32[h], 7) for a in vrange] for h in heads]

        for h in heads:
            for b in vrange:
                cv_ref[h, 2, pl.ds(b, 8, stride=8), :] = o_tr[h][8 * b:8 * b + 8, :]

        vb = [i_refs[hd[h]][0, rows[h], :].astype(BF16) for h in heads]
        o_nat = [jnp.dot(s_nat[h].astype(BF16), vb[h], preferred_element_type=F32) for h in heads]
        upd = [lax.dot_general(vb[h], jnp.concatenate(cn[h], axis=0).astype(BF16),
                               (((0,), (0,)), ((), ())), preferred_element_type=F32)
               for h in heads]
        for hh in range(nh):
            st = st_ref[hh]
            for h in range(hh, len(heads), nh):
                qs = jnp.concatenate(an[h], axis=0).astype(BF16)
                o_nat[h] = o_nat[h] + lax.dot_general(qs, st.astype(BF16), _NT,
                                                      preferred_element_type=F32)
                dec = t32[h][3:4, :] * t32[h][7:8, :]
                st = st * dec + upd[h]
            st_ref[hh] = st

        for h in heads:
            o_h = o_nat[h] + cv_ref[h, 2]
            ms = jnp.mean(o_h * o_h, axis=-1, keepdims=True)
            xgs = 0.5 * g_refs[hd[h]][0, rows[h], :]
            gate = xgs * jnp.tanh(xgs) + xgs
            o_ref[0, rows[h], lanes[h]] = (o_h * lax.rsqrt(ms + EPS) * og * gate).astype(o_ref.dtype)
        return carry

    lax.fori_loop(0, HGRN_BLOCK // (c_rows * HGRN_UNROLL), chunk, 0)


def _hgrn(proj3, lb_logits, out_gain):
    b, s, _ = proj3.shape

    def col(k):
        return pl.BlockSpec((1, HGRN_BLOCK, HGRN_DK), lambda bi, j, k=k: (bi, j, k))

    head_cols = [col(part * HGRN_HEADS + h) for part in range(4) for h in range(HGRN_HEADS)]
    return pl.pallas_call(
        _hgrn_kernel,
        grid=(b, s // HGRN_BLOCK),
        in_specs=head_cols + [
            pl.BlockSpec((2, HGRN_WIDTH), lambda bi, j: (0, 0)),
            pl.BlockSpec((1, HGRN_DV), lambda bi, j: (0, 0)),
        ],
        out_specs=pl.BlockSpec((1, HGRN_BLOCK, HGRN_WIDTH), lambda bi, j: (bi, j, 0)),
        out_shape=jax.ShapeDtypeStruct((b, s, HGRN_WIDTH), BF16),
        scratch_shapes=[pltpu.VMEM((HGRN_HEADS, HGRN_DV, HGRN_DK), F32),
                        pltpu.VMEM((HGRN_HEADS * HGRN_UNROLL, 3, HGRN_CHUNK, HGRN_DK), F32)],
        compiler_params=pltpu.CompilerParams(
            dimension_semantics=("arbitrary", "arbitrary"),
            vmem_limit_bytes=VMEM_LIMIT_BYTES),
        name="hgrn2",
    )(*([proj3] * (4 * HGRN_HEADS)), lb_logits, out_gain)


def _pair_rms(x, ones_bd):
    ss = jnp.dot((x * x).astype(BF16), ones_bd, preferred_element_type=F32)
    return x * lax.rsqrt(ss * (1.0 / ATTN_HD) + EPS)


def _swa_kernel(q_ref, kc_ref, vc_ref, kp_ref, vp_ref, qg_ref, kg_ref, sink_ref, o_ref):
    blk = ATTN_BLOCK
    nblk = ATTN_STEP // blk
    first_step = pl.program_id(1) == 0

    lo128 = lax.broadcasted_iota(jnp.int32, (1, 2 * ATTN_HD), 1) < ATTN_HD
    hr = lax.broadcasted_iota(jnp.int32, (2 * ATTN_HD, 2 * ATTN_HD), 0) < ATTN_HD
    hc = lax.broadcasted_iota(jnp.int32, (2 * ATTN_HD, 2 * ATTN_HD), 1) < ATTN_HD
    ones_bd = jnp.where(hr == hc, 1.0, 0.0).astype(BF16)
    pair = lambda r: jnp.concatenate([r, r], axis=1)
    qgain = pair(qg_ref[...]) * (ATTN_HD ** -0.5 * LOG2E)
    kgain = pair(kg_ref[...])
    sink2 = sink_ref[...] * LOG2E

    k_all = jnp.concatenate([kp_ref[0], kc_ref[0]], axis=0)
    v_all = jnp.concatenate([vp_ref[0], vc_ref[0]], axis=0)
    kn = _pair_rms(k_all, ones_bd) * kgain
    kn_sw = pltpu.roll(kn, ATTN_HD, 1)
    v_sw = pltpu.roll(v_all, ATTN_HD, 1)
    zero = jnp.zeros_like(kn)
    k_pad = [[jnp.where(lo128, kn, zero).astype(BF16), jnp.where(lo128, zero, kn_sw).astype(BF16)],
             [jnp.where(lo128, kn_sw, zero).astype(BF16), jnp.where(lo128, zero, kn).astype(BF16)]]
    v_pad = [[jnp.where(lo128, v_all, zero).astype(BF16), jnp.where(lo128, zero, v_sw).astype(BF16)],
             [jnp.where(lo128, v_sw, zero).astype(BF16), jnp.where(lo128, zero, v_all).astype(BF16)]]
    ones_lo = jnp.broadcast_to(jnp.where(lo128, 1.0, 0.0), (2 * blk, 2 * ATTN_HD))
    ones_pad = jnp.concatenate([ones_lo, 1.0 - ones_lo], axis=0).astype(BF16)

    qi = lax.broadcasted_iota(jnp.int32, (blk, 2 * blk), 0)
    kj = lax.broadcasted_iota(jnp.int32, (blk, 2 * blk), 1)
    rel = qi + blk - kj
    in_window = (rel >= 0) & (rel < WINDOW)
    mask_any = jnp.concatenate([in_window, in_window], axis=1)
    has_prev = jnp.logical_not(first_step)
    in_window0 = in_window & ((kj >= blk) | has_prev)
    mask_first = jnp.concatenate([in_window0, in_window0], axis=1)

    n_slab = ATTN_WIDTH // (2 * ATTN_HD)
    for c in range(n_slab):
        g = (2 * c) // ATTN_GROUP
        lanes = slice(c * 2 * ATTN_HD, (c + 1) * 2 * ATTN_HD)
        qs = (_pair_rms(q_ref[0, :, lanes], ones_bd) * qgain).astype(BF16)
        sink_a = sink2[:, 2 * c:2 * c + 1]
        sink_b = sink2[:, 2 * c + 1:2 * c + 2]
        sink_slab = jnp.where(lo128, sink_a, sink_b)
        sink_keys = jnp.concatenate([jnp.broadcast_to(sink_a, (1, 2 * blk)),
                                     jnp.broadcast_to(sink_b, (1, 2 * blk))], axis=1)
        for n in range(nblk):
            keys = slice(n * blk, (n + 2) * blk)
            k_cat = jnp.concatenate([k_pad[g][0][keys], k_pad[g][1][keys]], axis=0)
            s = lax.dot_general(qs[n * blk:(n + 1) * blk], k_cat, _NT,
                                preferred_element_type=F32)
            s = jnp.where(mask_first if n == 0 else mask_any, s, NEG_INF)
            m_a = jnp.max(s[:, :2 * blk], axis=-1, keepdims=True)
            m_b = jnp.max(s[:, 2 * blk:], axis=-1, keepdims=True)
            m = jnp.maximum(jnp.concatenate([jnp.broadcast_to(m_a, (blk, 2 * blk)),
                                             jnp.broadcast_to(m_b, (blk, 2 * blk))], axis=1),
                            sink_keys)
            p = jnp.exp2(s - m).astype(BF16)
            v_cat = jnp.concatenate([v_pad[g][0][keys], v_pad[g][1][keys]], axis=0)
            rhs = jnp.concatenate([v_cat, ones_pad], axis=1)
            nd = jnp.dot(p, rhs, preferred_element_type=F32)
            m_slab = jnp.where(lo128, m[:, :2 * ATTN_HD], m[:, 2 * blk:2 * blk + 2 * ATTN_HD])
            den = nd[:, 2 * ATTN_HD:] + jnp.exp2(sink_slab - m_slab)
            o_ref[0, n * blk:(n + 1) * blk, lanes] = (nd[:, :2 * ATTN_HD] / den).astype(o_ref.dtype)


def _swa(proj3, q_gain, k_gain, sinks):
    b, s, _ = proj3.shape
    q_col = 4 * HGRN_WIDTH // ATTN_WIDTH
    k_col = (4 * HGRN_WIDTH + ATTN_WIDTH) // KV_WIDTH
    v_col = k_col + 1
    ratio = ATTN_STEP // ATTN_BLOCK

    def cur(width, colblk):
        return pl.BlockSpec((1, ATTN_STEP, width), lambda bi, n, c=colblk: (bi, n, c))

    def prev(width, colblk):
        return pl.BlockSpec((1, ATTN_BLOCK, width),
                            lambda bi, n, c=colblk: (bi, jnp.maximum(n * ratio - 1, 0), c))

    return pl.pallas_call(
        _swa_kernel,
        grid=(b, s // ATTN_STEP),
        in_specs=[
            cur(ATTN_WIDTH, q_col),
            cur(KV_WIDTH, k_col), cur(KV_WIDTH, v_col),
            prev(KV_WIDTH, k_col), prev(KV_WIDTH, v_col),
            pl.BlockSpec((1, ATTN_HD), lambda bi, n: (0, 0)),
            pl.BlockSpec((1, ATTN_HD), lambda bi, n: (0, 0)),
            pl.BlockSpec((1, ATTN_HEADS), lambda bi, n: (0, 0)),
        ],
        out_specs=pl.BlockSpec((1, ATTN_STEP, ATTN_WIDTH), lambda bi, n: (bi, n, 0)),
        out_shape=jax.ShapeDtypeStruct((b, s, ATTN_WIDTH), BF16),
        compiler_params=pltpu.CompilerParams(
            dimension_semantics=("arbitrary", "arbitrary"),
            vmem_limit_bytes=VMEM_LIMIT_BYTES),
        name="swa",
    )(proj3, proj3, proj3, proj3, proj3, q_gain, k_gain, sinks)


def _ffn_kernel(x_ref, rec_ref, att_ref, wo_ref, gain_ref, wg_ref, wu_ref, wd_ref, o_ref):
    mixed = jnp.concatenate([rec_ref[...], att_ref[...]], axis=1)
    h = x_ref[...] + jnp.dot(mixed, wo_ref[...], preferred_element_type=F32)
    ms = jnp.mean(h * h, axis=-1, keepdims=True)
    u = (h * lax.rsqrt(ms + EPS) * gain_ref[...]).astype(BF16)
    gate = jnp.dot(u, wg_ref[...], preferred_element_type=F32)
    up = jnp.dot(u, wu_ref[...], preferred_element_type=F32)
    act = (gate * _sigmoid(gate) * up).astype(BF16)
    o_ref[...] = h + jnp.dot(act, wd_ref[...], preferred_element_type=F32)


def _ffn(x2, rec2, att2, wo, gain, wg, wu, wd):
    t = x2.shape[0]

    def resident(shape):
        return pl.BlockSpec(shape, lambda i: (0, 0), pipeline_mode=pl.Buffered(1))

    return pl.pallas_call(
        _ffn_kernel,
        grid=(t // FFN_ROWS,),
        in_specs=[
            pl.BlockSpec((FFN_ROWS, D_MODEL), lambda i: (i, 0)),
            pl.BlockSpec((FFN_ROWS, HGRN_WIDTH), lambda i: (i, 0)),
            pl.BlockSpec((FFN_ROWS, ATTN_WIDTH), lambda i: (i, 0)),
            resident((D_MODEL, D_MODEL)),
            resident((1, D_MODEL)),
            resident((D_MODEL, D_FF)),
            resident((D_MODEL, D_FF)),
            resident((D_FF, D_MODEL)),
        ],
        out_specs=pl.BlockSpec((FFN_ROWS, D_MODEL), lambda i: (i, 0)),
        out_shape=jax.ShapeDtypeStruct((t, D_MODEL), F32),
        compiler_params=pltpu.CompilerParams(
            dimension_semantics=("arbitrary",), vmem_limit_bytes=VMEM_LIMIT_BYTES),
        name="outproj_ffn",
    )(x2, rec2, att2, wo, gain, wg, wu, wd)


def kernel(x, norm1_gain, w_in, hgrn_lb_logits, hgrn_out_gain, q_norm_gain, k_norm_gain,
           attn_sinks, w_out, norm2_gain, w_ffn_gate, w_ffn_up, w_ffn_down):
    b, s, d = x.shape
    assert (d, w_in.shape[0]) == (D_MODEL, 1), "single-layer kernel"
    assert s % HGRN_BLOCK == 0 and s % ATTN_STEP == 0 and (b * s) % PROJ_ROWS == 0
    t = b * s
    x2 = x.reshape(t, d)

    proj = _inproj(x2, norm1_gain[0][None, :], w_in[0].astype(BF16))
    proj3 = proj.reshape(b, s, PROJ_WIDTH)
    rec = _hgrn(proj3, hgrn_lb_logits, hgrn_out_gain[0][None, :])
    att = _swa(proj3, q_norm_gain[0][None, :], k_norm_gain[0][None, :],
               attn_sinks[0][None, :])
    out = _ffn(x2, rec.reshape(t, HGRN_WIDTH), att.reshape(t, ATTN_WIDTH),
               w_out[0].astype(BF16), norm2_gain[0][None, :],
               w_ffn_gate[0].astype(BF16), w_ffn_up[0].astype(BF16),
               w_ffn_down[0].astype(BF16))
    return out.reshape(b, s, d)
```

```python
import jax
import jax.numpy as jnp
from jax import lax
from jax.experimental import pallas as pl
from jax.experimental.pallas import tpu as pltpu

D_MODEL = 1024
HGRN_HEADS = 4
HGRN_DK = 128
HGRN_DV = 128
HGRN_WIDTH = HGRN_HEADS * HGRN_DK
ATTN_HEADS = 8
ATTN_KV_HEADS = 2
ATTN_GROUP = ATTN_HEADS // ATTN_KV_HEADS
ATTN_HD = 64
ATTN_WIDTH = ATTN_HEADS * ATTN_HD
KV_WIDTH = ATTN_KV_HEADS * ATTN_HD
WINDOW = 128
PROJ_WIDTH = 4 * HGRN_WIDTH + ATTN_WIDTH + 2 * KV_WIDTH
D_FF = 2816
EPS = 1e-6
NEG_INF = -1e30
LOG2E = 1.4426950408889634

F32 = jnp.float32
BF16 = jnp.bfloat16

LANES = 128
VMEM_LIMIT_BYTES = 56 * 1024 * 1024

HGRN_CHUNK = 64
HGRN_LEVELS = 6
HGRN_UNROLL = 2
ATTN_BLOCK = 128
MIX_ROWS = 512
MIX_ITERS = 4
SEQ_BLOCKS = 4
FFN_ROWS = 512

N_SLABS = PROJ_WIDTH // LANES
LOOP_SLABS = 5
TAIL_SLABS = N_SLABS - MIX_ITERS * LOOP_SLABS
Q_SLAB = 4 * HGRN_HEADS
K_SLAB = Q_SLAB + ATTN_WIDTH // LANES
V_SLAB = K_SLAB + 1
assert TAIL_SLABS == 2 and V_SLAB == N_SLABS - 1
assert MIX_ROWS == MIX_ITERS * HGRN_UNROLL * HGRN_CHUNK == MIX_ITERS * ATTN_BLOCK
assert ATTN_WIDTH // LANES == MIX_ITERS

_NT = (((1,), (1,)), ((), ()))
_TN = (((0,), (0,)), ((), ()))


def _pair_level(ri, ci):
    xor = ri ^ ci
    lvl = jnp.zeros(ri.shape, jnp.int32)
    for k in range(HGRN_LEVELS):
        lvl = lvl + (xor >= (1 << k)).astype(jnp.int32)
    return jnp.where(ri >= ci, lvl, -1)


def _mul1(a, b):
    if a is None:
        return b
    if b is None:
        return a
    return a * b


def _pair_rms(x, ones_bd):
    ss = jnp.dot((x * x).astype(BF16), ones_bd, preferred_element_type=F32)
    return x * lax.rsqrt(ss * (1.0 / ATTN_HD) + EPS)


def _hgrn_chunks(load, store, r0s, c0, c1, og, level_tr, level_nat, st_ref, cv_ref):
    nh = HGRN_HEADS
    c_rows = HGRN_CHUNK
    nv = c_rows // 8
    assert nv == 8 and HGRN_LEVELS == 6

    def bc(t, r):
        return jnp.broadcast_to(t[r:r + 1, :], (8, HGRN_DK))

    def xxt(pieces):
        x = jnp.concatenate(pieces, axis=0).astype(BF16)
        return lax.dot_general(x, x, _NT, preferred_element_type=F32)

    m_tr = [level_tr == l for l in range(4)]
    m_nat = [level_nat == l for l in range(4, 7)]

    units = range(nh * len(r0s))
    hd = [u % nh for u in units]
    r0 = [r0s[u // nh] for u in units]
    rows = [pl.ds(r0[u], c_rows) for u in units]
    lanes = [slice(hd[u] * HGRN_DK, (hd[u] + 1) * HGRN_DK) for u in units]
    vrange = range(nv)

    fp, kp, qp, vtr = [], [], [], []
    for u in units:
        c0h, c1h = c0[:, lanes[u]], c1[:, lanes[u]]
        fh, kh, qh, vh = [], [], [], []
        for b in vrange:
            srows = pl.ds(r0[u] + b, 8, stride=8)
            ct = c1h * jnp.tanh(0.5 * load(1, hd[u], srows))
            fh.append(c0h + ct)
            kh.append(c1h - ct)
            xq = load(0, hd[u], srows)
            xs = xq * (0.5 * HGRN_DK ** -0.5)
            qh.append(xs * jnp.tanh(0.5 * xq) + xs)
            vh.append(load(2, hd[u], srows))
        fp.append(fh)
        kp.append(kh)
        qp.append(qh)
        vtr.append(jnp.concatenate(vh, axis=0).astype(BF16))

    s_tr = [jnp.where(m_tr[0],
                      lax.dot_general(jnp.concatenate(qp[u], axis=0).astype(BF16),
                                      jnp.concatenate(kp[u], axis=0).astype(BF16), _NT,
                                      preferred_element_type=F32), 0.0) for u in units]
    s_tr = [jnp.where(m_tr[1], xxt([qp[u][b] * fp[u][b] if b & 1 else kp[u][b] for b in vrange]),
                      s_tr[u]) for u in units]
    p2 = [[fp[u][b] * fp[u][b - 1] if b & 1 else fp[u][b] for b in vrange] for u in units]
    x2 = [[None if b & 1 else fp[u][b + 1] for b in vrange] for u in units]
    s_tr = [jnp.where(m_tr[2], xxt([qp[u][b] * p2[u][b] if b & 2 else _mul1(kp[u][b], x2[u][b])
                                    for b in vrange]), s_tr[u]) for u in units]
    p4 = [[p2[u][b] * p2[u][(b & ~3) + 1] if b & 2 else p2[u][b] for b in vrange] for u in units]
    x4 = [[x2[u][b] if b & 2 else _mul1(x2[u][b], p2[u][(b & ~3) + 3]) for b in vrange]
          for u in units]
    s_tr = [jnp.where(m_tr[3], xxt([qp[u][b] * p4[u][b] if b & 4 else _mul1(kp[u][b], x4[u][b])
                                    for b in vrange]), s_tr[u]) for u in units]
    p8 = [[p4[u][b] * p4[u][3] if b & 4 else p4[u][b] for b in vrange] for u in units]
    x8 = [[x4[u][b] if b & 4 else _mul1(x4[u][b], p4[u][7]) for b in vrange] for u in units]
    tot8 = [p8[u][7] for u in units]

    for u in units:
        for b in vrange:
            srows = pl.ds(b, 8, stride=8)
            cv_ref[u, 0, srows, :] = qp[u][b] * p8[u][b]
            cv_ref[u, 1, srows, :] = _mul1(kp[u][b], x8[u][b])
    o_tr = [jnp.dot(s_tr[u].astype(BF16), vtr[u], preferred_element_type=F32) for u in units]
    an = [[cv_ref[u, 0, 8 * a:8 * a + 8, :] for a in vrange] for u in units]
    cn = [[cv_ref[u, 1, 8 * a:8 * a + 8, :] for a in vrange] for u in units]

    s_nat = [jnp.where(m_nat[0], xxt([an[u][a] if a & 1 else cn[u][a] for a in vrange]), 0.0)
             for u in units]
    an = [[an[u][a] * bc(tot8[u], a - 1) if a & 1 else an[u][a] for a in vrange] for u in units]
    cn = [[cn[u][a] if a & 1 else cn[u][a] * bc(tot8[u], a + 1) for a in vrange] for u in units]
    t16 = [tot8[u] * pltpu.roll(tot8[u], 1, 0) for u in units]
    s_nat = [jnp.where(m_nat[1], xxt([an[u][a] if a & 2 else cn[u][a] for a in vrange]), s_nat[u])
             for u in units]
    an = [[an[u][a] * bc(t16[u], (a & ~3) + 1) if a & 2 else an[u][a] for a in vrange]
          for u in units]
    cn = [[cn[u][a] if a & 2 else cn[u][a] * bc(t16[u], (a & ~3) + 3) for a in vrange]
          for u in units]
    t32 = [t16[u] * pltpu.roll(t16[u], 2, 0) for u in units]
    s_nat = [jnp.where(m_nat[2], xxt([an[u][a] if a & 4 else cn[u][a] for a in vrange]), s_nat[u])
             for u in units]
    an = [[an[u][a] * bc(t32[u], 3) if a & 4 else an[u][a] for a in vrange] for u in units]
    cn = [[cn[u][a] if a & 4 else cn[u][a] * bc(t32[u], 7) for a in vrange] for u in units]

    for u in units:
        for b in vrange:
            cv_ref[u, 2, pl.ds(b, 8, stride=8), :] = o_tr[u][8 * b:8 * b + 8, :]

    vb = [load(2, hd[u], rows[u]).astype(BF16) for u in units]
    o_nat = [jnp.dot(s_nat[u].astype(BF16), vb[u], preferred_element_type=F32) for u in units]
    upd = [lax.dot_general(vb[u], jnp.concatenate(cn[u], axis=0).astype(BF16),
                           _TN, preferred_element_type=F32) for u in units]
    for h in range(nh):
        st = st_ref[h]
        for u in range(h, len(units), nh):
            qs = jnp.concatenate(an[u], axis=0).astype(BF16)
            o_nat[u] = o_nat[u] + lax.dot_general(qs, st.astype(BF16), _NT,
                                                  preferred_element_type=F32)
            dec = t32[u][3:4, :] * t32[u][7:8, :]
            st = st * dec + upd[u]
        st_ref[h] = st

    for u in units:
        o_h = o_nat[u] + cv_ref[u, 2]
        ms = jnp.mean(o_h * o_h, axis=-1, keepdims=True)
        xgs = 0.5 * load(3, hd[u], rows[u])
        gate = xgs * jnp.tanh(xgs) + xgs
        store(hd[u], rows[u], o_h * lax.rsqrt(ms + EPS) * og * gate)


def _swa_slab(q_slab, k_lo, k_hi, v_lo, v_hi, ones_pad, bias_first, bias_any,
              sink_a, sink_b, qgain, ones_bd, lo128, store):
    blk = ATTN_BLOCK
    qs = (_pair_rms(q_slab, ones_bd) * qgain).astype(BF16)
    sink_slab = jnp.where(lo128, sink_a, sink_b)
    sink_keys = jnp.concatenate([jnp.full((1, 2 * blk), sink_a, F32),
                                 jnp.full((1, 2 * blk), sink_b, F32)], axis=1)
    for n in range(q_slab.shape[0] // blk):
        keys = slice(n * blk, (n + 2) * blk)
        k_cat = jnp.concatenate([k_lo[keys], k_hi[keys]], axis=0)
        s = lax.dot_general(qs[n * blk:(n + 1) * blk], k_cat, _NT,
                            preferred_element_type=F32)
        s = s + (bias_first if n == 0 else bias_any)
        m_a = jnp.max(s[:, :2 * blk], axis=-1, keepdims=True)
        m_b = jnp.max(s[:, 2 * blk:], axis=-1, keepdims=True)
        m = jnp.maximum(jnp.concatenate([jnp.broadcast_to(m_a, (blk, 2 * blk)),
                                         jnp.broadcast_to(m_b, (blk, 2 * blk))], axis=1),
                        sink_keys)
        p = jnp.exp2(s - m).astype(BF16)
        v_cat = jnp.concatenate([v_lo[keys], v_hi[keys]], axis=0)
        rhs = jnp.concatenate([v_cat, ones_pad], axis=1)
        nd = jnp.dot(p, rhs, preferred_element_type=F32)
        m_slab = jnp.where(lo128, m[:, :LANES], m[:, 2 * blk:2 * blk + LANES])
        den = nd[:, LANES:] + jnp.exp2(sink_slab - m_slab)
        store(n, nd[:, :LANES] / den)


def _mix_kernel(x_ref, gain_ref, wm_ref, wt_ref, lbl_ref, og_ref, qg_ref, kg_ref, sink_ref,
                rec_ref, att_ref,
                u_ref, pa_ref, pb_ref, st_ref, cv_ref, kvp_ref, kpad_ref, vpad_ref, bias_ref,
                lev_ref):
    g = pl.program_id(0)
    blk = ATTN_BLOCK

    lo128 = lax.broadcasted_iota(jnp.int32, (1, LANES), 1) < ATTN_HD

    @pl.when(g == 0)
    def _():
        pb_ref[...] = jnp.zeros_like(pb_ref)
        kvp_ref[...] = jnp.zeros_like(kvp_ref)
        st_ref[...] = jnp.zeros_like(st_ref)
        ri = lax.broadcasted_iota(jnp.int32, (HGRN_CHUNK, HGRN_CHUNK), 0)
        ci = lax.broadcasted_iota(jnp.int32, (HGRN_CHUNK, HGRN_CHUNK), 1)
        tok = lambda r: ((r & 7) << 3) | (r >> 3)
        lev_ref[0] = _pair_level(tok(ri), tok(ci))
        lev_ref[1] = _pair_level(ri, ci)
        qi = lax.broadcasted_iota(jnp.int32, (blk, 2 * blk), 0)
        kj = lax.broadcasted_iota(jnp.int32, (blk, 2 * blk), 1)
        rel = qi + blk - kj
        in_window = (rel >= 0) & (rel < WINDOW)
        b_any = jnp.where(in_window, 0.0, NEG_INF)
        b_first = jnp.where(in_window & (kj >= blk), 0.0, NEG_INF)
        bias_ref[0] = jnp.concatenate([b_any, b_any], axis=1)
        bias_ref[1] = jnp.concatenate([b_first, b_first], axis=1)

    first_blk = lax.rem(g + SEQ_BLOCKS - 1, SEQ_BLOCKS) == 0

    @pl.when(first_blk)
    def _():
        st_ref[...] = jnp.zeros_like(st_ref)

    x = x_ref[...]
    ms = jnp.mean(x * x, axis=-1, keepdims=True)
    u_ref[...] = (x * lax.rsqrt(ms + EPS) * gain_ref[...]).astype(BF16)

    hr = lax.broadcasted_iota(jnp.int32, (LANES, LANES), 0) < ATTN_HD
    hc = lax.broadcasted_iota(jnp.int32, (LANES, LANES), 1) < ATTN_HD
    ones_bd = jnp.where(hr == hc, 1.0, 0.0).astype(BF16)
    pair = lambda r: jnp.concatenate([r, r], axis=1)
    qgain = pair(qg_ref[...]) * (ATTN_HD ** -0.5 * LOG2E)
    kgain = pair(kg_ref[...])
    ones_lo = jnp.broadcast_to(jnp.where(lo128, 1.0, 0.0), (2 * blk, LANES))
    ones_pad = jnp.concatenate([ones_lo, 1.0 - ones_lo], axis=0).astype(BF16)

    l0 = lbl_ref[0:1, :]
    l1 = lbl_ref[1:2, :]
    lmax = jnp.maximum(l0, l1)
    e0 = jnp.exp(l0 - lmax)
    e1 = jnp.exp(l1 - lmax)
    lb = e0 / (e0 + e1)
    c0 = 0.5 * (1.0 + lb)
    c1 = 0.5 * (1.0 - lb)
    og = og_ref[...]
    first_idx = first_blk.astype(jnp.int32)

    def step(pc_ref, pp_ref):
        _mix_step(pc_ref, pp_ref, u_ref, wm_ref, wt_ref, rec_ref, att_ref, st_ref, cv_ref,
                  kvp_ref, kpad_ref, vpad_ref, bias_ref, lev_ref, sink_ref,
                  (lo128, ones_bd, qgain, kgain, ones_pad, c0, c1, og, first_idx))

    @pl.when(lax.rem(g, 2) == 0)
    def _():
        step(pa_ref, pb_ref)

    @pl.when(lax.rem(g, 2) == 1)
    def _():
        step(pb_ref, pa_ref)


def _mix_step(pc_ref, pp_ref, u_ref, wm_ref, wt_ref, rec_ref, att_ref, st_ref, cv_ref, kvp_ref,
              kpad_ref, vpad_ref, bias_ref, lev_ref, sink_ref, consts):
    (lo128, ones_bd, qgain, kgain, ones_pad, c0, c1, og, first_idx) = consts
    blk = ATTN_BLOCK

    tail = jnp.dot(u_ref[...], wt_ref[...], preferred_element_type=F32)
    for t in range(TAIL_SLABS):
        pc_ref[MIX_ITERS * LOOP_SLABS + t] = tail[:, t * LANES:(t + 1) * LANES]

    k_all = jnp.concatenate([kvp_ref[0], pp_ref[K_SLAB]], axis=0)
    v_all = jnp.concatenate([kvp_ref[1], pp_ref[V_SLAB]], axis=0)
    kvp_ref[0] = pp_ref[K_SLAB, MIX_ROWS - blk:, :]
    kvp_ref[1] = pp_ref[V_SLAB, MIX_ROWS - blk:, :]
    kn = _pair_rms(k_all, ones_bd) * kgain
    kn_sw = pltpu.roll(kn, ATTN_HD, 1)
    v_sw = pltpu.roll(v_all, ATTN_HD, 1)
    zero = jnp.zeros_like(kn)
    kpad_ref[0, 0] = jnp.where(lo128, kn, zero).astype(BF16)
    kpad_ref[0, 1] = jnp.where(lo128, zero, kn_sw).astype(BF16)
    kpad_ref[1, 0] = jnp.where(lo128, kn_sw, zero).astype(BF16)
    kpad_ref[1, 1] = jnp.where(lo128, zero, kn).astype(BF16)
    vpad_ref[0, 0] = jnp.where(lo128, v_all, zero).astype(BF16)
    vpad_ref[0, 1] = jnp.where(lo128, zero, v_sw).astype(BF16)
    vpad_ref[1, 0] = jnp.where(lo128, v_sw, zero).astype(BF16)
    vpad_ref[1, 1] = jnp.where(lo128, zero, v_all).astype(BF16)

    def body(i, carry):
        res = jnp.dot(u_ref[...], wm_ref[i], preferred_element_type=F32)
        for t in range(LOOP_SLABS):
            pc_ref[i * LOOP_SLABS + t] = res[:, t * LANES:(t + 1) * LANES]

        def load(part, h, rows):
            return pp_ref[part * HGRN_HEADS + h, rows, :]

        def store(h, rows, tile):
            rec_ref[0, rows, h * HGRN_DV:(h + 1) * HGRN_DV] = tile.astype(rec_ref.dtype)

        r0s = [pl.multiple_of((i * HGRN_UNROLL + cc) * HGRN_CHUNK, HGRN_CHUNK)
               for cc in range(HGRN_UNROLL)]
        _hgrn_chunks(load, store, r0s, c0, c1, og, lev_ref[0], lev_ref[1], st_ref, cv_ref)

        kvh = lax.div(i, ATTN_GROUP // 2)

        def store_att(n, tile):
            att_ref[0, i, n * blk:(n + 1) * blk, :] = tile.astype(att_ref.dtype)

        _swa_slab(pp_ref[Q_SLAB + i], kpad_ref[kvh, 0], kpad_ref[kvh, 1],
                  vpad_ref[kvh, 0], vpad_ref[kvh, 1], ones_pad,
                  bias_ref[first_idx], bias_ref[0],
                  sink_ref[2 * i] * LOG2E, sink_ref[2 * i + 1] * LOG2E,
                  qgain, ones_bd, lo128, store_att)
        return carry

    lax.fori_loop(0, MIX_ITERS, body, 0)


def _mix(x2, seq_len, gain, w_main, w_tail, lb_logits, out_gain, q_gain, k_gain, sinks):
    t = x2.shape[0]
    nblk = t // MIX_ROWS
    assert seq_len // MIX_ROWS == SEQ_BLOCKS
    batch = t // seq_len

    def in_rows(gi):
        return (jnp.minimum(gi, nblk - 1), 0)

    def out_block(gi):
        blk_i = jnp.maximum(gi - 1, 0)
        return blk_i // SEQ_BLOCKS, lax.rem(blk_i, SEQ_BLOCKS)

    full = lambda shape: pl.BlockSpec(shape, lambda gi: (0,) * len(shape))
    return pl.pallas_call(
        _mix_kernel,
        grid=(nblk + 1,),
        in_specs=[
            pl.BlockSpec((MIX_ROWS, D_MODEL), in_rows),
            full((1, D_MODEL)),
            full((MIX_ITERS, D_MODEL, LOOP_SLABS * LANES)),
            full((D_MODEL, TAIL_SLABS * LANES)),
            full((2, HGRN_WIDTH)),
            full((1, HGRN_DV)),
            full((1, ATTN_HD)),
            full((1, ATTN_HD)),
            pl.BlockSpec(memory_space=pltpu.SMEM),
        ],
        out_specs=[
            pl.BlockSpec((1, MIX_ROWS, HGRN_WIDTH),
                         lambda gi: (*out_block(gi), 0)),
            pl.BlockSpec((1, MIX_ITERS, MIX_ROWS, LANES),
                         lambda gi: (out_block(gi)[0], 0, out_block(gi)[1], 0)),
        ],
        out_shape=[
            jax.ShapeDtypeStruct((batch, seq_len, HGRN_WIDTH), BF16),
            jax.ShapeDtypeStruct((batch, MIX_ITERS, seq_len, LANES), BF16),
        ],
        scratch_shapes=[
            pltpu.VMEM((MIX_ROWS, D_MODEL), BF16),
            pltpu.VMEM((N_SLABS, MIX_ROWS, LANES), F32),
            pltpu.VMEM((N_SLABS, MIX_ROWS, LANES), F32),
            pltpu.VMEM((HGRN_HEADS, HGRN_DV, HGRN_DK), F32),
            pltpu.VMEM((HGRN_HEADS * HGRN_UNROLL, 3, HGRN_CHUNK, HGRN_DK), F32),
            pltpu.VMEM((2, ATTN_BLOCK, LANES), F32),
            pltpu.VMEM((ATTN_KV_HEADS, 2, ATTN_BLOCK + MIX_ROWS, LANES), BF16),
            pltpu.VMEM((ATTN_KV_HEADS, 2, ATTN_BLOCK + MIX_ROWS, LANES), BF16),
            pltpu.VMEM((2, ATTN_BLOCK, 4 * ATTN_BLOCK), F32),
            pltpu.VMEM((2, HGRN_CHUNK, HGRN_CHUNK), jnp.int32),
        ],
        compiler_params=pltpu.CompilerParams(
            dimension_semantics=("arbitrary",), vmem_limit_bytes=VMEM_LIMIT_BYTES),
        name="mixer",
    )(x2, gain, w_main, w_tail, lb_logits, out_gain, q_gain, k_gain, sinks)


def _ffn_kernel(x_ref, rec_ref, att_ref, wo_ref, gain_ref, wg_ref, wu_ref, wd_ref, o_ref):
    mixed = jnp.concatenate([rec_ref[0]] + [att_ref[0, c] for c in range(MIX_ITERS)],
                            axis=1)
    h = x_ref[...] + jnp.dot(mixed, wo_ref[...], preferred_element_type=F32)
    ms = jnp.mean(h * h, axis=-1, keepdims=True)
    u = (h * lax.rsqrt(ms + EPS) * gain_ref[...]).astype(BF16)
    gate = jnp.dot(u, wg_ref[...], preferred_element_type=F32)
    up = jnp.dot(u, wu_ref[...], preferred_element_type=F32)
    gs = 0.5 * gate
    act = ((gs * jnp.tanh(gs) + gs) * up).astype(BF16)
    o_ref[...] = h + jnp.dot(act, wd_ref[...], preferred_element_type=F32)


def _ffn(x2, rec, att, wo, gain, wg, wu, wd):
    t = x2.shape[0]
    seq_blocks = rec.shape[1] // FFN_ROWS

    def resident(shape):
        return pl.BlockSpec(shape, lambda i: (0, 0), pipeline_mode=pl.Buffered(1))

    return pl.pallas_call(
        _ffn_kernel,
        grid=(t // FFN_ROWS,),
        in_specs=[
            pl.BlockSpec((FFN_ROWS, D_MODEL), lambda i: (i, 0)),
            pl.BlockSpec((1, FFN_ROWS, HGRN_WIDTH),
                         lambda i: (i // seq_blocks, lax.rem(i, seq_blocks), 0)),
            pl.BlockSpec((1, MIX_ITERS, FFN_ROWS, LANES),
                         lambda i: (i // seq_blocks, 0, lax.rem(i, seq_blocks), 0)),
            resident((D_MODEL, D_MODEL)),
            resident((1, D_MODEL)),
            resident((D_MODEL, D_FF)),
            resident((D_MODEL, D_FF)),
            resident((D_FF, D_MODEL)),
        ],
        out_specs=pl.BlockSpec((FFN_ROWS, D_MODEL), lambda i: (i, 0)),
        out_shape=jax.ShapeDtypeStruct((t, D_MODEL), F32),
        compiler_params=pltpu.CompilerParams(
            dimension_semantics=("arbitrary",), vmem_limit_bytes=VMEM_LIMIT_BYTES),
        name="outproj_ffn",
    )(x2, rec, att, wo, gain, wg, wu, wd)


def kernel(x, norm1_gain, w_in, hgrn_lb_logits, hgrn_out_gain, q_norm_gain, k_norm_gain,
           attn_sinks, w_out, norm2_gain, w_ffn_gate, w_ffn_up, w_ffn_down):
    b, s, d = x.shape
    assert (d, w_in.shape[0]) == (D_MODEL, 1), "single-layer kernel"
    assert s == SEQ_BLOCKS * MIX_ROWS and MIX_ROWS == FFN_ROWS
    t = b * s
    x2 = x.reshape(t, d)

    w_bf = w_in[0].astype(BF16)
    n_main = MIX_ITERS * LOOP_SLABS * LANES
    w_main = w_bf[:, :n_main].reshape(D_MODEL, MIX_ITERS, LOOP_SLABS * LANES).transpose(1, 0, 2)
    w_tail = w_bf[:, n_main:]
    rec, att = _mix(x2, s, norm1_gain[0][None, :], w_main, w_tail, hgrn_lb_logits,
                    hgrn_out_gain[0][None, :], q_norm_gain[0][None, :], k_norm_gain[0][None, :],
                    attn_sinks[0])
    out = _ffn(x2, rec, att, w_out[0].astype(BF16), norm2_gain[0][None, :],
               w_ffn_gate[0].astype(BF16), w_ffn_up[0].astype(BF16),
               w_ffn_down[0].astype(BF16))
    return out.reshape(b, s, d)
```

```python
import jax
import jax.numpy as jnp
from jax import lax
from jax.experimental import pallas as pl
from jax.experimental.pallas import tpu as pltpu

D_MODEL = 1024
HGRN_HEADS = 4
HGRN_DK = 128
HGRN_DV = 128
HGRN_WIDTH = HGRN_HEADS * HGRN_DK
ATTN_HEADS = 8
ATTN_KV_HEADS = 2
ATTN_GROUP = ATTN_HEADS // ATTN_KV_HEADS
ATTN_HD = 64
ATTN_WIDTH = ATTN_HEADS * ATTN_HD
KV_WIDTH = ATTN_KV_HEADS * ATTN_HD
WINDOW = 128
PROJ_WIDTH = 4 * HGRN_WIDTH + ATTN_WIDTH + 2 * KV_WIDTH
D_FF = 2816
EPS = 1e-6
NEG_INF = -1e30
LOG2E = 1.4426950408889634

F32 = jnp.float32
BF16 = jnp.bfloat16

LANES = 128
VMEM_LIMIT_BYTES = 56 * 1024 * 1024

HGRN_CHUNK = 64
HGRN_LEVELS = 6
HGRN_UNROLL = 2
ATTN_BLOCK = 128
MIX_ROWS = 512
MIX_ITERS = 4
SEQ_BLOCKS = 4
FFN_ROWS = 512

N_SLABS = PROJ_WIDTH // LANES
LOOP_SLABS = 5
TAIL_SLABS = N_SLABS - MIX_ITERS * LOOP_SLABS
Q_SLAB = 4 * HGRN_HEADS
K_SLAB = Q_SLAB + ATTN_WIDTH // LANES
V_SLAB = K_SLAB + 1
assert TAIL_SLABS == 2 and V_SLAB == N_SLABS - 1
assert MIX_ROWS == MIX_ITERS * HGRN_UNROLL * HGRN_CHUNK == MIX_ITERS * ATTN_BLOCK
assert ATTN_WIDTH // LANES == MIX_ITERS

_NT = (((1,), (1,)), ((), ()))
_TN = (((0,), (0,)), ((), ()))


_LEVEL_Q = ((1, 3, 5, 7), (2, 3, 6, 7), (4, 5, 6, 7))
_LEVEL_K = ((0, 2, 4, 6), (0, 1, 4, 5), (0, 1, 2, 3))
LEVEL_ROWS = 8 * sum(len(q) for q in _LEVEL_Q)


def _level_masks():
    r = lax.broadcasted_iota(jnp.int32, (LEVEL_ROWS, LEVEL_ROWS), 0)
    c = lax.broadcasted_iota(jnp.int32, (LEVEL_ROWS, LEVEL_ROWS), 1)
    lev_r, lev_c = r >> 5, c >> 5
    j_r, j_c = (r >> 3) & 3, (c >> 3) & 3
    same_block = (((lev_r == 0) & (j_r == j_c))
                  | ((lev_r == 1) & ((j_r >> 1) == (j_c >> 1)))
                  | (lev_r == 2))
    any_row = (lev_r == lev_c) & same_block
    same_row = any_row & ((r & 7) == (c & 7))
    return same_row.astype(jnp.int32), any_row.astype(jnp.int32)


def _mul1(a, b):
    if a is None:
        return b
    if b is None:
        return a
    return a * b


def _pair_rms(x, ones_bd):
    ss = jnp.dot((x * x).astype(BF16), ones_bd, preferred_element_type=F32)
    return x * lax.rsqrt(ss * (1.0 / ATTN_HD) + EPS)


def _hgrn_chunks(load, store, r0s, c0, c1, og, mask_tr, mask_nat, st_ref, cv_ref):
    nh = HGRN_HEADS
    c_rows = HGRN_CHUNK
    nv = c_rows // 8
    assert nv == 8 and HGRN_LEVELS == 6

    def bc(t, r):
        return jnp.broadcast_to(t[r:r + 1, :], (8, HGRN_DK))

    def cat16(pieces):
        return jnp.concatenate(pieces, axis=0).astype(BF16)

    q_order = _LEVEL_Q[0] + _LEVEL_Q[1] + _LEVEL_Q[2]
    k_order = _LEVEL_K[0] + _LEVEL_K[1] + _LEVEL_K[2]

    def stacked_attend(q_pieces, k_pieces, v_pieces, mask):
        s = [lax.dot_general(cat16(q_pieces[u]), cat16(k_pieces[u]), _NT,
                             preferred_element_type=F32) for u in units]
        s = [jnp.where(mask, s[u], 0.0).astype(BF16) for u in units]
        o = [jnp.dot(s[u], cat16([v_pieces[u][j] for j in k_order]),
                     preferred_element_type=F32) for u in units]
        return [[o[u][8 * n:8 * n + 8, :] for n in range(len(q_order))] for u in units]

    def scatter_add(acc, pieces):
        for n, j in enumerate(q_order):
            acc[j] = pieces[n] if acc[j] is None else acc[j] + pieces[n]
        return acc

    m_tr = mask_tr != 0
    m_nat = mask_nat != 0

    units = range(nh * len(r0s))
    hd = [u % nh for u in units]
    r0 = [r0s[u // nh] for u in units]
    rows = [pl.ds(r0[u], c_rows) for u in units]
    lanes = [slice(hd[u] * HGRN_DK, (hd[u] + 1) * HGRN_DK) for u in units]
    vrange = range(nv)

    fp, kp, qp, vtr = [], [], [], []
    for u in units:
        c0h, c1h = c0[:, lanes[u]], c1[:, lanes[u]]
        fh, kh, qh, vh = [], [], [], []
        for b in vrange:
            srows = pl.ds(r0[u] + b, 8, stride=8)
            ct = c1h * jnp.tanh(0.5 * load(1, hd[u], srows))
            fh.append(c0h + ct)
            kh.append(c1h - ct)
            xq = load(0, hd[u], srows)
            xs = xq * (0.5 * HGRN_DK ** -0.5)
            qh.append(xs * jnp.tanh(0.5 * xq) + xs)
            vh.append(load(2, hd[u], srows))
        fp.append(fh)
        kp.append(kh)
        qp.append(qh)
        vtr.append(vh)

    o_tr = [[jnp.sum(qp[u][b] * kp[u][b], axis=-1, keepdims=True) * vtr[u][b] for b in vrange]
            for u in units]

    p2 = [[fp[u][b] * fp[u][b - 1] if b & 1 else fp[u][b] for b in vrange] for u in units]
    x2 = [[None if b & 1 else fp[u][b + 1] for b in vrange] for u in units]
    p4 = [[p2[u][b] * p2[u][(b & ~3) + 1] if b & 2 else p2[u][b] for b in vrange] for u in units]
    x4 = [[x2[u][b] if b & 2 else _mul1(x2[u][b], p2[u][(b & ~3) + 3]) for b in vrange]
          for u in units]
    p8 = [[p4[u][b] * p4[u][3] if b & 4 else p4[u][b] for b in vrange] for u in units]
    x8 = [[x4[u][b] if b & 4 else _mul1(x4[u][b], p4[u][7]) for b in vrange] for u in units]
    tot8 = [p8[u][7] for u in units]

    for u in units:
        for b in vrange:
            srows = pl.ds(b, 8, stride=8)
            cv_ref[u, 0, srows, :] = qp[u][b] * p8[u][b]
            cv_ref[u, 1, srows, :] = _mul1(kp[u][b], x8[u][b])

    q_pieces = [([qp[u][b] * fp[u][b] for b in _LEVEL_Q[0]]
                 + [qp[u][b] * p2[u][b] for b in _LEVEL_Q[1]]
                 + [qp[u][b] * p4[u][b] for b in _LEVEL_Q[2]]) for u in units]
    k_pieces = [([kp[u][b] for b in _LEVEL_K[0]]
                 + [_mul1(kp[u][b], x2[u][b]) for b in _LEVEL_K[1]]
                 + [_mul1(kp[u][b], x4[u][b]) for b in _LEVEL_K[2]]) for u in units]
    att_tr = stacked_attend(q_pieces, k_pieces, vtr, m_tr)
    o_tr = [scatter_add(o_tr[u], att_tr[u]) for u in units]

    an = [[cv_ref[u, 0, 8 * a:8 * a + 8, :] for a in vrange] for u in units]
    cn = [[cv_ref[u, 1, 8 * a:8 * a + 8, :] for a in vrange] for u in units]
    vnat = [[load(2, hd[u], pl.ds(r0[u] + 8 * a, 8)) for a in vrange] for u in units]

    q4 = [[an[u][a] for a in _LEVEL_Q[0]] for u in units]
    k4 = [[cn[u][a] for a in _LEVEL_K[0]] for u in units]
    an = [[an[u][a] * bc(tot8[u], a - 1) if a & 1 else an[u][a] for a in vrange] for u in units]
    cn = [[cn[u][a] if a & 1 else cn[u][a] * bc(tot8[u], a + 1) for a in vrange] for u in units]
    t16 = [tot8[u] * pltpu.roll(tot8[u], 1, 0) for u in units]
    q5 = [[an[u][a] for a in _LEVEL_Q[1]] for u in units]
    k5 = [[cn[u][a] for a in _LEVEL_K[1]] for u in units]
    an = [[an[u][a] * bc(t16[u], (a & ~3) + 1) if a & 2 else an[u][a] for a in vrange]
          for u in units]
    cn = [[cn[u][a] if a & 2 else cn[u][a] * bc(t16[u], (a & ~3) + 3) for a in vrange]
          for u in units]
    t32 = [t16[u] * pltpu.roll(t16[u], 2, 0) for u in units]
    q6 = [[an[u][a] for a in _LEVEL_Q[2]] for u in units]
    k6 = [[cn[u][a] for a in _LEVEL_K[2]] for u in units]
    an = [[an[u][a] * bc(t32[u], 3) if a & 4 else an[u][a] for a in vrange] for u in units]
    cn = [[cn[u][a] if a & 4 else cn[u][a] * bc(t32[u], 7) for a in vrange] for u in units]
    att_nat = stacked_attend([q4[u] + q5[u] + q6[u] for u in units],
                             [k4[u] + k5[u] + k6[u] for u in units], vnat, m_nat)
    o_nat = []
    for u in units:
        pieces = scatter_add([None] * nv, att_nat[u])
        pieces[0] = jnp.zeros((8, HGRN_DV), F32)
        o_nat.append(jnp.concatenate(pieces, axis=0))

    for u in units:
        for b in vrange:
            cv_ref[u, 2, pl.ds(b, 8, stride=8), :] = o_tr[u][b]

    upd = [lax.dot_general(cat16(vnat[u]), cat16(cn[u]),
                           _TN, preferred_element_type=F32) for u in units]
    for h in range(nh):
        st = st_ref[h]
        for u in range(h, len(units), nh):
            qs = cat16(an[u])
            o_nat[u] = o_nat[u] + lax.dot_general(qs, st.astype(BF16), _NT,
                                                  preferred_element_type=F32)
            dec = t32[u][3:4, :] * t32[u][7:8, :]
            st = st * dec + upd[u]
        st_ref[h] = st

    for u in units:
        o_h = o_nat[u] + cv_ref[u, 2]
        ms = jnp.mean(o_h * o_h, axis=-1, keepdims=True)
        xgs = 0.5 * load(3, hd[u], rows[u])
        gate = xgs * jnp.tanh(xgs) + xgs
        store(hd[u], rows[u], o_h * lax.rsqrt(ms + EPS) * og * gate)


def _swa_slab(q_slab, k_lo, k_hi, v_lo, v_hi, ones_pad, bias_first, bias_any,
              sink_a, sink_b, qgain, ones_bd, lo128, store):
    blk = ATTN_BLOCK
    qs = (_pair_rms(q_slab, ones_bd) * qgain).astype(BF16)
    sink_slab = jnp.where(lo128, sink_a, sink_b)
    sink_keys = jnp.concatenate([jnp.full((1, 2 * blk), sink_a, F32),
                                 jnp.full((1, 2 * blk), sink_b, F32)], axis=1)
    for n in range(q_slab.shape[0] // blk):
        keys = slice(n * blk, (n + 2) * blk)
        k_cat = jnp.concatenate([k_lo[keys], k_hi[keys]], axis=0)
        s = lax.dot_general(qs[n * blk:(n + 1) * blk], k_cat, _NT,
                            preferred_element_type=F32)
        s = s + (bias_first if n == 0 else bias_any)
        m_a = jnp.max(s[:, :2 * blk], axis=-1, keepdims=True)
        m_b = jnp.max(s[:, 2 * blk:], axis=-1, keepdims=True)
        m = jnp.maximum(jnp.concatenate([jnp.broadcast_to(m_a, (blk, 2 * blk)),
                                         jnp.broadcast_to(m_b, (blk, 2 * blk))], axis=1),
                        sink_keys)
        p = jnp.exp2(s - m).astype(BF16)
        v_cat = jnp.concatenate([v_lo[keys], v_hi[keys]], axis=0)
        rhs = jnp.concatenate([v_cat, ones_pad], axis=1)
        nd = jnp.dot(p, rhs, preferred_element_type=F32)
        m_slab = jnp.where(lo128, m[:, :LANES], m[:, 2 * blk:2 * blk + LANES])
        den = nd[:, LANES:] + jnp.exp2(sink_slab - m_slab)
        store(n, nd[:, :LANES] / den)


def _mix_kernel(x_ref, gain_ref, wm_ref, wt_ref, lbl_ref, og_ref, qg_ref, kg_ref, sink_ref,
                rec_ref, att_ref,
                u_ref, pa_ref, pb_ref, st_ref, cv_ref, kvp_ref, kpad_ref, vpad_ref, bias_ref,
                lev_ref):
    g = pl.program_id(0)
    blk = ATTN_BLOCK

    lo128 = lax.broadcasted_iota(jnp.int32, (1, LANES), 1) < ATTN_HD

    @pl.when(g == 0)
    def _():
        pb_ref[...] = jnp.zeros_like(pb_ref)
        kvp_ref[...] = jnp.zeros_like(kvp_ref)
        st_ref[...] = jnp.zeros_like(st_ref)
        lev_ref[0], lev_ref[1] = _level_masks()
        qi = lax.broadcasted_iota(jnp.int32, (blk, 2 * blk), 0)
        kj = lax.broadcasted_iota(jnp.int32, (blk, 2 * blk), 1)
        rel = qi + blk - kj
        in_window = (rel >= 0) & (rel < WINDOW)
        b_any = jnp.where(in_window, 0.0, NEG_INF)
        b_first = jnp.where(in_window & (kj >= blk), 0.0, NEG_INF)
        bias_ref[0] = jnp.concatenate([b_any, b_any], axis=1)
        bias_ref[1] = jnp.concatenate([b_first, b_first], axis=1)

    first_blk = lax.rem(g + SEQ_BLOCKS - 1, SEQ_BLOCKS) == 0

    @pl.when(first_blk)
    def _():
        st_ref[...] = jnp.zeros_like(st_ref)

    x = x_ref[...]
    ms = jnp.mean(x * x, axis=-1, keepdims=True)
    u_ref[...] = (x * lax.rsqrt(ms + EPS) * gain_ref[...]).astype(BF16)

    hr = lax.broadcasted_iota(jnp.int32, (LANES, LANES), 0) < ATTN_HD
    hc = lax.broadcasted_iota(jnp.int32, (LANES, LANES), 1) < ATTN_HD
    ones_bd = jnp.where(hr == hc, 1.0, 0.0).astype(BF16)
    pair = lambda r: jnp.concatenate([r, r], axis=1)
    qgain = pair(qg_ref[...]) * (ATTN_HD ** -0.5 * LOG2E)
    kgain = pair(kg_ref[...])
    ones_lo = jnp.broadcast_to(jnp.where(lo128, 1.0, 0.0), (2 * blk, LANES))
    ones_pad = jnp.concatenate([ones_lo, 1.0 - ones_lo], axis=0).astype(BF16)

    l0 = lbl_ref[0:1, :]
    l1 = lbl_ref[1:2, :]
    lmax = jnp.maximum(l0, l1)
    e0 = jnp.exp(l0 - lmax)
    e1 = jnp.exp(l1 - lmax)
    lb = e0 / (e0 + e1)
    c0 = 0.5 * (1.0 + lb)
    c1 = 0.5 * (1.0 - lb)
    og = og_ref[...]
    first_idx = first_blk.astype(jnp.int32)

    def step(pc_ref, pp_ref):
        _mix_step(pc_ref, pp_ref, u_ref, wm_ref, wt_ref, rec_ref, att_ref, st_ref, cv_ref,
                  kvp_ref, kpad_ref, vpad_ref, bias_ref, lev_ref, sink_ref,
                  (lo128, ones_bd, qgain, kgain, ones_pad, c0, c1, og, first_idx))

    @pl.when(lax.rem(g, 2) == 0)
    def _():
        step(pa_ref, pb_ref)

    @pl.when(lax.rem(g, 2) == 1)
    def _():
        step(pb_ref, pa_ref)


def _mix_step(pc_ref, pp_ref, u_ref, wm_ref, wt_ref, rec_ref, att_ref, st_ref, cv_ref, kvp_ref,
              kpad_ref, vpad_ref, bias_ref, lev_ref, sink_ref, consts):
    (lo128, ones_bd, qgain, kgain, ones_pad, c0, c1, og, first_idx) = consts
    blk = ATTN_BLOCK

    tail = jnp.dot(u_ref[...], wt_ref[...], preferred_element_type=F32)
    for t in range(TAIL_SLABS):
        pc_ref[MIX_ITERS * LOOP_SLABS + t] = tail[:, t * LANES:(t + 1) * LANES]

    k_all = jnp.concatenate([kvp_ref[0], pp_ref[K_SLAB]], axis=0)
    v_all = jnp.concatenate([kvp_ref[1], pp_ref[V_SLAB]], axis=0)
    kvp_ref[0] = pp_ref[K_SLAB, MIX_ROWS - blk:, :]
    kvp_ref[1] = pp_ref[V_SLAB, MIX_ROWS - blk:, :]
    kn = _pair_rms(k_all, ones_bd) * kgain
    kn_sw = pltpu.roll(kn, ATTN_HD, 1)
    v_sw = pltpu.roll(v_all, ATTN_HD, 1)
    zero = jnp.zeros_like(kn)
    kpad_ref[0, 0] = jnp.where(lo128, kn, zero).astype(BF16)
    kpad_ref[0, 1] = jnp.where(lo128, zero, kn_sw).astype(BF16)
    kpad_ref[1, 0] = jnp.where(lo128, kn_sw, zero).astype(BF16)
    kpad_ref[1, 1] = jnp.where(lo128, zero, kn).astype(BF16)
    vpad_ref[0, 0] = jnp.where(lo128, v_all, zero).astype(BF16)
    vpad_ref[0, 1] = jnp.where(lo128, zero, v_sw).astype(BF16)
    vpad_ref[1, 0] = jnp.where(lo128, v_sw, zero).astype(BF16)
    vpad_ref[1, 1] = jnp.where(lo128, zero, v_all).astype(BF16)

    def body(i, carry):
        res = jnp.dot(u_ref[...], wm_ref[i], preferred_element_type=F32)
        for t in range(LOOP_SLABS):
            pc_ref[i * LOOP_SLABS + t] = res[:, t * LANES:(t + 1) * LANES]

        def load(part, h, rows):
            return pp_ref[part * HGRN_HEADS + h, rows, :]

        def store(h, rows, tile):
            rec_ref[0, rows, h * HGRN_DV:(h + 1) * HGRN_DV] = tile.astype(rec_ref.dtype)

        r0s = [pl.multiple_of((i * HGRN_UNROLL + cc) * HGRN_CHUNK, HGRN_CHUNK)
               for cc in range(HGRN_UNROLL)]
        _hgrn_chunks(load, store, r0s, c0, c1, og, lev_ref[0], lev_ref[1], st_ref, cv_ref)

        kvh = lax.div(i, ATTN_GROUP // 2)

        def store_att(n, tile):
            att_ref[0, i, n * blk:(n + 1) * blk, :] = tile.astype(att_ref.dtype)

        _swa_slab(pp_ref[Q_SLAB + i], kpad_ref[kvh, 0], kpad_ref[kvh, 1],
                  vpad_ref[kvh, 0], vpad_ref[kvh, 1], ones_pad,
                  bias_ref[first_idx], bias_ref[0],
                  sink_ref[2 * i] * LOG2E, sink_ref[2 * i + 1] * LOG2E,
                  qgain, ones_bd, lo128, store_att)
        return carry

    lax.fori_loop(0, MIX_ITERS, body, 0)


def _mix(x2, seq_len, gain, w_main, w_tail, lb_logits, out_gain, q_gain, k_gain, sinks):
    t = x2.shape[0]
    nblk = t // MIX_ROWS
    assert seq_len // MIX_ROWS == SEQ_BLOCKS
    batch = t // seq_len

    def in_rows(gi):
        return (jnp.minimum(gi, nblk - 1), 0)

    def out_block(gi):
        blk_i = jnp.maximum(gi - 1, 0)
        return blk_i // SEQ_BLOCKS, lax.rem(blk_i, SEQ_BLOCKS)

    full = lambda shape: pl.BlockSpec(shape, lambda gi: (0,) * len(shape))
    return pl.pallas_call(
        _mix_kernel,
        grid=(nblk + 1,),
        in_specs=[
            pl.BlockSpec((MIX_ROWS, D_MODEL), in_rows),
            full((1, D_MODEL)),
            full((MIX_ITERS, D_MODEL, LOOP_SLABS * LANES)),
            full((D_MODEL, TAIL_SLABS * LANES)),
            full((2, HGRN_WIDTH)),
            full((1, HGRN_DV)),
            full((1, ATTN_HD)),
            full((1, ATTN_HD)),
            pl.BlockSpec(memory_space=pltpu.SMEM),
        ],
        out_specs=[
            pl.BlockSpec((1, MIX_ROWS, HGRN_WIDTH),
                         lambda gi: (*out_block(gi), 0)),
            pl.BlockSpec((1, MIX_ITERS, MIX_ROWS, LANES),
                         lambda gi: (out_block(gi)[0], 0, out_block(gi)[1], 0)),
        ],
        out_shape=[
            jax.ShapeDtypeStruct((batch, seq_len, HGRN_WIDTH), BF16),
            jax.ShapeDtypeStruct((batch, MIX_ITERS, seq_len, LANES), BF16),
        ],
        scratch_shapes=[
            pltpu.VMEM((MIX_ROWS, D_MODEL), BF16),
            pltpu.VMEM((N_SLABS, MIX_ROWS, LANES), F32),
            pltpu.VMEM((N_SLABS, MIX_ROWS, LANES), F32),
            pltpu.VMEM((HGRN_HEADS, HGRN_DV, HGRN_DK), F32),
            pltpu.VMEM((HGRN_HEADS * HGRN_UNROLL, 3, HGRN_CHUNK, HGRN_DK), F32),
            pltpu.VMEM((2, ATTN_BLOCK, LANES), F32),
            pltpu.VMEM((ATTN_KV_HEADS, 2, ATTN_BLOCK + MIX_ROWS, LANES), BF16),
            pltpu.VMEM((ATTN_KV_HEADS, 2, ATTN_BLOCK + MIX_ROWS, LANES), BF16),
            pltpu.VMEM((2, ATTN_BLOCK, 4 * ATTN_BLOCK), F32),
            pltpu.VMEM((2, LEVEL_ROWS, LEVEL_ROWS), jnp.int32),
        ],
        compiler_params=pltpu.CompilerParams(
            dimension_semantics=("arbitrary",), vmem_limit_bytes=VMEM_LIMIT_BYTES),
        name="mixer",
    )(x2, gain, w_main, w_tail, lb_logits, out_gain, q_gain, k_gain, sinks)


def _ffn_kernel(x_ref, rec_ref, att_ref, wo_ref, gain_ref, wg_ref, wu_ref, wd_ref, o_ref):
    mixed = jnp.concatenate([rec_ref[0]] + [att_ref[0, c] for c in range(MIX_ITERS)],
                            axis=1)
    h = x_ref[...] + jnp.dot(mixed, wo_ref[...], preferred_element_type=F32)
    ms = jnp.mean(h * h, axis=-1, keepdims=True)
    u = (h * lax.rsqrt(ms + EPS) * gain_ref[...]).astype(BF16)
    gate = jnp.dot(u, wg_ref[...], preferred_element_type=F32)
    up = jnp.dot(u, wu_ref[...], preferred_element_type=F32)
    gs = 0.5 * gate
    act = ((gs * jnp.tanh(gs) + gs) * up).astype(BF16)
    o_ref[...] = h + jnp.dot(act, wd_ref[...], preferred_element_type=F32)


def _ffn(x2, rec, att, wo, gain, wg, wu, wd):
    t = x2.shape[0]
    seq_blocks = rec.shape[1] // FFN_ROWS

    def resident(shape):
        return pl.BlockSpec(shape, lambda i: (0, 0), pipeline_mode=pl.Buffered(1))

    return pl.pallas_call(
        _ffn_kernel,
        grid=(t // FFN_ROWS,),
        in_specs=[
            pl.BlockSpec((FFN_ROWS, D_MODEL), lambda i: (i, 0)),
            pl.BlockSpec((1, FFN_ROWS, HGRN_WIDTH),
                         lambda i: (i // seq_blocks, lax.rem(i, seq_blocks), 0)),
            pl.BlockSpec((1, MIX_ITERS, FFN_ROWS, LANES),
                         lambda i: (i // seq_blocks, 0, lax.rem(i, seq_blocks), 0)),
            resident((D_MODEL, D_MODEL)),
            resident((1, D_MODEL)),
            resident((D_MODEL, D_FF)),
            resident((D_MODEL, D_FF)),
            resident((D_FF, D_MODEL)),
        ],
        out_specs=pl.BlockSpec((FFN_ROWS, D_MODEL), lambda i: (i, 0)),
        out_shape=jax.ShapeDtypeStruct((t, D_MODEL), F32),
        compiler_params=pltpu.CompilerParams(
            dimension_semantics=("arbitrary",), vmem_limit_bytes=VMEM_LIMIT_BYTES),
        name="outproj_ffn",
    )(x2, rec, att, wo, gain, wg, wu, wd)


def kernel(x, norm1_gain, w_in, hgrn_lb_logits, hgrn_out_gain, q_norm_gain, k_norm_gain,
           attn_sinks, w_out, norm2_gain, w_ffn_gate, w_ffn_up, w_ffn_down):
    b, s, d = x.shape
    assert (d, w_in.shape[0]) == (D_MODEL, 1), "single-layer kernel"
    assert s == SEQ_BLOCKS * MIX_ROWS and MIX_ROWS == FFN_ROWS
    t = b * s
    x2 = x.reshape(t, d)

    w_bf = w_in[0].astype(BF16)
    n_main = MIX_ITERS * LOOP_SLABS * LANES
    w_main = w_bf[:, :n_main].reshape(D_MODEL, MIX_ITERS, LOOP_SLABS * LANES).transpose(1, 0, 2)
    w_tail = w_bf[:, n_main:]
    rec, att = _mix(x2, s, norm1_gain[0][None, :], w_main, w_tail, hgrn_lb_logits,
                    hgrn_out_gain[0][None, :], q_norm_gain[0][None, :], k_norm_gain[0][None, :],
                    attn_sinks[0])
    out = _ffn(x2, rec, att, w_out[0].astype(BF16), norm2_gain[0][None, :],
               w_ffn_gate[0].astype(BF16), w_ffn_up[0].astype(BF16),
               w_ffn_down[0].astype(BF16))
    return out.reshape(b, s, d)
```

```python
import jax
import jax.numpy as jnp
from jax import lax
from jax.experimental import pallas as pl
from jax.experimental.pallas import tpu as pltpu

D_MODEL = 1024
HGRN_HEADS = 4
HGRN_DK = 128
HGRN_DV = 128
HGRN_WIDTH = HGRN_HEADS * HGRN_DK
ATTN_HEADS = 8
ATTN_KV_HEADS = 2
ATTN_GROUP = ATTN_HEADS // ATTN_KV_HEADS
ATTN_HD = 64
ATTN_WIDTH = ATTN_HEADS * ATTN_HD
KV_WIDTH = ATTN_KV_HEADS * ATTN_HD
WINDOW = 128
PROJ_WIDTH = 4 * HGRN_WIDTH + ATTN_WIDTH + 2 * KV_WIDTH
D_FF = 2816
EPS = 1e-6
NEG_INF = -1e30
LOG2E = 1.4426950408889634

F32 = jnp.float32
BF16 = jnp.bfloat16

LANES = 128
VMEM_LIMIT_BYTES = 56 * 1024 * 1024

HGRN_CHUNK = 64
HGRN_LEVELS = 6
HGRN_UNROLL = 2
ATTN_BLOCK = 128
MIX_ROWS = 512
MIX_ITERS = 4
SEQ_BLOCKS = 4
FILL_PER_STAGE = 0
FFN_ROWS = 512

N_SLABS = PROJ_WIDTH // LANES
LOOP_SLABS = 4
TAIL_SLABS = N_SLABS - MIX_ITERS * LOOP_SLABS
Q_SLAB = 4 * HGRN_HEADS
K_SLAB = Q_SLAB + ATTN_WIDTH // LANES
V_SLAB = K_SLAB + 1
assert MIX_ITERS * LOOP_SLABS == Q_SLAB and V_SLAB == N_SLABS - 1
assert MIX_ROWS == MIX_ITERS * HGRN_UNROLL * HGRN_CHUNK == MIX_ITERS * ATTN_BLOCK
assert ATTN_WIDTH // LANES == MIX_ITERS

_NT = (((1,), (1,)), ((), ()))
_TN = (((0,), (0,)), ((), ()))


_LEVEL_Q = ((1, 3, 5, 7), (2, 3, 6, 7), (4, 5, 6, 7))
_LEVEL_K = ((0, 2, 4, 6), (0, 1, 4, 5), (0, 1, 2, 3))
LEVEL_ROWS = 8 * sum(len(q) for q in _LEVEL_Q)


def _level_masks():
    r = lax.broadcasted_iota(jnp.int32, (LEVEL_ROWS, LEVEL_ROWS), 0)
    c = lax.broadcasted_iota(jnp.int32, (LEVEL_ROWS, LEVEL_ROWS), 1)
    lev_r, lev_c = r >> 5, c >> 5
    j_r, j_c = (r >> 3) & 3, (c >> 3) & 3
    same_block = (((lev_r == 0) & (j_r == j_c))
                  | ((lev_r == 1) & ((j_r >> 1) == (j_c >> 1)))
                  | (lev_r == 2))
    any_row = (lev_r == lev_c) & same_block
    same_row = any_row & ((r & 7) == (c & 7))
    return same_row.astype(jnp.int32), any_row.astype(jnp.int32)


def _mul1(a, b):
    if a is None:
        return b
    if b is None:
        return a
    return a * b


def _pair_rms(x, ones_bd):
    ss = jnp.dot((x * x).astype(BF16), ones_bd, preferred_element_type=F32)
    return x * lax.rsqrt(ss * (1.0 / ATTN_HD) + EPS)


def _hgrn_chunks(load, store, r0s, c0, c1, og, mask_tr, mask_nat, st_ref, cv_ref, fill):
    nh = HGRN_HEADS
    c_rows = HGRN_CHUNK
    nv = c_rows // 8
    assert nv == 8 and HGRN_LEVELS == 6

    def bc(t, r):
        return jnp.broadcast_to(t[r:r + 1, :], (8, HGRN_DK))

    def cat16(pieces):
        return jnp.concatenate(pieces, axis=0).astype(BF16)

    q_order = _LEVEL_Q[0] + _LEVEL_Q[1] + _LEVEL_Q[2]
    k_order = _LEVEL_K[0] + _LEVEL_K[1] + _LEVEL_K[2]

    def stacked_attend(q_pieces, k_pieces, v_pieces, mask):
        s = [lax.dot_general(cat16(q_pieces[u]), cat16(k_pieces[u]), _NT,
                             preferred_element_type=F32) for u in units]
        fill()
        s = [jnp.where(mask, s[u], 0.0).astype(BF16) for u in units]
        o = [jnp.dot(s[u], cat16([v_pieces[u][j] for j in k_order]),
                     preferred_element_type=F32) for u in units]
        fill()
        return [[o[u][8 * n:8 * n + 8, :] for n in range(len(q_order))] for u in units]

    def scatter_add(acc, pieces):
        for n, j in enumerate(q_order):
            acc[j] = pieces[n] if acc[j] is None else acc[j] + pieces[n]
        return acc

    m_tr = mask_tr != 0
    m_nat = mask_nat != 0

    units = range(nh * len(r0s))
    hd = [u % nh for u in units]
    r0 = [r0s[u // nh] for u in units]
    rows = [pl.ds(r0[u], c_rows) for u in units]
    lanes = [slice(hd[u] * HGRN_DK, (hd[u] + 1) * HGRN_DK) for u in units]
    vrange = range(nv)

    fp, kp, qp, vtr = [], [], [], []
    for u in units:
        c0h, c1h = c0[:, lanes[u]], c1[:, lanes[u]]
        fh, kh, qh, vh = [], [], [], []
        for b in vrange:
            srows = pl.ds(r0[u] + b, 8, stride=8)
            ct = c1h * jnp.tanh(0.5 * load(1, hd[u], srows))
            fh.append(c0h + ct)
            kh.append(c1h - ct)
            xq = load(0, hd[u], srows)
            xs = xq * (0.5 * HGRN_DK ** -0.5)
            qh.append(xs * jnp.tanh(0.5 * xq) + xs)
            vh.append(load(2, hd[u], srows))
        fp.append(fh)
        kp.append(kh)
        qp.append(qh)
        vtr.append(vh)

    fill()
    o_tr = [[jnp.sum(qp[u][b] * kp[u][b], axis=-1, keepdims=True) * vtr[u][b] for b in vrange]
            for u in units]

    p2 = [[fp[u][b] * fp[u][b - 1] if b & 1 else fp[u][b] for b in vrange] for u in units]
    x2 = [[None if b & 1 else fp[u][b + 1] for b in vrange] for u in units]
    p4 = [[p2[u][b] * p2[u][(b & ~3) + 1] if b & 2 else p2[u][b] for b in vrange] for u in units]
    x4 = [[x2[u][b] if b & 2 else _mul1(x2[u][b], p2[u][(b & ~3) + 3]) for b in vrange]
          for u in units]
    p8 = [[p4[u][b] * p4[u][3] if b & 4 else p4[u][b] for b in vrange] for u in units]
    x8 = [[x4[u][b] if b & 4 else _mul1(x4[u][b], p4[u][7]) for b in vrange] for u in units]
    tot8 = [p8[u][7] for u in units]

    for u in units:
        for b in vrange:
            srows = pl.ds(b, 8, stride=8)
            cv_ref[u, 0, srows, :] = qp[u][b] * p8[u][b]
            cv_ref[u, 1, srows, :] = _mul1(kp[u][b], x8[u][b])

    fill()
    q_pieces = [([qp[u][b] * fp[u][b] for b in _LEVEL_Q[0]]
                 + [qp[u][b] * p2[u][b] for b in _LEVEL_Q[1]]
                 + [qp[u][b] * p4[u][b] for b in _LEVEL_Q[2]]) for u in units]
    k_pieces = [([kp[u][b] for b in _LEVEL_K[0]]
                 + [_mul1(kp[u][b], x2[u][b]) for b in _LEVEL_K[1]]
                 + [_mul1(kp[u][b], x4[u][b]) for b in _LEVEL_K[2]]) for u in units]
    att_tr = stacked_attend(q_pieces, k_pieces, vtr, m_tr)
    o_tr = [scatter_add(o_tr[u], att_tr[u]) for u in units]

    an = [[cv_ref[u, 0, 8 * a:8 * a + 8, :] for a in vrange] for u in units]
    cn = [[cv_ref[u, 1, 8 * a:8 * a + 8, :] for a in vrange] for u in units]
    vnat = [[load(2, hd[u], pl.ds(r0[u] + 8 * a, 8)) for a in vrange] for u in units]

    q4 = [[an[u][a] for a in _LEVEL_Q[0]] for u in units]
    k4 = [[cn[u][a] for a in _LEVEL_K[0]] for u in units]
    an = [[an[u][a] * bc(tot8[u], a - 1) if a & 1 else an[u][a] for a in vrange] for u in units]
    cn = [[cn[u][a] if a & 1 else cn[u][a] * bc(tot8[u], a + 1) for a in vrange] for u in units]
    t16 = [tot8[u] * pltpu.roll(tot8[u], 1, 0) for u in units]
    q5 = [[an[u][a] for a in _LEVEL_Q[1]] for u in units]
    k5 = [[cn[u][a] for a in _LEVEL_K[1]] for u in units]
    an = [[an[u][a] * bc(t16[u], (a & ~3) + 1) if a & 2 else an[u][a] for a in vrange]
          for u in units]
    cn = [[cn[u][a] if a & 2 else cn[u][a] * bc(t16[u], (a & ~3) + 3) for a in vrange]
          for u in units]
    t32 = [t16[u] * pltpu.roll(t16[u], 2, 0) for u in units]
    q6 = [[an[u][a] for a in _LEVEL_Q[2]] for u in units]
    k6 = [[cn[u][a] for a in _LEVEL_K[2]] for u in units]
    an = [[an[u][a] * bc(t32[u], 3) if a & 4 else an[u][a] for a in vrange] for u in units]
    cn = [[cn[u][a] if a & 4 else cn[u][a] * bc(t32[u], 7) for a in vrange] for u in units]
    att_nat = stacked_attend([q4[u] + q5[u] + q6[u] for u in units],
                             [k4[u] + k5[u] + k6[u] for u in units], vnat, m_nat)
    o_nat = []
    for u in units:
        pieces = scatter_add([None] * nv, att_nat[u])
        pieces[0] = jnp.zeros((8, HGRN_DV), F32)
        o_nat.append(jnp.concatenate(pieces, axis=0))

    for u in units:
        for b in vrange:
            cv_ref[u, 2, pl.ds(b, 8, stride=8), :] = o_tr[u][b]

    fill()
    upd = [lax.dot_general(cat16(vnat[u]), cat16(cn[u]),
                           _TN, preferred_element_type=F32) for u in units]
    for h in range(nh):
        st = st_ref[h]
        for u in range(h, len(units), nh):
            qs = cat16(an[u])
            o_nat[u] = o_nat[u] + lax.dot_general(qs, st.astype(BF16), _NT,
                                                  preferred_element_type=F32)
            dec = t32[u][3:4, :] * t32[u][7:8, :]
            st = st * dec + upd[u]
        st_ref[h] = st

    fill()
    for u in units:
        o_h = o_nat[u] + cv_ref[u, 2]
        ms = jnp.mean(o_h * o_h, axis=-1, keepdims=True)
        xgs = 0.5 * load(3, hd[u], rows[u])
        gate = xgs * jnp.tanh(xgs) + xgs
        store(hd[u], rows[u], o_h * lax.rsqrt(ms + EPS) * og * gate)


def _swa_slab(q_slab, k_lo, k_hi, v_lo, v_hi, ones_pad, bias_first, bias_any,
              sink_a, sink_b, qgain, ones_bd, lo128, store):
    blk = ATTN_BLOCK
    nblk = MIX_ROWS // blk
    state = {}

    def prep():
        state["qs"] = (_pair_rms(q_slab(), ones_bd) * qgain).astype(BF16)
        state["sink_slab"] = jnp.where(lo128, sink_a, sink_b)
        state["sink_keys"] = jnp.concatenate([jnp.full((1, 2 * blk), sink_a, F32),
                                              jnp.full((1, 2 * blk), sink_b, F32)], axis=1)

    def scores(n):
        keys = slice(n * blk, (n + 2) * blk)
        k_cat = jnp.concatenate([k_lo()[keys], k_hi()[keys]], axis=0)
        s = lax.dot_general(state["qs"][n * blk:(n + 1) * blk], k_cat, _NT,
                            preferred_element_type=F32)
        s = s + (bias_first() if n == 0 else bias_any())
        m_a = jnp.max(s[:, :2 * blk], axis=-1, keepdims=True)
        m_b = jnp.max(s[:, 2 * blk:], axis=-1, keepdims=True)
        m = jnp.maximum(jnp.concatenate([jnp.broadcast_to(m_a, (blk, 2 * blk)),
                                         jnp.broadcast_to(m_b, (blk, 2 * blk))], axis=1),
                        state["sink_keys"])
        state["p", n] = jnp.exp2(s - m).astype(BF16)
        state["m", n] = jnp.where(lo128, m[:, :LANES], m[:, 2 * blk:2 * blk + LANES])

    def attend(n):
        keys = slice(n * blk, (n + 2) * blk)
        v_cat = jnp.concatenate([v_lo()[keys], v_hi()[keys]], axis=0)
        rhs = jnp.concatenate([v_cat, ones_pad], axis=1)
        nd = jnp.dot(state.pop(("p", n)), rhs, preferred_element_type=F32)
        den = nd[:, LANES:] + jnp.exp2(state["sink_slab"] - state.pop(("m", n)))
        store(n, nd[:, :LANES] / den)

    stages = [prep]
    for n in range(nblk):
        stages += [lambda n=n: scores(n), lambda n=n: attend(n)]
    return stages


def _mix_kernel(x_ref, gain_ref, wm_ref, wt_ref, lbl_ref, og_ref, qg_ref, kg_ref, sink_ref,
                rec_ref, att_ref,
                u_ref, pa_ref, pb_ref, st_ref, cv_ref, kvp_ref, kpad_ref, vpad_ref, bias_ref,
                lev_ref):
    g = pl.program_id(0)
    blk = ATTN_BLOCK

    lo128 = lax.broadcasted_iota(jnp.int32, (1, LANES), 1) < ATTN_HD

    @pl.when(g == 0)
    def _():
        pb_ref[...] = jnp.zeros_like(pb_ref)
        kvp_ref[...] = jnp.zeros_like(kvp_ref)
        st_ref[...] = jnp.zeros_like(st_ref)
        lev_ref[0], lev_ref[1] = _level_masks()
        qi = lax.broadcasted_iota(jnp.int32, (blk, 2 * blk), 0)
        kj = lax.broadcasted_iota(jnp.int32, (blk, 2 * blk), 1)
        rel = qi + blk - kj
        in_window = (rel >= 0) & (rel < WINDOW)
        b_any = jnp.where(in_window, 0.0, NEG_INF)
        b_first = jnp.where(in_window & (kj >= blk), 0.0, NEG_INF)
        bias_ref[0] = jnp.concatenate([b_any, b_any], axis=1)
        bias_ref[1] = jnp.concatenate([b_first, b_first], axis=1)

    first_blk = lax.rem(g + SEQ_BLOCKS - 1, SEQ_BLOCKS) == 0

    @pl.when(first_blk)
    def _():
        st_ref[...] = jnp.zeros_like(st_ref)

    x = x_ref[...]
    ms = jnp.mean(x * x, axis=-1, keepdims=True)
    u_ref[...] = (x * lax.rsqrt(ms + EPS) * gain_ref[...]).astype(BF16)

    hr = lax.broadcasted_iota(jnp.int32, (LANES, LANES), 0) < ATTN_HD
    hc = lax.broadcasted_iota(jnp.int32, (LANES, LANES), 1) < ATTN_HD
    ones_bd = jnp.where(hr == hc, 1.0, 0.0).astype(BF16)
    pair = lambda r: jnp.concatenate([r, r], axis=1)
    qgain = pair(qg_ref[...]) * (ATTN_HD ** -0.5 * LOG2E)
    kgain = pair(kg_ref[...])
    ones_lo = jnp.broadcast_to(jnp.where(lo128, 1.0, 0.0), (2 * blk, LANES))
    ones_pad = jnp.concatenate([ones_lo, 1.0 - ones_lo], axis=0).astype(BF16)

    l0 = lbl_ref[0:1, :]
    l1 = lbl_ref[1:2, :]
    lmax = jnp.maximum(l0, l1)
    e0 = jnp.exp(l0 - lmax)
    e1 = jnp.exp(l1 - lmax)
    lb = e0 / (e0 + e1)
    c0 = 0.5 * (1.0 + lb)
    c1 = 0.5 * (1.0 - lb)
    og = og_ref[...]
    first_idx = first_blk.astype(jnp.int32)

    def step(pc_ref, pp_ref):
        _mix_step(pc_ref, pp_ref, u_ref, wm_ref, wt_ref, rec_ref, att_ref, st_ref, cv_ref,
                  kvp_ref, kpad_ref, vpad_ref, bias_ref, lev_ref, sink_ref,
                  (lo128, ones_bd, qgain, kgain, ones_pad, c0, c1, og, first_idx))

    @pl.when(lax.rem(g, 2) == 0)
    def _():
        step(pa_ref, pb_ref)

    @pl.when(lax.rem(g, 2) == 1)
    def _():
        step(pb_ref, pa_ref)


def _mix_step(pc_ref, pp_ref, u_ref, wm_ref, wt_ref, rec_ref, att_ref, st_ref, cv_ref, kvp_ref,
              kpad_ref, vpad_ref, bias_ref, lev_ref, sink_ref, consts):
    (lo128, ones_bd, qgain, kgain, ones_pad, c0, c1, og, first_idx) = consts
    blk = ATTN_BLOCK

    tail = jnp.dot(u_ref[...], wt_ref[...], preferred_element_type=F32)
    for t in range(TAIL_SLABS):
        pc_ref[MIX_ITERS * LOOP_SLABS + t] = tail[:, t * LANES:(t + 1) * LANES]

    k_all = jnp.concatenate([kvp_ref[0], pp_ref[K_SLAB]], axis=0)
    v_all = jnp.concatenate([kvp_ref[1], pp_ref[V_SLAB]], axis=0)
    kvp_ref[0] = pp_ref[K_SLAB, MIX_ROWS - blk:, :]
    kvp_ref[1] = pp_ref[V_SLAB, MIX_ROWS - blk:, :]
    kn = _pair_rms(k_all, ones_bd) * kgain
    kn_sw = pltpu.roll(kn, ATTN_HD, 1)
    v_sw = pltpu.roll(v_all, ATTN_HD, 1)
    zero = jnp.zeros_like(kn)
    kpad_ref[0, 0] = jnp.where(lo128, kn, zero).astype(BF16)
    kpad_ref[0, 1] = jnp.where(lo128, zero, kn_sw).astype(BF16)
    kpad_ref[1, 0] = jnp.where(lo128, kn_sw, zero).astype(BF16)
    kpad_ref[1, 1] = jnp.where(lo128, zero, kn).astype(BF16)
    vpad_ref[0, 0] = jnp.where(lo128, v_all, zero).astype(BF16)
    vpad_ref[0, 1] = jnp.where(lo128, zero, v_sw).astype(BF16)
    vpad_ref[1, 0] = jnp.where(lo128, v_sw, zero).astype(BF16)
    vpad_ref[1, 1] = jnp.where(lo128, zero, v_all).astype(BF16)

    def body(i, carry):
        def load(part, h, rows):
            return pp_ref[part * HGRN_HEADS + h, rows, :]

        def store(h, rows, tile):
            rec_ref[0, rows, h * HGRN_DV:(h + 1) * HGRN_DV] = tile.astype(rec_ref.dtype)

        r0s = [pl.multiple_of((i * HGRN_UNROLL + cc) * HGRN_CHUNK, HGRN_CHUNK)
               for cc in range(HGRN_UNROLL)]

        kvh = lax.div(i, ATTN_GROUP // 2)

        def store_att(n, tile):
            att_ref[0, i, n * blk:(n + 1) * blk, :] = tile.astype(att_ref.dtype)

        swa = _swa_slab(lambda: pp_ref[Q_SLAB + i],
                        lambda: kpad_ref[kvh, 0], lambda: kpad_ref[kvh, 1],
                        lambda: vpad_ref[kvh, 0], lambda: vpad_ref[kvh, 1], ones_pad,
                        lambda: bias_ref[first_idx], lambda: bias_ref[0],
                        sink_ref[2 * i] * LOG2E, sink_ref[2 * i + 1] * LOG2E,
                        qgain, ones_bd, lo128, store_att)

        def proj(lo, hi):
            def run():
                res = jnp.dot(u_ref[...], wm_ref[i, :, lo * LANES:hi * LANES],
                              preferred_element_type=F32)
                for t in range(lo, hi):
                    pc_ref[i * LOOP_SLABS + t] = res[:, (t - lo) * LANES:(t - lo + 1) * LANES]
            return run

        proj(0, LOOP_SLABS)()
        queue = list(swa)

        def fill():
            for _ in range(FILL_PER_STAGE):
                if queue:
                    queue.pop(0)()

        _hgrn_chunks(load, store, r0s, c0, c1, og, lev_ref[0], lev_ref[1], st_ref, cv_ref, fill)
        while queue:
            queue.pop(0)()
        return carry

    lax.fori_loop(0, MIX_ITERS, body, 0, unroll=2)


def _mix(x2, seq_len, gain, w_main, w_tail, lb_logits, out_gain, q_gain, k_gain, sinks):
    t = x2.shape[0]
    nblk = t // MIX_ROWS
    assert seq_len // MIX_ROWS == SEQ_BLOCKS
    batch = t // seq_len

    def in_rows(gi):
        return (jnp.minimum(gi, nblk - 1), 0)

    def out_block(gi):
        blk_i = jnp.maximum(gi - 1, 0)
        return blk_i // SEQ_BLOCKS, lax.rem(blk_i, SEQ_BLOCKS)

    full = lambda shape: pl.BlockSpec(shape, lambda gi: (0,) * len(shape))
    return pl.pallas_call(
        _mix_kernel,
        grid=(nblk + 1,),
        in_specs=[
            pl.BlockSpec((MIX_ROWS, D_MODEL), in_rows),
            full((1, D_MODEL)),
            full((MIX_ITERS, D_MODEL, LOOP_SLABS * LANES)),
            full((D_MODEL, TAIL_SLABS * LANES)),
            full((2, HGRN_WIDTH)),
            full((1, HGRN_DV)),
            full((1, ATTN_HD)),
            full((1, ATTN_HD)),
            pl.BlockSpec(memory_space=pltpu.SMEM),
        ],
        out_specs=[
            pl.BlockSpec((1, MIX_ROWS, HGRN_WIDTH),
                         lambda gi: (*out_block(gi), 0)),
            pl.BlockSpec((1, MIX_ITERS, MIX_ROWS, LANES),
                         lambda gi: (out_block(gi)[0], 0, out_block(gi)[1], 0)),
        ],
        out_shape=[
            jax.ShapeDtypeStruct((batch, seq_len, HGRN_WIDTH), BF16),
            jax.ShapeDtypeStruct((batch, MIX_ITERS, seq_len, LANES), BF16),
        ],
        scratch_shapes=[
            pltpu.VMEM((MIX_ROWS, D_MODEL), BF16),
            pltpu.VMEM((N_SLABS, MIX_ROWS, LANES), F32),
            pltpu.VMEM((N_SLABS, MIX_ROWS, LANES), F32),
            pltpu.VMEM((HGRN_HEADS, HGRN_DV, HGRN_DK), F32),
            pltpu.VMEM((HGRN_HEADS * HGRN_UNROLL, 3, HGRN_CHUNK, HGRN_DK), F32),
            pltpu.VMEM((2, ATTN_BLOCK, LANES), F32),
            pltpu.VMEM((ATTN_KV_HEADS, 2, ATTN_BLOCK + MIX_ROWS, LANES), BF16),
            pltpu.VMEM((ATTN_KV_HEADS, 2, ATTN_BLOCK + MIX_ROWS, LANES), BF16),
            pltpu.VMEM((2, ATTN_BLOCK, 4 * ATTN_BLOCK), F32),
            pltpu.VMEM((2, LEVEL_ROWS, LEVEL_ROWS), jnp.int32),
        ],
        compiler_params=pltpu.CompilerParams(
            dimension_semantics=("arbitrary",), vmem_limit_bytes=VMEM_LIMIT_BYTES),
        name="mixer",
    )(x2, gain, w_main, w_tail, lb_logits, out_gain, q_gain, k_gain, sinks)


def _ffn_kernel(x_ref, rec_ref, att_ref, wo_ref, gain_ref, wg_ref, wu_ref, wd_ref, o_ref):
    mixed = jnp.concatenate([rec_ref[0]] + [att_ref[0, c] for c in range(MIX_ITERS)],
                            axis=1)
    h = x_ref[...] + jnp.dot(mixed, wo_ref[...], preferred_element_type=F32)
    ms = jnp.mean(h * h, axis=-1, keepdims=True)
    u = (h * lax.rsqrt(ms + EPS) * gain_ref[...]).astype(BF16)
    gate = jnp.dot(u, wg_ref[...], preferred_element_type=F32)
    up = jnp.dot(u, wu_ref[...], preferred_element_type=F32)
    gs = 0.5 * gate
    act = ((gs * jnp.tanh(gs) + gs) * up).astype(BF16)
    o_ref[...] = h + jnp.dot(act, wd_ref[...], preferred_element_type=F32)


def _ffn(x2, rec, att, wo, gain, wg, wu, wd):
    t = x2.shape[0]
    seq_blocks = rec.shape[1] // FFN_ROWS

    def resident(shape):
        return pl.BlockSpec(shape, lambda i: (0, 0), pipeline_mode=pl.Buffered(1))

    return pl.pallas_call(
        _ffn_kernel,
        grid=(t // FFN_ROWS,),
        in_specs=[
            pl.BlockSpec((FFN_ROWS, D_MODEL), lambda i: (i, 0)),
            pl.BlockSpec((1, FFN_ROWS, HGRN_WIDTH),
                         lambda i: (i // seq_blocks, lax.rem(i, seq_blocks), 0)),
            pl.BlockSpec((1, MIX_ITERS, FFN_ROWS, LANES),
                         lambda i: (i // seq_blocks, 0, lax.rem(i, seq_blocks), 0)),
            resident((D_MODEL, D_MODEL)),
            resident((1, D_MODEL)),
            resident((D_MODEL, D_FF)),
            resident((D_MODEL, D_FF)),
            resident((D_FF, D_MODEL)),
        ],
        out_specs=pl.BlockSpec((FFN_ROWS, D_MODEL), lambda i: (i, 0)),
        out_shape=jax.ShapeDtypeStruct((t, D_MODEL), F32),
        compiler_params=pltpu.CompilerParams(
            dimension_semantics=("arbitrary",), vmem_limit_bytes=VMEM_LIMIT_BYTES),
        name="outproj_ffn",
    )(x2, rec, att, wo, gain, wg, wu, wd)


def kernel(x, norm1_gain, w_in, hgrn_lb_logits, hgrn_out_gain, q_norm_gain, k_norm_gain,
           attn_sinks, w_out, norm2_gain, w_ffn_gate, w_ffn_up, w_ffn_down):
    b, s, d = x.shape
    assert (d, w_in.shape[0]) == (D_MODEL, 1), "single-layer kernel"
    assert s == SEQ_BLOCKS * MIX_ROWS and MIX_ROWS == FFN_ROWS
    t = b * s
    x2 = x.reshape(t, d)

    w_bf = w_in[0].astype(BF16)
    n_main = MIX_ITERS * LOOP_SLABS * LANES
    w_main = w_bf[:, :n_main].reshape(D_MODEL, MIX_ITERS, LOOP_SLABS * LANES).transpose(1, 0, 2)
    w_tail = w_bf[:, n_main:]
    rec, att = _mix(x2, s, norm1_gain[0][None, :], w_main, w_tail, hgrn_lb_logits,
                    hgrn_out_gain[0][None, :], q_norm_gain[0][None, :], k_norm_gain[0][None, :],
                    attn_sinks[0])
    out = _ffn(x2, rec, att, w_out[0].astype(BF16), norm2_gain[0][None, :],
               w_ffn_gate[0].astype(BF16), w_ffn_up[0].astype(BF16),
               w_ffn_down[0].astype(BF16))
    return out.reshape(b, s, d)
```

```python
import jax
import jax.numpy as jnp
from jax import lax
from jax.experimental import pallas as pl
from jax.experimental.pallas import tpu as pltpu

D_MODEL = 1024
HGRN_HEADS = 4
HGRN_DK = 128
HGRN_DV = 128
HGRN_WIDTH = HGRN_HEADS * HGRN_DK
ATTN_HEADS = 8
ATTN_KV_HEADS = 2
ATTN_GROUP = ATTN_HEADS // ATTN_KV_HEADS
ATTN_HD = 64
ATTN_WIDTH = ATTN_HEADS * ATTN_HD
KV_WIDTH = ATTN_KV_HEADS * ATTN_HD
WINDOW = 128
PROJ_WIDTH = 4 * HGRN_WIDTH + ATTN_WIDTH + 2 * KV_WIDTH
D_FF = 2816
EPS = 1e-6
NEG_INF = -1e30
LOG2E = 1.4426950408889634

F32 = jnp.float32
BF16 = jnp.bfloat16

LANES = 128
VMEM_LIMIT_BYTES = 56 * 1024 * 1024

HGRN_CHUNK = 64
HGRN_LEVELS = 6
HGRN_UNROLL = 2
ATTN_BLOCK = 128
MIX_ROWS = 512
MIX_ITERS = 4
SEQ_BLOCKS = 4
FILL_PER_STAGE = 0
FFN_ROWS = 512

N_SLABS = PROJ_WIDTH // LANES
LOOP_SLABS = 4
TAIL_SLABS = N_SLABS - MIX_ITERS * LOOP_SLABS
Q_SLAB = 4 * HGRN_HEADS
K_SLAB = Q_SLAB + ATTN_WIDTH // LANES
V_SLAB = K_SLAB + 1
assert MIX_ITERS * LOOP_SLABS == Q_SLAB and V_SLAB == N_SLABS - 1
assert MIX_ROWS == MIX_ITERS * HGRN_UNROLL * HGRN_CHUNK == MIX_ITERS * ATTN_BLOCK
assert ATTN_WIDTH // LANES == MIX_ITERS

_NT = (((1,), (1,)), ((), ()))
_TN = (((0,), (0,)), ((), ()))


_LEVEL_Q = ((1, 3, 5, 7), (2, 3, 6, 7), (4, 5, 6, 7))
_LEVEL_K = ((0, 2, 4, 6), (0, 1, 4, 5), (0, 1, 2, 3))
LEVEL_ROWS = 8 * sum(len(q) for q in _LEVEL_Q)


def _level_mask():
    r = lax.broadcasted_iota(jnp.int32, (LEVEL_ROWS, LEVEL_ROWS), 0)
    c = lax.broadcasted_iota(jnp.int32, (LEVEL_ROWS, LEVEL_ROWS), 1)
    lev_r, lev_c = r >> 5, c >> 5
    j_r, j_c = (r >> 3) & 3, (c >> 3) & 3
    same_block = (((lev_r == 0) & (j_r == j_c))
                  | ((lev_r == 1) & ((j_r >> 1) == (j_c >> 1)))
                  | (lev_r == 2))
    return ((lev_r == lev_c) & same_block).astype(jnp.int32)


def _mul1(a, b):
    if a is None:
        return b
    if b is None:
        return a
    return a * b


def _pair_rms(x, ones_bd):
    ss = jnp.dot((x * x).astype(BF16), ones_bd, preferred_element_type=F32)
    return x * lax.rsqrt(ss * (1.0 / ATTN_HD) + EPS)


def _hgrn_chunks(load, store, r0s, c0, c1, og, mask_nat, st_ref, cv_ref, fill):
    nh = HGRN_HEADS
    c_rows = HGRN_CHUNK
    nv = c_rows // 8
    assert nv == 8 and HGRN_LEVELS == 6

    def bc(t, r):
        return jnp.broadcast_to(t[r:r + 1, :], (8, HGRN_DK))

    def cat16(pieces):
        return jnp.concatenate(pieces, axis=0).astype(BF16)

    q_order = _LEVEL_Q[0] + _LEVEL_Q[1] + _LEVEL_Q[2]
    k_order = _LEVEL_K[0] + _LEVEL_K[1] + _LEVEL_K[2]

    def stacked_attend(q_pieces, k_pieces, v_pieces, mask):
        s = [lax.dot_general(cat16(q_pieces[u]), cat16(k_pieces[u]), _NT,
                             preferred_element_type=F32) for u in units]
        fill()
        s = [jnp.where(mask, s[u], 0.0).astype(BF16) for u in units]
        o = [jnp.dot(s[u], cat16([v_pieces[u][j] for j in k_order]),
                     preferred_element_type=F32) for u in units]
        fill()
        return [[o[u][8 * n:8 * n + 8, :] for n in range(len(q_order))] for u in units]

    def scatter_add(acc, pieces):
        for n, j in enumerate(q_order):
            acc[j] = pieces[n] if acc[j] is None else acc[j] + pieces[n]
        return acc

    m_nat = mask_nat != 0

    units = range(nh * len(r0s))
    hd = [u % nh for u in units]
    r0 = [r0s[u // nh] for u in units]
    rows = [pl.ds(r0[u], c_rows) for u in units]
    lanes = [slice(hd[u] * HGRN_DK, (hd[u] + 1) * HGRN_DK) for u in units]
    vrange = range(nv)

    fp, kp, qp, vtr = [], [], [], []
    for u in units:
        c0h, c1h = c0[:, lanes[u]], c1[:, lanes[u]]
        fh, kh, qh, vh = [], [], [], []
        for b in vrange:
            srows = pl.ds(r0[u] + b, 8, stride=8)
            ct = c1h * jnp.tanh(0.5 * load(1, hd[u], srows))
            fh.append(c0h + ct)
            kh.append(c1h - ct)
            xq = load(0, hd[u], srows)
            xs = xq * (0.5 * HGRN_DK ** -0.5)
            qh.append(xs * jnp.tanh(0.5 * xq) + xs)
            vh.append(load(2, hd[u], srows))
        fp.append(fh)
        kp.append(kh)
        qp.append(qh)
        vtr.append(vh)

    fill()
    o_tr = [[jnp.sum(qp[u][b] * kp[u][b], axis=-1, keepdims=True) * vtr[u][b] for b in vrange]
            for u in units]

    p2 = [[fp[u][b] * fp[u][b - 1] if b & 1 else fp[u][b] for b in vrange] for u in units]
    x2 = [[None if b & 1 else fp[u][b + 1] for b in vrange] for u in units]
    p4 = [[p2[u][b] * p2[u][(b & ~3) + 1] if b & 2 else p2[u][b] for b in vrange] for u in units]
    x4 = [[x2[u][b] if b & 2 else _mul1(x2[u][b], p2[u][(b & ~3) + 3]) for b in vrange]
          for u in units]
    p8 = [[p4[u][b] * p4[u][3] if b & 4 else p4[u][b] for b in vrange] for u in units]
    x8 = [[x4[u][b] if b & 4 else _mul1(x4[u][b], p4[u][7]) for b in vrange] for u in units]
    tot8 = [p8[u][7] for u in units]

    for u in units:
        for b in vrange:
            srows = pl.ds(b, 8, stride=8)
            cv_ref[u, 0, srows, :] = qp[u][b] * p8[u][b]
            cv_ref[u, 1, srows, :] = _mul1(kp[u][b], x8[u][b])

    fill()
    q_pieces = [([qp[u][b] * fp[u][b] for b in _LEVEL_Q[0]]
                 + [qp[u][b] * p2[u][b] for b in _LEVEL_Q[1]]
                 + [qp[u][b] * p4[u][b] for b in _LEVEL_Q[2]]) for u in units]
    k_pieces = [([kp[u][b] for b in _LEVEL_K[0]]
                 + [_mul1(kp[u][b], x2[u][b]) for b in _LEVEL_K[1]]
                 + [_mul1(kp[u][b], x4[u][b]) for b in _LEVEL_K[2]]) for u in units]
    for lvl in range(3):
        group = 1 << lvl
        for jq, b_q in enumerate(_LEVEL_Q[lvl]):
            for jk, b_k in enumerate(_LEVEL_K[lvl]):
                if jq // group != jk // group:
                    continue
                nq, nk = 4 * lvl + jq, 4 * lvl + jk
                for u in units:
                    d = jnp.sum(q_pieces[u][nq] * k_pieces[u][nk], axis=-1, keepdims=True)
                    o_tr[u][b_q] = o_tr[u][b_q] + d * vtr[u][b_k]

    an = [[cv_ref[u, 0, 8 * a:8 * a + 8, :] for a in vrange] for u in units]
    cn = [[cv_ref[u, 1, 8 * a:8 * a + 8, :] for a in vrange] for u in units]
    vnat = [[load(2, hd[u], pl.ds(r0[u] + 8 * a, 8)) for a in vrange] for u in units]

    q4 = [[an[u][a] for a in _LEVEL_Q[0]] for u in units]
    k4 = [[cn[u][a] for a in _LEVEL_K[0]] for u in units]
    an = [[an[u][a] * bc(tot8[u], a - 1) if a & 1 else an[u][a] for a in vrange] for u in units]
    cn = [[cn[u][a] if a & 1 else cn[u][a] * bc(tot8[u], a + 1) for a in vrange] for u in units]
    t16 = [tot8[u] * pltpu.roll(tot8[u], 1, 0) for u in units]
    q5 = [[an[u][a] for a in _LEVEL_Q[1]] for u in units]
    k5 = [[cn[u][a] for a in _LEVEL_K[1]] for u in units]
    an = [[an[u][a] * bc(t16[u], (a & ~3) + 1) if a & 2 else an[u][a] for a in vrange]
          for u in units]
    cn = [[cn[u][a] if a & 2 else cn[u][a] * bc(t16[u], (a & ~3) + 3) for a in vrange]
          for u in units]
    t32 = [t16[u] * pltpu.roll(t16[u], 2, 0) for u in units]
    q6 = [[an[u][a] for a in _LEVEL_Q[2]] for u in units]
    k6 = [[cn[u][a] for a in _LEVEL_K[2]] for u in units]
    an = [[an[u][a] * bc(t32[u], 3) if a & 4 else an[u][a] for a in vrange] for u in units]
    cn = [[cn[u][a] if a & 4 else cn[u][a] * bc(t32[u], 7) for a in vrange] for u in units]
    att_nat = stacked_attend([q4[u] + q5[u] + q6[u] for u in units],
                             [k4[u] + k5[u] + k6[u] for u in units], vnat, m_nat)
    o_nat = []
    for u in units:
        pieces = scatter_add([None] * nv, att_nat[u])
        pieces[0] = jnp.zeros((8, HGRN_DV), F32)
        o_nat.append(jnp.concatenate(pieces, axis=0))

    for u in units:
        for b in vrange:
            cv_ref[u, 2, pl.ds(b, 8, stride=8), :] = o_tr[u][b]

    fill()
    upd = [lax.dot_general(cat16(vnat[u]), cat16(cn[u]),
                           _TN, preferred_element_type=F32) for u in units]
    for h in range(nh):
        st = st_ref[h]
        for u in range(h, len(units), nh):
            qs = cat16(an[u])
            o_nat[u] = o_nat[u] + lax.dot_general(qs, st.astype(BF16), _NT,
                                                  preferred_element_type=F32)
            dec = t32[u][3:4, :] * t32[u][7:8, :]
            st = st * dec + upd[u]
        st_ref[h] = st

    fill()
    for u in units:
        o_h = o_nat[u] + cv_ref[u, 2]
        ms = jnp.mean(o_h * o_h, axis=-1, keepdims=True)
        xgs = 0.5 * load(3, hd[u], rows[u])
        gate = xgs * jnp.tanh(xgs) + xgs
        store(hd[u], rows[u], o_h * lax.rsqrt(ms + EPS) * og * gate)


def _swa_slab(q_slab, k_lo, k_hi, v_lo, v_hi, ones_pad, bias_first, bias_any,
              sink_a, sink_b, qgain, ones_bd, lo128, store):
    blk = ATTN_BLOCK
    nblk = MIX_ROWS // blk
    state = {}

    def prep():
        state["qs"] = (_pair_rms(q_slab(), ones_bd) * qgain).astype(BF16)
        state["sink_slab"] = jnp.where(lo128, sink_a, sink_b)
        state["sink_keys"] = jnp.concatenate([jnp.full((1, 2 * blk), sink_a, F32),
                                              jnp.full((1, 2 * blk), sink_b, F32)], axis=1)

    def scores(n):
        keys = slice(n * blk, (n + 2) * blk)
        k_cat = jnp.concatenate([k_lo()[keys], k_hi()[keys]], axis=0)
        s = lax.dot_general(state["qs"][n * blk:(n + 1) * blk], k_cat, _NT,
                            preferred_element_type=F32)
        s = s + (bias_first() if n == 0 else bias_any())
        m_a = jnp.max(s[:, :2 * blk], axis=-1, keepdims=True)
        m_b = jnp.max(s[:, 2 * blk:], axis=-1, keepdims=True)
        m = jnp.maximum(jnp.concatenate([jnp.broadcast_to(m_a, (blk, 2 * blk)),
                                         jnp.broadcast_to(m_b, (blk, 2 * blk))], axis=1),
                        state["sink_keys"])
        state["p", n] = jnp.exp2(s - m).astype(BF16)
        state["m", n] = jnp.where(lo128, m[:, :LANES], m[:, 2 * blk:2 * blk + LANES])

    def attend(n):
        keys = slice(n * blk, (n + 2) * blk)
        v_cat = jnp.concatenate([v_lo()[keys], v_hi()[keys]], axis=0)
        rhs = jnp.concatenate([v_cat, ones_pad], axis=1)
        nd = jnp.dot(state.pop(("p", n)), rhs, preferred_element_type=F32)
        den = nd[:, LANES:] + jnp.exp2(state["sink_slab"] - state.pop(("m", n)))
        store(n, nd[:, :LANES] / den)

    stages = [prep]
    for n in range(nblk):
        stages += [lambda n=n: scores(n), lambda n=n: attend(n)]
    return stages


def _mix_kernel(x_ref, gain_ref, wm_ref, wt_ref, lbl_ref, og_ref, qg_ref, kg_ref, sink_ref,
                rec_ref, att_ref,
                u_ref, pa_ref, pb_ref, st_ref, cv_ref, kvp_ref, kpad_ref, vpad_ref, bias_ref,
                lev_ref):
    g = pl.program_id(0)
    blk = ATTN_BLOCK

    lo128 = lax.broadcasted_iota(jnp.int32, (1, LANES), 1) < ATTN_HD

    @pl.when(g == 0)
    def _():
        pb_ref[...] = jnp.zeros_like(pb_ref)
        kvp_ref[...] = jnp.zeros_like(kvp_ref)
        st_ref[...] = jnp.zeros_like(st_ref)
        lev_ref[...] = _level_mask()
        qi = lax.broadcasted_iota(jnp.int32, (blk, 2 * blk), 0)
        kj = lax.broadcasted_iota(jnp.int32, (blk, 2 * blk), 1)
        rel = qi + blk - kj
        in_window = (rel >= 0) & (rel < WINDOW)
        b_any = jnp.where(in_window, 0.0, NEG_INF)
        b_first = jnp.where(in_window & (kj >= blk), 0.0, NEG_INF)
        bias_ref[0] = jnp.concatenate([b_any, b_any], axis=1)
        bias_ref[1] = jnp.concatenate([b_first, b_first], axis=1)

    first_blk = lax.rem(g + SEQ_BLOCKS - 1, SEQ_BLOCKS) == 0

    @pl.when(first_blk)
    def _():
        st_ref[...] = jnp.zeros_like(st_ref)

    x = x_ref[...]
    ms = jnp.mean(x * x, axis=-1, keepdims=True)
    u_ref[...] = (x * lax.rsqrt(ms + EPS) * gain_ref[...]).astype(BF16)

    hr = lax.broadcasted_iota(jnp.int32, (LANES, LANES), 0) < ATTN_HD
    hc = lax.broadcasted_iota(jnp.int32, (LANES, LANES), 1) < ATTN_HD
    ones_bd = jnp.where(hr == hc, 1.0, 0.0).astype(BF16)
    pair = lambda r: jnp.concatenate([r, r], axis=1)
    qgain = pair(qg_ref[...]) * (ATTN_HD ** -0.5 * LOG2E)
    kgain = pair(kg_ref[...])
    ones_lo = jnp.broadcast_to(jnp.where(lo128, 1.0, 0.0), (2 * blk, LANES))
    ones_pad = jnp.concatenate([ones_lo, 1.0 - ones_lo], axis=0).astype(BF16)

    l0 = lbl_ref[0:1, :]
    l1 = lbl_ref[1:2, :]
    lmax = jnp.maximum(l0, l1)
    e0 = jnp.exp(l0 - lmax)
    e1 = jnp.exp(l1 - lmax)
    lb = e0 / (e0 + e1)
    c0 = 0.5 * (1.0 + lb)
    c1 = 0.5 * (1.0 - lb)
    og = og_ref[...]
    first_idx = first_blk.astype(jnp.int32)

    def step(pc_ref, pp_ref):
        _mix_step(pc_ref, pp_ref, u_ref, wm_ref, wt_ref, rec_ref, att_ref, st_ref, cv_ref,
                  kvp_ref, kpad_ref, vpad_ref, bias_ref, lev_ref, sink_ref,
                  (lo128, ones_bd, qgain, kgain, ones_pad, c0, c1, og, first_idx))

    @pl.when(lax.rem(g, 2) == 0)
    def _():
        step(pa_ref, pb_ref)

    @pl.when(lax.rem(g, 2) == 1)
    def _():
        step(pb_ref, pa_ref)


def _mix_step(pc_ref, pp_ref, u_ref, wm_ref, wt_ref, rec_ref, att_ref, st_ref, cv_ref, kvp_ref,
              kpad_ref, vpad_ref, bias_ref, lev_ref, sink_ref, consts):
    (lo128, ones_bd, qgain, kgain, ones_pad, c0, c1, og, first_idx) = consts
    blk = ATTN_BLOCK

    tail = jnp.dot(u_ref[...], wt_ref[...], preferred_element_type=F32)
    for t in range(TAIL_SLABS):
        pc_ref[MIX_ITERS * LOOP_SLABS + t] = tail[:, t * LANES:(t + 1) * LANES]

    k_all = jnp.concatenate([kvp_ref[0], pp_ref[K_SLAB]], axis=0)
    v_all = jnp.concatenate([kvp_ref[1], pp_ref[V_SLAB]], axis=0)
    kvp_ref[0] = pp_ref[K_SLAB, MIX_ROWS - blk:, :]
    kvp_ref[1] = pp_ref[V_SLAB, MIX_ROWS - blk:, :]
    kn = _pair_rms(k_all, ones_bd) * kgain
    kn_sw = pltpu.roll(kn, ATTN_HD, 1)
    v_sw = pltpu.roll(v_all, ATTN_HD, 1)
    zero = jnp.zeros_like(kn)
    kpad_ref[0, 0] = jnp.where(lo128, kn, zero).astype(BF16)
    kpad_ref[0, 1] = jnp.where(lo128, zero, kn_sw).astype(BF16)
    kpad_ref[1, 0] = jnp.where(lo128, kn_sw, zero).astype(BF16)
    kpad_ref[1, 1] = jnp.where(lo128, zero, kn).astype(BF16)
    vpad_ref[0, 0] = jnp.where(lo128, v_all, zero).astype(BF16)
    vpad_ref[0, 1] = jnp.where(lo128, zero, v_sw).astype(BF16)
    vpad_ref[1, 0] = jnp.where(lo128, v_sw, zero).astype(BF16)
    vpad_ref[1, 1] = jnp.where(lo128, zero, v_all).astype(BF16)

    def body(i, carry):
        def load(part, h, rows):
            return pp_ref[part * HGRN_HEADS + h, rows, :]

        def store(h, rows, tile):
            rec_ref[0, rows, h * HGRN_DV:(h + 1) * HGRN_DV] = tile.astype(rec_ref.dtype)

        r0s = [pl.multiple_of((i * HGRN_UNROLL + cc) * HGRN_CHUNK, HGRN_CHUNK)
               for cc in range(HGRN_UNROLL)]

        kvh = lax.div(i, ATTN_GROUP // 2)

        def store_att(n, tile):
            att_ref[0, i, n * blk:(n + 1) * blk, :] = tile.astype(att_ref.dtype)

        swa = _swa_slab(lambda: pp_ref[Q_SLAB + i],
                        lambda: kpad_ref[kvh, 0], lambda: kpad_ref[kvh, 1],
                        lambda: vpad_ref[kvh, 0], lambda: vpad_ref[kvh, 1], ones_pad,
                        lambda: bias_ref[first_idx], lambda: bias_ref[0],
                        sink_ref[2 * i] * LOG2E, sink_ref[2 * i + 1] * LOG2E,
                        qgain, ones_bd, lo128, store_att)

        def proj(lo, hi):
            def run():
                res = jnp.dot(u_ref[...], wm_ref[i, :, lo * LANES:hi * LANES],
                              preferred_element_type=F32)
                for t in range(lo, hi):
                    pc_ref[i * LOOP_SLABS + t] = res[:, (t - lo) * LANES:(t - lo + 1) * LANES]
            return run

        proj(0, LOOP_SLABS)()
        queue = list(swa)

        def fill():
            for _ in range(FILL_PER_STAGE):
                if queue:
                    queue.pop(0)()

        _hgrn_chunks(load, store, r0s, c0, c1, og, lev_ref[...], st_ref, cv_ref, fill)
        while queue:
            queue.pop(0)()
        return carry

    lax.fori_loop(0, MIX_ITERS, body, 0, unroll=2)


def _mix(x2, seq_len, gain, w_main, w_tail, lb_logits, out_gain, q_gain, k_gain, sinks):
    t = x2.shape[0]
    nblk = t // MIX_ROWS
    assert seq_len // MIX_ROWS == SEQ_BLOCKS
    batch = t // seq_len

    def in_rows(gi):
        return (jnp.minimum(gi, nblk - 1), 0)

    def out_block(gi):
        blk_i = jnp.maximum(gi - 1, 0)
        return blk_i // SEQ_BLOCKS, lax.rem(blk_i, SEQ_BLOCKS)

    full = lambda shape: pl.BlockSpec(shape, lambda gi: (0,) * len(shape))
    return pl.pallas_call(
        _mix_kernel,
        grid=(nblk + 1,),
        in_specs=[
            pl.BlockSpec((MIX_ROWS, D_MODEL), in_rows),
            full((1, D_MODEL)),
            full((MIX_ITERS, D_MODEL, LOOP_SLABS * LANES)),
            full((D_MODEL, TAIL_SLABS * LANES)),
            full((2, HGRN_WIDTH)),
            full((1, HGRN_DV)),
            full((1, ATTN_HD)),
            full((1, ATTN_HD)),
            pl.BlockSpec(memory_space=pltpu.SMEM),
        ],
        out_specs=[
            pl.BlockSpec((1, MIX_ROWS, HGRN_WIDTH),
                         lambda gi: (*out_block(gi), 0)),
            pl.BlockSpec((1, MIX_ITERS, MIX_ROWS, LANES),
                         lambda gi: (out_block(gi)[0], 0, out_block(gi)[1], 0)),
        ],
        out_shape=[
            jax.ShapeDtypeStruct((batch, seq_len, HGRN_WIDTH), BF16),
            jax.ShapeDtypeStruct((batch, MIX_ITERS, seq_len, LANES), BF16),
        ],
        scratch_shapes=[
            pltpu.VMEM((MIX_ROWS, D_MODEL), BF16),
            pltpu.VMEM((N_SLABS, MIX_ROWS, LANES), F32),
            pltpu.VMEM((N_SLABS, MIX_ROWS, LANES), F32),
            pltpu.VMEM((HGRN_HEADS, HGRN_DV, HGRN_DK), F32),
            pltpu.VMEM((HGRN_HEADS * HGRN_UNROLL, 3, HGRN_CHUNK, HGRN_DK), F32),
            pltpu.VMEM((2, ATTN_BLOCK, LANES), F32),
            pltpu.VMEM((ATTN_KV_HEADS, 2, ATTN_BLOCK + MIX_ROWS, LANES), BF16),
            pltpu.VMEM((ATTN_KV_HEADS, 2, ATTN_BLOCK + MIX_ROWS, LANES), BF16),
            pltpu.VMEM((2, ATTN_BLOCK, 4 * ATTN_BLOCK), F32),
            pltpu.VMEM((LEVEL_ROWS, LEVEL_ROWS), jnp.int32),
        ],
        compiler_params=pltpu.CompilerParams(
            dimension_semantics=("arbitrary",), vmem_limit_bytes=VMEM_LIMIT_BYTES),
        name="mixer",
    )(x2, gain, w_main, w_tail, lb_logits, out_gain, q_gain, k_gain, sinks)


def _ffn_kernel(x_ref, rec_ref, att_ref, wo_ref, gain_ref, wg_ref, wu_ref, wd_ref, o_ref):
    mixed = jnp.concatenate([rec_ref[0]] + [att_ref[0, c] for c in range(MIX_ITERS)],
                            axis=1)
    h = x_ref[...] + jnp.dot(mixed, wo_ref[...], preferred_element_type=F32)
    ms = jnp.mean(h * h, axis=-1, keepdims=True)
    u = (h * lax.rsqrt(ms + EPS) * gain_ref[...]).astype(BF16)
    gate = jnp.dot(u, wg_ref[...], preferred_element_type=F32)
    up = jnp.dot(u, wu_ref[...], preferred_element_type=F32)
    gs = 0.5 * gate
    act = ((gs * jnp.tanh(gs) + gs) * up).astype(BF16)
    o_ref[...] = h + jnp.dot(act, wd_ref[...], preferred_element_type=F32)


def _ffn(x2, rec, att, wo, gain, wg, wu, wd):
    t = x2.shape[0]
    seq_blocks = rec.shape[1] // FFN_ROWS

    def resident(shape):
        return pl.BlockSpec(shape, lambda i: (0, 0), pipeline_mode=pl.Buffered(1))

    return pl.pallas_call(
        _ffn_kernel,
        grid=(t // FFN_ROWS,),
        in_specs=[
            pl.BlockSpec((FFN_ROWS, D_MODEL), lambda i: (i, 0)),
            pl.BlockSpec((1, FFN_ROWS, HGRN_WIDTH),
                         lambda i: (i // seq_blocks, lax.rem(i, seq_blocks), 0)),
            pl.BlockSpec((1, MIX_ITERS, FFN_ROWS, LANES),
                         lambda i: (i // seq_blocks, 0, lax.rem(i, seq_blocks), 0)),
            resident((D_MODEL, D_MODEL)),
            resident((1, D_MODEL)),
            resident((D_MODEL, D_FF)),
            resident((D_MODEL, D_FF)),
            resident((D_FF, D_MODEL)),
        ],
        out_specs=pl.BlockSpec((FFN_ROWS, D_MODEL), lambda i: (i, 0)),
        out_shape=jax.ShapeDtypeStruct((t, D_MODEL), F32),
        compiler_params=pltpu.CompilerParams(
            dimension_semantics=("arbitrary",), vmem_limit_bytes=VMEM_LIMIT_BYTES),
        name="outproj_ffn",
    )(x2, rec, att, wo, gain, wg, wu, wd)


def kernel(x, norm1_gain, w_in, hgrn_lb_logits, hgrn_out_gain, q_norm_gain, k_norm_gain,
           attn_sinks, w_out, norm2_gain, w_ffn_gate, w_ffn_up, w_ffn_down):
    b, s, d = x.shape
    assert (d, w_in.shape[0]) == (D_MODEL, 1), "single-layer kernel"
    assert s == SEQ_BLOCKS * MIX_ROWS and MIX_ROWS == FFN_ROWS
    t = b * s
    x2 = x.reshape(t, d)

    w_bf = w_in[0].astype(BF16)
    n_main = MIX_ITERS * LOOP_SLABS * LANES
    w_main = w_bf[:, :n_main].reshape(D_MODEL, MIX_ITERS, LOOP_SLABS * LANES).transpose(1, 0, 2)
    w_tail = w_bf[:, n_main:]
    rec, att = _mix(x2, s, norm1_gain[0][None, :], w_main, w_tail, hgrn_lb_logits,
                    hgrn_out_gain[0][None, :], q_norm_gain[0][None, :], k_norm_gain[0][None, :],
                    attn_sinks[0])
    out = _ffn(x2, rec, att, w_out[0].astype(BF16), norm2_gain[0][None, :],
               w_ffn_gate[0].astype(BF16), w_ffn_up[0].astype(BF16),
               w_ffn_down[0].astype(BF16))
    return out.reshape(b, s, d)
```

```python
import jax
import jax.numpy as jnp
from jax import lax
from jax.experimental import pallas as pl
from jax.experimental.pallas import tpu as pltpu

D_MODEL = 1024
HGRN_HEADS = 4
HGRN_DK = 128
HGRN_DV = 128
HGRN_WIDTH = HGRN_HEADS * HGRN_DK
ATTN_HEADS = 8
ATTN_KV_HEADS = 2
ATTN_GROUP = ATTN_HEADS // ATTN_KV_HEADS
ATTN_HD = 64
ATTN_WIDTH = ATTN_HEADS * ATTN_HD
KV_WIDTH = ATTN_KV_HEADS * ATTN_HD
WINDOW = 128
PROJ_WIDTH = 4 * HGRN_WIDTH + ATTN_WIDTH + 2 * KV_WIDTH
D_FF = 2816
EPS = 1e-6
NEG_INF = -1e30
LOG2E = 1.4426950408889634

F32 = jnp.float32
BF16 = jnp.bfloat16

LANES = 128
VMEM_LIMIT_BYTES = 56 * 1024 * 1024

HGRN_CHUNK = 64
HGRN_LEVELS = 6
HGRN_UNROLL = 2
ATTN_BLOCK = 128
MIX_ROWS = 512
MIX_ITERS = 4
MIX_UNROLL = 2
SEQ_BLOCKS = 4
FFN_ROWS = 512

N_SLABS = PROJ_WIDTH // LANES
ATT_SLABS = ATTN_WIDTH // LANES
SLABS_PER_ITER = ATT_SLABS // MIX_ITERS
Q_SLAB = 4 * HGRN_HEADS
K_SLAB = Q_SLAB + ATT_SLABS
V_SLAB = K_SLAB + 1
LOOP_SLABS = Q_SLAB // MIX_ITERS
TAIL_SLABS = N_SLABS - MIX_ITERS * LOOP_SLABS
assert MIX_ITERS * LOOP_SLABS == Q_SLAB and V_SLAB == N_SLABS - 1 and LOOP_SLABS % 2 == 0
assert MIX_ROWS == MIX_ITERS * HGRN_UNROLL * HGRN_CHUNK
assert ATT_SLABS == MIX_ITERS * SLABS_PER_ITER

_NT = (((1,), (1,)), ((), ()))
_TN = (((0,), (0,)), ((), ()))


_LEVEL_Q = ((1, 3, 5, 7), (2, 3, 6, 7), (4, 5, 6, 7))
_LEVEL_K = ((0, 2, 4, 6), (0, 1, 4, 5), (0, 1, 2, 3))
LEVEL_ROWS = 8 * sum(len(q) for q in _LEVEL_Q)


def _level_mask():
    r = lax.broadcasted_iota(jnp.int32, (LEVEL_ROWS, LEVEL_ROWS), 0)
    c = lax.broadcasted_iota(jnp.int32, (LEVEL_ROWS, LEVEL_ROWS), 1)
    lev_r, lev_c = r >> 5, c >> 5
    j_r, j_c = (r >> 3) & 3, (c >> 3) & 3
    same_block = (((lev_r == 0) & (j_r == j_c))
                  | ((lev_r == 1) & ((j_r >> 1) == (j_c >> 1)))
                  | (lev_r == 2))
    return ((lev_r == lev_c) & same_block).astype(jnp.int32)


def _mul1(a, b):
    if a is None:
        return b
    if b is None:
        return a
    return a * b


def _pair_rms(x, ones_bd):
    ss = jnp.dot((x * x).astype(BF16), ones_bd, preferred_element_type=F32)
    return x * lax.rsqrt(ss * (1.0 / ATTN_HD) + EPS)


def _hgrn_chunks(load, store, r0s, c0, c1, og, mask_nat, st_ref, cv_ref, fill):
    nh = HGRN_HEADS
    c_rows = HGRN_CHUNK
    nv = c_rows // 8
    assert nv == 8 and HGRN_LEVELS == 6

    def bc(t, r):
        return jnp.broadcast_to(t[r:r + 1, :], (8, HGRN_DK))

    def cat16(pieces):
        return jnp.concatenate(pieces, axis=0).astype(BF16)

    q_order = _LEVEL_Q[0] + _LEVEL_Q[1] + _LEVEL_Q[2]
    k_order = _LEVEL_K[0] + _LEVEL_K[1] + _LEVEL_K[2]

    def stacked_attend(q_pieces, k_pieces, v_pieces, mask):
        s = [lax.dot_general(cat16(q_pieces[u]), cat16(k_pieces[u]), _NT,
                             preferred_element_type=F32) for u in units]
        fill()
        s = [jnp.where(mask, s[u], 0.0).astype(BF16) for u in units]
        o = [jnp.dot(s[u], cat16([v_pieces[u][j] for j in k_order]),
                     preferred_element_type=F32) for u in units]
        fill()
        return [[o[u][8 * n:8 * n + 8, :] for n in range(len(q_order))] for u in units]

    def scatter_add(acc, pieces):
        for n, j in enumerate(q_order):
            acc[j] = pieces[n] if acc[j] is None else acc[j] + pieces[n]
        return acc

    m_nat = mask_nat != 0

    units = range(nh * len(r0s))
    hd = [u % nh for u in units]
    r0 = [r0s[u // nh] for u in units]
    rows = [pl.ds(r0[u], c_rows) for u in units]
    lanes = [slice(hd[u] * HGRN_DK, (hd[u] + 1) * HGRN_DK) for u in units]
    vrange = range(nv)

    fp, kp, qp, vtr = [], [], [], []
    for u in units:
        c0h, c1h = c0[:, lanes[u]], c1[:, lanes[u]]
        fh, kh, qh, vh = [], [], [], []
        for b in vrange:
            srows = pl.ds(r0[u] + b, 8, stride=8)
            ct = c1h * jnp.tanh(0.5 * load(1, hd[u], srows))
            fh.append(c0h + ct)
            kh.append(c1h - ct)
            xq = load(0, hd[u], srows)
            xs = xq * (0.5 * HGRN_DK ** -0.5)
            qh.append(xs * jnp.tanh(0.5 * xq) + xs)
            vh.append(load(2, hd[u], srows))
        fp.append(fh)
        kp.append(kh)
        qp.append(qh)
        vtr.append(vh)

    fill()
    o_tr = [[jnp.sum(qp[u][b] * kp[u][b], axis=-1, keepdims=True) * vtr[u][b] for b in vrange]
            for u in units]

    p2 = [[fp[u][b] * fp[u][b - 1] if b & 1 else fp[u][b] for b in vrange] for u in units]
    x2 = [[None if b & 1 else fp[u][b + 1] for b in vrange] for u in units]
    p4 = [[p2[u][b] * p2[u][(b & ~3) + 1] if b & 2 else p2[u][b] for b in vrange] for u in units]
    x4 = [[x2[u][b] if b & 2 else _mul1(x2[u][b], p2[u][(b & ~3) + 3]) for b in vrange]
          for u in units]
    p8 = [[p4[u][b] * p4[u][3] if b & 4 else p4[u][b] for b in vrange] for u in units]
    x8 = [[x4[u][b] if b & 4 else _mul1(x4[u][b], p4[u][7]) for b in vrange] for u in units]
    tot8 = [p8[u][7] for u in units]

    for u in units:
        for b in vrange:
            srows = pl.ds(b, 8, stride=8)
            cv_ref[u, 0, srows, :] = qp[u][b] * p8[u][b]
            cv_ref[u, 1, srows, :] = _mul1(kp[u][b], x8[u][b])

    fill()
    q_pieces = [([qp[u][b] * fp[u][b] for b in _LEVEL_Q[0]]
                 + [qp[u][b] * p2[u][b] for b in _LEVEL_Q[1]]
                 + [qp[u][b] * p4[u][b] for b in _LEVEL_Q[2]]) for u in units]
    k_pieces = [([kp[u][b] for b in _LEVEL_K[0]]
                 + [_mul1(kp[u][b], x2[u][b]) for b in _LEVEL_K[1]]
                 + [_mul1(kp[u][b], x4[u][b]) for b in _LEVEL_K[2]]) for u in units]
    for lvl in range(3):
        group = 1 << lvl
        for jq, b_q in enumerate(_LEVEL_Q[lvl]):
            for jk, b_k in enumerate(_LEVEL_K[lvl]):
                if jq // group != jk // group:
                    continue
                nq, nk = 4 * lvl + jq, 4 * lvl + jk
                for u in units:
                    d = jnp.sum(q_pieces[u][nq] * k_pieces[u][nk], axis=-1, keepdims=True)
                    o_tr[u][b_q] = o_tr[u][b_q] + d * vtr[u][b_k]

    an = [[cv_ref[u, 0, 8 * a:8 * a + 8, :] for a in vrange] for u in units]
    cn = [[cv_ref[u, 1, 8 * a:8 * a + 8, :] for a in vrange] for u in units]
    vnat = [[load(2, hd[u], pl.ds(r0[u] + 8 * a, 8)) for a in vrange] for u in units]

    q4 = [[an[u][a] for a in _LEVEL_Q[0]] for u in units]
    k4 = [[cn[u][a] for a in _LEVEL_K[0]] for u in units]
    an = [[an[u][a] * bc(tot8[u], a - 1) if a & 1 else an[u][a] for a in vrange] for u in units]
    cn = [[cn[u][a] if a & 1 else cn[u][a] * bc(tot8[u], a + 1) for a in vrange] for u in units]
    t16 = [tot8[u] * pltpu.roll(tot8[u], 1, 0) for u in units]
    q5 = [[an[u][a] for a in _LEVEL_Q[1]] for u in units]
    k5 = [[cn[u][a] for a in _LEVEL_K[1]] for u in units]
    an = [[an[u][a] * bc(t16[u], (a & ~3) + 1) if a & 2 else an[u][a] for a in vrange]
          for u in units]
    cn = [[cn[u][a] if a & 2 else cn[u][a] * bc(t16[u], (a & ~3) + 3) for a in vrange]
          for u in units]
    t32 = [t16[u] * pltpu.roll(t16[u], 2, 0) for u in units]
    q6 = [[an[u][a] for a in _LEVEL_Q[2]] for u in units]
    k6 = [[cn[u][a] for a in _LEVEL_K[2]] for u in units]
    an = [[an[u][a] * bc(t32[u], 3) if a & 4 else an[u][a] for a in vrange] for u in units]
    cn = [[cn[u][a] if a & 4 else cn[u][a] * bc(t32[u], 7) for a in vrange] for u in units]
    att_nat = stacked_attend([q4[u] + q5[u] + q6[u] for u in units],
                             [k4[u] + k5[u] + k6[u] for u in units], vnat, m_nat)
    o_nat = []
    for u in units:
        pieces = scatter_add([None] * nv, att_nat[u])
        pieces[0] = jnp.zeros((8, HGRN_DV), F32)
        o_nat.append(jnp.concatenate(pieces, axis=0))

    for u in units:
        for b in vrange:
            cv_ref[u, 2, pl.ds(b, 8, stride=8), :] = o_tr[u][b]

    fill()
    upd = [lax.dot_general(cat16(vnat[u]), cat16(cn[u]),
                           _TN, preferred_element_type=F32) for u in units]
    for h in range(nh):
        st = st_ref[h]
        for u in range(h, len(units), nh):
            qs = cat16(an[u])
            o_nat[u] = o_nat[u] + lax.dot_general(qs, st.astype(BF16), _NT,
                                                  preferred_element_type=F32)
            dec = t32[u][3:4, :] * t32[u][7:8, :]
            st = st * dec + upd[u]
        st_ref[h] = st

    fill()
    for u in units:
        o_h = o_nat[u] + cv_ref[u, 2]
        ms = jnp.mean(o_h * o_h, axis=-1, keepdims=True)
        xgs = 0.5 * load(3, hd[u], rows[u])
        gate = xgs * jnp.tanh(xgs) + xgs
        store(hd[u], rows[u], o_h * lax.rsqrt(ms + EPS) * og * gate)


def _swa_slab(q_slab, k_lo, k_hi, v_lo, v_hi, ones_pad, bias_first, bias_any,
              sink_a, sink_b, qgain, ones_bd, lo128, store):
    blk = ATTN_BLOCK
    nblk = MIX_ROWS // blk
    state = {}

    def prep():
        state["qs"] = (_pair_rms(q_slab(), ones_bd) * qgain).astype(BF16)
        state["sink_slab"] = jnp.where(lo128, sink_a, sink_b)
        state["sink_keys"] = jnp.concatenate([jnp.full((1, 2 * blk), sink_a, F32),
                                              jnp.full((1, 2 * blk), sink_b, F32)], axis=1)

    def scores(n):
        keys = slice(n * blk, (n + 2) * blk)
        k_cat = jnp.concatenate([k_lo()[keys], k_hi()[keys]], axis=0)
        s = lax.dot_general(state["qs"][n * blk:(n + 1) * blk], k_cat, _NT,
                            preferred_element_type=F32)
        s = s + (bias_first() if n == 0 else bias_any())
        m_a = jnp.max(s[:, :2 * blk], axis=-1, keepdims=True)
        m_b = jnp.max(s[:, 2 * blk:], axis=-1, keepdims=True)
        m = jnp.maximum(jnp.concatenate([jnp.broadcast_to(m_a, (blk, 2 * blk)),
                                         jnp.broadcast_to(m_b, (blk, 2 * blk))], axis=1),
                        state["sink_keys"])
        state["p", n] = jnp.exp2(s - m).astype(BF16)
        state["m", n] = jnp.where(lo128, m[:, :LANES], m[:, 2 * blk:2 * blk + LANES])

    def attend(n):
        keys = slice(n * blk, (n + 2) * blk)
        v_cat = jnp.concatenate([v_lo()[keys], v_hi()[keys]], axis=0)
        rhs = jnp.concatenate([v_cat, ones_pad], axis=1)
        nd = jnp.dot(state.pop(("p", n)), rhs, preferred_element_type=F32)
        den = nd[:, LANES:] + jnp.exp2(state["sink_slab"] - state.pop(("m", n)))
        store(n, nd[:, :LANES] / den)

    stages = [prep]
    for n in range(nblk):
        stages += [lambda n=n: scores(n), lambda n=n: attend(n)]
    return stages


def _mix_kernel(x_ref, gain_ref, wm_ref, wt_ref, lbl_ref, og_ref, qg_ref, kg_ref, sink_ref,
                rec_ref, att_ref,
                u_ref, pa_ref, pb_ref, st_ref, cv_ref, kvp_ref, kpad_ref, vpad_ref, bias_ref,
                lev_ref):
    g = pl.program_id(0)
    blk = ATTN_BLOCK

    lo128 = lax.broadcasted_iota(jnp.int32, (1, LANES), 1) < ATTN_HD

    @pl.when(g == 0)
    def _():
        kvp_ref[...] = jnp.zeros_like(kvp_ref)
        lev_ref[...] = _level_mask()
        qi = lax.broadcasted_iota(jnp.int32, (blk, 2 * blk), 0)
        kj = lax.broadcasted_iota(jnp.int32, (blk, 2 * blk), 1)
        rel = qi + blk - kj
        in_window = (rel >= 0) & (rel < WINDOW)
        b_any = jnp.where(in_window, 0.0, NEG_INF)
        b_first = jnp.where(in_window & (kj >= blk), 0.0, NEG_INF)
        bias_ref[0] = jnp.concatenate([b_any, b_any], axis=1)
        bias_ref[1] = jnp.concatenate([b_first, b_first], axis=1)

    first_blk = lax.rem(g + SEQ_BLOCKS - 1, SEQ_BLOCKS) == 0

    @pl.when(first_blk)
    def _():
        st_ref[...] = jnp.zeros_like(st_ref)

    hr = lax.broadcasted_iota(jnp.int32, (LANES, LANES), 0) < ATTN_HD
    hc = lax.broadcasted_iota(jnp.int32, (LANES, LANES), 1) < ATTN_HD
    ones_bd = jnp.where(hr == hc, 1.0, 0.0).astype(BF16)
    pair = lambda r: jnp.concatenate([r, r], axis=1)
    qgain = pair(qg_ref[...]) * (ATTN_HD ** -0.5 * LOG2E)
    kgain = pair(kg_ref[...])
    ones_lo = jnp.broadcast_to(jnp.where(lo128, 1.0, 0.0), (2 * blk, LANES))
    ones_pad = jnp.concatenate([ones_lo, 1.0 - ones_lo], axis=0).astype(BF16)

    l0 = lbl_ref[0:1, :]
    l1 = lbl_ref[1:2, :]
    lmax = jnp.maximum(l0, l1)
    e0 = jnp.exp(l0 - lmax)
    e1 = jnp.exp(l1 - lmax)
    lb = e0 / (e0 + e1)
    c0 = 0.5 * (1.0 + lb)
    c1 = 0.5 * (1.0 - lb)
    og = og_ref[...]
    first_idx = first_blk.astype(jnp.int32)

    def step(pc_ref, pp_ref):
        _mix_step(pc_ref, pp_ref, x_ref, gain_ref, u_ref, wm_ref, wt_ref, rec_ref, att_ref,
                  st_ref, cv_ref, kvp_ref, kpad_ref, vpad_ref, bias_ref, lev_ref, sink_ref,
                  (lo128, ones_bd, qgain, kgain, ones_pad, c0, c1, og, first_idx))

    @pl.when(g == 0)
    def _():
        _project_only(pa_ref, x_ref, gain_ref, u_ref, wm_ref, wt_ref)

    @pl.when((lax.rem(g, 2) == 0) & (g > 0))
    def _():
        step(pa_ref, pb_ref)

    @pl.when(lax.rem(g, 2) == 1)
    def _():
        step(pb_ref, pa_ref)


def _project_tail(pc_ref, x_ref, gain_ref, u_ref, wt_ref):
    x = x_ref[...]
    ms = jnp.mean(x * x, axis=-1, keepdims=True)
    u = (x * lax.rsqrt(ms + EPS) * gain_ref[...]).astype(BF16)
    u_ref[...] = u
    tail = jnp.dot(u, wt_ref[...], preferred_element_type=F32)
    for t in range(TAIL_SLABS):
        pc_ref[MIX_ITERS * LOOP_SLABS + t] = tail[:, t * LANES:(t + 1) * LANES]


def _project_main(pc_ref, u_ref, wm_ref, i):
    res = jnp.dot(u_ref[...], wm_ref[i], preferred_element_type=F32)
    for t in range(LOOP_SLABS):
        pc_ref[i * LOOP_SLABS + t] = res[:, t * LANES:(t + 1) * LANES]


def _project_only(pc_ref, x_ref, gain_ref, u_ref, wm_ref, wt_ref):
    _project_tail(pc_ref, x_ref, gain_ref, u_ref, wt_ref)
    for i in range(MIX_ITERS):
        _project_main(pc_ref, u_ref, wm_ref, i)


def _mix_step(pc_ref, pp_ref, x_ref, gain_ref, u_ref, wm_ref, wt_ref, rec_ref, att_ref, st_ref,
              cv_ref, kvp_ref, kpad_ref, vpad_ref, bias_ref, lev_ref, sink_ref, consts):
    (lo128, ones_bd, qgain, kgain, ones_pad, c0, c1, og, first_idx) = consts
    blk = ATTN_BLOCK
    _project_tail(pc_ref, x_ref, gain_ref, u_ref, wt_ref)

    k_all = jnp.concatenate([kvp_ref[0], pp_ref[K_SLAB]], axis=0)
    v_all = jnp.concatenate([kvp_ref[1], pp_ref[V_SLAB]], axis=0)
    kvp_ref[0] = pp_ref[K_SLAB, MIX_ROWS - blk:, :]
    kvp_ref[1] = pp_ref[V_SLAB, MIX_ROWS - blk:, :]
    kn = _pair_rms(k_all, ones_bd) * kgain
    kn_sw = pltpu.roll(kn, ATTN_HD, 1)
    v_sw = pltpu.roll(v_all, ATTN_HD, 1)
    zero = jnp.zeros_like(kn)
    kpad_ref[0, 0] = jnp.where(lo128, kn, zero).astype(BF16)
    kpad_ref[0, 1] = jnp.where(lo128, zero, kn_sw).astype(BF16)
    kpad_ref[1, 0] = jnp.where(lo128, kn_sw, zero).astype(BF16)
    kpad_ref[1, 1] = jnp.where(lo128, zero, kn).astype(BF16)
    vpad_ref[0, 0] = jnp.where(lo128, v_all, zero).astype(BF16)
    vpad_ref[0, 1] = jnp.where(lo128, zero, v_sw).astype(BF16)
    vpad_ref[1, 0] = jnp.where(lo128, v_sw, zero).astype(BF16)
    vpad_ref[1, 1] = jnp.where(lo128, zero, v_all).astype(BF16)

    def body(i, carry):
        def load(part, h, rows):
            return pp_ref[part * HGRN_HEADS + h, rows, :]

        def store(h, rows, tile):
            rec_ref[0, rows, h * HGRN_DV:(h + 1) * HGRN_DV] = tile.astype(rec_ref.dtype)

        r0s = [pl.multiple_of((i * HGRN_UNROLL + cc) * HGRN_CHUNK, HGRN_CHUNK)
               for cc in range(HGRN_UNROLL)]

        def slab_stages(c):
            kvh = lax.div(c, ATTN_GROUP // 2)

            def store_att(n, tile):
                att_ref[0, c, n * blk:(n + 1) * blk, :] = tile.astype(att_ref.dtype)

            return _swa_slab(lambda: pp_ref[Q_SLAB + c],
                             lambda: kpad_ref[kvh, 0], lambda: kpad_ref[kvh, 1],
                             lambda: vpad_ref[kvh, 0], lambda: vpad_ref[kvh, 1], ones_pad,
                             lambda: bias_ref[first_idx], lambda: bias_ref[0],
                             sink_ref[2 * c] * LOG2E, sink_ref[2 * c + 1] * LOG2E,
                             qgain, ones_bd, lo128, store_att)

        swa = [slab_stages(i * SLABS_PER_ITER + s) for s in range(SLABS_PER_ITER)]

        _project_main(pc_ref, u_ref, wm_ref, i)
        _hgrn_chunks(load, store, r0s, c0, c1, og, lev_ref[...], st_ref, cv_ref, lambda: None)
        for stages in zip(*swa):
            for stage in stages:
                stage()
        return carry

    lax.fori_loop(0, MIX_ITERS, body, 0, unroll=MIX_UNROLL)


def _mix(x2, seq_len, gain, w_main, w_tail, lb_logits, out_gain, q_gain, k_gain, sinks):
    t = x2.shape[0]
    nblk = t // MIX_ROWS
    assert seq_len // MIX_ROWS == SEQ_BLOCKS
    batch = t // seq_len

    def in_rows(gi):
        return (jnp.minimum(gi, nblk - 1), 0)

    def out_block(gi):
        blk_i = jnp.maximum(gi - 1, 0)
        return blk_i // SEQ_BLOCKS, lax.rem(blk_i, SEQ_BLOCKS)

    full = lambda shape: pl.BlockSpec(shape, lambda gi: (0,) * len(shape))
    return pl.pallas_call(
        _mix_kernel,
        grid=(nblk + 1,),
        in_specs=[
            pl.BlockSpec((MIX_ROWS, D_MODEL), in_rows),
            full((1, D_MODEL)),
            full((MIX_ITERS, D_MODEL, LOOP_SLABS * LANES)),
            full((D_MODEL, TAIL_SLABS * LANES)),
            full((2, HGRN_WIDTH)),
            full((1, HGRN_DV)),
            full((1, ATTN_HD)),
            full((1, ATTN_HD)),
            pl.BlockSpec(memory_space=pltpu.SMEM),
        ],
        out_specs=[
            pl.BlockSpec((1, MIX_ROWS, HGRN_WIDTH),
                         lambda gi: (*out_block(gi), 0)),
            pl.BlockSpec((1, ATT_SLABS, MIX_ROWS, LANES),
                         lambda gi: (out_block(gi)[0], 0, out_block(gi)[1], 0)),
        ],
        out_shape=[
            jax.ShapeDtypeStruct((batch, seq_len, HGRN_WIDTH), BF16),
            jax.ShapeDtypeStruct((batch, ATT_SLABS, seq_len, LANES), BF16),
        ],
        scratch_shapes=[
            pltpu.VMEM((MIX_ROWS, D_MODEL), BF16),
            pltpu.VMEM((N_SLABS, MIX_ROWS, LANES), F32),
            pltpu.VMEM((N_SLABS, MIX_ROWS, LANES), F32),
            pltpu.VMEM((HGRN_HEADS, HGRN_DV, HGRN_DK), F32),
            pltpu.VMEM((HGRN_HEADS * HGRN_UNROLL, 3, HGRN_CHUNK, HGRN_DK), F32),
            pltpu.VMEM((2, ATTN_BLOCK, LANES), F32),
            pltpu.VMEM((ATTN_KV_HEADS, 2, ATTN_BLOCK + MIX_ROWS, LANES), BF16),
            pltpu.VMEM((ATTN_KV_HEADS, 2, ATTN_BLOCK + MIX_ROWS, LANES), BF16),
            pltpu.VMEM((2, ATTN_BLOCK, 4 * ATTN_BLOCK), F32),
            pltpu.VMEM((LEVEL_ROWS, LEVEL_ROWS), jnp.int32),
        ],
        compiler_params=pltpu.CompilerParams(
            dimension_semantics=("arbitrary",), vmem_limit_bytes=VMEM_LIMIT_BYTES),
        name="mixer",
    )(x2, gain, w_main, w_tail, lb_logits, out_gain, q_gain, k_gain, sinks)


def _ffn_kernel(x_ref, rec_ref, att_ref, wo_ref, gain_ref, wg_ref, wu_ref, wd_ref, o_ref):
    mixed = jnp.concatenate([rec_ref[0]] + [att_ref[0, c] for c in range(ATT_SLABS)],
                            axis=1)
    h = x_ref[...] + jnp.dot(mixed, wo_ref[...], preferred_element_type=F32)
    ms = jnp.mean(h * h, axis=-1, keepdims=True)
    u = (h * lax.rsqrt(ms + EPS) * gain_ref[...]).astype(BF16)
    gate = jnp.dot(u, wg_ref[...], preferred_element_type=F32)
    up = jnp.dot(u, wu_ref[...], preferred_element_type=F32)
    gs = 0.5 * gate
    act = ((gs * jnp.tanh(gs) + gs) * up).astype(BF16)
    o_ref[...] = h + jnp.dot(act, wd_ref[...], preferred_element_type=F32)


def _ffn(x2, rec, att, wo, gain, wg, wu, wd):
    t = x2.shape[0]
    seq_blocks = rec.shape[1] // FFN_ROWS

    def resident(shape):
        return pl.BlockSpec(shape, lambda i: (0, 0), pipeline_mode=pl.Buffered(1))

    return pl.pallas_call(
        _ffn_kernel,
        grid=(t // FFN_ROWS,),
        in_specs=[
            pl.BlockSpec((FFN_ROWS, D_MODEL), lambda i: (i, 0)),
            pl.BlockSpec((1, FFN_ROWS, HGRN_WIDTH),
                         lambda i: (i // seq_blocks, lax.rem(i, seq_blocks), 0)),
            pl.BlockSpec((1, ATT_SLABS, FFN_ROWS, LANES),
                         lambda i: (i // seq_blocks, 0, lax.rem(i, seq_blocks), 0)),
            resident((D_MODEL, D_MODEL)),
            resident((1, D_MODEL)),
            resident((D_MODEL, D_FF)),
            resident((D_MODEL, D_FF)),
            resident((D_FF, D_MODEL)),
        ],
        out_specs=pl.BlockSpec((FFN_ROWS, D_MODEL), lambda i: (i, 0)),
        out_shape=jax.ShapeDtypeStruct((t, D_MODEL), F32),
        compiler_params=pltpu.CompilerParams(
            dimension_semantics=("arbitrary",), vmem_limit_bytes=VMEM_LIMIT_BYTES),
        name="outproj_ffn",
    )(x2, rec, att, wo, gain, wg, wu, wd)


def kernel(x, norm1_gain, w_in, hgrn_lb_logits, hgrn_out_gain, q_norm_gain, k_norm_gain,
           attn_sinks, w_out, norm2_gain, w_ffn_gate, w_ffn_up, w_ffn_down):
    b, s, d = x.shape
    assert (d, w_in.shape[0]) == (D_MODEL, 1), "single-layer kernel"
    assert s == SEQ_BLOCKS * MIX_ROWS and MIX_ROWS == FFN_ROWS
    t = b * s
    x2 = x.reshape(t, d)

    w_bf = w_in[0].astype(BF16)
    n_main = MIX_ITERS * LOOP_SLABS * LANES
    w_main = w_bf[:, :n_main].reshape(D_MODEL, MIX_ITERS, LOOP_SLABS * LANES).transpose(1, 0, 2)
    w_tail = w_bf[:, n_main:]
    rec, att = _mix(x2, s, norm1_gain[0][None, :], w_main, w_tail, hgrn_lb_logits,
                    hgrn_out_gain[0][None, :], q_norm_gain[0][None, :], k_norm_gain[0][None, :],
                    attn_sinks[0])
    out = _ffn(x2, rec, att, w_out[0].astype(BF16), norm2_gain[0][None, :],
               w_ffn_gate[0].astype(BF16), w_ffn_up[0].astype(BF16),
               w_ffn_down[0].astype(BF16))
    return out.reshape(b, s, d)
```

```python
import jax
import jax.numpy as jnp
from jax import lax
from jax.experimental import pallas as pl
from jax.experimental.pallas import tpu as pltpu

D_MODEL = 1024
HGRN_HEADS = 4
HGRN_DK = 128
HGRN_DV = 128
HGRN_WIDTH = HGRN_HEADS * HGRN_DK
ATTN_HEADS = 8
ATTN_KV_HEADS = 2
ATTN_GROUP = ATTN_HEADS // ATTN_KV_HEADS
ATTN_HD = 64
ATTN_WIDTH = ATTN_HEADS * ATTN_HD
KV_WIDTH = ATTN_KV_HEADS * ATTN_HD
WINDOW = 128
PROJ_WIDTH = 4 * HGRN_WIDTH + ATTN_WIDTH + 2 * KV_WIDTH
D_FF = 2816
EPS = 1e-6
NEG_INF = -1e30
LOG2E = 1.4426950408889634

F32 = jnp.float32
BF16 = jnp.bfloat16

LANES = 128
VMEM_LIMIT_BYTES = 56 * 1024 * 1024

HGRN_CHUNK = 64
HGRN_LEVELS = 6
HGRN_UNROLL = 2
ATTN_BLOCK = 128
MIX_ROWS = 512
MIX_ITERS = 4
MIX_UNROLL = 2
PROJ_PIECE_ROWS = 512
SEQ_BLOCKS = 4
FFN_ROWS = 512

N_SLABS = PROJ_WIDTH // LANES
ATT_SLABS = ATTN_WIDTH // LANES
SLABS_PER_ITER = ATT_SLABS // MIX_ITERS
Q_SLAB = 4 * HGRN_HEADS
K_SLAB = Q_SLAB + ATT_SLABS
V_SLAB = K_SLAB + 1
LOOP_SLABS = Q_SLAB // MIX_ITERS
TAIL_SLABS = N_SLABS - MIX_ITERS * LOOP_SLABS
assert MIX_ITERS * LOOP_SLABS == Q_SLAB and V_SLAB == N_SLABS - 1 and LOOP_SLABS % 2 == 0
assert MIX_ROWS == MIX_ITERS * HGRN_UNROLL * HGRN_CHUNK
assert ATT_SLABS == MIX_ITERS * SLABS_PER_ITER

_NT = (((1,), (1,)), ((), ()))
_TN = (((0,), (0,)), ((), ()))


_LEVEL_Q = ((1, 3, 5, 7), (2, 3, 6, 7), (4, 5, 6, 7))
_LEVEL_K = ((0, 2, 4, 6), (0, 1, 4, 5), (0, 1, 2, 3))
LEVEL_ROWS = 8 * sum(len(q) for q in _LEVEL_Q)


def _level_mask():
    r = lax.broadcasted_iota(jnp.int32, (LEVEL_ROWS, LEVEL_ROWS), 0)
    c = lax.broadcasted_iota(jnp.int32, (LEVEL_ROWS, LEVEL_ROWS), 1)
    lev_r, lev_c = r >> 5, c >> 5
    j_r, j_c = (r >> 3) & 3, (c >> 3) & 3
    same_block = (((lev_r == 0) & (j_r == j_c))
                  | ((lev_r == 1) & ((j_r >> 1) == (j_c >> 1)))
                  | (lev_r == 2))
    return ((lev_r == lev_c) & same_block).astype(jnp.int32)


def _mul1(a, b):
    if a is None:
        return b
    if b is None:
        return a
    return a * b


def _pair_rms(x, ones_bd):
    ss = jnp.dot((x * x).astype(BF16), ones_bd, preferred_element_type=F32)
    return x * lax.rsqrt(ss * (1.0 / ATTN_HD) + EPS)


def _hgrn_chunks(load, store, r0s, c0, c1, og, mask_nat, st_ref, cv_ref, fill):
    nh = HGRN_HEADS
    c_rows = HGRN_CHUNK
    nv = c_rows // 8
    assert nv == 8 and HGRN_LEVELS == 6

    def bc(t, r):
        return jnp.broadcast_to(t[r:r + 1, :], (8, HGRN_DK))

    def cat16(pieces):
        return jnp.concatenate(pieces, axis=0).astype(BF16)

    q_order = _LEVEL_Q[0] + _LEVEL_Q[1] + _LEVEL_Q[2]
    k_order = _LEVEL_K[0] + _LEVEL_K[1] + _LEVEL_K[2]

    def stacked_attend(q_pieces, k_pieces, v_pieces, mask):
        s = [lax.dot_general(cat16(q_pieces[u]), cat16(k_pieces[u]), _NT,
                             preferred_element_type=F32) for u in units]
        fill()
        s = [jnp.where(mask, s[u], 0.0).astype(BF16) for u in units]
        o = [jnp.dot(s[u], cat16([v_pieces[u][j] for j in k_order]),
                     preferred_element_type=F32) for u in units]
        fill()
        return [[o[u][8 * n:8 * n + 8, :] for n in range(len(q_order))] for u in units]

    def scatter_add(acc, pieces):
        for n, j in enumerate(q_order):
            acc[j] = pieces[n] if acc[j] is None else acc[j] + pieces[n]
        return acc

    m_nat = mask_nat != 0

    units = range(nh * len(r0s))
    hd = [u % nh for u in units]
    r0 = [r0s[u // nh] for u in units]
    rows = [pl.ds(r0[u], c_rows) for u in units]
    lanes = [slice(hd[u] * HGRN_DK, (hd[u] + 1) * HGRN_DK) for u in units]
    vrange = range(nv)

    fp, kp, qp, vtr = [], [], [], []
    for u in units:
        c0h, c1h = c0[:, lanes[u]], c1[:, lanes[u]]
        fh, kh, qh, vh = [], [], [], []
        for b in vrange:
            srows = pl.ds(r0[u] + b, 8, stride=8)
            ct = c1h * jnp.tanh(0.5 * load(1, hd[u], srows))
            fh.append(c0h + ct)
            kh.append(c1h - ct)
            xq = load(0, hd[u], srows)
            xs = xq * (0.5 * HGRN_DK ** -0.5)
            qh.append(xs * jnp.tanh(0.5 * xq) + xs)
            vh.append(load(2, hd[u], srows))
        fp.append(fh)
        kp.append(kh)
        qp.append(qh)
        vtr.append(vh)

    fill()
    o_tr = [[jnp.sum(qp[u][b] * kp[u][b], axis=-1, keepdims=True) * vtr[u][b] for b in vrange]
            for u in units]

    p2 = [[fp[u][b] * fp[u][b - 1] if b & 1 else fp[u][b] for b in vrange] for u in units]
    x2 = [[None if b & 1 else fp[u][b + 1] for b in vrange] for u in units]
    p4 = [[p2[u][b] * p2[u][(b & ~3) + 1] if b & 2 else p2[u][b] for b in vrange] for u in units]
    x4 = [[x2[u][b] if b & 2 else _mul1(x2[u][b], p2[u][(b & ~3) + 3]) for b in vrange]
          for u in units]
    p8 = [[p4[u][b] * p4[u][3] if b & 4 else p4[u][b] for b in vrange] for u in units]
    x8 = [[x4[u][b] if b & 4 else _mul1(x4[u][b], p4[u][7]) for b in vrange] for u in units]
    tot8 = [p8[u][7] for u in units]

    for u in units:
        for b in vrange:
            srows = pl.ds(b, 8, stride=8)
            cv_ref[u, 0, srows, :] = qp[u][b] * p8[u][b]
            cv_ref[u, 1, srows, :] = _mul1(kp[u][b], x8[u][b])

    fill()
    q_pieces = [([qp[u][b] * fp[u][b] for b in _LEVEL_Q[0]]
                 + [qp[u][b] * p2[u][b] for b in _LEVEL_Q[1]]
                 + [qp[u][b] * p4[u][b] for b in _LEVEL_Q[2]]) for u in units]
    k_pieces = [([kp[u][b] for b in _LEVEL_K[0]]
                 + [_mul1(kp[u][b], x2[u][b]) for b in _LEVEL_K[1]]
                 + [_mul1(kp[u][b], x4[u][b]) for b in _LEVEL_K[2]]) for u in units]
    for lvl in range(3):
        group = 1 << lvl
        for jq, b_q in enumerate(_LEVEL_Q[lvl]):
            for jk, b_k in enumerate(_LEVEL_K[lvl]):
                if jq // group != jk // group:
                    continue
                nq, nk = 4 * lvl + jq, 4 * lvl + jk
                for u in units:
                    d = jnp.sum(q_pieces[u][nq] * k_pieces[u][nk], axis=-1, keepdims=True)
                    o_tr[u][b_q] = o_tr[u][b_q] + d * vtr[u][b_k]

    an = [[cv_ref[u, 0, 8 * a:8 * a + 8, :] for a in vrange] for u in units]
    cn = [[cv_ref[u, 1, 8 * a:8 * a + 8, :] for a in vrange] for u in units]
    vnat = [[load(2, hd[u], pl.ds(r0[u] + 8 * a, 8)) for a in vrange] for u in units]

    q4 = [[an[u][a] for a in _LEVEL_Q[0]] for u in units]
    k4 = [[cn[u][a] for a in _LEVEL_K[0]] for u in units]
    an = [[an[u][a] * bc(tot8[u], a - 1) if a & 1 else an[u][a] for a in vrange] for u in units]
    cn = [[cn[u][a] if a & 1 else cn[u][a] * bc(tot8[u], a + 1) for a in vrange] for u in units]
    t16 = [tot8[u] * pltpu.roll(tot8[u], 1, 0) for u in units]
    q5 = [[an[u][a] for a in _LEVEL_Q[1]] for u in units]
    k5 = [[cn[u][a] for a in _LEVEL_K[1]] for u in units]
    an = [[an[u][a] * bc(t16[u], (a & ~3) + 1) if a & 2 else an[u][a] for a in vrange]
          for u in units]
    cn = [[cn[u][a] if a & 2 else cn[u][a] * bc(t16[u], (a & ~3) + 3) for a in vrange]
          for u in units]
    t32 = [t16[u] * pltpu.roll(t16[u], 2, 0) for u in units]
    q6 = [[an[u][a] for a in _LEVEL_Q[2]] for u in units]
    k6 = [[cn[u][a] for a in _LEVEL_K[2]] for u in units]
    an = [[an[u][a] * bc(t32[u], 3) if a & 4 else an[u][a] for a in vrange] for u in units]
    cn = [[cn[u][a] if a & 4 else cn[u][a] * bc(t32[u], 7) for a in vrange] for u in units]
    att_nat = stacked_attend([q4[u] + q5[u] + q6[u] for u in units],
                             [k4[u] + k5[u] + k6[u] for u in units], vnat, m_nat)
    o_nat = []
    for u in units:
        pieces = scatter_add([None] * nv, att_nat[u])
        pieces[0] = jnp.zeros((8, HGRN_DV), F32)
        o_nat.append(jnp.concatenate(pieces, axis=0))

    for u in units:
        for b in vrange:
            cv_ref[u, 2, pl.ds(b, 8, stride=8), :] = o_tr[u][b]

    fill()
    upd = [lax.dot_general(cat16(vnat[u]), cat16(cn[u]),
                           _TN, preferred_element_type=F32) for u in units]
    for h in range(nh):
        st = st_ref[h]
        for u in range(h, len(units), nh):
            qs = cat16(an[u])
            o_nat[u] = o_nat[u] + lax.dot_general(qs, st.astype(BF16), _NT,
                                                  preferred_element_type=F32)
            dec = t32[u][3:4, :] * t32[u][7:8, :]
            st = st * dec + upd[u]
        st_ref[h] = st

    fill()
    for u in units:
        o_h = o_nat[u] + cv_ref[u, 2]
        ms = jnp.mean(o_h * o_h, axis=-1, keepdims=True)
        xgs = 0.5 * load(3, hd[u], rows[u])
        gate = xgs * jnp.tanh(xgs) + xgs
        store(hd[u], rows[u], o_h * lax.rsqrt(ms + EPS) * og * gate)


def _swa_slab(q_slab, k_lo, k_hi, v_lo, v_hi, ones_pad, bias_first, bias_any,
              sink_a, sink_b, qgain, ones_bd, lo128, store):
    blk = ATTN_BLOCK
    nblk = MIX_ROWS // blk
    state = {}

    def prep():
        state["qs"] = (_pair_rms(q_slab(), ones_bd) * qgain).astype(BF16)
        state["sink_slab"] = jnp.where(lo128, sink_a, sink_b)

    def scores(n):
        keys = slice(n * blk, (n + 2) * blk)
        k_cat = jnp.concatenate([k_lo()[keys], k_hi()[keys]], axis=0)
        s = lax.dot_general(state["qs"][n * blk:(n + 1) * blk], k_cat, _NT,
                            preferred_element_type=F32)
        s = s + (bias_first() if n == 0 else bias_any())
        m_a = jnp.max(s[:, :2 * blk], axis=-1, keepdims=True)
        m_b = jnp.max(s[:, 2 * blk:], axis=-1, keepdims=True)
        m_a = jnp.maximum(jnp.broadcast_to(m_a, (blk, LANES)), sink_a)
        m_b = jnp.maximum(jnp.broadcast_to(m_b, (blk, LANES)), sink_b)
        cols = [s[:, c * LANES:(c + 1) * LANES] - (m_a if c < 2 * blk // LANES else m_b)
                for c in range(4 * blk // LANES)]
        state["p", n] = jnp.exp2(jnp.concatenate(cols, axis=1)).astype(BF16)
        state["m", n] = jnp.where(lo128, m_a, m_b)

    def attend(n):
        keys = slice(n * blk, (n + 2) * blk)
        v_cat = jnp.concatenate([v_lo()[keys], v_hi()[keys]], axis=0)
        rhs = jnp.concatenate([v_cat, ones_pad], axis=1)
        nd = jnp.dot(state.pop(("p", n)), rhs, preferred_element_type=F32)
        den = nd[:, LANES:] + jnp.exp2(state["sink_slab"] - state.pop(("m", n)))
        store(n, nd[:, :LANES] / den)

    stages = [prep]
    for n in range(nblk):
        stages += [lambda n=n: scores(n), lambda n=n: attend(n)]
    return stages


_CAST_SHAPES = ((D_MODEL, D_MODEL, 32), (D_MODEL, D_FF, 32), (D_MODEL, D_FF, 32), (D_FF, D_MODEL, 96))
N_CAST = len(_CAST_SHAPES)


def _cast_chunk(k, step):
    rows, _, per = _CAST_SHAPES[k]
    n = -(-rows // per)
    start = jnp.minimum(step * per, rows - per)
    return n, pl.multiple_of(start, 16)


def _cast_copies(k, step, src_refs, dst_refs, fbuf_refs, bbuf_refs, sem_in, sem_out):
    per = _CAST_SHAPES[k][2]
    _, start = _cast_chunk(k, step)
    rows = pl.ds(start, per)
    return (pltpu.make_async_copy(src_refs[k].at[rows, :], fbuf_refs[k], sem_in.at[k]),
            pltpu.make_async_copy(bbuf_refs[k], dst_refs[k].at[rows, :], sem_out.at[k]))


def _mix_kernel(x_ref, gain_ref, wm_ref, wt_ref, lbl_ref, og_ref, qg_ref, kg_ref, sink_ref,
                *refs):
    cast_src = refs[:N_CAST]
    rec_ref, att_ref = refs[N_CAST:N_CAST + 2]
    cast_dst = refs[N_CAST + 2:2 * N_CAST + 2]
    (u_ref, pa_ref, pb_ref, st_ref, cv_ref, kvp_ref, kpad_ref, vpad_ref, bias_ref,
     lev_ref) = refs[2 * N_CAST + 2:2 * N_CAST + 12]
    cast_f = refs[2 * N_CAST + 12:3 * N_CAST + 12]
    cast_b = refs[3 * N_CAST + 12:4 * N_CAST + 12]
    sem_in, sem_out = refs[4 * N_CAST + 12:]
    g = pl.program_id(0)
    blk = ATTN_BLOCK

    def cast_copies(k, step):
        return _cast_copies(k, step, cast_src, cast_dst, cast_f, cast_b, sem_in, sem_out)

    for k in range(N_CAST):
        n_chunks = _cast_chunk(k, g)[0]

        @pl.when((g >= 1) & (g <= n_chunks))
        def _(k=k):
            cast_copies(k, g - 1)[1].wait()

        @pl.when(g < n_chunks)
        def _(k=k):
            cast_copies(k, g)[0].start()

    lo128 = lax.broadcasted_iota(jnp.int32, (1, LANES), 1) < ATTN_HD

    @pl.when(g == 0)
    def _():
        kvp_ref[...] = jnp.zeros_like(kvp_ref)
        lev_ref[...] = _level_mask()
        qi = lax.broadcasted_iota(jnp.int32, (blk, 2 * blk), 0)
        kj = lax.broadcasted_iota(jnp.int32, (blk, 2 * blk), 1)
        rel = qi + blk - kj
        in_window = (rel >= 0) & (rel < WINDOW)
        b_any = jnp.where(in_window, 0.0, NEG_INF)
        b_first = jnp.where(in_window & (kj >= blk), 0.0, NEG_INF)
        bias_ref[0] = jnp.concatenate([b_any, b_any], axis=1)
        bias_ref[1] = jnp.concatenate([b_first, b_first], axis=1)

    first_blk = lax.rem(g + SEQ_BLOCKS - 1, SEQ_BLOCKS) == 0

    @pl.when(first_blk)
    def _():
        st_ref[...] = jnp.zeros_like(st_ref)

    hr = lax.broadcasted_iota(jnp.int32, (LANES, LANES), 0) < ATTN_HD
    hc = lax.broadcasted_iota(jnp.int32, (LANES, LANES), 1) < ATTN_HD
    ones_bd = jnp.where(hr == hc, 1.0, 0.0).astype(BF16)
    pair = lambda r: jnp.concatenate([r, r], axis=1)
    qgain = pair(qg_ref[...]) * (ATTN_HD ** -0.5 * LOG2E)
    kgain = pair(kg_ref[...])
    ones_lo = jnp.broadcast_to(jnp.where(lo128, 1.0, 0.0), (2 * blk, LANES))
    ones_pad = jnp.concatenate([ones_lo, 1.0 - ones_lo], axis=0).astype(BF16)

    l0 = lbl_ref[0:1, :]
    l1 = lbl_ref[1:2, :]
    lmax = jnp.maximum(l0, l1)
    e0 = jnp.exp(l0 - lmax)
    e1 = jnp.exp(l1 - lmax)
    lb = e0 / (e0 + e1)
    c0 = 0.5 * (1.0 + lb)
    c1 = 0.5 * (1.0 - lb)
    og = og_ref[...]
    first_idx = first_blk.astype(jnp.int32)

    def step(pc_ref, pp_ref):
        _mix_step(pc_ref, pp_ref, x_ref, gain_ref, u_ref, wm_ref, wt_ref, rec_ref, att_ref,
                  st_ref, cv_ref, kvp_ref, kpad_ref, vpad_ref, bias_ref, lev_ref, sink_ref,
                  (lo128, ones_bd, qgain, kgain, ones_pad, c0, c1, og, first_idx))

    @pl.when(g == 0)
    def _():
        _project_only(pa_ref, x_ref, gain_ref, u_ref, wm_ref, wt_ref)

    @pl.when((lax.rem(g, 2) == 0) & (g > 0))
    def _():
        step(pa_ref, pb_ref)

    @pl.when(lax.rem(g, 2) == 1)
    def _():
        step(pb_ref, pa_ref)

    for k in range(N_CAST):
        @pl.when(g < _cast_chunk(k, g)[0])
        def _(k=k):
            copy_in, copy_out = cast_copies(k, g)
            copy_in.wait()
            cast_b[k][...] = cast_f[k][...].astype(BF16)
            copy_out.start()


def _project_tail(pc_ref, x_ref, gain_ref, u_ref, wt_ref):
    x = x_ref[...]
    ms = jnp.mean(x * x, axis=-1, keepdims=True)
    u = (x * lax.rsqrt(ms + EPS) * gain_ref[...]).astype(BF16)
    u_ref[...] = u
    tail = jnp.dot(u, wt_ref[...], preferred_element_type=F32)
    for t in range(TAIL_SLABS):
        pc_ref[MIX_ITERS * LOOP_SLABS + t] = tail[:, t * LANES:(t + 1) * LANES]


def _project_main(pc_ref, u_ref, wm_ref, i):
    for r in range(0, MIX_ROWS, PROJ_PIECE_ROWS):
        res = jnp.dot(u_ref[r:r + PROJ_PIECE_ROWS, :], wm_ref[i], preferred_element_type=F32)
        for t in range(LOOP_SLABS):
            pc_ref[i * LOOP_SLABS + t, r:r + PROJ_PIECE_ROWS, :] = res[:, t * LANES:(t + 1) * LANES]


def _project_only(pc_ref, x_ref, gain_ref, u_ref, wm_ref, wt_ref):
    _project_tail(pc_ref, x_ref, gain_ref, u_ref, wt_ref)
    for i in range(MIX_ITERS):
        _project_main(pc_ref, u_ref, wm_ref, i)


def _mix_step(pc_ref, pp_ref, x_ref, gain_ref, u_ref, wm_ref, wt_ref, rec_ref, att_ref, st_ref,
              cv_ref, kvp_ref, kpad_ref, vpad_ref, bias_ref, lev_ref, sink_ref, consts):
    (lo128, ones_bd, qgain, kgain, ones_pad, c0, c1, og, first_idx) = consts
    blk = ATTN_BLOCK
    _project_tail(pc_ref, x_ref, gain_ref, u_ref, wt_ref)

    k_all = jnp.concatenate([kvp_ref[0], pp_ref[K_SLAB]], axis=0)
    v_all = jnp.concatenate([kvp_ref[1], pp_ref[V_SLAB]], axis=0)
    kvp_ref[0] = pp_ref[K_SLAB, MIX_ROWS - blk:, :]
    kvp_ref[1] = pp_ref[V_SLAB, MIX_ROWS - blk:, :]
    kn = _pair_rms(k_all, ones_bd) * kgain
    kn_sw = pltpu.roll(kn, ATTN_HD, 1)
    v_sw = pltpu.roll(v_all, ATTN_HD, 1)
    zero = jnp.zeros_like(kn)
    kpad_ref[0, 0] = jnp.where(lo128, kn, zero).astype(BF16)
    kpad_ref[0, 1] = jnp.where(lo128, zero, kn_sw).astype(BF16)
    kpad_ref[1, 0] = jnp.where(lo128, kn_sw, zero).astype(BF16)
    kpad_ref[1, 1] = jnp.where(lo128, zero, kn).astype(BF16)
    vpad_ref[0, 0] = jnp.where(lo128, v_all, zero).astype(BF16)
    vpad_ref[0, 1] = jnp.where(lo128, zero, v_sw).astype(BF16)
    vpad_ref[1, 0] = jnp.where(lo128, v_sw, zero).astype(BF16)
    vpad_ref[1, 1] = jnp.where(lo128, zero, v_all).astype(BF16)

    def body(i, carry):
        def load(part, h, rows):
            return pp_ref[part * HGRN_HEADS + h, rows, :]

        def store(h, rows, tile):
            rec_ref[0, rows, h * HGRN_DV:(h + 1) * HGRN_DV] = tile.astype(rec_ref.dtype)

        r0s = [pl.multiple_of((i * HGRN_UNROLL + cc) * HGRN_CHUNK, HGRN_CHUNK)
               for cc in range(HGRN_UNROLL)]

        def slab_stages(c):
            kvh = lax.div(c, ATTN_GROUP // 2)

            def store_att(n, tile):
                att_ref[0, c, n * blk:(n + 1) * blk, :] = tile.astype(att_ref.dtype)

            return _swa_slab(lambda: pp_ref[Q_SLAB + c],
                             lambda: kpad_ref[kvh, 0], lambda: kpad_ref[kvh, 1],
                             lambda: vpad_ref[kvh, 0], lambda: vpad_ref[kvh, 1], ones_pad,
                             lambda: bias_ref[first_idx], lambda: bias_ref[0],
                             sink_ref[2 * c] * LOG2E, sink_ref[2 * c + 1] * LOG2E,
                             qgain, ones_bd, lo128, store_att)

        swa = [slab_stages(i * SLABS_PER_ITER + s) for s in range(SLABS_PER_ITER)]

        _project_main(pc_ref, u_ref, wm_ref, i)
        _hgrn_chunks(load, store, r0s, c0, c1, og, lev_ref[...], st_ref, cv_ref, lambda: None)
        for stages in zip(*swa):
            for stage in stages:
                stage()
        return carry

    lax.fori_loop(0, MIX_ITERS, body, 0, unroll=MIX_UNROLL)


def _mix(x2, seq_len, gain, w_main, w_tail, lb_logits, out_gain, q_gain, k_gain, sinks,
         ffn_weights):
    t = x2.shape[0]
    nblk = t // MIX_ROWS
    assert seq_len // MIX_ROWS == SEQ_BLOCKS
    batch = t // seq_len
    assert tuple(w.shape for w in ffn_weights) == tuple(s[:2] for s in _CAST_SHAPES)
    assert all(-(-rows // per) <= nblk for rows, _, per in _CAST_SHAPES)
    hbm = pl.BlockSpec(memory_space=pl.ANY)

    def in_rows(gi):
        return (jnp.minimum(gi, nblk - 1), 0)

    def out_block(gi):
        blk_i = jnp.maximum(gi - 1, 0)
        return blk_i // SEQ_BLOCKS, lax.rem(blk_i, SEQ_BLOCKS)

    full = lambda shape: pl.BlockSpec(shape, lambda gi: (0,) * len(shape))
    return pl.pallas_call(
        _mix_kernel,
        grid=(nblk + 1,),
        in_specs=[
            pl.BlockSpec((MIX_ROWS, D_MODEL), in_rows),
            full((1, D_MODEL)),
            full((MIX_ITERS, D_MODEL, LOOP_SLABS * LANES)),
            full((D_MODEL, TAIL_SLABS * LANES)),
            full((2, HGRN_WIDTH)),
            full((1, HGRN_DV)),
            full((1, ATTN_HD)),
            full((1, ATTN_HD)),
            pl.BlockSpec(memory_space=pltpu.SMEM),
        ] + [hbm] * N_CAST,
        out_specs=[
            pl.BlockSpec((1, MIX_ROWS, HGRN_WIDTH),
                         lambda gi: (*out_block(gi), 0)),
            pl.BlockSpec((1, ATT_SLABS, MIX_ROWS, LANES),
                         lambda gi: (out_block(gi)[0], 0, out_block(gi)[1], 0)),
        ] + [hbm] * N_CAST,
        out_shape=[
            jax.ShapeDtypeStruct((batch, seq_len, HGRN_WIDTH), BF16),
            jax.ShapeDtypeStruct((batch, ATT_SLABS, seq_len, LANES), BF16),
        ] + [jax.ShapeDtypeStruct(w.shape, BF16) for w in ffn_weights],
        scratch_shapes=[
            pltpu.VMEM((MIX_ROWS, D_MODEL), BF16),
            pltpu.VMEM((N_SLABS, MIX_ROWS, LANES), F32),
            pltpu.VMEM((N_SLABS, MIX_ROWS, LANES), F32),
            pltpu.VMEM((HGRN_HEADS, HGRN_DV, HGRN_DK), F32),
            pltpu.VMEM((HGRN_HEADS * HGRN_UNROLL, 3, HGRN_CHUNK, HGRN_DK), F32),
            pltpu.VMEM((2, ATTN_BLOCK, LANES), F32),
            pltpu.VMEM((ATTN_KV_HEADS, 2, ATTN_BLOCK + MIX_ROWS, LANES), BF16),
            pltpu.VMEM((ATTN_KV_HEADS, 2, ATTN_BLOCK + MIX_ROWS, LANES), BF16),
            pltpu.VMEM((2, ATTN_BLOCK, 4 * ATTN_BLOCK), F32),
            pltpu.VMEM((LEVEL_ROWS, LEVEL_ROWS), jnp.int32),
        ] + [pltpu.VMEM((per, cols), F32) for _, cols, per in _CAST_SHAPES]
        + [pltpu.VMEM((per, cols), BF16) for _, cols, per in _CAST_SHAPES]
        + [pltpu.SemaphoreType.DMA((N_CAST,)), pltpu.SemaphoreType.DMA((N_CAST,))],
        compiler_params=pltpu.CompilerParams(
            dimension_semantics=("arbitrary",), vmem_limit_bytes=VMEM_LIMIT_BYTES),
        name="mixer",
    )(x2, gain, w_main, w_tail, lb_logits, out_gain, q_gain, k_gain, sinks, *ffn_weights)


def _ffn_kernel(x_ref, rec_ref, att_ref, wo_ref, gain_ref, wg_ref, wu_ref, wd_ref, o_ref):
    mixed = jnp.concatenate([rec_ref[0]] + [att_ref[0, c] for c in range(ATT_SLABS)],
                            axis=1)
    h = x_ref[...] + jnp.dot(mixed, wo_ref[...], preferred_element_type=F32)
    ms = jnp.mean(h * h, axis=-1, keepdims=True)
    u = (h * lax.rsqrt(ms + EPS) * gain_ref[...]).astype(BF16)
    gate = jnp.dot(u, wg_ref[...], preferred_element_type=F32)
    up = jnp.dot(u, wu_ref[...], preferred_element_type=F32)
    gs = 0.5 * gate
    act = ((gs * jnp.tanh(gs) + gs) * up).astype(BF16)
    o_ref[...] = h + jnp.dot(act, wd_ref[...], preferred_element_type=F32)


def _ffn(x2, rec, att, wo, gain, wg, wu, wd):
    t = x2.shape[0]
    seq_blocks = rec.shape[1] // FFN_ROWS

    def resident(shape):
        return pl.BlockSpec(shape, lambda i: (0, 0), pipeline_mode=pl.Buffered(1))

    return pl.pallas_call(
        _ffn_kernel,
        grid=(t // FFN_ROWS,),
        in_specs=[
            pl.BlockSpec((FFN_ROWS, D_MODEL), lambda i: (i, 0)),
            pl.BlockSpec((1, FFN_ROWS, HGRN_WIDTH),
                         lambda i: (i // seq_blocks, lax.rem(i, seq_blocks), 0)),
            pl.BlockSpec((1, ATT_SLABS, FFN_ROWS, LANES),
                         lambda i: (i // seq_blocks, 0, lax.rem(i, seq_blocks), 0)),
            resident((D_MODEL, D_MODEL)),
            resident((1, D_MODEL)),
            resident((D_MODEL, D_FF)),
            resident((D_MODEL, D_FF)),
            resident((D_FF, D_MODEL)),
        ],
        out_specs=pl.BlockSpec((FFN_ROWS, D_MODEL), lambda i: (i, 0)),
        out_shape=jax.ShapeDtypeStruct((t, D_MODEL), F32),
        compiler_params=pltpu.CompilerParams(
            dimension_semantics=("arbitrary",), vmem_limit_bytes=VMEM_LIMIT_BYTES),
        name="outproj_ffn",
    )(x2, rec, att, wo, gain, wg, wu, wd)


def kernel(x, norm1_gain, w_in, hgrn_lb_logits, hgrn_out_gain, q_norm_gain, k_norm_gain,
           attn_sinks, w_out, norm2_gain, w_ffn_gate, w_ffn_up, w_ffn_down):
    b, s, d = x.shape
    assert (d, w_in.shape[0]) == (D_MODEL, 1), "single-layer kernel"
    assert s == SEQ_BLOCKS * MIX_ROWS and MIX_ROWS == FFN_ROWS
    t = b * s
    x2 = x.reshape(t, d)

    w_bf = w_in[0].astype(BF16)
    n_main = MIX_ITERS * LOOP_SLABS * LANES
    w_main = w_bf[:, :n_main].reshape(D_MODEL, MIX_ITERS, LOOP_SLABS * LANES).transpose(1, 0, 2)
    w_tail = w_bf[:, n_main:]
    rec, att, wo, wg, wu, wd = _mix(
        x2, s, norm1_gain[0][None, :], w_main, w_tail, hgrn_lb_logits,
        hgrn_out_gain[0][None, :], q_norm_gain[0][None, :], k_norm_gain[0][None, :],
        attn_sinks[0], (w_out[0], w_ffn_gate[0], w_ffn_up[0], w_ffn_down[0]))
    out = _ffn(x2, rec, att, wo, norm2_gain[0][None, :], wg, wu, wd)
    return out.reshape(b, s, d)
```

```python
import jax
import jax.numpy as jnp
from jax import lax
from jax.experimental import pallas as pl
from jax.experimental.pallas import tpu as pltpu

D_MODEL = 1024
HGRN_HEADS = 4
HGRN_DK = 128
HGRN_DV = 128
HGRN_WIDTH = HGRN_HEADS * HGRN_DK
ATTN_HEADS = 8
ATTN_KV_HEADS = 2
ATTN_GROUP = ATTN_HEADS // ATTN_KV_HEADS
ATTN_HD = 64
ATTN_WIDTH = ATTN_HEADS * ATTN_HD
KV_WIDTH = ATTN_KV_HEADS * ATTN_HD
WINDOW = 128
PROJ_WIDTH = 4 * HGRN_WIDTH + ATTN_WIDTH + 2 * KV_WIDTH
D_FF = 2816
EPS = 1e-6
NEG_INF = -1e30
LOG2E = 1.4426950408889634

F32 = jnp.float32
BF16 = jnp.bfloat16

LANES = 128
VMEM_LIMIT_BYTES = 56 * 1024 * 1024

HGRN_CHUNK = 64
HGRN_LEVELS = 6
HGRN_UNROLL = 2
ATTN_BLOCK = 128
MIX_ROWS = 512
MIX_ITERS = 4
MIX_UNROLL = 2
PROJ_PIECE_ROWS = 512
SEQ_BLOCKS = 4
FFN_ROWS = 512

N_SLABS = PROJ_WIDTH // LANES
ATT_SLABS = ATTN_WIDTH // LANES
SLABS_PER_ITER = ATT_SLABS // MIX_ITERS
Q_SLAB = 4 * HGRN_HEADS
K_SLAB = Q_SLAB + ATT_SLABS
V_SLAB = K_SLAB + 1
LOOP_SLABS = Q_SLAB // MIX_ITERS
TAIL_SLABS = N_SLABS - MIX_ITERS * LOOP_SLABS
assert MIX_ITERS * LOOP_SLABS == Q_SLAB and V_SLAB == N_SLABS - 1 and LOOP_SLABS % 2 == 0
assert MIX_ROWS == MIX_ITERS * HGRN_UNROLL * HGRN_CHUNK
assert ATT_SLABS == MIX_ITERS * SLABS_PER_ITER

_NT = (((1,), (1,)), ((), ()))
_TN = (((0,), (0,)), ((), ()))


_LEVEL_Q = ((1, 3, 5, 7), (2, 3, 6, 7), (4, 5, 6, 7))
_LEVEL_K = ((0, 2, 4, 6), (0, 1, 4, 5), (0, 1, 2, 3))
LEVEL_ROWS = 8 * sum(len(q) for q in _LEVEL_Q)


def _level_mask():
    r = lax.broadcasted_iota(jnp.int32, (LEVEL_ROWS, LEVEL_ROWS), 0)
    c = lax.broadcasted_iota(jnp.int32, (LEVEL_ROWS, LEVEL_ROWS), 1)
    lev_r, lev_c = r >> 5, c >> 5
    j_r, j_c = (r >> 3) & 3, (c >> 3) & 3
    same_block = (((lev_r == 0) & (j_r == j_c))
                  | ((lev_r == 1) & ((j_r >> 1) == (j_c >> 1)))
                  | (lev_r == 2))
    return ((lev_r == lev_c) & same_block).astype(jnp.int32)


def _mul1(a, b):
    if a is None:
        return b
    if b is None:
        return a
    return a * b


def _pair_rms(x, ones_bd):
    ss = jnp.dot((x * x).astype(BF16), ones_bd, preferred_element_type=F32)
    return x * lax.rsqrt(ss * (1.0 / ATTN_HD) + EPS)


def _hgrn_chunks(load, store, r0s, c0, c1, og, mask_nat, st_ref, cv_ref, fill):
    nh = HGRN_HEADS
    c_rows = HGRN_CHUNK
    nv = c_rows // 8
    assert nv == 8 and HGRN_LEVELS == 6

    def bc(t, r):
        return jnp.broadcast_to(t[r:r + 1, :], (8, HGRN_DK))

    def cat16(pieces):
        return jnp.concatenate(pieces, axis=0).astype(BF16)

    q_order = _LEVEL_Q[0] + _LEVEL_Q[1] + _LEVEL_Q[2]
    k_order = _LEVEL_K[0] + _LEVEL_K[1] + _LEVEL_K[2]

    def stacked_attend(q_pieces, k_pieces, v_pieces, mask):
        s = [lax.dot_general(cat16(q_pieces[u]), cat16(k_pieces[u]), _NT,
                             preferred_element_type=F32) for u in units]
        fill()
        s = [jnp.where(mask, s[u], 0.0).astype(BF16) for u in units]
        o = [jnp.dot(s[u], cat16([v_pieces[u][j] for j in k_order]),
                     preferred_element_type=F32) for u in units]
        fill()
        return [[o[u][8 * n:8 * n + 8, :] for n in range(len(q_order))] for u in units]

    def scatter_add(acc, pieces):
        for n, j in enumerate(q_order):
            acc[j] = pieces[n] if acc[j] is None else acc[j] + pieces[n]
        return acc

    m_nat = mask_nat != 0

    units = range(nh * len(r0s))
    hd = [u % nh for u in units]
    r0 = [r0s[u // nh] for u in units]
    rows = [pl.ds(r0[u], c_rows) for u in units]
    lanes = [slice(hd[u] * HGRN_DK, (hd[u] + 1) * HGRN_DK) for u in units]
    vrange = range(nv)

    fp, kp, qp, vtr = [], [], [], []
    for u in units:
        c0h, c1h = c0[:, lanes[u]], c1[:, lanes[u]]
        fh, kh, qh, vh = [], [], [], []
        for b in vrange:
            srows = pl.ds(r0[u] + b, 8, stride=8)
            ct = c1h * jnp.tanh(0.5 * load(1, hd[u], srows))
            fh.append(c0h + ct)
            kh.append(c1h - ct)
            xq = load(0, hd[u], srows)
            xs = xq * (0.5 * HGRN_DK ** -0.5)
            qh.append(xs * jnp.tanh(0.5 * xq) + xs)
            vh.append(load(2, hd[u], srows))
        fp.append(fh)
        kp.append(kh)
        qp.append(qh)
        vtr.append(vh)

    fill()
    o_tr = [[jnp.sum(qp[u][b] * kp[u][b], axis=-1, keepdims=True) * vtr[u][b] for b in vrange]
            for u in units]

    p2 = [[fp[u][b] * fp[u][b - 1] if b & 1 else fp[u][b] for b in vrange] for u in units]
    x2 = [[None if b & 1 else fp[u][b + 1] for b in vrange] for u in units]
    p4 = [[p2[u][b] * p2[u][(b & ~3) + 1] if b & 2 else p2[u][b] for b in vrange] for u in units]
    x4 = [[x2[u][b] if b & 2 else _mul1(x2[u][b], p2[u][(b & ~3) + 3]) for b in vrange]
          for u in units]
    p8 = [[p4[u][b] * p4[u][3] if b & 4 else p4[u][b] for b in vrange] for u in units]
    x8 = [[x4[u][b] if b & 4 else _mul1(x4[u][b], p4[u][7]) for b in vrange] for u in units]
    tot8 = [p8[u][7] for u in units]

    for u in units:
        for b in vrange:
            srows = pl.ds(b, 8, stride=8)
            cv_ref[u, 0, srows, :] = qp[u][b] * p8[u][b]
            cv_ref[u, 1, srows, :] = _mul1(kp[u][b], x8[u][b])

    fill()
    q_pieces = [([qp[u][b] * fp[u][b] for b in _LEVEL_Q[0]]
                 + [qp[u][b] * p2[u][b] for b in _LEVEL_Q[1]]
                 + [qp[u][b] * p4[u][b] for b in _LEVEL_Q[2]]) for u in units]
    k_pieces = [([kp[u][b] for b in _LEVEL_K[0]]
                 + [_mul1(kp[u][b], x2[u][b]) for b in _LEVEL_K[1]]
                 + [_mul1(kp[u][b], x4[u][b]) for b in _LEVEL_K[2]]) for u in units]
    for lvl in range(3):
        group = 1 << lvl
        for jq, b_q in enumerate(_LEVEL_Q[lvl]):
            for jk, b_k in enumerate(_LEVEL_K[lvl]):
                if jq // group != jk // group:
                    continue
                nq, nk = 4 * lvl + jq, 4 * lvl + jk
                for u in units:
                    d = jnp.sum(q_pieces[u][nq] * k_pieces[u][nk], axis=-1, keepdims=True)
                    o_tr[u][b_q] = o_tr[u][b_q] + d * vtr[u][b_k]

    an = [[cv_ref[u, 0, 8 * a:8 * a + 8, :] for a in vrange] for u in units]
    cn = [[cv_ref[u, 1, 8 * a:8 * a + 8, :] for a in vrange] for u in units]
    vnat = [[load(2, hd[u], pl.ds(r0[u] + 8 * a, 8)) for a in vrange] for u in units]

    q4 = [[an[u][a] for a in _LEVEL_Q[0]] for u in units]
    k4 = [[cn[u][a] for a in _LEVEL_K[0]] for u in units]
    an = [[an[u][a] * bc(tot8[u], a - 1) if a & 1 else an[u][a] for a in vrange] for u in units]
    cn = [[cn[u][a] if a & 1 else cn[u][a] * bc(tot8[u], a + 1) for a in vrange] for u in units]
    t16 = [tot8[u] * pltpu.roll(tot8[u], 1, 0) for u in units]
    q5 = [[an[u][a] for a in _LEVEL_Q[1]] for u in units]
    k5 = [[cn[u][a] for a in _LEVEL_K[1]] for u in units]
    an = [[an[u][a] * bc(t16[u], (a & ~3) + 1) if a & 2 else an[u][a] for a in vrange]
          for u in units]
    cn = [[cn[u][a] if a & 2 else cn[u][a] * bc(t16[u], (a & ~3) + 3) for a in vrange]
          for u in units]
    t32 = [t16[u] * pltpu.roll(t16[u], 2, 0) for u in units]
    q6 = [[an[u][a] for a in _LEVEL_Q[2]] for u in units]
    k6 = [[cn[u][a] for a in _LEVEL_K[2]] for u in units]
    an = [[an[u][a] * bc(t32[u], 3) if a & 4 else an[u][a] for a in vrange] for u in units]
    cn = [[cn[u][a] if a & 4 else cn[u][a] * bc(t32[u], 7) for a in vrange] for u in units]
    att_nat = stacked_attend([q4[u] + q5[u] + q6[u] for u in units],
                             [k4[u] + k5[u] + k6[u] for u in units], vnat, m_nat)
    o_nat = []
    for u in units:
        pieces = scatter_add([None] * nv, att_nat[u])
        pieces[0] = jnp.zeros((8, HGRN_DV), F32)
        o_nat.append(jnp.concatenate(pieces, axis=0))

    for u in units:
        for b in vrange:
            cv_ref[u, 2, pl.ds(b, 8, stride=8), :] = o_tr[u][b]

    fill()
    upd = [lax.dot_general(cat16(vnat[u]), cat16(cn[u]),
                           _TN, preferred_element_type=F32) for u in units]
    for h in range(nh):
        st = st_ref[h]
        for u in range(h, len(units), nh):
            qs = cat16(an[u])
            o_nat[u] = o_nat[u] + lax.dot_general(qs, st.astype(BF16), _NT,
                                                  preferred_element_type=F32)
            dec = t32[u][3:4, :] * t32[u][7:8, :]
            st = st * dec + upd[u]
        st_ref[h] = st

    fill()
    for u in units:
        o_h = o_nat[u] + cv_ref[u, 2]
        ms = jnp.mean(o_h * o_h, axis=-1, keepdims=True)
        xgs = 0.5 * load(3, hd[u], rows[u])
        gate = xgs * jnp.tanh(xgs) + xgs
        store(hd[u], rows[u], o_h * lax.rsqrt(ms + EPS) * og * gate)


def _swa_slab(q_slab, k_lo, k_hi, v_lo, v_hi, ones_pad, bias_first, bias_any,
              sink_a, sink_b, qgain, ones_bd, lo128, store):
    blk = ATTN_BLOCK
    nblk = MIX_ROWS // blk
    state = {}

    def prep():
        state["qs"] = (_pair_rms(q_slab(), ones_bd) * qgain).astype(BF16)
        state["sink_slab"] = jnp.where(lo128, sink_a, sink_b)

    def scores(n):
        keys = slice(n * blk, (n + 2) * blk)
        k_cat = jnp.concatenate([k_lo()[keys], k_hi()[keys]], axis=0)
        s = lax.dot_general(state["qs"][n * blk:(n + 1) * blk], k_cat, _NT,
                            preferred_element_type=F32)
        s = s + (bias_first() if n == 0 else bias_any())
        m_a = jnp.max(s[:, :2 * blk], axis=-1, keepdims=True)
        m_b = jnp.max(s[:, 2 * blk:], axis=-1, keepdims=True)
        m_a = jnp.maximum(jnp.broadcast_to(m_a, (blk, LANES)), sink_a)
        m_b = jnp.maximum(jnp.broadcast_to(m_b, (blk, LANES)), sink_b)
        cols = [s[:, c * LANES:(c + 1) * LANES] - (m_a if c < 2 * blk // LANES else m_b)
                for c in range(4 * blk // LANES)]
        state["p", n] = jnp.exp2(jnp.concatenate(cols, axis=1)).astype(BF16)
        state["m", n] = jnp.where(lo128, m_a, m_b)

    def attend(n):
        keys = slice(n * blk, (n + 2) * blk)
        v_cat = jnp.concatenate([v_lo()[keys], v_hi()[keys]], axis=0)
        rhs = jnp.concatenate([v_cat, ones_pad], axis=1)
        nd = jnp.dot(state.pop(("p", n)), rhs, preferred_element_type=F32)
        den = nd[:, LANES:] + jnp.exp2(state["sink_slab"] - state.pop(("m", n)))
        store(n, nd[:, :LANES] / den)

    stages = [prep]
    for n in range(nblk):
        stages += [lambda n=n: scores(n), lambda n=n: attend(n)]
    return stages


_CAST_SHAPES = ((D_MODEL, D_MODEL, 32), (D_MODEL, D_FF, 32), (D_MODEL, D_FF, 32), (D_FF, D_MODEL, 96))
N_CAST = len(_CAST_SHAPES)


def _cast_chunk(k, step):
    rows, _, per = _CAST_SHAPES[k]
    n = -(-rows // per)
    start = jnp.minimum(step * per, rows - per)
    return n, pl.multiple_of(start, 16)


def _cast_copies(k, step, src_refs, dst_refs, fbuf_refs, bbuf_refs, sem_in, sem_out):
    per = _CAST_SHAPES[k][2]
    _, start = _cast_chunk(k, step)
    rows = pl.ds(start, per)
    return (pltpu.make_async_copy(src_refs[k].at[rows, :], fbuf_refs[k], sem_in.at[k]),
            pltpu.make_async_copy(bbuf_refs[k], dst_refs[k].at[rows, :], sem_out.at[k]))


def _mix_kernel(x_ref, gain_ref, wm_ref, wt_ref, lbl_ref, og_ref, qg_ref, kg_ref, sink_ref,
                *refs):
    cast_src = refs[:N_CAST]
    rec_ref, att_ref = refs[N_CAST:N_CAST + 2]
    cast_dst = refs[N_CAST + 2:2 * N_CAST + 2]
    (u_ref, pa_ref, pb_ref, st_ref, cv_ref, kvp_ref, kpad_ref, vpad_ref, bias_ref,
     lev_ref) = refs[2 * N_CAST + 2:2 * N_CAST + 12]
    cast_f = refs[2 * N_CAST + 12:3 * N_CAST + 12]
    cast_b = refs[3 * N_CAST + 12:4 * N_CAST + 12]
    sem_in, sem_out = refs[4 * N_CAST + 12:]
    g = pl.program_id(0)
    blk = ATTN_BLOCK

    def cast_copies(k, step):
        return _cast_copies(k, step, cast_src, cast_dst, cast_f, cast_b, sem_in, sem_out)

    for k in range(N_CAST):
        @pl.when(g < _cast_chunk(k, g)[0])
        def _(k=k):
            cast_copies(k, g)[0].start()

    lo128 = lax.broadcasted_iota(jnp.int32, (1, LANES), 1) < ATTN_HD

    @pl.when(g == 0)
    def _():
        kvp_ref[...] = jnp.zeros_like(kvp_ref)
        lev_ref[...] = _level_mask()
        qi = lax.broadcasted_iota(jnp.int32, (blk, 2 * blk), 0)
        kj = lax.broadcasted_iota(jnp.int32, (blk, 2 * blk), 1)
        rel = qi + blk - kj
        in_window = (rel >= 0) & (rel < WINDOW)
        b_any = jnp.where(in_window, 0.0, NEG_INF)
        b_first = jnp.where(in_window & (kj >= blk), 0.0, NEG_INF)
        bias_ref[0] = jnp.concatenate([b_any, b_any], axis=1)
        bias_ref[1] = jnp.concatenate([b_first, b_first], axis=1)

    first_blk = lax.rem(g + SEQ_BLOCKS - 1, SEQ_BLOCKS) == 0

    @pl.when(first_blk)
    def _():
        st_ref[...] = jnp.zeros_like(st_ref)

    hr = lax.broadcasted_iota(jnp.int32, (LANES, LANES), 0) < ATTN_HD
    hc = lax.broadcasted_iota(jnp.int32, (LANES, LANES), 1) < ATTN_HD
    ones_bd = jnp.where(hr == hc, 1.0, 0.0).astype(BF16)
    pair = lambda r: jnp.concatenate([r, r], axis=1)
    qgain = pair(qg_ref[...]) * (ATTN_HD ** -0.5 * LOG2E)
    kgain = pair(kg_ref[...])
    ones_lo = jnp.broadcast_to(jnp.where(lo128, 1.0, 0.0), (2 * blk, LANES))
    ones_pad = jnp.concatenate([ones_lo, 1.0 - ones_lo], axis=0).astype(BF16)

    l0 = lbl_ref[0:1, :]
    l1 = lbl_ref[1:2, :]
    lmax = jnp.maximum(l0, l1)
    e0 = jnp.exp(l0 - lmax)
    e1 = jnp.exp(l1 - lmax)
    lb = e0 / (e0 + e1)
    c0 = 0.5 * (1.0 + lb)
    c1 = 0.5 * (1.0 - lb)
    og = og_ref[...]
    first_idx = first_blk.astype(jnp.int32)

    def step(pc_ref, pp_ref):
        _mix_step(pc_ref, pp_ref, x_ref, gain_ref, u_ref, wm_ref, wt_ref, rec_ref, att_ref,
                  st_ref, cv_ref, kvp_ref, kpad_ref, vpad_ref, bias_ref, lev_ref, sink_ref,
                  (lo128, ones_bd, qgain, kgain, ones_pad, c0, c1, og, first_idx))

    @pl.when(g == 0)
    def _():
        _project_only(pa_ref, x_ref, gain_ref, u_ref, wm_ref, wt_ref)

    @pl.when((lax.rem(g, 2) == 0) & (g > 0))
    def _():
        step(pa_ref, pb_ref)

    @pl.when(lax.rem(g, 2) == 1)
    def _():
        step(pb_ref, pa_ref)

    for k in range(N_CAST):
        n_chunks = _cast_chunk(k, g)[0]

        @pl.when((g >= 1) & (g <= n_chunks))
        def _(k=k):
            cast_copies(k, g - 1)[1].wait()

        @pl.when(g < n_chunks)
        def _(k=k):
            copy_in, copy_out = cast_copies(k, g)
            copy_in.wait()
            cast_b[k][...] = cast_f[k][...].astype(BF16)
            copy_out.start()


def _project_tail(pc_ref, x_ref, gain_ref, u_ref, wt_ref):
    x = x_ref[...]
    ms = jnp.mean(x * x, axis=-1, keepdims=True)
    u = (x * lax.rsqrt(ms + EPS) * gain_ref[...]).astype(BF16)
    u_ref[...] = u
    tail = jnp.dot(u, wt_ref[...], preferred_element_type=F32)
    for t in range(TAIL_SLABS):
        pc_ref[MIX_ITERS * LOOP_SLABS + t] = tail[:, t * LANES:(t + 1) * LANES]


def _project_main(pc_ref, u_ref, wm_ref, i):
    for r in range(0, MIX_ROWS, PROJ_PIECE_ROWS):
        res = jnp.dot(u_ref[r:r + PROJ_PIECE_ROWS, :], wm_ref[i], preferred_element_type=F32)
        for t in range(LOOP_SLABS):
            pc_ref[i * LOOP_SLABS + t, r:r + PROJ_PIECE_ROWS, :] = res[:, t * LANES:(t + 1) * LANES]


def _project_only(pc_ref, x_ref, gain_ref, u_ref, wm_ref, wt_ref):
    _project_tail(pc_ref, x_ref, gain_ref, u_ref, wt_ref)
    for i in range(MIX_ITERS):
        _project_main(pc_ref, u_ref, wm_ref, i)


def _mix_step(pc_ref, pp_ref, x_ref, gain_ref, u_ref, wm_ref, wt_ref, rec_ref, att_ref, st_ref,
              cv_ref, kvp_ref, kpad_ref, vpad_ref, bias_ref, lev_ref, sink_ref, consts):
    (lo128, ones_bd, qgain, kgain, ones_pad, c0, c1, og, first_idx) = consts
    blk = ATTN_BLOCK
    _project_tail(pc_ref, x_ref, gain_ref, u_ref, wt_ref)

    k_all = jnp.concatenate([kvp_ref[0], pp_ref[K_SLAB]], axis=0)
    v_all = jnp.concatenate([kvp_ref[1], pp_ref[V_SLAB]], axis=0)
    kvp_ref[0] = pp_ref[K_SLAB, MIX_ROWS - blk:, :]
    kvp_ref[1] = pp_ref[V_SLAB, MIX_ROWS - blk:, :]
    kn = _pair_rms(k_all, ones_bd) * kgain
    kn_sw = pltpu.roll(kn, ATTN_HD, 1)
    v_sw = pltpu.roll(v_all, ATTN_HD, 1)
    zero = jnp.zeros_like(kn)
    kpad_ref[0, 0] = jnp.where(lo128, kn, zero).astype(BF16)
    kpad_ref[0, 1] = jnp.where(lo128, zero, kn_sw).astype(BF16)
    kpad_ref[1, 0] = jnp.where(lo128, kn_sw, zero).astype(BF16)
    kpad_ref[1, 1] = jnp.where(lo128, zero, kn).astype(BF16)
    vpad_ref[0, 0] = jnp.where(lo128, v_all, zero).astype(BF16)
    vpad_ref[0, 1] = jnp.where(lo128, zero, v_sw).astype(BF16)
    vpad_ref[1, 0] = jnp.where(lo128, v_sw, zero).astype(BF16)
    vpad_ref[1, 1] = jnp.where(lo128, zero, v_all).astype(BF16)

    def body(i, carry):
        def load(part, h, rows):
            return pp_ref[part * HGRN_HEADS + h, rows, :]

        def store(h, rows, tile):
            rec_ref[0, rows, h * HGRN_DV:(h + 1) * HGRN_DV] = tile.astype(rec_ref.dtype)

        r0s = [pl.multiple_of((i * HGRN_UNROLL + cc) * HGRN_CHUNK, HGRN_CHUNK)
               for cc in range(HGRN_UNROLL)]

        def slab_stages(c):
            kvh = lax.div(c, ATTN_GROUP // 2)

            def store_att(n, tile):
                att_ref[0, c, n * blk:(n + 1) * blk, :] = tile.astype(att_ref.dtype)

            return _swa_slab(lambda: pp_ref[Q_SLAB + c],
                             lambda: kpad_ref[kvh, 0], lambda: kpad_ref[kvh, 1],
                             lambda: vpad_ref[kvh, 0], lambda: vpad_ref[kvh, 1], ones_pad,
                             lambda: bias_ref[first_idx], lambda: bias_ref[0],
                             sink_ref[2 * c] * LOG2E, sink_ref[2 * c + 1] * LOG2E,
                             qgain, ones_bd, lo128, store_att)

        swa = [slab_stages(i * SLABS_PER_ITER + s) for s in range(SLABS_PER_ITER)]

        _project_main(pc_ref, u_ref, wm_ref, i)
        _hgrn_chunks(load, store, r0s, c0, c1, og, lev_ref[...], st_ref, cv_ref, lambda: None)
        for stages in zip(*swa):
            for stage in stages:
                stage()
        return carry

    lax.fori_loop(0, MIX_ITERS, body, 0, unroll=MIX_UNROLL)


def _mix(x2, seq_len, gain, w_main, w_tail, lb_logits, out_gain, q_gain, k_gain, sinks,
         ffn_weights):
    t = x2.shape[0]
    nblk = t // MIX_ROWS
    assert seq_len // MIX_ROWS == SEQ_BLOCKS
    batch = t // seq_len
    assert tuple(w.shape for w in ffn_weights) == tuple(s[:2] for s in _CAST_SHAPES)
    assert all(-(-rows // per) <= nblk for rows, _, per in _CAST_SHAPES)
    hbm = pl.BlockSpec(memory_space=pl.ANY)

    def in_rows(gi):
        return (jnp.minimum(gi, nblk - 1), 0)

    def out_block(gi):
        blk_i = jnp.maximum(gi - 1, 0)
        return blk_i // SEQ_BLOCKS, lax.rem(blk_i, SEQ_BLOCKS)

    full = lambda shape: pl.BlockSpec(shape, lambda gi: (0,) * len(shape))
    return pl.pallas_call(
        _mix_kernel,
        grid=(nblk + 1,),
        in_specs=[
            pl.BlockSpec((MIX_ROWS, D_MODEL), in_rows),
            full((1, D_MODEL)),
            full((MIX_ITERS, D_MODEL, LOOP_SLABS * LANES)),
            full((D_MODEL, TAIL_SLABS * LANES)),
            full((2, HGRN_WIDTH)),
            full((1, HGRN_DV)),
            full((1, ATTN_HD)),
            full((1, ATTN_HD)),
            pl.BlockSpec(memory_space=pltpu.SMEM),
        ] + [hbm] * N_CAST,
        out_specs=[
            pl.BlockSpec((1, MIX_ROWS, HGRN_WIDTH),
                         lambda gi: (*out_block(gi), 0)),
            pl.BlockSpec((1, ATT_SLABS, MIX_ROWS, LANES),
                         lambda gi: (out_block(gi)[0], 0, out_block(gi)[1], 0)),
        ] + [hbm] * N_CAST,
        out_shape=[
            jax.ShapeDtypeStruct((batch, seq_len, HGRN_WIDTH), BF16),
            jax.ShapeDtypeStruct((batch, ATT_SLABS, seq_len, LANES), BF16),
        ] + [jax.ShapeDtypeStruct(w.shape, BF16) for w in ffn_weights],
        scratch_shapes=[
            pltpu.VMEM((MIX_ROWS, D_MODEL), BF16),
            pltpu.VMEM((N_SLABS, MIX_ROWS, LANES), F32),
            pltpu.VMEM((N_SLABS, MIX_ROWS, LANES), F32),
            pltpu.VMEM((HGRN_HEADS, HGRN_DV, HGRN_DK), F32),
            pltpu.VMEM((HGRN_HEADS * HGRN_UNROLL, 3, HGRN_CHUNK, HGRN_DK), F32),
            pltpu.VMEM((2, ATTN_BLOCK, LANES), F32),
            pltpu.VMEM((ATTN_KV_HEADS, 2, ATTN_BLOCK + MIX_ROWS, LANES), BF16),
            pltpu.VMEM((ATTN_KV_HEADS, 2, ATTN_BLOCK + MIX_ROWS, LANES), BF16),
            pltpu.VMEM((2, ATTN_BLOCK, 4 * ATTN_BLOCK), F32),
            pltpu.VMEM((LEVEL_ROWS, LEVEL_ROWS), jnp.int32),
        ] + [pltpu.VMEM((per, cols), F32) for _, cols, per in _CAST_SHAPES]
        + [pltpu.VMEM((per, cols), BF16) for _, cols, per in _CAST_SHAPES]
        + [pltpu.SemaphoreType.DMA((N_CAST,)), pltpu.SemaphoreType.DMA((N_CAST,))],
        compiler_params=pltpu.CompilerParams(
            dimension_semantics=("arbitrary",), vmem_limit_bytes=VMEM_LIMIT_BYTES),
        name="mixer",
    )(x2, gain, w_main, w_tail, lb_logits, out_gain, q_gain, k_gain, sinks, *ffn_weights)


def _ffn_kernel(x_ref, rec_ref, att_ref, wo_ref, gain_ref, wg_ref, wu_ref, wd_ref, o_ref):
    mixed = jnp.concatenate([rec_ref[0]] + [att_ref[0, c] for c in range(ATT_SLABS)],
                            axis=1)
    h = x_ref[...] + jnp.dot(mixed, wo_ref[...], preferred_element_type=F32)
    ms = jnp.mean(h * h, axis=-1, keepdims=True)
    u = (h * lax.rsqrt(ms + EPS) * gain_ref[...]).astype(BF16)
    gate = jnp.dot(u, wg_ref[...], preferred_element_type=F32)
    up = jnp.dot(u, wu_ref[...], preferred_element_type=F32)
    gs = 0.5 * gate
    act = ((gs * jnp.tanh(gs) + gs) * up).astype(BF16)
    o_ref[...] = h + jnp.dot(act, wd_ref[...], preferred_element_type=F32)


def _ffn(x2, rec, att, wo, gain, wg, wu, wd):
    t = x2.shape[0]
    seq_blocks = rec.shape[1] // FFN_ROWS

    def resident(shape):
        return pl.BlockSpec(shape, lambda i: (0, 0), pipeline_mode=pl.Buffered(1))

    return pl.pallas_call(
        _ffn_kernel,
        grid=(t // FFN_ROWS,),
        in_specs=[
            pl.BlockSpec((FFN_ROWS, D_MODEL), lambda i: (i, 0)),
            pl.BlockSpec((1, FFN_ROWS, HGRN_WIDTH),
                         lambda i: (i // seq_blocks, lax.rem(i, seq_blocks), 0)),
            pl.BlockSpec((1, ATT_SLABS, FFN_ROWS, LANES),
                         lambda i: (i // seq_blocks, 0, lax.rem(i, seq_blocks), 0)),
            resident((D_MODEL, D_MODEL)),
            resident((1, D_MODEL)),
            resident((D_MODEL, D_FF)),
            resident((D_MODEL, D_FF)),
            resident((D_FF, D_MODEL)),
        ],
        out_specs=pl.BlockSpec((FFN_ROWS, D_MODEL), lambda i: (i, 0)),
        out_shape=jax.ShapeDtypeStruct((t, D_MODEL), F32),
        compiler_params=pltpu.CompilerParams(
            dimension_semantics=("arbitrary",), vmem_limit_bytes=VMEM_LIMIT_BYTES),
        name="outproj_ffn",
    )(x2, rec, att, wo, gain, wg, wu, wd)


def kernel(x, norm1_gain, w_in, hgrn_lb_logits, hgrn_out_gain, q_norm_gain, k_norm_gain,
           attn_sinks, w_out, norm2_gain, w_ffn_gate, w_ffn_up, w_ffn_down):
    b, s, d = x.shape
    assert (d, w_in.shape[0]) == (D_MODEL, 1), "single-layer kernel"
    assert s == SEQ_BLOCKS * MIX_ROWS and MIX_ROWS == FFN_ROWS
    t = b * s
    x2 = x.reshape(t, d)

    w_bf = w_in[0].astype(BF16)
    n_main = MIX_ITERS * LOOP_SLABS * LANES
    w_main = w_bf[:, :n_main].reshape(D_MODEL, MIX_ITERS, LOOP_SLABS * LANES).transpose(1, 0, 2)
    w_tail = w_bf[:, n_main:]
    rec, att, wo, wg, wu, wd = _mix(
        x2, s, norm1_gain[0][None, :], w_main, w_tail, hgrn_lb_logits,
        hgrn_out_gain[0][None, :], q_norm_gain[0][None, :], k_norm_gain[0][None, :],
        attn_sinks[0], (w_out[0], w_ffn_gate[0], w_ffn_up[0], w_ffn_down[0]))
    out = _ffn(x2, rec, att, wo, norm2_gain[0][None, :], wg, wu, wd)
    return out.reshape(b, s, d)
```

```python
import jax
import jax.numpy as jnp
from jax import lax
from jax.experimental import pallas as pl
from jax.experimental.pallas import tpu as pltpu

D_MODEL = 1024
HGRN_HEADS = 4
HGRN_DK = 128
HGRN_DV = 128
HGRN_WIDTH = HGRN_HEADS * HGRN_DK
ATTN_HEADS = 8
ATTN_KV_HEADS = 2
ATTN_GROUP = ATTN_HEADS // ATTN_KV_HEADS
ATTN_HD = 64
ATTN_WIDTH = ATTN_HEADS * ATTN_HD
KV_WIDTH = ATTN_KV_HEADS * ATTN_HD
WINDOW = 128
PROJ_WIDTH = 4 * HGRN_WIDTH + ATTN_WIDTH + 2 * KV_WIDTH
D_FF = 2816
EPS = 1e-6
NEG_INF = -1e30
LOG2E = 1.4426950408889634

F32 = jnp.float32
BF16 = jnp.bfloat16

LANES = 128
VMEM_LIMIT_BYTES = 56 * 1024 * 1024

HGRN_CHUNK = 64
HGRN_LEVELS = 6
HGRN_UNROLL = 2
ATTN_BLOCK = 128
MIX_ROWS = 512
MIX_ITERS = 4
MIX_UNROLL = 4
PROJ_PIECE_ROWS = 512
SEQ_BLOCKS = 4
FFN_ROWS = 512

N_SLABS = PROJ_WIDTH // LANES
ATT_SLABS = ATTN_WIDTH // LANES
SLABS_PER_ITER = ATT_SLABS // MIX_ITERS
Q_SLAB = 4 * HGRN_HEADS
K_SLAB = Q_SLAB + ATT_SLABS
V_SLAB = K_SLAB + 1
LOOP_SLABS = Q_SLAB // MIX_ITERS
TAIL_SLABS = N_SLABS - MIX_ITERS * LOOP_SLABS
assert MIX_ITERS * LOOP_SLABS == Q_SLAB and V_SLAB == N_SLABS - 1 and LOOP_SLABS % 2 == 0
assert MIX_ROWS == MIX_ITERS * HGRN_UNROLL * HGRN_CHUNK
assert ATT_SLABS == MIX_ITERS * SLABS_PER_ITER

_NT = (((1,), (1,)), ((), ()))
_TN = (((0,), (0,)), ((), ()))


_LEVEL_Q = ((1, 3, 5, 7), (2, 3, 6, 7), (4, 5, 6, 7))
_LEVEL_K = ((0, 2, 4, 6), (0, 1, 4, 5), (0, 1, 2, 3))
LEVEL_ROWS = 8 * sum(len(q) for q in _LEVEL_Q)


def _level_mask():
    r = lax.broadcasted_iota(jnp.int32, (LEVEL_ROWS, LEVEL_ROWS), 0)
    c = lax.broadcasted_iota(jnp.int32, (LEVEL_ROWS, LEVEL_ROWS), 1)
    lev_r, lev_c = r >> 5, c >> 5
    j_r, j_c = (r >> 3) & 3, (c >> 3) & 3
    same_block = (((lev_r == 0) & (j_r == j_c))
                  | ((lev_r == 1) & ((j_r >> 1) == (j_c >> 1)))
                  | (lev_r == 2))
    return ((lev_r == lev_c) & same_block).astype(jnp.int32)


def _mul1(a, b):
    if a is None:
        return b
    if b is None:
        return a
    return a * b


def _pair_rms(x, ones_bd):
    ss = jnp.dot((x * x).astype(BF16), ones_bd, preferred_element_type=F32)
    return x * lax.rsqrt(ss * (1.0 / ATTN_HD) + EPS)


def _hgrn_chunks(load, store, r0s, c0, c1, og, mask_nat, st_ref, cv_ref, fill):
    nh = HGRN_HEADS
    c_rows = HGRN_CHUNK
    nv = c_rows // 8
    assert nv == 8 and HGRN_LEVELS == 6

    def bc(t, r):
        return jnp.broadcast_to(t[r:r + 1, :], (8, HGRN_DK))

    def cat16(pieces):
        return jnp.concatenate(pieces, axis=0).astype(BF16)

    q_order = _LEVEL_Q[0] + _LEVEL_Q[1] + _LEVEL_Q[2]
    k_order = _LEVEL_K[0] + _LEVEL_K[1] + _LEVEL_K[2]

    def stacked_attend(q_pieces, k_pieces, v_pieces, mask):
        s = [lax.dot_general(cat16(q_pieces[u]), cat16(k_pieces[u]), _NT,
                             preferred_element_type=F32) for u in units]
        fill()
        s = [jnp.where(mask, s[u], 0.0).astype(BF16) for u in units]
        o = [jnp.dot(s[u], cat16([v_pieces[u][j] for j in k_order]),
                     preferred_element_type=F32) for u in units]
        fill()
        return [[o[u][8 * n:8 * n + 8, :] for n in range(len(q_order))] for u in units]

    def scatter_add(acc, pieces):
        for n, j in enumerate(q_order):
            acc[j] = pieces[n] if acc[j] is None else acc[j] + pieces[n]
        return acc

    m_nat = mask_nat != 0

    units = range(nh * len(r0s))
    hd = [u % nh for u in units]
    r0 = [r0s[u // nh] for u in units]
    rows = [pl.ds(r0[u], c_rows) for u in units]
    lanes = [slice(hd[u] * HGRN_DK, (hd[u] + 1) * HGRN_DK) for u in units]
    vrange = range(nv)

    fp, kp, qp, vtr = [], [], [], []
    for u in units:
        c0h, c1h = c0[:, lanes[u]], c1[:, lanes[u]]
        fh, kh, qh, vh = [], [], [], []
        for b in vrange:
            srows = pl.ds(r0[u] + b, 8, stride=8)
            ct = c1h * jnp.tanh(0.5 * load(1, hd[u], srows))
            fh.append(c0h + ct)
            kh.append(c1h - ct)
            xq = load(0, hd[u], srows)
            xs = xq * (0.5 * HGRN_DK ** -0.5)
            qh.append(xs * jnp.tanh(0.5 * xq) + xs)
            vh.append(load(2, hd[u], srows))
        fp.append(fh)
        kp.append(kh)
        qp.append(qh)
        vtr.append(vh)

    fill()
    o_tr = [[jnp.sum(qp[u][b] * kp[u][b], axis=-1, keepdims=True) * vtr[u][b] for b in vrange]
            for u in units]

    p2 = [[fp[u][b] * fp[u][b - 1] if b & 1 else fp[u][b] for b in vrange] for u in units]
    x2 = [[None if b & 1 else fp[u][b + 1] for b in vrange] for u in units]
    p4 = [[p2[u][b] * p2[u][(b & ~3) + 1] if b & 2 else p2[u][b] for b in vrange] for u in units]
    x4 = [[x2[u][b] if b & 2 else _mul1(x2[u][b], p2[u][(b & ~3) + 3]) for b in vrange]
          for u in units]
    p8 = [[p4[u][b] * p4[u][3] if b & 4 else p4[u][b] for b in vrange] for u in units]
    x8 = [[x4[u][b] if b & 4 else _mul1(x4[u][b], p4[u][7]) for b in vrange] for u in units]
    tot8 = [p8[u][7] for u in units]

    for u in units:
        for b in vrange:
            srows = pl.ds(b, 8, stride=8)
            cv_ref[u, 0, srows, :] = qp[u][b] * p8[u][b]
            cv_ref[u, 1, srows, :] = _mul1(kp[u][b], x8[u][b])

    fill()
    q_pieces = [([qp[u][b] * fp[u][b] for b in _LEVEL_Q[0]]
                 + [qp[u][b] * p2[u][b] for b in _LEVEL_Q[1]]
                 + [qp[u][b] * p4[u][b] for b in _LEVEL_Q[2]]) for u in units]
    k_pieces = [([kp[u][b] for b in _LEVEL_K[0]]
                 + [_mul1(kp[u][b], x2[u][b]) for b in _LEVEL_K[1]]
                 + [_mul1(kp[u][b], x4[u][b]) for b in _LEVEL_K[2]]) for u in units]
    for lvl in range(3):
        group = 1 << lvl
        for jq, b_q in enumerate(_LEVEL_Q[lvl]):
            for jk, b_k in enumerate(_LEVEL_K[lvl]):
                if jq // group != jk // group:
                    continue
                nq, nk = 4 * lvl + jq, 4 * lvl + jk
                for u in units:
                    d = jnp.sum(q_pieces[u][nq] * k_pieces[u][nk], axis=-1, keepdims=True)
                    o_tr[u][b_q] = o_tr[u][b_q] + d * vtr[u][b_k]

    an = [[cv_ref[u, 0, 8 * a:8 * a + 8, :] for a in vrange] for u in units]
    cn = [[cv_ref[u, 1, 8 * a:8 * a + 8, :] for a in vrange] for u in units]
    vnat = [[load(2, hd[u], pl.ds(r0[u] + 8 * a, 8)) for a in vrange] for u in units]

    q4 = [[an[u][a] for a in _LEVEL_Q[0]] for u in units]
    k4 = [[cn[u][a] for a in _LEVEL_K[0]] for u in units]
    an = [[an[u][a] * bc(tot8[u], a - 1) if a & 1 else an[u][a] for a in vrange] for u in units]
    cn = [[cn[u][a] if a & 1 else cn[u][a] * bc(tot8[u], a + 1) for a in vrange] for u in units]
    t16 = [tot8[u] * pltpu.roll(tot8[u], 1, 0) for u in units]
    q5 = [[an[u][a] for a in _LEVEL_Q[1]] for u in units]
    k5 = [[cn[u][a] for a in _LEVEL_K[1]] for u in units]
    an = [[an[u][a] * bc(t16[u], (a & ~3) + 1) if a & 2 else an[u][a] for a in vrange]
          for u in units]
    cn = [[cn[u][a] if a & 2 else cn[u][a] * bc(t16[u], (a & ~3) + 3) for a in vrange]
          for u in units]
    t32 = [t16[u] * pltpu.roll(t16[u], 2, 0) for u in units]
    q6 = [[an[u][a] for a in _LEVEL_Q[2]] for u in units]
    k6 = [[cn[u][a] for a in _LEVEL_K[2]] for u in units]
    an = [[an[u][a] * bc(t32[u], 3) if a & 4 else an[u][a] for a in vrange] for u in units]
    cn = [[cn[u][a] if a & 4 else cn[u][a] * bc(t32[u], 7) for a in vrange] for u in units]
    att_nat = stacked_attend([q4[u] + q5[u] + q6[u] for u in units],
                             [k4[u] + k5[u] + k6[u] for u in units], vnat, m_nat)
    o_nat = []
    for u in units:
        pieces = scatter_add([None] * nv, att_nat[u])
        pieces[0] = jnp.zeros((8, HGRN_DV), F32)
        o_nat.append(jnp.concatenate(pieces, axis=0))

    for u in units:
        for b in vrange:
            cv_ref[u, 2, pl.ds(b, 8, stride=8), :] = o_tr[u][b]

    fill()
    upd = [lax.dot_general(cat16(vnat[u]), cat16(cn[u]),
                           _TN, preferred_element_type=F32) for u in units]
    for h in range(nh):
        st = st_ref[h]
        for u in range(h, len(units), nh):
            qs = cat16(an[u])
            o_nat[u] = o_nat[u] + lax.dot_general(qs, st.astype(BF16), _NT,
                                                  preferred_element_type=F32)
            dec = t32[u][3:4, :] * t32[u][7:8, :]
            st = st * dec + upd[u]
        st_ref[h] = st

    fill()
    for u in units:
        o_h = o_nat[u] + cv_ref[u, 2]
        ms = jnp.mean(o_h * o_h, axis=-1, keepdims=True)
        xgs = 0.5 * load(3, hd[u], rows[u])
        gate = xgs * jnp.tanh(xgs) + xgs
        store(hd[u], rows[u], o_h * lax.rsqrt(ms + EPS) * og * gate)


def _swa_slab(q_slab, k_lo, k_hi, v_lo, v_hi, ones_pad, bias_first, bias_any,
              sink_a, sink_b, qgain, ones_bd, lo128, store):
    blk = ATTN_BLOCK
    nblk = MIX_ROWS // blk
    state = {}

    def prep():
        state["qs"] = (_pair_rms(q_slab(), ones_bd) * qgain).astype(BF16)
        state["sink_slab"] = jnp.where(lo128, sink_a, sink_b)

    def scores(n):
        keys = slice(n * blk, (n + 2) * blk)
        k_cat = jnp.concatenate([k_lo()[keys], k_hi()[keys]], axis=0)
        s = lax.dot_general(state["qs"][n * blk:(n + 1) * blk], k_cat, _NT,
                            preferred_element_type=F32)
        s = s + (bias_first() if n == 0 else bias_any())
        m_a = jnp.max(s[:, :2 * blk], axis=-1, keepdims=True)
        m_b = jnp.max(s[:, 2 * blk:], axis=-1, keepdims=True)
        m_a = jnp.maximum(jnp.broadcast_to(m_a, (blk, LANES)), sink_a)
        m_b = jnp.maximum(jnp.broadcast_to(m_b, (blk, LANES)), sink_b)
        cols = [s[:, c * LANES:(c + 1) * LANES] - (m_a if c < 2 * blk // LANES else m_b)
                for c in range(4 * blk // LANES)]
        state["p", n] = jnp.exp2(jnp.concatenate(cols, axis=1)).astype(BF16)
        state["m", n] = jnp.where(lo128, m_a, m_b)

    def attend(n):
        keys = slice(n * blk, (n + 2) * blk)
        v_cat = jnp.concatenate([v_lo()[keys], v_hi()[keys]], axis=0)
        rhs = jnp.concatenate([v_cat, ones_pad], axis=1)
        nd = jnp.dot(state.pop(("p", n)), rhs, preferred_element_type=F32)
        den = nd[:, LANES:] + jnp.exp2(state["sink_slab"] - state.pop(("m", n)))
        store(n, nd[:, :LANES] / den)

    stages = [prep]
    for n in range(nblk):
        stages += [lambda n=n: scores(n), lambda n=n: attend(n)]
    return stages


_CAST_SHAPES = ((D_MODEL, D_MODEL, 32), (D_MODEL, D_FF, 32), (D_MODEL, D_FF, 32), (D_FF, D_MODEL, 96))
N_CAST = len(_CAST_SHAPES)


def _cast_chunk(k, step):
    rows, _, per = _CAST_SHAPES[k]
    n = -(-rows // per)
    start = jnp.minimum(step * per, rows - per)
    return n, pl.multiple_of(start, 16)


def _cast_copies(k, step, src_refs, dst_refs, fbuf_refs, bbuf_refs, sem_in, sem_out):
    per = _CAST_SHAPES[k][2]
    _, start = _cast_chunk(k, step)
    rows = pl.ds(start, per)
    return (pltpu.make_async_copy(src_refs[k].at[rows, :], fbuf_refs[k], sem_in.at[k]),
            pltpu.make_async_copy(bbuf_refs[k], dst_refs[k].at[rows, :], sem_out.at[k]))


def _mix_kernel(x_ref, gain_ref, wm_ref, wt_ref, lbl_ref, og_ref, qg_ref, kg_ref, sink_ref,
                *refs):
    cast_src = refs[:N_CAST]
    rec_ref, att_ref = refs[N_CAST:N_CAST + 2]
    cast_dst = refs[N_CAST + 2:2 * N_CAST + 2]
    (u_ref, pa_ref, pb_ref, st_ref, cv_ref, kvp_ref, kpad_ref, vpad_ref, bias_ref,
     lev_ref) = refs[2 * N_CAST + 2:2 * N_CAST + 12]
    cast_f = refs[2 * N_CAST + 12:3 * N_CAST + 12]
    cast_b = refs[3 * N_CAST + 12:4 * N_CAST + 12]
    sem_in, sem_out = refs[4 * N_CAST + 12:]
    g = pl.program_id(0)
    blk = ATTN_BLOCK

    def cast_copies(k, step):
        return _cast_copies(k, step, cast_src, cast_dst, cast_f, cast_b, sem_in, sem_out)

    for k in range(N_CAST):
        @pl.when(g < _cast_chunk(k, g)[0])
        def _(k=k):
            cast_copies(k, g)[0].start()

    lo128 = lax.broadcasted_iota(jnp.int32, (1, LANES), 1) < ATTN_HD

    @pl.when(g == 0)
    def _():
        kvp_ref[...] = jnp.zeros_like(kvp_ref)
        lev_ref[...] = _level_mask()
        qi = lax.broadcasted_iota(jnp.int32, (blk, 2 * blk), 0)
        kj = lax.broadcasted_iota(jnp.int32, (blk, 2 * blk), 1)
        rel = qi + blk - kj
        in_window = (rel >= 0) & (rel < WINDOW)
        b_any = jnp.where(in_window, 0.0, NEG_INF)
        b_first = jnp.where(in_window & (kj >= blk), 0.0, NEG_INF)
        bias_ref[0] = jnp.concatenate([b_any, b_any], axis=1)
        bias_ref[1] = jnp.concatenate([b_first, b_first], axis=1)

    first_blk = lax.rem(g + SEQ_BLOCKS - 1, SEQ_BLOCKS) == 0

    @pl.when(first_blk)
    def _():
        st_ref[...] = jnp.zeros_like(st_ref)

    hr = lax.broadcasted_iota(jnp.int32, (LANES, LANES), 0) < ATTN_HD
    hc = lax.broadcasted_iota(jnp.int32, (LANES, LANES), 1) < ATTN_HD
    ones_bd = jnp.where(hr == hc, 1.0, 0.0).astype(BF16)
    pair = lambda r: jnp.concatenate([r, r], axis=1)
    qgain = pair(qg_ref[...]) * (ATTN_HD ** -0.5 * LOG2E)
    kgain = pair(kg_ref[...])
    ones_lo = jnp.broadcast_to(jnp.where(lo128, 1.0, 0.0), (2 * blk, LANES))
    ones_pad = jnp.concatenate([ones_lo, 1.0 - ones_lo], axis=0).astype(BF16)

    l0 = lbl_ref[0:1, :]
    l1 = lbl_ref[1:2, :]
    lmax = jnp.maximum(l0, l1)
    e0 = jnp.exp(l0 - lmax)
    e1 = jnp.exp(l1 - lmax)
    lb = e0 / (e0 + e1)
    c0 = 0.5 * (1.0 + lb)
    c1 = 0.5 * (1.0 - lb)
    og = og_ref[...]
    first_idx = first_blk.astype(jnp.int32)

    def step(pc_ref, pp_ref):
        _mix_step(pc_ref, pp_ref, x_ref, gain_ref, u_ref, wm_ref, wt_ref, rec_ref, att_ref,
                  st_ref, cv_ref, kvp_ref, kpad_ref, vpad_ref, bias_ref, lev_ref, sink_ref,
                  (lo128, ones_bd, qgain, kgain, ones_pad, c0, c1, og, first_idx))

    @pl.when(g == 0)
    def _():
        _project_only(pa_ref, x_ref, gain_ref, u_ref, wm_ref, wt_ref)

    @pl.when((lax.rem(g, 2) == 0) & (g > 0))
    def _():
        step(pa_ref, pb_ref)

    @pl.when(lax.rem(g, 2) == 1)
    def _():
        step(pb_ref, pa_ref)

    for k in range(N_CAST):
        n_chunks = _cast_chunk(k, g)[0]

        @pl.when((g >= 1) & (g <= n_chunks))
        def _(k=k):
            cast_copies(k, g - 1)[1].wait()

        @pl.when(g < n_chunks)
        def _(k=k):
            copy_in, copy_out = cast_copies(k, g)
            copy_in.wait()
            cast_b[k][...] = cast_f[k][...].astype(BF16)
            copy_out.start()


def _project_tail(pc_ref, x_ref, gain_ref, u_ref, wt_ref):
    x = x_ref[...]
    ms = jnp.mean(x * x, axis=-1, keepdims=True)
    u = (x * lax.rsqrt(ms + EPS) * gain_ref[...]).astype(BF16)
    u_ref[...] = u
    tail = jnp.dot(u, wt_ref[...], preferred_element_type=F32)
    for t in range(TAIL_SLABS):
        pc_ref[MIX_ITERS * LOOP_SLABS + t] = tail[:, t * LANES:(t + 1) * LANES]


def _project_main(pc_ref, u_ref, wm_ref, i):
    for r in range(0, MIX_ROWS, PROJ_PIECE_ROWS):
        res = jnp.dot(u_ref[r:r + PROJ_PIECE_ROWS, :], wm_ref[i], preferred_element_type=F32)
        for t in range(LOOP_SLABS):
            pc_ref[i * LOOP_SLABS + t, r:r + PROJ_PIECE_ROWS, :] = res[:, t * LANES:(t + 1) * LANES]


def _project_only(pc_ref, x_ref, gain_ref, u_ref, wm_ref, wt_ref):
    _project_tail(pc_ref, x_ref, gain_ref, u_ref, wt_ref)
    for i in range(MIX_ITERS):
        _project_main(pc_ref, u_ref, wm_ref, i)


def _mix_step(pc_ref, pp_ref, x_ref, gain_ref, u_ref, wm_ref, wt_ref, rec_ref, att_ref, st_ref,
              cv_ref, kvp_ref, kpad_ref, vpad_ref, bias_ref, lev_ref, sink_ref, consts):
    (lo128, ones_bd, qgain, kgain, ones_pad, c0, c1, og, first_idx) = consts
    blk = ATTN_BLOCK
    _project_tail(pc_ref, x_ref, gain_ref, u_ref, wt_ref)

    k_all = jnp.concatenate([kvp_ref[0], pp_ref[K_SLAB]], axis=0)
    v_all = jnp.concatenate([kvp_ref[1], pp_ref[V_SLAB]], axis=0)
    kvp_ref[0] = pp_ref[K_SLAB, MIX_ROWS - blk:, :]
    kvp_ref[1] = pp_ref[V_SLAB, MIX_ROWS - blk:, :]
    kn = _pair_rms(k_all, ones_bd) * kgain
    kn_sw = pltpu.roll(kn, ATTN_HD, 1)
    v_sw = pltpu.roll(v_all, ATTN_HD, 1)
    zero = jnp.zeros_like(kn)
    kpad_ref[0, 0] = jnp.where(lo128, kn, zero).astype(BF16)
    kpad_ref[0, 1] = jnp.where(lo128, zero, kn_sw).astype(BF16)
    kpad_ref[1, 0] = jnp.where(lo128, kn_sw, zero).astype(BF16)
    kpad_ref[1, 1] = jnp.where(lo128, zero, kn).astype(BF16)
    vpad_ref[0, 0] = jnp.where(lo128, v_all, zero).astype(BF16)
    vpad_ref[0, 1] = jnp.where(lo128, zero, v_sw).astype(BF16)
    vpad_ref[1, 0] = jnp.where(lo128, v_sw, zero).astype(BF16)
    vpad_ref[1, 1] = jnp.where(lo128, zero, v_all).astype(BF16)

    def body(i, carry):
        def load(part, h, rows):
            return pp_ref[part * HGRN_HEADS + h, rows, :]

        def store(h, rows, tile):
            rec_ref[0, rows, h * HGRN_DV:(h + 1) * HGRN_DV] = tile.astype(rec_ref.dtype)

        r0s = [pl.multiple_of((i * HGRN_UNROLL + cc) * HGRN_CHUNK, HGRN_CHUNK)
               for cc in range(HGRN_UNROLL)]

        def slab_stages(c):
            kvh = lax.div(c, ATTN_GROUP // 2)

            def store_att(n, tile):
                att_ref[0, c, n * blk:(n + 1) * blk, :] = tile.astype(att_ref.dtype)

            return _swa_slab(lambda: pp_ref[Q_SLAB + c],
                             lambda: kpad_ref[kvh, 0], lambda: kpad_ref[kvh, 1],
                             lambda: vpad_ref[kvh, 0], lambda: vpad_ref[kvh, 1], ones_pad,
                             lambda: bias_ref[first_idx], lambda: bias_ref[0],
                             sink_ref[2 * c] * LOG2E, sink_ref[2 * c + 1] * LOG2E,
                             qgain, ones_bd, lo128, store_att)

        swa = [slab_stages(i * SLABS_PER_ITER + s) for s in range(SLABS_PER_ITER)]

        _project_main(pc_ref, u_ref, wm_ref, i)
        _hgrn_chunks(load, store, r0s, c0, c1, og, lev_ref[...], st_ref, cv_ref, lambda: None)
        for stages in zip(*swa):
            for stage in stages:
                stage()
        return carry

    lax.fori_loop(0, MIX_ITERS, body, 0, unroll=MIX_UNROLL)


def _mix(x2, seq_len, gain, w_main, w_tail, lb_logits, out_gain, q_gain, k_gain, sinks,
         ffn_weights):
    t = x2.shape[0]
    nblk = t // MIX_ROWS
    assert seq_len // MIX_ROWS == SEQ_BLOCKS
    batch = t // seq_len
    assert tuple(w.shape for w in ffn_weights) == tuple(s[:2] for s in _CAST_SHAPES)
    assert all(-(-rows // per) <= nblk for rows, _, per in _CAST_SHAPES)
    hbm = pl.BlockSpec(memory_space=pl.ANY)

    def in_rows(gi):
        return (jnp.minimum(gi, nblk - 1), 0)

    def out_block(gi):
        blk_i = jnp.maximum(gi - 1, 0)
        return blk_i // SEQ_BLOCKS, lax.rem(blk_i, SEQ_BLOCKS)

    full = lambda shape: pl.BlockSpec(shape, lambda gi: (0,) * len(shape))
    return pl.pallas_call(
        _mix_kernel,
        grid=(nblk + 1,),
        in_specs=[
            pl.BlockSpec((MIX_ROWS, D_MODEL), in_rows),
            full((1, D_MODEL)),
            full((MIX_ITERS, D_MODEL, LOOP_SLABS * LANES)),
            full((D_MODEL, TAIL_SLABS * LANES)),
            full((2, HGRN_WIDTH)),
            full((1, HGRN_DV)),
            full((1, ATTN_HD)),
            full((1, ATTN_HD)),
            pl.BlockSpec(memory_space=pltpu.SMEM),
        ] + [hbm] * N_CAST,
        out_specs=[
            pl.BlockSpec((1, MIX_ROWS, HGRN_WIDTH),
                         lambda gi: (*out_block(gi), 0)),
            pl.BlockSpec((1, ATT_SLABS, MIX_ROWS, LANES),
                         lambda gi: (out_block(gi)[0], 0, out_block(gi)[1], 0)),
        ] + [hbm] * N_CAST,
        out_shape=[
            jax.ShapeDtypeStruct((batch, seq_len, HGRN_WIDTH), BF16),
            jax.ShapeDtypeStruct((batch, ATT_SLABS, seq_len, LANES), BF16),
        ] + [jax.ShapeDtypeStruct(w.shape, BF16) for w in ffn_weights],
        scratch_shapes=[
            pltpu.VMEM((MIX_ROWS, D_MODEL), BF16),
            pltpu.VMEM((N_SLABS, MIX_ROWS, LANES), F32),
            pltpu.VMEM((N_SLABS, MIX_ROWS, LANES), F32),
            pltpu.VMEM((HGRN_HEADS, HGRN_DV, HGRN_DK), F32),
            pltpu.VMEM((HGRN_HEADS * HGRN_UNROLL, 3, HGRN_CHUNK, HGRN_DK), F32),
            pltpu.VMEM((2, ATTN_BLOCK, LANES), F32),
            pltpu.VMEM((ATTN_KV_HEADS, 2, ATTN_BLOCK + MIX_ROWS, LANES), BF16),
            pltpu.VMEM((ATTN_KV_HEADS, 2, ATTN_BLOCK + MIX_ROWS, LANES), BF16),
            pltpu.VMEM((2, ATTN_BLOCK, 4 * ATTN_BLOCK), F32),
            pltpu.VMEM((LEVEL_ROWS, LEVEL_ROWS), jnp.int32),
        ] + [pltpu.VMEM((per, cols), F32) for _, cols, per in _CAST_SHAPES]
        + [pltpu.VMEM((per, cols), BF16) for _, cols, per in _CAST_SHAPES]
        + [pltpu.SemaphoreType.DMA((N_CAST,)), pltpu.SemaphoreType.DMA((N_CAST,))],
        compiler_params=pltpu.CompilerParams(
            dimension_semantics=("arbitrary",), vmem_limit_bytes=VMEM_LIMIT_BYTES),
        name="mixer",
    )(x2, gain, w_main, w_tail, lb_logits, out_gain, q_gain, k_gain, sinks, *ffn_weights)


def _ffn_kernel(x_ref, rec_ref, att_ref, wo_ref, gain_ref, wg_ref, wu_ref, wd_ref, o_ref):
    mixed = jnp.concatenate([rec_ref[0]] + [att_ref[0, c] for c in range(ATT_SLABS)],
                            axis=1)
    h = x_ref[...] + jnp.dot(mixed, wo_ref[...], preferred_element_type=F32)
    ms = jnp.mean(h * h, axis=-1, keepdims=True)
    u = (h * lax.rsqrt(ms + EPS) * gain_ref[...]).astype(BF16)
    gate = jnp.dot(u, wg_ref[...], preferred_element_type=F32)
    up = jnp.dot(u, wu_ref[...], preferred_element_type=F32)
    gs = 0.5 * gate
    act = ((gs * jnp.tanh(gs) + gs) * up).astype(BF16)
    o_ref[...] = h + jnp.dot(act, wd_ref[...], preferred_element_type=F32)


def _ffn(x2, rec, att, wo, gain, wg, wu, wd):
    t = x2.shape[0]
    seq_blocks = rec.shape[1] // FFN_ROWS

    def resident(shape):
        return pl.BlockSpec(shape, lambda i: (0, 0), pipeline_mode=pl.Buffered(1))

    return pl.pallas_call(
        _ffn_kernel,
        grid=(t // FFN_ROWS,),
        in_specs=[
            pl.BlockSpec((FFN_ROWS, D_MODEL), lambda i: (i, 0)),
            pl.BlockSpec((1, FFN_ROWS, HGRN_WIDTH),
                         lambda i: (i // seq_blocks, lax.rem(i, seq_blocks), 0)),
            pl.BlockSpec((1, ATT_SLABS, FFN_ROWS, LANES),
                         lambda i: (i // seq_blocks, 0, lax.rem(i, seq_blocks), 0)),
            resident((D_MODEL, D_MODEL)),
            resident((1, D_MODEL)),
            resident((D_MODEL, D_FF)),
            resident((D_MODEL, D_FF)),
            resident((D_FF, D_MODEL)),
        ],
        out_specs=pl.BlockSpec((FFN_ROWS, D_MODEL), lambda i: (i, 0)),
        out_shape=jax.ShapeDtypeStruct((t, D_MODEL), F32),
        compiler_params=pltpu.CompilerParams(
            dimension_semantics=("arbitrary",), vmem_limit_bytes=VMEM_LIMIT_BYTES),
        name="outproj_ffn",
    )(x2, rec, att, wo, gain, wg, wu, wd)


def kernel(x, norm1_gain, w_in, hgrn_lb_logits, hgrn_out_gain, q_norm_gain, k_norm_gain,
           attn_sinks, w_out, norm2_gain, w_ffn_gate, w_ffn_up, w_ffn_down):
    b, s, d = x.shape
    assert (d, w_in.shape[0]) == (D_MODEL, 1), "single-layer kernel"
    assert s == SEQ_BLOCKS * MIX_ROWS and MIX_ROWS == FFN_ROWS
    t = b * s
    x2 = x.reshape(t, d)

    w_bf = w_in[0].astype(BF16)
    n_main = MIX_ITERS * LOOP_SLABS * LANES
    w_main = w_bf[:, :n_main].reshape(D_MODEL, MIX_ITERS, LOOP_SLABS * LANES).transpose(1, 0, 2)
    w_tail = w_bf[:, n_main:]
    rec, att, wo, wg, wu, wd = _mix(
        x2, s, norm1_gain[0][None, :], w_main, w_tail, hgrn_lb_logits,
        hgrn_out_gain[0][None, :], q_norm_gain[0][None, :], k_norm_gain[0][None, :],
        attn_sinks[0], (w_out[0], w_ffn_gate[0], w_ffn_up[0], w_ffn_down[0]))
    out = _ffn(x2, rec, att, wo, norm2_gain[0][None, :], wg, wu, wd)
    return out.reshape(b, s, d)
```

```python
import jax
import jax.numpy as jnp
from jax import lax
from jax.experimental import pallas as pl
from jax.experimental.pallas import tpu as pltpu

D_MODEL = 1024
HGRN_HEADS = 4
HGRN_DK = 128
HGRN_DV = 128
HGRN_WIDTH = HGRN_HEADS * HGRN_DK
ATTN_HEADS = 8
ATTN_KV_HEADS = 2
ATTN_GROUP = ATTN_HEADS // ATTN_KV_HEADS
ATTN_HD = 64
ATTN_WIDTH = ATTN_HEADS * ATTN_HD
KV_WIDTH = ATTN_KV_HEADS * ATTN_HD
WINDOW = 128
PROJ_WIDTH = 4 * HGRN_WIDTH + ATTN_WIDTH + 2 * KV_WIDTH
D_FF = 2816
EPS = 1e-6
NEG_INF = -1e30
LOG2E = 1.4426950408889634

F32 = jnp.float32
BF16 = jnp.bfloat16

LANES = 128
VMEM_LIMIT_BYTES = 56 * 1024 * 1024

HGRN_CHUNK = 64
HGRN_LEVELS = 6
HGRN_UNROLL = 2
ATTN_BLOCK = 128
MIX_ROWS = 512
MIX_ITERS = 4
MIX_UNROLL = 4
PROJ_PIECE_ROWS = 512
SEQ_BLOCKS = 4
FFN_ROWS = 512

N_SLABS = PROJ_WIDTH // LANES
ATT_SLABS = ATTN_WIDTH // LANES
SLABS_PER_ITER = ATT_SLABS // MIX_ITERS
Q_SLAB = 4 * HGRN_HEADS
K_SLAB = Q_SLAB + ATT_SLABS
V_SLAB = K_SLAB + 1
LOOP_SLABS = Q_SLAB // MIX_ITERS
TAIL_SLABS = N_SLABS - MIX_ITERS * LOOP_SLABS
assert MIX_ITERS * LOOP_SLABS == Q_SLAB and V_SLAB == N_SLABS - 1 and LOOP_SLABS % 2 == 0
assert MIX_ROWS == MIX_ITERS * HGRN_UNROLL * HGRN_CHUNK
assert ATT_SLABS == MIX_ITERS * SLABS_PER_ITER

_NT = (((1,), (1,)), ((), ()))
_TN = (((0,), (0,)), ((), ()))


_LEVEL_Q = ((1, 3, 5, 7), (2, 3, 6, 7), (4, 5, 6, 7))
_LEVEL_K = ((0, 2, 4, 6), (0, 1, 4, 5), (0, 1, 2, 3))
LEVEL_ROWS = 8 * sum(len(q) for q in _LEVEL_Q)


def _level_mask():
    r = lax.broadcasted_iota(jnp.int32, (LEVEL_ROWS, LEVEL_ROWS), 0)
    c = lax.broadcasted_iota(jnp.int32, (LEVEL_ROWS, LEVEL_ROWS), 1)
    lev_r, lev_c = r >> 5, c >> 5
    j_r, j_c = (r >> 3) & 3, (c >> 3) & 3
    same_block = (((lev_r == 0) & (j_r == j_c))
                  | ((lev_r == 1) & ((j_r >> 1) == (j_c >> 1)))
                  | (lev_r == 2))
    return ((lev_r == lev_c) & same_block).astype(jnp.int32)


def _mul1(a, b):
    if a is None:
        return b
    if b is None:
        return a
    return a * b


def _pair_rms(x, ones_bd):
    ss = jnp.dot((x * x).astype(BF16), ones_bd, preferred_element_type=F32)
    return x * lax.rsqrt(ss * (1.0 / ATTN_HD) + EPS)


def _hgrn_chunks(load, store, r0s, c0, c1, og, mask_nat, st_ref, cv_ref, fill):
    nh = HGRN_HEADS
    c_rows = HGRN_CHUNK
    nv = c_rows // 8
    assert nv == 8 and HGRN_LEVELS == 6

    def bc(t, r):
        return jnp.broadcast_to(t[r:r + 1, :], (8, HGRN_DK))

    def cat16(pieces):
        return jnp.concatenate(pieces, axis=0).astype(BF16)

    q_order = _LEVEL_Q[0] + _LEVEL_Q[1] + _LEVEL_Q[2]
    k_order = _LEVEL_K[0] + _LEVEL_K[1] + _LEVEL_K[2]

    def stacked_attend(q_pieces, k_pieces, v_pieces, mask):
        s = [lax.dot_general(cat16(q_pieces[u]), cat16(k_pieces[u]), _NT,
                             preferred_element_type=F32) for u in units]
        fill()
        s = [jnp.where(mask, s[u], 0.0).astype(BF16) for u in units]
        o = [jnp.dot(s[u], cat16([v_pieces[u][j] for j in k_order]),
                     preferred_element_type=F32) for u in units]
        fill()
        return [[o[u][8 * n:8 * n + 8, :] for n in range(len(q_order))] for u in units]

    def scatter_add(acc, pieces):
        for n, j in enumerate(q_order):
            acc[j] = pieces[n] if acc[j] is None else acc[j] + pieces[n]
        return acc

    m_nat = mask_nat != 0

    units = range(nh * len(r0s))
    hd = [u % nh for u in units]
    r0 = [r0s[u // nh] for u in units]
    rows = [pl.ds(r0[u], c_rows) for u in units]
    lanes = [slice(hd[u] * HGRN_DK, (hd[u] + 1) * HGRN_DK) for u in units]
    vrange = range(nv)

    fp, kp, qp, vtr = [], [], [], []
    for u in units:
        c0h, c1h = c0[:, lanes[u]], c1[:, lanes[u]]
        fh, kh, qh, vh = [], [], [], []
        for b in vrange:
            srows = pl.ds(r0[u] + b, 8, stride=8)
            ct = c1h * jnp.tanh(0.5 * load(1, hd[u], srows))
            fh.append(c0h + ct)
            kh.append(c1h - ct)
            xq = load(0, hd[u], srows)
            xs = xq * (0.5 * HGRN_DK ** -0.5)
            qh.append(xs * jnp.tanh(0.5 * xq) + xs)
            vh.append(load(2, hd[u], srows))
        fp.append(fh)
        kp.append(kh)
        qp.append(qh)
        vtr.append(vh)

    fill()
    o_tr = [[jnp.sum(qp[u][b] * kp[u][b], axis=-1, keepdims=True) * vtr[u][b] for b in vrange]
            for u in units]

    p2 = [[fp[u][b] * fp[u][b - 1] if b & 1 else fp[u][b] for b in vrange] for u in units]
    x2 = [[None if b & 1 else fp[u][b + 1] for b in vrange] for u in units]
    p4 = [[p2[u][b] * p2[u][(b & ~3) + 1] if b & 2 else p2[u][b] for b in vrange] for u in units]
    x4 = [[x2[u][b] if b & 2 else _mul1(x2[u][b], p2[u][(b & ~3) + 3]) for b in vrange]
          for u in units]
    p8 = [[p4[u][b] * p4[u][3] if b & 4 else p4[u][b] for b in vrange] for u in units]
    x8 = [[x4[u][b] if b & 4 else _mul1(x4[u][b], p4[u][7]) for b in vrange] for u in units]
    tot8 = [p8[u][7] for u in units]

    for u in units:
        for b in vrange:
            srows = pl.ds(b, 8, stride=8)
            cv_ref[u, 0, srows, :] = qp[u][b] * p8[u][b]
            cv_ref[u, 1, srows, :] = _mul1(kp[u][b], x8[u][b])

    fill()
    q_pieces = [([qp[u][b] * fp[u][b] for b in _LEVEL_Q[0]]
                 + [qp[u][b] * p2[u][b] for b in _LEVEL_Q[1]]
                 + [qp[u][b] * p4[u][b] for b in _LEVEL_Q[2]]) for u in units]
    k_pieces = [([kp[u][b] for b in _LEVEL_K[0]]
                 + [_mul1(kp[u][b], x2[u][b]) for b in _LEVEL_K[1]]
                 + [_mul1(kp[u][b], x4[u][b]) for b in _LEVEL_K[2]]) for u in units]
    for lvl in range(3):
        group = 1 << lvl
        for jq, b_q in enumerate(_LEVEL_Q[lvl]):
            for jk, b_k in enumerate(_LEVEL_K[lvl]):
                if jq // group != jk // group:
                    continue
                nq, nk = 4 * lvl + jq, 4 * lvl + jk
                for u in units:
                    d = jnp.sum(q_pieces[u][nq] * k_pieces[u][nk], axis=-1, keepdims=True)
                    o_tr[u][b_q] = o_tr[u][b_q] + d * vtr[u][b_k]

    an = [[cv_ref[u, 0, 8 * a:8 * a + 8, :] for a in vrange] for u in units]
    cn = [[cv_ref[u, 1, 8 * a:8 * a + 8, :] for a in vrange] for u in units]
    vnat = [[load(2, hd[u], pl.ds(r0[u] + 8 * a, 8)) for a in vrange] for u in units]

    q4 = [[an[u][a] for a in _LEVEL_Q[0]] for u in units]
    k4 = [[cn[u][a] for a in _LEVEL_K[0]] for u in units]
    an = [[an[u][a] * bc(tot8[u], a - 1) if a & 1 else an[u][a] for a in vrange] for u in units]
    cn = [[cn[u][a] if a & 1 else cn[u][a] * bc(tot8[u], a + 1) for a in vrange] for u in units]
    t16 = [tot8[u] * pltpu.roll(tot8[u], 1, 0) for u in units]
    q5 = [[an[u][a] for a in _LEVEL_Q[1]] for u in units]
    k5 = [[cn[u][a] for a in _LEVEL_K[1]] for u in units]
    an = [[an[u][a] * bc(t16[u], (a & ~3) + 1) if a & 2 else an[u][a] for a in vrange]
          for u in units]
    cn = [[cn[u][a] if a & 2 else cn[u][a] * bc(t16[u], (a & ~3) + 3) for a in vrange]
          for u in units]
    t32 = [t16[u] * pltpu.roll(t16[u], 2, 0) for u in units]
    q6 = [[an[u][a] for a in _LEVEL_Q[2]] for u in units]
    k6 = [[cn[u][a] for a in _LEVEL_K[2]] for u in units]
    an = [[an[u][a] * bc(t32[u], 3) if a & 4 else an[u][a] for a in vrange] for u in units]
    cn = [[cn[u][a] if a & 4 else cn[u][a] * bc(t32[u], 7) for a in vrange] for u in units]
    att_nat = stacked_attend([q4[u] + q5[u] + q6[u] for u in units],
                             [k4[u] + k5[u] + k6[u] for u in units], vnat, m_nat)
    o_nat = []
    for u in units:
        pieces = scatter_add([None] * nv, att_nat[u])
        pieces[0] = jnp.zeros((8, HGRN_DV), F32)
        o_nat.append(jnp.concatenate(pieces, axis=0))

    for u in units:
        for b in vrange:
            cv_ref[u, 2, pl.ds(b, 8, stride=8), :] = o_tr[u][b]

    fill()
    upd = [lax.dot_general(cat16(vnat[u]), cat16(cn[u]),
                           _TN, preferred_element_type=F32) for u in units]
    for h in range(nh):
        st = st_ref[h]
        for u in range(h, len(units), nh):
            qs = cat16(an[u])
            o_nat[u] = o_nat[u] + lax.dot_general(qs, st.astype(BF16), _NT,
                                                  preferred_element_type=F32)
            dec = t32[u][3:4, :] * t32[u][7:8, :]
            st = st * dec + upd[u]
        st_ref[h] = st

    fill()
    for u in units:
        o_h = o_nat[u] + cv_ref[u, 2]
        ms = jnp.mean(o_h * o_h, axis=-1, keepdims=True)
        xgs = 0.5 * load(3, hd[u], rows[u])
        gate = xgs * jnp.tanh(xgs) + xgs
        store(hd[u], rows[u], o_h * lax.rsqrt(ms + EPS) * og * gate)


def _swa_slab(q_slab, k_lo, k_hi, v_lo, v_hi, ones_pad, bias_first, bias_any,
              sink_a, sink_b, qgain, ones_bd, lo128, store):
    blk = ATTN_BLOCK
    nblk = MIX_ROWS // blk
    state = {}

    def prep():
        state["qs"] = (_pair_rms(q_slab(), ones_bd) * qgain).astype(BF16)
        state["sink_slab"] = jnp.where(lo128, sink_a, sink_b)

    def scores(n):
        keys = slice(n * blk, (n + 2) * blk)
        k_cat = jnp.concatenate([k_lo()[keys], k_hi()[keys]], axis=0)
        s = lax.dot_general(state["qs"][n * blk:(n + 1) * blk], k_cat, _NT,
                            preferred_element_type=F32)
        s = s + (bias_first() if n == 0 else bias_any())
        m_a = jnp.max(s[:, :2 * blk], axis=-1, keepdims=True)
        m_b = jnp.max(s[:, 2 * blk:], axis=-1, keepdims=True)
        m_a = jnp.maximum(jnp.broadcast_to(m_a, (blk, LANES)), sink_a)
        m_b = jnp.maximum(jnp.broadcast_to(m_b, (blk, LANES)), sink_b)
        cols = [s[:, c * LANES:(c + 1) * LANES] - (m_a if c < 2 * blk // LANES else m_b)
                for c in range(4 * blk // LANES)]
        state["p", n] = jnp.exp2(jnp.concatenate(cols, axis=1)).astype(BF16)
        state["m", n] = jnp.where(lo128, m_a, m_b)

    def attend(n):
        keys = slice(n * blk, (n + 2) * blk)
        v_cat = jnp.concatenate([v_lo()[keys], v_hi()[keys]], axis=0)
        rhs = jnp.concatenate([v_cat, ones_pad], axis=1)
        nd = jnp.dot(state.pop(("p", n)), rhs, preferred_element_type=F32)
        den = nd[:, LANES:] + jnp.exp2(state["sink_slab"] - state.pop(("m", n)))
        store(n, nd[:, :LANES] / den)

    stages = [prep]
    for n in range(nblk):
        stages += [lambda n=n: scores(n), lambda n=n: attend(n)]
    return stages


_CAST_SHAPES = ((D_MODEL, D_MODEL, 32), (D_MODEL, D_FF, 32), (D_MODEL, D_FF, 32), (D_FF, D_MODEL, 96))
N_CAST = len(_CAST_SHAPES)


def _cast_chunk(k, step):
    rows, _, per = _CAST_SHAPES[k]
    n = -(-rows // per)
    start = jnp.minimum(step * per, rows - per)
    return n, pl.multiple_of(start, 16)


def _cast_copies(k, step, src_refs, dst_refs, fbuf_refs, bbuf_refs, sem_in, sem_out):
    per = _CAST_SHAPES[k][2]
    _, start = _cast_chunk(k, step)
    rows = pl.ds(start, per)
    return (pltpu.make_async_copy(src_refs[k].at[rows, :], fbuf_refs[k], sem_in.at[k]),
            pltpu.make_async_copy(bbuf_refs[k], dst_refs[k].at[rows, :], sem_out.at[k]))


def _mix_kernel(x_ref, gain_ref, wm_ref, wt_ref, lbl_ref, og_ref, qg_ref, kg_ref, sink_ref,
                *refs):
    cast_src = refs[:N_CAST]
    rec_ref, att_ref = refs[N_CAST:N_CAST + 2]
    cast_dst = refs[N_CAST + 2:2 * N_CAST + 2]
    (u_ref, pa_ref, pb_ref, st_ref, cv_ref, kvp_ref, kpad_ref, vpad_ref, bias_ref,
     lev_ref) = refs[2 * N_CAST + 2:2 * N_CAST + 12]
    cast_f = refs[2 * N_CAST + 12:3 * N_CAST + 12]
    cast_b = refs[3 * N_CAST + 12:4 * N_CAST + 12]
    sem_in, sem_out = refs[4 * N_CAST + 12:]
    g = pl.program_id(0)
    blk = ATTN_BLOCK

    def cast_copies(k, step):
        return _cast_copies(k, step, cast_src, cast_dst, cast_f, cast_b, sem_in, sem_out)

    for k in range(N_CAST):
        @pl.when(g < _cast_chunk(k, g)[0])
        def _(k=k):
            cast_copies(k, g)[0].start()

    lo128 = lax.broadcasted_iota(jnp.int32, (1, LANES), 1) < ATTN_HD

    @pl.when(g == 0)
    def _():
        kvp_ref[...] = jnp.zeros_like(kvp_ref)
        lev_ref[...] = _level_mask()
        qi = lax.broadcasted_iota(jnp.int32, (blk, 2 * blk), 0)
        kj = lax.broadcasted_iota(jnp.int32, (blk, 2 * blk), 1)
        rel = qi + blk - kj
        in_window = (rel >= 0) & (rel < WINDOW)
        b_any = jnp.where(in_window, 0.0, NEG_INF)
        b_first = jnp.where(in_window & (kj >= blk), 0.0, NEG_INF)
        bias_ref[0] = jnp.concatenate([b_any, b_any], axis=1)
        bias_ref[1] = jnp.concatenate([b_first, b_first], axis=1)

    first_blk = lax.rem(g + SEQ_BLOCKS - 1, SEQ_BLOCKS) == 0

    @pl.when(first_blk)
    def _():
        st_ref[...] = jnp.zeros_like(st_ref)

    hr = lax.broadcasted_iota(jnp.int32, (LANES, LANES), 0) < ATTN_HD
    hc = lax.broadcasted_iota(jnp.int32, (LANES, LANES), 1) < ATTN_HD
    ones_bd = jnp.where(hr == hc, 1.0, 0.0).astype(BF16)
    pair = lambda r: jnp.concatenate([r, r], axis=1)
    qgain = pair(qg_ref[...]) * (ATTN_HD ** -0.5 * LOG2E)
    kgain = pair(kg_ref[...])
    ones_lo = jnp.broadcast_to(jnp.where(lo128, 1.0, 0.0), (2 * blk, LANES))
    ones_pad = jnp.concatenate([ones_lo, 1.0 - ones_lo], axis=0).astype(BF16)

    l0 = lbl_ref[0:1, :]
    l1 = lbl_ref[1:2, :]
    lmax = jnp.maximum(l0, l1)
    e0 = jnp.exp(l0 - lmax)
    e1 = jnp.exp(l1 - lmax)
    lb = e0 / (e0 + e1)
    c0 = 0.5 * (1.0 + lb)
    c1 = 0.5 * (1.0 - lb)
    og = og_ref[...]
    first_idx = first_blk.astype(jnp.int32)

    def step(pc_ref, pp_ref):
        _mix_step(pc_ref, pp_ref, x_ref, gain_ref, u_ref, wm_ref, wt_ref, rec_ref, att_ref,
                  st_ref, cv_ref, kvp_ref, kpad_ref, vpad_ref, bias_ref, lev_ref, sink_ref,
                  (lo128, ones_bd, qgain, kgain, ones_pad, c0, c1, og, first_idx))

    @pl.when(g == 0)
    def _():
        _project_only(pa_ref, x_ref, gain_ref, u_ref, wm_ref, wt_ref)

    @pl.when((lax.rem(g, 2) == 0) & (g > 0))
    def _():
        step(pa_ref, pb_ref)

    @pl.when(lax.rem(g, 2) == 1)
    def _():
        step(pb_ref, pa_ref)

    for k in range(N_CAST):
        n_chunks = _cast_chunk(k, g)[0]

        @pl.when((g >= 1) & (g <= n_chunks))
        def _(k=k):
            cast_copies(k, g - 1)[1].wait()

        @pl.when(g < n_chunks)
        def _(k=k):
            copy_in, copy_out = cast_copies(k, g)
            copy_in.wait()
            cast_b[k][...] = cast_f[k][...].astype(BF16)
            copy_out.start()


def _normalize(x_ref, gain_ref, u_ref):
    x = x_ref[...]
    ms = jnp.mean(x * x, axis=-1, keepdims=True)
    u_ref[...] = (x * lax.rsqrt(ms + EPS) * gain_ref[...]).astype(BF16)


def _project_tail(pc_ref, u_ref, wt_ref):
    tail = jnp.dot(u_ref[...], wt_ref[...], preferred_element_type=F32)
    for t in range(TAIL_SLABS):
        pc_ref[MIX_ITERS * LOOP_SLABS + t] = tail[:, t * LANES:(t + 1) * LANES]


def _project_main(pc_ref, u_ref, wm_ref, i):
    for r in range(0, MIX_ROWS, PROJ_PIECE_ROWS):
        res = jnp.dot(u_ref[r:r + PROJ_PIECE_ROWS, :], wm_ref[i], preferred_element_type=F32)
        for t in range(LOOP_SLABS):
            pc_ref[i * LOOP_SLABS + t, r:r + PROJ_PIECE_ROWS, :] = res[:, t * LANES:(t + 1) * LANES]


def _project_only(pc_ref, x_ref, gain_ref, u_ref, wm_ref, wt_ref):
    _normalize(x_ref, gain_ref, u_ref)
    for i in range(MIX_ITERS):
        _project_main(pc_ref, u_ref, wm_ref, i)
    _project_tail(pc_ref, u_ref, wt_ref)


def _mix_step(pc_ref, pp_ref, x_ref, gain_ref, u_ref, wm_ref, wt_ref, rec_ref, att_ref, st_ref,
              cv_ref, kvp_ref, kpad_ref, vpad_ref, bias_ref, lev_ref, sink_ref, consts):
    (lo128, ones_bd, qgain, kgain, ones_pad, c0, c1, og, first_idx) = consts
    blk = ATTN_BLOCK
    _normalize(x_ref, gain_ref, u_ref)

    k_all = jnp.concatenate([kvp_ref[0], pp_ref[K_SLAB]], axis=0)
    v_all = jnp.concatenate([kvp_ref[1], pp_ref[V_SLAB]], axis=0)
    kvp_ref[0] = pp_ref[K_SLAB, MIX_ROWS - blk:, :]
    kvp_ref[1] = pp_ref[V_SLAB, MIX_ROWS - blk:, :]
    kn = _pair_rms(k_all, ones_bd) * kgain
    kn_sw = pltpu.roll(kn, ATTN_HD, 1)
    v_sw = pltpu.roll(v_all, ATTN_HD, 1)
    zero = jnp.zeros_like(kn)
    kpad_ref[0, 0] = jnp.where(lo128, kn, zero).astype(BF16)
    kpad_ref[0, 1] = jnp.where(lo128, zero, kn_sw).astype(BF16)
    kpad_ref[1, 0] = jnp.where(lo128, kn_sw, zero).astype(BF16)
    kpad_ref[1, 1] = jnp.where(lo128, zero, kn).astype(BF16)
    vpad_ref[0, 0] = jnp.where(lo128, v_all, zero).astype(BF16)
    vpad_ref[0, 1] = jnp.where(lo128, zero, v_sw).astype(BF16)
    vpad_ref[1, 0] = jnp.where(lo128, v_sw, zero).astype(BF16)
    vpad_ref[1, 1] = jnp.where(lo128, zero, v_all).astype(BF16)

    def body(i, carry):
        def load(part, h, rows):
            return pp_ref[part * HGRN_HEADS + h, rows, :]

        def store(h, rows, tile):
            rec_ref[0, rows, h * HGRN_DV:(h + 1) * HGRN_DV] = tile.astype(rec_ref.dtype)

        r0s = [pl.multiple_of((i * HGRN_UNROLL + cc) * HGRN_CHUNK, HGRN_CHUNK)
               for cc in range(HGRN_UNROLL)]

        def slab_stages(c):
            kvh = lax.div(c, ATTN_GROUP // 2)

            def store_att(n, tile):
                att_ref[0, c, n * blk:(n + 1) * blk, :] = tile.astype(att_ref.dtype)

            return _swa_slab(lambda: pp_ref[Q_SLAB + c],
                             lambda: kpad_ref[kvh, 0], lambda: kpad_ref[kvh, 1],
                             lambda: vpad_ref[kvh, 0], lambda: vpad_ref[kvh, 1], ones_pad,
                             lambda: bias_ref[first_idx], lambda: bias_ref[0],
                             sink_ref[2 * c] * LOG2E, sink_ref[2 * c + 1] * LOG2E,
                             qgain, ones_bd, lo128, store_att)

        swa = [slab_stages(i * SLABS_PER_ITER + s) for s in range(SLABS_PER_ITER)]

        _project_main(pc_ref, u_ref, wm_ref, i)
        _hgrn_chunks(load, store, r0s, c0, c1, og, lev_ref[...], st_ref, cv_ref, lambda: None)
        for stages in zip(*swa):
            for stage in stages:
                stage()
        return carry

    lax.fori_loop(0, MIX_ITERS, body, 0, unroll=MIX_UNROLL)
    _project_tail(pc_ref, u_ref, wt_ref)


def _mix(x2, seq_len, gain, w_main, w_tail, lb_logits, out_gain, q_gain, k_gain, sinks,
         ffn_weights):
    t = x2.shape[0]
    nblk = t // MIX_ROWS
    assert seq_len // MIX_ROWS == SEQ_BLOCKS
    batch = t // seq_len
    assert tuple(w.shape for w in ffn_weights) == tuple(s[:2] for s in _CAST_SHAPES)
    assert all(-(-rows // per) <= nblk for rows, _, per in _CAST_SHAPES)
    hbm = pl.BlockSpec(memory_space=pl.ANY)

    def in_rows(gi):
        return (jnp.minimum(gi, nblk - 1), 0)

    def out_block(gi):
        blk_i = jnp.maximum(gi - 1, 0)
        return blk_i // SEQ_BLOCKS, lax.rem(blk_i, SEQ_BLOCKS)

    full = lambda shape: pl.BlockSpec(shape, lambda gi: (0,) * len(shape))
    return pl.pallas_call(
        _mix_kernel,
        grid=(nblk + 1,),
        in_specs=[
            pl.BlockSpec((MIX_ROWS, D_MODEL), in_rows),
            full((1, D_MODEL)),
            full((MIX_ITERS, D_MODEL, LOOP_SLABS * LANES)),
            full((D_MODEL, TAIL_SLABS * LANES)),
            full((2, HGRN_WIDTH)),
            full((1, HGRN_DV)),
            full((1, ATTN_HD)),
            full((1, ATTN_HD)),
            pl.BlockSpec(memory_space=pltpu.SMEM),
        ] + [hbm] * N_CAST,
        out_specs=[
            pl.BlockSpec((1, MIX_ROWS, HGRN_WIDTH),
                         lambda gi: (*out_block(gi), 0)),
            pl.BlockSpec((1, ATT_SLABS, MIX_ROWS, LANES),
                         lambda gi: (out_block(gi)[0], 0, out_block(gi)[1], 0)),
        ] + [hbm] * N_CAST,
        out_shape=[
            jax.ShapeDtypeStruct((batch, seq_len, HGRN_WIDTH), BF16),
            jax.ShapeDtypeStruct((batch, ATT_SLABS, seq_len, LANES), BF16),
        ] + [jax.ShapeDtypeStruct(w.shape, BF16) for w in ffn_weights],
        scratch_shapes=[
            pltpu.VMEM((MIX_ROWS, D_MODEL), BF16),
            pltpu.VMEM((N_SLABS, MIX_ROWS, LANES), F32),
            pltpu.VMEM((N_SLABS, MIX_ROWS, LANES), F32),
            pltpu.VMEM((HGRN_HEADS, HGRN_DV, HGRN_DK), F32),
            pltpu.VMEM((HGRN_HEADS * HGRN_UNROLL, 3, HGRN_CHUNK, HGRN_DK), F32),
            pltpu.VMEM((2, ATTN_BLOCK, LANES), F32),
            pltpu.VMEM((ATTN_KV_HEADS, 2, ATTN_BLOCK + MIX_ROWS, LANES), BF16),
            pltpu.VMEM((ATTN_KV_HEADS, 2, ATTN_BLOCK + MIX_ROWS, LANES), BF16),
            pltpu.VMEM((2, ATTN_BLOCK, 4 * ATTN_BLOCK), F32),
            pltpu.VMEM((LEVEL_ROWS, LEVEL_ROWS), jnp.int32),
        ] + [pltpu.VMEM((per, cols), F32) for _, cols, per in _CAST_SHAPES]
        + [pltpu.VMEM((per, cols), BF16) for _, cols, per in _CAST_SHAPES]
        + [pltpu.SemaphoreType.DMA((N_CAST,)), pltpu.SemaphoreType.DMA((N_CAST,))],
        compiler_params=pltpu.CompilerParams(
            dimension_semantics=("arbitrary",), vmem_limit_bytes=VMEM_LIMIT_BYTES),
        name="mixer",
    )(x2, gain, w_main, w_tail, lb_logits, out_gain, q_gain, k_gain, sinks, *ffn_weights)


def _ffn_kernel(x_ref, rec_ref, att_ref, wo_ref, gain_ref, wg_ref, wu_ref, wd_ref, o_ref):
    mixed = jnp.concatenate([rec_ref[0]] + [att_ref[0, c] for c in range(ATT_SLABS)],
                            axis=1)
    h = x_ref[...] + jnp.dot(mixed, wo_ref[...], preferred_element_type=F32)
    ms = jnp.mean(h * h, axis=-1, keepdims=True)
    u = (h * lax.rsqrt(ms + EPS) * gain_ref[...]).astype(BF16)
    gate = jnp.dot(u, wg_ref[...], preferred_element_type=F32)
    up = jnp.dot(u, wu_ref[...], preferred_element_type=F32)
    gs = 0.5 * gate
    act = ((gs * jnp.tanh(gs) + gs) * up).astype(BF16)
    o_ref[...] = h + jnp.dot(act, wd_ref[...], preferred_element_type=F32)


def _ffn(x2, rec, att, wo, gain, wg, wu, wd):
    t = x2.shape[0]
    seq_blocks = rec.shape[1] // FFN_ROWS

    def resident(shape):
        return pl.BlockSpec(shape, lambda i: (0, 0), pipeline_mode=pl.Buffered(1))

    return pl.pallas_call(
        _ffn_kernel,
        grid=(t // FFN_ROWS,),
        in_specs=[
            pl.BlockSpec((FFN_ROWS, D_MODEL), lambda i: (i, 0)),
            pl.BlockSpec((1, FFN_ROWS, HGRN_WIDTH),
                         lambda i: (i // seq_blocks, lax.rem(i, seq_blocks), 0)),
            pl.BlockSpec((1, ATT_SLABS, FFN_ROWS, LANES),
                         lambda i: (i // seq_blocks, 0, lax.rem(i, seq_blocks), 0)),
            resident((D_MODEL, D_MODEL)),
            resident((1, D_MODEL)),
            resident((D_MODEL, D_FF)),
            resident((D_MODEL, D_FF)),
            resident((D_FF, D_MODEL)),
        ],
        out_specs=pl.BlockSpec((FFN_ROWS, D_MODEL), lambda i: (i, 0)),
        out_shape=jax.ShapeDtypeStruct((t, D_MODEL), F32),
        compiler_params=pltpu.CompilerParams(
            dimension_semantics=("arbitrary",), vmem_limit_bytes=VMEM_LIMIT_BYTES),
        name="outproj_ffn",
    )(x2, rec, att, wo, gain, wg, wu, wd)


def kernel(x, norm1_gain, w_in, hgrn_lb_logits, hgrn_out_gain, q_norm_gain, k_norm_gain,
           attn_sinks, w_out, norm2_gain, w_ffn_gate, w_ffn_up, w_ffn_down):
    b, s, d = x.shape
    assert (d, w_in.shape[0]) == (D_MODEL, 1), "single-layer kernel"
    assert s == SEQ_BLOCKS * MIX_ROWS and MIX_ROWS == FFN_ROWS
    t = b * s
    x2 = x.reshape(t, d)

    w_bf = w_in[0].astype(BF16)
    n_main = MIX_ITERS * LOOP_SLABS * LANES
    w_main = w_bf[:, :n_main].reshape(D_MODEL, MIX_ITERS, LOOP_SLABS * LANES).transpose(1, 0, 2)
    w_tail = w_bf[:, n_main:]
    rec, att, wo, wg, wu, wd = _mix(
        x2, s, norm1_gain[0][None, :], w_main, w_tail, hgrn_lb_logits,
        hgrn_out_gain[0][None, :], q_norm_gain[0][None, :], k_norm_gain[0][None, :],
        attn_sinks[0], (w_out[0], w_ffn_gate[0], w_ffn_up[0], w_ffn_down[0]))
    out = _ffn(x2, rec, att, wo, norm2_gain[0][None, :], wg, wu, wd)
    return out.reshape(b, s, d)
```

```python
import jax
import jax.numpy as jnp
from jax import lax
from jax.experimental import pallas as pl
from jax.experimental.pallas import tpu as pltpu

D_MODEL = 1024
HGRN_HEADS = 4
HGRN_DK = 128
HGRN_DV = 128
HGRN_WIDTH = HGRN_HEADS * HGRN_DK
ATTN_HEADS = 8
ATTN_KV_HEADS = 2
ATTN_GROUP = ATTN_HEADS // ATTN_KV_HEADS
ATTN_HD = 64
ATTN_WIDTH = ATTN_HEADS * ATTN_HD
KV_WIDTH = ATTN_KV_HEADS * ATTN_HD
WINDOW = 128
PROJ_WIDTH = 4 * HGRN_WIDTH + ATTN_WIDTH + 2 * KV_WIDTH
D_FF = 2816
EPS = 1e-6
NEG_INF = -1e30
LOG2E = 1.4426950408889634

F32 = jnp.float32
BF16 = jnp.bfloat16

LANES = 128
VMEM_LIMIT_BYTES = 56 * 1024 * 1024

HGRN_CHUNK = 64
HGRN_LEVELS = 6
HGRN_UNROLL = 2
ATTN_BLOCK = 128
MIX_ROWS = 512
MIX_ITERS = 4
SEQ_BLOCKS = 4
FFN_ROWS = 512

N_SLABS = PROJ_WIDTH // LANES
ATT_SLABS = ATTN_WIDTH // LANES
SLABS_PER_ITER = ATT_SLABS // MIX_ITERS
Q_SLAB = 4 * HGRN_HEADS
K_SLAB = Q_SLAB + ATT_SLABS
V_SLAB = K_SLAB + 1
LOOP_SLABS = Q_SLAB // MIX_ITERS
TAIL_SLABS = N_SLABS - MIX_ITERS * LOOP_SLABS
assert MIX_ITERS * LOOP_SLABS == Q_SLAB and V_SLAB == N_SLABS - 1 and LOOP_SLABS % 2 == 0
assert MIX_ROWS == MIX_ITERS * HGRN_UNROLL * HGRN_CHUNK
assert ATT_SLABS == MIX_ITERS * SLABS_PER_ITER

_NT = (((1,), (1,)), ((), ()))
_TN = (((0,), (0,)), ((), ()))


_LEVEL_Q = ((1, 3, 5, 7), (2, 3, 6, 7), (4, 5, 6, 7))
_LEVEL_K = ((0, 2, 4, 6), (0, 1, 4, 5), (0, 1, 2, 3))
LEVEL_ROWS = 8 * sum(len(q) for q in _LEVEL_Q)


def _level_mask():
    r = lax.broadcasted_iota(jnp.int32, (LEVEL_ROWS, LEVEL_ROWS), 0)
    c = lax.broadcasted_iota(jnp.int32, (LEVEL_ROWS, LEVEL_ROWS), 1)
    lev_r, lev_c = r >> 5, c >> 5
    j_r, j_c = (r >> 3) & 3, (c >> 3) & 3
    same_block = (((lev_r == 0) & (j_r == j_c))
                  | ((lev_r == 1) & ((j_r >> 1) == (j_c >> 1)))
                  | (lev_r == 2))
    return ((lev_r == lev_c) & same_block).astype(jnp.int32)


def _mul1(a, b):
    if a is None:
        return b
    if b is None:
        return a
    return a * b


def _pair_rms(x, ones_bd):
    ss = jnp.dot((x * x).astype(BF16), ones_bd, preferred_element_type=F32)
    return x * lax.rsqrt(ss * (1.0 / ATTN_HD) + EPS)


def _hgrn_chunks(load, store, r0s, c0, c1, og, mask_nat, st_ref, cv_ref):
    nh = HGRN_HEADS
    c_rows = HGRN_CHUNK
    nv = c_rows // 8
    assert nv == 8 and HGRN_LEVELS == 6

    def bc(t, r):
        return jnp.broadcast_to(t[r:r + 1, :], (8, HGRN_DK))

    def cat16(pieces):
        return jnp.concatenate(pieces, axis=0).astype(BF16)

    q_order = _LEVEL_Q[0] + _LEVEL_Q[1] + _LEVEL_Q[2]
    k_order = _LEVEL_K[0] + _LEVEL_K[1] + _LEVEL_K[2]

    def stacked_attend(q_pieces, k_pieces, v_pieces, mask):
        s = [lax.dot_general(cat16(q_pieces[u]), cat16(k_pieces[u]), _NT,
                             preferred_element_type=F32) for u in units]
        s = [jnp.where(mask, s[u], 0.0).astype(BF16) for u in units]
        o = [jnp.dot(s[u], cat16([v_pieces[u][j] for j in k_order]),
                     preferred_element_type=F32) for u in units]
        return [[o[u][8 * n:8 * n + 8, :] for n in range(len(q_order))] for u in units]

    def scatter_add(acc, pieces):
        for n, j in enumerate(q_order):
            acc[j] = pieces[n] if acc[j] is None else acc[j] + pieces[n]
        return acc

    m_nat = mask_nat != 0

    units = range(nh * len(r0s))
    hd = [u % nh for u in units]
    r0 = [r0s[u // nh] for u in units]
    rows = [pl.ds(r0[u], c_rows) for u in units]
    lanes = [slice(hd[u] * HGRN_DK, (hd[u] + 1) * HGRN_DK) for u in units]
    vrange = range(nv)

    fp, kp, qp, vtr = [], [], [], []
    for u in units:
        c0h, c1h = c0[:, lanes[u]], c1[:, lanes[u]]
        fh, kh, qh, vh = [], [], [], []
        for b in vrange:
            srows = pl.ds(r0[u] + b, 8, stride=8)
            ct = c1h * jnp.tanh(0.5 * load(1, hd[u], srows))
            fh.append(c0h + ct)
            kh.append(c1h - ct)
            xq = load(0, hd[u], srows)
            xs = xq * (0.5 * HGRN_DK ** -0.5)
            qh.append(xs * jnp.tanh(0.5 * xq) + xs)
            vh.append(load(2, hd[u], srows))
        fp.append(fh)
        kp.append(kh)
        qp.append(qh)
        vtr.append(vh)

    o_tr = [[jnp.sum(qp[u][b] * kp[u][b], axis=-1, keepdims=True) * vtr[u][b] for b in vrange]
            for u in units]

    p2 = [[fp[u][b] * fp[u][b - 1] if b & 1 else fp[u][b] for b in vrange] for u in units]
    x2 = [[None if b & 1 else fp[u][b + 1] for b in vrange] for u in units]
    p4 = [[p2[u][b] * p2[u][(b & ~3) + 1] if b & 2 else p2[u][b] for b in vrange] for u in units]
    x4 = [[x2[u][b] if b & 2 else _mul1(x2[u][b], p2[u][(b & ~3) + 3]) for b in vrange]
          for u in units]
    p8 = [[p4[u][b] * p4[u][3] if b & 4 else p4[u][b] for b in vrange] for u in units]
    x8 = [[x4[u][b] if b & 4 else _mul1(x4[u][b], p4[u][7]) for b in vrange] for u in units]
    tot8 = [p8[u][7] for u in units]

    for u in units:
        for b in vrange:
            srows = pl.ds(b, 8, stride=8)
            cv_ref[u, 0, srows, :] = qp[u][b] * p8[u][b]
            cv_ref[u, 1, srows, :] = _mul1(kp[u][b], x8[u][b])

    q_pieces = [([qp[u][b] * fp[u][b] for b in _LEVEL_Q[0]]
                 + [qp[u][b] * p2[u][b] for b in _LEVEL_Q[1]]
                 + [qp[u][b] * p4[u][b] for b in _LEVEL_Q[2]]) for u in units]
    k_pieces = [([kp[u][b] for b in _LEVEL_K[0]]
                 + [_mul1(kp[u][b], x2[u][b]) for b in _LEVEL_K[1]]
                 + [_mul1(kp[u][b], x4[u][b]) for b in _LEVEL_K[2]]) for u in units]
    for lvl in range(3):
        group = 1 << lvl
        for jq, b_q in enumerate(_LEVEL_Q[lvl]):
            for jk, b_k in enumerate(_LEVEL_K[lvl]):
                if jq // group != jk // group:
                    continue
                nq, nk = 4 * lvl + jq, 4 * lvl + jk
                for u in units:
                    d = jnp.sum(q_pieces[u][nq] * k_pieces[u][nk], axis=-1, keepdims=True)
                    o_tr[u][b_q] = o_tr[u][b_q] + d * vtr[u][b_k]

    an = [[cv_ref[u, 0, 8 * a:8 * a + 8, :] for a in vrange] for u in units]
    cn = [[cv_ref[u, 1, 8 * a:8 * a + 8, :] for a in vrange] for u in units]
    vnat = [[load(2, hd[u], pl.ds(r0[u] + 8 * a, 8)) for a in vrange] for u in units]

    q4 = [[an[u][a] for a in _LEVEL_Q[0]] for u in units]
    k4 = [[cn[u][a] for a in _LEVEL_K[0]] for u in units]
    an = [[an[u][a] * bc(tot8[u], a - 1) if a & 1 else an[u][a] for a in vrange] for u in units]
    cn = [[cn[u][a] if a & 1 else cn[u][a] * bc(tot8[u], a + 1) for a in vrange] for u in units]
    t16 = [tot8[u] * pltpu.roll(tot8[u], 1, 0) for u in units]
    q5 = [[an[u][a] for a in _LEVEL_Q[1]] for u in units]
    k5 = [[cn[u][a] for a in _LEVEL_K[1]] for u in units]
    an = [[an[u][a] * bc(t16[u], (a & ~3) + 1) if a & 2 else an[u][a] for a in vrange]
          for u in units]
    cn = [[cn[u][a] if a & 2 else cn[u][a] * bc(t16[u], (a & ~3) + 3) for a in vrange]
          for u in units]
    t32 = [t16[u] * pltpu.roll(t16[u], 2, 0) for u in units]
    q6 = [[an[u][a] for a in _LEVEL_Q[2]] for u in units]
    k6 = [[cn[u][a] for a in _LEVEL_K[2]] for u in units]
    an = [[an[u][a] * bc(t32[u], 3) if a & 4 else an[u][a] for a in vrange] for u in units]
    cn = [[cn[u][a] if a & 4 else cn[u][a] * bc(t32[u], 7) for a in vrange] for u in units]
    att_nat = stacked_attend([q4[u] + q5[u] + q6[u] for u in units],
                             [k4[u] + k5[u] + k6[u] for u in units], vnat, m_nat)
    o_nat = []
    for u in units:
        pieces = scatter_add([None] * nv, att_nat[u])
        pieces[0] = jnp.zeros((8, HGRN_DV), F32)
        o_nat.append(jnp.concatenate(pieces, axis=0))

    for u in units:
        for b in vrange:
            cv_ref[u, 2, pl.ds(b, 8, stride=8), :] = o_tr[u][b]

    upd = [lax.dot_general(cat16(vnat[u]), cat16(cn[u]),
                           _TN, preferred_element_type=F32) for u in units]
    for h in range(nh):
        st = st_ref[h]
        for u in range(h, len(units), nh):
            qs = cat16(an[u])
            o_nat[u] = o_nat[u] + lax.dot_general(qs, st.astype(BF16), _NT,
                                                  preferred_element_type=F32)
            dec = t32[u][3:4, :] * t32[u][7:8, :]
            st = st * dec + upd[u]
        st_ref[h] = st

    for u in units:
        o_h = o_nat[u] + cv_ref[u, 2]
        ms = jnp.mean(o_h * o_h, axis=-1, keepdims=True)
        xgs = 0.5 * load(3, hd[u], rows[u])
        gate = xgs * jnp.tanh(xgs) + xgs
        store(hd[u], rows[u], o_h * lax.rsqrt(ms + EPS) * og * gate)


def _swa_slab(q_slab, k_lo, k_hi, v_lo, v_hi, ones_pad, bias_first, bias_any,
              sink_a, sink_b, qgain, ones_bd, lo128, store):
    blk = ATTN_BLOCK
    nblk = MIX_ROWS // blk
    state = {}

    def prep():
        state["qs"] = (_pair_rms(q_slab(), ones_bd) * qgain).astype(BF16)
        state["sink_slab"] = jnp.where(lo128, sink_a, sink_b)

    def scores(n):
        keys = slice(n * blk, (n + 2) * blk)
        k_cat = jnp.concatenate([k_lo()[keys], k_hi()[keys]], axis=0)
        s = lax.dot_general(state["qs"][n * blk:(n + 1) * blk], k_cat, _NT,
                            preferred_element_type=F32)
        s = s + (bias_first() if n == 0 else bias_any())
        m_a = jnp.max(s[:, :2 * blk], axis=-1, keepdims=True)
        m_b = jnp.max(s[:, 2 * blk:], axis=-1, keepdims=True)
        m_a = jnp.maximum(jnp.broadcast_to(m_a, (blk, LANES)), sink_a)
        m_b = jnp.maximum(jnp.broadcast_to(m_b, (blk, LANES)), sink_b)
        cols = [s[:, c * LANES:(c + 1) * LANES] - (m_a if c < 2 * blk // LANES else m_b)
                for c in range(4 * blk // LANES)]
        state["p", n] = jnp.exp2(jnp.concatenate(cols, axis=1)).astype(BF16)
        state["m", n] = jnp.where(lo128, m_a, m_b)

    def attend(n):
        keys = slice(n * blk, (n + 2) * blk)
        v_cat = jnp.concatenate([v_lo()[keys], v_hi()[keys]], axis=0)
        rhs = jnp.concatenate([v_cat, ones_pad], axis=1)
        nd = jnp.dot(state.pop(("p", n)), rhs, preferred_element_type=F32)
        den = nd[:, LANES:] + jnp.exp2(state["sink_slab"] - state.pop(("m", n)))
        store(n, nd[:, :LANES] / den)

    stages = [prep]
    for n in range(nblk):
        stages += [lambda n=n: scores(n), lambda n=n: attend(n)]
    return stages


_CAST_SHAPES = ((D_MODEL, D_MODEL, 32), (D_MODEL, D_FF, 32), (D_MODEL, D_FF, 32), (D_FF, D_MODEL, 96))
N_CAST = len(_CAST_SHAPES)


def _cast_chunk(k, step):
    rows, _, per = _CAST_SHAPES[k]
    n = -(-rows // per)
    start = jnp.minimum(step * per, rows - per)
    return n, pl.multiple_of(start, 16)


def _cast_copies(k, step, src_refs, dst_refs, fbuf_refs, bbuf_refs, sem_in, sem_out):
    per = _CAST_SHAPES[k][2]
    _, start = _cast_chunk(k, step)
    rows = pl.ds(start, per)
    return (pltpu.make_async_copy(src_refs[k].at[rows, :], fbuf_refs[k], sem_in.at[k]),
            pltpu.make_async_copy(bbuf_refs[k], dst_refs[k].at[rows, :], sem_out.at[k]))


def _mix_kernel(x_ref, gain_ref, wm_ref, wt_ref, lbl_ref, og_ref, qg_ref, kg_ref, sink_ref,
                *refs):
    cast_src = refs[:N_CAST]
    rec_ref, att_ref = refs[N_CAST:N_CAST + 2]
    cast_dst = refs[N_CAST + 2:2 * N_CAST + 2]
    (u_ref, pa_ref, pb_ref, st_ref, cv_ref, kvp_ref, kpad_ref, vpad_ref, bias_ref,
     lev_ref) = refs[2 * N_CAST + 2:2 * N_CAST + 12]
    cast_f = refs[2 * N_CAST + 12:3 * N_CAST + 12]
    cast_b = refs[3 * N_CAST + 12:4 * N_CAST + 12]
    sem_in, sem_out = refs[4 * N_CAST + 12:]
    g = pl.program_id(0)
    blk = ATTN_BLOCK

    def cast_copies(k, step):
        return _cast_copies(k, step, cast_src, cast_dst, cast_f, cast_b, sem_in, sem_out)

    cast_groups = {}
    for k in range(N_CAST):
        cast_groups.setdefault(_cast_chunk(k, g)[0], []).append(k)

    for n_chunks, members in cast_groups.items():
        @pl.when(g < n_chunks)
        def _(members=members):
            for k in members:
                cast_copies(k, g)[0].start()

    lo128 = lax.broadcasted_iota(jnp.int32, (1, LANES), 1) < ATTN_HD

    @pl.when(g == 0)
    def _():
        kvp_ref[...] = jnp.zeros_like(kvp_ref)
        lev_ref[...] = _level_mask()
        qi = lax.broadcasted_iota(jnp.int32, (blk, 2 * blk), 0)
        kj = lax.broadcasted_iota(jnp.int32, (blk, 2 * blk), 1)
        rel = qi + blk - kj
        in_window = (rel >= 0) & (rel < WINDOW)
        b_any = jnp.where(in_window, 0.0, NEG_INF)
        b_first = jnp.where(in_window & (kj >= blk), 0.0, NEG_INF)
        bias_ref[0] = jnp.concatenate([b_any, b_any], axis=1)
        bias_ref[1] = jnp.concatenate([b_first, b_first], axis=1)

    first_blk = lax.rem(g + SEQ_BLOCKS - 1, SEQ_BLOCKS) == 0

    @pl.when(first_blk)
    def _():
        st_ref[...] = jnp.zeros_like(st_ref)

    hr = lax.broadcasted_iota(jnp.int32, (LANES, LANES), 0) < ATTN_HD
    hc = lax.broadcasted_iota(jnp.int32, (LANES, LANES), 1) < ATTN_HD
    ones_bd = jnp.where(hr == hc, 1.0, 0.0).astype(BF16)
    pair = lambda r: jnp.concatenate([r, r], axis=1)
    qgain = pair(qg_ref[...]) * (ATTN_HD ** -0.5 * LOG2E)
    kgain = pair(kg_ref[...])
    ones_lo = jnp.broadcast_to(jnp.where(lo128, 1.0, 0.0), (2 * blk, LANES))
    ones_pad = jnp.concatenate([ones_lo, 1.0 - ones_lo], axis=0).astype(BF16)

    l0 = lbl_ref[0:1, :]
    l1 = lbl_ref[1:2, :]
    lmax = jnp.maximum(l0, l1)
    e0 = jnp.exp(l0 - lmax)
    e1 = jnp.exp(l1 - lmax)
    lb = e0 / (e0 + e1)
    c0 = 0.5 * (1.0 + lb)
    c1 = 0.5 * (1.0 - lb)
    og = og_ref[...]
    first_idx = first_blk.astype(jnp.int32)

    def step(pc_ref, pp_ref):
        _mix_step(pc_ref, pp_ref, x_ref, gain_ref, u_ref, wm_ref, wt_ref, rec_ref, att_ref,
                  st_ref, cv_ref, kvp_ref, kpad_ref, vpad_ref, bias_ref, lev_ref, sink_ref,
                  (lo128, ones_bd, qgain, kgain, ones_pad, c0, c1, og, first_idx))

    @pl.when(g == 0)
    def _():
        _project_only(pa_ref, x_ref, gain_ref, u_ref, wm_ref, wt_ref)

    @pl.when((lax.rem(g, 2) == 0) & (g > 0))
    def _():
        step(pa_ref, pb_ref)

    @pl.when(lax.rem(g, 2) == 1)
    def _():
        step(pb_ref, pa_ref)

    for n_chunks, members in cast_groups.items():
        @pl.when((g >= 1) & (g <= n_chunks))
        def _(members=members):
            for k in members:
                cast_copies(k, g - 1)[1].wait()

        @pl.when(g < n_chunks)
        def _(members=members):
            for k in members:
                copy_in, copy_out = cast_copies(k, g)
                copy_in.wait()
                cast_b[k][...] = cast_f[k][...].astype(BF16)
                copy_out.start()


def _normalize(x_ref, gain_ref, u_ref):
    x = x_ref[...]
    ms = jnp.mean(x * x, axis=-1, keepdims=True)
    u_ref[...] = (x * lax.rsqrt(ms + EPS) * gain_ref[...]).astype(BF16)


def _project_tail(pc_ref, u_ref, wt_ref):
    tail = jnp.dot(u_ref[...], wt_ref[...], preferred_element_type=F32)
    for t in range(TAIL_SLABS):
        pc_ref[MIX_ITERS * LOOP_SLABS + t] = tail[:, t * LANES:(t + 1) * LANES]


def _project_main(pc_ref, u_ref, wm_ref, i):
    res = jnp.dot(u_ref[...], wm_ref[i], preferred_element_type=F32)
    for t in range(LOOP_SLABS):
        pc_ref[i * LOOP_SLABS + t] = res[:, t * LANES:(t + 1) * LANES]


def _project_only(pc_ref, x_ref, gain_ref, u_ref, wm_ref, wt_ref):
    _normalize(x_ref, gain_ref, u_ref)
    for i in range(MIX_ITERS):
        _project_main(pc_ref, u_ref, wm_ref, i)
    _project_tail(pc_ref, u_ref, wt_ref)


def _mix_step(pc_ref, pp_ref, x_ref, gain_ref, u_ref, wm_ref, wt_ref, rec_ref, att_ref, st_ref,
              cv_ref, kvp_ref, kpad_ref, vpad_ref, bias_ref, lev_ref, sink_ref, consts):
    (lo128, ones_bd, qgain, kgain, ones_pad, c0, c1, og, first_idx) = consts
    blk = ATTN_BLOCK
    _normalize(x_ref, gain_ref, u_ref)

    k_all = jnp.concatenate([kvp_ref[0], pp_ref[K_SLAB]], axis=0)
    v_all = jnp.concatenate([kvp_ref[1], pp_ref[V_SLAB]], axis=0)
    kvp_ref[0] = pp_ref[K_SLAB, MIX_ROWS - blk:, :]
    kvp_ref[1] = pp_ref[V_SLAB, MIX_ROWS - blk:, :]
    kn = _pair_rms(k_all, ones_bd) * kgain
    kn_sw = pltpu.roll(kn, ATTN_HD, 1)
    v_sw = pltpu.roll(v_all, ATTN_HD, 1)
    zero = jnp.zeros_like(kn)
    kpad_ref[0, 0] = jnp.where(lo128, kn, zero).astype(BF16)
    kpad_ref[0, 1] = jnp.where(lo128, zero, kn_sw).astype(BF16)
    kpad_ref[1, 0] = jnp.where(lo128, kn_sw, zero).astype(BF16)
    kpad_ref[1, 1] = jnp.where(lo128, zero, kn).astype(BF16)
    vpad_ref[0, 0] = jnp.where(lo128, v_all, zero).astype(BF16)
    vpad_ref[0, 1] = jnp.where(lo128, zero, v_sw).astype(BF16)
    vpad_ref[1, 0] = jnp.where(lo128, v_sw, zero).astype(BF16)
    vpad_ref[1, 1] = jnp.where(lo128, zero, v_all).astype(BF16)

    def body(i, carry):
        def load(part, h, rows):
            return pp_ref[part * HGRN_HEADS + h, rows, :]

        def store(h, rows, tile):
            rec_ref[0, rows, h * HGRN_DV:(h + 1) * HGRN_DV] = tile.astype(rec_ref.dtype)

        r0s = [pl.multiple_of((i * HGRN_UNROLL + cc) * HGRN_CHUNK, HGRN_CHUNK)
               for cc in range(HGRN_UNROLL)]

        def slab_stages(c):
            kvh = lax.div(c, ATTN_GROUP // 2)

            def store_att(n, tile):
                att_ref[0, c, n * blk:(n + 1) * blk, :] = tile.astype(att_ref.dtype)

            return _swa_slab(lambda: pp_ref[Q_SLAB + c],
                             lambda: kpad_ref[kvh, 0], lambda: kpad_ref[kvh, 1],
                             lambda: vpad_ref[kvh, 0], lambda: vpad_ref[kvh, 1], ones_pad,
                             lambda: bias_ref[first_idx], lambda: bias_ref[0],
                             sink_ref[2 * c] * LOG2E, sink_ref[2 * c + 1] * LOG2E,
                             qgain, ones_bd, lo128, store_att)

        swa = [slab_stages(i * SLABS_PER_ITER + s) for s in range(SLABS_PER_ITER)]

        _project_main(pc_ref, u_ref, wm_ref, i)
        _hgrn_chunks(load, store, r0s, c0, c1, og, lev_ref[...], st_ref, cv_ref)
        for stages in zip(*swa):
            for stage in stages:
                stage()
        return carry

    lax.fori_loop(0, MIX_ITERS, body, 0, unroll=True)
    _project_tail(pc_ref, u_ref, wt_ref)


def _mix(x2, seq_len, gain, w_main, w_tail, lb_logits, out_gain, q_gain, k_gain, sinks,
         ffn_weights):
    t = x2.shape[0]
    nblk = t // MIX_ROWS
    assert seq_len // MIX_ROWS == SEQ_BLOCKS
    batch = t // seq_len
    assert tuple(w.shape for w in ffn_weights) == tuple(s[:2] for s in _CAST_SHAPES)
    assert all(-(-rows // per) <= nblk for rows, _, per in _CAST_SHAPES)
    hbm = pl.BlockSpec(memory_space=pl.ANY)

    def in_rows(gi):
        return (jnp.minimum(gi, nblk - 1), 0)

    def out_block(gi):
        blk_i = jnp.maximum(gi - 1, 0)
        return blk_i // SEQ_BLOCKS, lax.rem(blk_i, SEQ_BLOCKS)

    full = lambda shape: pl.BlockSpec(shape, lambda gi: (0,) * len(shape))
    return pl.pallas_call(
        _mix_kernel,
        grid=(nblk + 1,),
        in_specs=[
            pl.BlockSpec((MIX_ROWS, D_MODEL), in_rows),
            full((1, D_MODEL)),
            full((MIX_ITERS, D_MODEL, LOOP_SLABS * LANES)),
            full((D_MODEL, TAIL_SLABS * LANES)),
            full((2, HGRN_WIDTH)),
            full((1, HGRN_DV)),
            full((1, ATTN_HD)),
            full((1, ATTN_HD)),
            pl.BlockSpec(memory_space=pltpu.SMEM),
        ] + [hbm] * N_CAST,
        out_specs=[
            pl.BlockSpec((1, MIX_ROWS, HGRN_WIDTH),
                         lambda gi: (*out_block(gi), 0)),
            pl.BlockSpec((1, ATT_SLABS, MIX_ROWS, LANES),
                         lambda gi: (out_block(gi)[0], 0, out_block(gi)[1], 0)),
        ] + [hbm] * N_CAST,
        out_shape=[
            jax.ShapeDtypeStruct((batch, seq_len, HGRN_WIDTH), BF16),
            jax.ShapeDtypeStruct((batch, ATT_SLABS, seq_len, LANES), BF16),
        ] + [jax.ShapeDtypeStruct(w.shape, BF16) for w in ffn_weights],
        scratch_shapes=[
            pltpu.VMEM((MIX_ROWS, D_MODEL), BF16),
            pltpu.VMEM((N_SLABS, MIX_ROWS, LANES), F32),
            pltpu.VMEM((N_SLABS, MIX_ROWS, LANES), F32),
            pltpu.VMEM((HGRN_HEADS, HGRN_DV, HGRN_DK), F32),
            pltpu.VMEM((HGRN_HEADS * HGRN_UNROLL, 3, HGRN_CHUNK, HGRN_DK), F32),
            pltpu.VMEM((2, ATTN_BLOCK, LANES), F32),
            pltpu.VMEM((ATTN_KV_HEADS, 2, ATTN_BLOCK + MIX_ROWS, LANES), BF16),
            pltpu.VMEM((ATTN_KV_HEADS, 2, ATTN_BLOCK + MIX_ROWS, LANES), BF16),
            pltpu.VMEM((2, ATTN_BLOCK, 4 * ATTN_BLOCK), F32),
            pltpu.VMEM((LEVEL_ROWS, LEVEL_ROWS), jnp.int32),
        ] + [pltpu.VMEM((per, cols), F32) for _, cols, per in _CAST_SHAPES]
        + [pltpu.VMEM((per, cols), BF16) for _, cols, per in _CAST_SHAPES]
        + [pltpu.SemaphoreType.DMA((N_CAST,)), pltpu.SemaphoreType.DMA((N_CAST,))],
        compiler_params=pltpu.CompilerParams(
            dimension_semantics=("arbitrary",), vmem_limit_bytes=VMEM_LIMIT_BYTES),
        name="mixer",
    )(x2, gain, w_main, w_tail, lb_logits, out_gain, q_gain, k_gain, sinks, *ffn_weights)


def _ffn_kernel(x_ref, rec_ref, att_ref, wo_ref, gain_ref, wg_ref, wu_ref, wd_ref, o_ref):
    mixed = jnp.concatenate([rec_ref[0]] + [att_ref[0, c] for c in range(ATT_SLABS)],
                            axis=1)
    h = x_ref[...] + jnp.dot(mixed, wo_ref[...], preferred_element_type=F32)
    ms = jnp.mean(h * h, axis=-1, keepdims=True)
    u = (h * lax.rsqrt(ms + EPS) * gain_ref[...]).astype(BF16)
    gate = jnp.dot(u, wg_ref[...], preferred_element_type=F32)
    up = jnp.dot(u, wu_ref[...], preferred_element_type=F32)
    gs = 0.5 * gate
    act = ((gs * jnp.tanh(gs) + gs) * up).astype(BF16)
    o_ref[...] = h + jnp.dot(act, wd_ref[...], preferred_element_type=F32)


def _ffn(x2, rec, att, wo, gain, wg, wu, wd):
    t = x2.shape[0]
    seq_blocks = rec.shape[1] // FFN_ROWS

    def resident(shape):
        return pl.BlockSpec(shape, lambda i: (0, 0), pipeline_mode=pl.Buffered(1))

    return pl.pallas_call(
        _ffn_kernel,
        grid=(t // FFN_ROWS,),
        in_specs=[
            pl.BlockSpec((FFN_ROWS, D_MODEL), lambda i: (i, 0)),
            pl.BlockSpec((1, FFN_ROWS, HGRN_WIDTH),
                         lambda i: (i // seq_blocks, lax.rem(i, seq_blocks), 0)),
            pl.BlockSpec((1, ATT_SLABS, FFN_ROWS, LANES),
                         lambda i: (i // seq_blocks, 0, lax.rem(i, seq_blocks), 0)),
            resident((D_MODEL, D_MODEL)),
            resident((1, D_MODEL)),
            resident((D_MODEL, D_FF)),
            resident((D_MODEL, D_FF)),
            resident((D_FF, D_MODEL)),
        ],
        out_specs=pl.BlockSpec((FFN_ROWS, D_MODEL), lambda i: (i, 0)),
        out_shape=jax.ShapeDtypeStruct((t, D_MODEL), F32),
        compiler_params=pltpu.CompilerParams(
            dimension_semantics=("arbitrary",), vmem_limit_bytes=VMEM_LIMIT_BYTES),
        name="outproj_ffn",
    )(x2, rec, att, wo, gain, wg, wu, wd)


def kernel(x, norm1_gain, w_in, hgrn_lb_logits, hgrn_out_gain, q_norm_gain, k_norm_gain,
           attn_sinks, w_out, norm2_gain, w_ffn_gate, w_ffn_up, w_ffn_down):
    b, s, d = x.shape
    assert (d, w_in.shape[0]) == (D_MODEL, 1), "single-layer kernel"
    assert s == SEQ_BLOCKS * MIX_ROWS and MIX_ROWS == FFN_ROWS
    t = b * s
    x2 = x.reshape(t, d)

    w_bf = w_in[0].astype(BF16)
    n_main = MIX_ITERS * LOOP_SLABS * LANES
    w_main = w_bf[:, :n_main].reshape(D_MODEL, MIX_ITERS, LOOP_SLABS * LANES).transpose(1, 0, 2)
    w_tail = w_bf[:, n_main:]
    rec, att, wo, wg, wu, wd = _mix(
        x2, s, norm1_gain[0][None, :], w_main, w_tail, hgrn_lb_logits,
        hgrn_out_gain[0][None, :], q_norm_gain[0][None, :], k_norm_gain[0][None, :],
        attn_sinks[0], (w_out[0], w_ffn_gate[0], w_ffn_up[0], w_ffn_down[0]))
    out = _ffn(x2, rec, att, wo, norm2_gain[0][None, :], wg, wu, wd)
    return out.reshape(b, s, d)
```

```python
import jax
import jax.numpy as jnp
from jax import lax
from jax.experimental import pallas as pl
from jax.experimental.pallas import tpu as pltpu

D_MODEL = 1024
HGRN_HEADS = 4
HGRN_DK = 128
HGRN_DV = 128
HGRN_WIDTH = HGRN_HEADS * HGRN_DK
ATTN_HEADS = 8
ATTN_KV_HEADS = 2
ATTN_GROUP = ATTN_HEADS // ATTN_KV_HEADS
ATTN_HD = 64
ATTN_WIDTH = ATTN_HEADS * ATTN_HD
KV_WIDTH = ATTN_KV_HEADS * ATTN_HD
WINDOW = 128
PROJ_WIDTH = 4 * HGRN_WIDTH + ATTN_WIDTH + 2 * KV_WIDTH
D_FF = 2816
EPS = 1e-6
NEG_INF = -1e30
LOG2E = 1.4426950408889634

F32 = jnp.float32
BF16 = jnp.bfloat16

LANES = 128
VMEM_LIMIT_BYTES = 56 * 1024 * 1024

HGRN_CHUNK = 64
HGRN_LEVELS = 6
HGRN_UNROLL = 2
ATTN_BLOCK = 128
MIX_ROWS = 512
MIX_ITERS = 4
SEQ_BLOCKS = 4
FFN_ROWS = 512

N_SLABS = PROJ_WIDTH // LANES
ATT_SLABS = ATTN_WIDTH // LANES
SLABS_PER_ITER = ATT_SLABS // MIX_ITERS
Q_SLAB = 4 * HGRN_HEADS
K_SLAB = Q_SLAB + ATT_SLABS
V_SLAB = K_SLAB + 1
LOOP_SLABS = Q_SLAB // MIX_ITERS
TAIL_SLABS = N_SLABS - MIX_ITERS * LOOP_SLABS
assert MIX_ITERS * LOOP_SLABS == Q_SLAB and V_SLAB == N_SLABS - 1 and LOOP_SLABS % 2 == 0
assert MIX_ROWS == MIX_ITERS * HGRN_UNROLL * HGRN_CHUNK
assert ATT_SLABS == MIX_ITERS * SLABS_PER_ITER

_NT = (((1,), (1,)), ((), ()))
_TN = (((0,), (0,)), ((), ()))


_LEVEL_Q = ((1, 3, 5, 7), (2, 3, 6, 7), (4, 5, 6, 7))
_LEVEL_K = ((0, 2, 4, 6), (0, 1, 4, 5), (0, 1, 2, 3))
LEVEL_ROWS = 8 * sum(len(q) for q in _LEVEL_Q)


def _level_mask():
    r = lax.broadcasted_iota(jnp.int32, (LEVEL_ROWS, LEVEL_ROWS), 0)
    c = lax.broadcasted_iota(jnp.int32, (LEVEL_ROWS, LEVEL_ROWS), 1)
    lev_r, lev_c = r >> 5, c >> 5
    j_r, j_c = (r >> 3) & 3, (c >> 3) & 3
    same_block = (((lev_r == 0) & (j_r == j_c))
                  | ((lev_r == 1) & ((j_r >> 1) == (j_c >> 1)))
                  | (lev_r == 2))
    return ((lev_r == lev_c) & same_block).astype(jnp.int32)


def _mul1(a, b):
    if a is None:
        return b
    if b is None:
        return a
    return a * b


def _pair_rms(x, ones_bd):
    ss = jnp.dot((x * x).astype(BF16), ones_bd, preferred_element_type=F32)
    return x * lax.rsqrt(ss * (1.0 / ATTN_HD) + EPS)


def _hgrn_chunks(load, store, r0s, c0, c1, og, mask_nat, st_ref, cv_ref):
    nh = HGRN_HEADS
    c_rows = HGRN_CHUNK
    nv = c_rows // 8
    assert nv == 8 and HGRN_LEVELS == 6

    def bc(t, r):
        return jnp.broadcast_to(t[r:r + 1, :], (8, HGRN_DK))

    def cat16(pieces):
        return jnp.concatenate(pieces, axis=0).astype(BF16)

    q_order = _LEVEL_Q[0] + _LEVEL_Q[1] + _LEVEL_Q[2]
    k_order = _LEVEL_K[0] + _LEVEL_K[1] + _LEVEL_K[2]

    def stacked_attend(q_pieces, k_pieces, v_pieces, mask):
        s = [lax.dot_general(cat16(q_pieces[u]), cat16(k_pieces[u]), _NT,
                             preferred_element_type=F32) for u in units]
        s = [jnp.where(mask, s[u], 0.0).astype(BF16) for u in units]
        o = [jnp.dot(s[u], cat16([v_pieces[u][j] for j in k_order]),
                     preferred_element_type=F32) for u in units]
        return [[o[u][8 * n:8 * n + 8, :] for n in range(len(q_order))] for u in units]

    def scatter_add(acc, pieces):
        for n, j in enumerate(q_order):
            acc[j] = pieces[n] if acc[j] is None else acc[j] + pieces[n]
        return acc

    m_nat = mask_nat != 0

    units = range(nh * len(r0s))
    hd = [u % nh for u in units]
    r0 = [r0s[u // nh] for u in units]
    rows = [pl.ds(r0[u], c_rows) for u in units]
    lanes = [slice(hd[u] * HGRN_DK, (hd[u] + 1) * HGRN_DK) for u in units]
    vrange = range(nv)

    fp, kp, qp, vtr = [], [], [], []
    for u in units:
        c0h, c1h = c0[:, lanes[u]], c1[:, lanes[u]]
        fh, kh, qh, vh = [], [], [], []
        for b in vrange:
            srows = pl.ds(r0[u] + b, 8, stride=8)
            ct = c1h * jnp.tanh(0.5 * load(1, hd[u], srows))
            fh.append(c0h + ct)
            kh.append(c1h - ct)
            xq = load(0, hd[u], srows)
            xs = xq * (0.5 * HGRN_DK ** -0.5)
            qh.append(xs * jnp.tanh(0.5 * xq) + xs)
            vh.append(load(2, hd[u], srows))
        fp.append(fh)
        kp.append(kh)
        qp.append(qh)
        vtr.append(vh)

    o_tr = [[jnp.sum(qp[u][b] * kp[u][b], axis=-1, keepdims=True) * vtr[u][b] for b in vrange]
            for u in units]

    p2 = [[fp[u][b] * fp[u][b - 1] if b & 1 else fp[u][b] for b in vrange] for u in units]
    x2 = [[None if b & 1 else fp[u][b + 1] for b in vrange] for u in units]
    p4 = [[p2[u][b] * p2[u][(b & ~3) + 1] if b & 2 else p2[u][b] for b in vrange] for u in units]
    x4 = [[x2[u][b] if b & 2 else _mul1(x2[u][b], p2[u][(b & ~3) + 3]) for b in vrange]
          for u in units]
    p8 = [[p4[u][b] * p4[u][3] if b & 4 else p4[u][b] for b in vrange] for u in units]
    x8 = [[x4[u][b] if b & 4 else _mul1(x4[u][b], p4[u][7]) for b in vrange] for u in units]
    tot8 = [p8[u][7] for u in units]

    for u in units:
        for b in vrange:
            srows = pl.ds(b, 8, stride=8)
            cv_ref[u, 0, srows, :] = qp[u][b] * p8[u][b]
            cv_ref[u, 1, srows, :] = _mul1(kp[u][b], x8[u][b])

    q_pieces = [([qp[u][b] * fp[u][b] for b in _LEVEL_Q[0]]
                 + [qp[u][b] * p2[u][b] for b in _LEVEL_Q[1]]
                 + [qp[u][b] * p4[u][b] for b in _LEVEL_Q[2]]) for u in units]
    k_pieces = [([kp[u][b] for b in _LEVEL_K[0]]
                 + [_mul1(kp[u][b], x2[u][b]) for b in _LEVEL_K[1]]
                 + [_mul1(kp[u][b], x4[u][b]) for b in _LEVEL_K[2]]) for u in units]
    for lvl in range(3):
        group = 1 << lvl
        for jq, b_q in enumerate(_LEVEL_Q[lvl]):
            for jk, b_k in enumerate(_LEVEL_K[lvl]):
                if jq // group != jk // group:
                    continue
                nq, nk = 4 * lvl + jq, 4 * lvl + jk
                for u in units:
                    d = jnp.sum(q_pieces[u][nq] * k_pieces[u][nk], axis=-1, keepdims=True)
                    o_tr[u][b_q] = o_tr[u][b_q] + d * vtr[u][b_k]

    an = [[cv_ref[u, 0, 8 * a:8 * a + 8, :] for a in vrange] for u in units]
    cn = [[cv_ref[u, 1, 8 * a:8 * a + 8, :] for a in vrange] for u in units]
    vnat = [[load(2, hd[u], pl.ds(r0[u] + 8 * a, 8)) for a in vrange] for u in units]

    q4 = [[an[u][a] for a in _LEVEL_Q[0]] for u in units]
    k4 = [[cn[u][a] for a in _LEVEL_K[0]] for u in units]
    an = [[an[u][a] * bc(tot8[u], a - 1) if a & 1 else an[u][a] for a in vrange] for u in units]
    cn = [[cn[u][a] if a & 1 else cn[u][a] * bc(tot8[u], a + 1) for a in vrange] for u in units]
    t16 = [tot8[u] * pltpu.roll(tot8[u], 1, 0) for u in units]
    q5 = [[an[u][a] for a in _LEVEL_Q[1]] for u in units]
    k5 = [[cn[u][a] for a in _LEVEL_K[1]] for u in units]
    an = [[an[u][a] * bc(t16[u], (a & ~3) + 1) if a & 2 else an[u][a] for a in vrange]
          for u in units]
    cn = [[cn[u][a] if a & 2 else cn[u][a] * bc(t16[u], (a & ~3) + 3) for a in vrange]
          for u in units]
    t32 = [t16[u] * pltpu.roll(t16[u], 2, 0) for u in units]
    q6 = [[an[u][a] for a in _LEVEL_Q[2]] for u in units]
    k6 = [[cn[u][a] for a in _LEVEL_K[2]] for u in units]
    an = [[an[u][a] * bc(t32[u], 3) if a & 4 else an[u][a] for a in vrange] for u in units]
    cn = [[cn[u][a] if a & 4 else cn[u][a] * bc(t32[u], 7) for a in vrange] for u in units]
    att_nat = stacked_attend([q4[u] + q5[u] + q6[u] for u in units],
                             [k4[u] + k5[u] + k6[u] for u in units], vnat, m_nat)
    o_nat = []
    for u in units:
        pieces = scatter_add([None] * nv, att_nat[u])
        pieces[0] = jnp.zeros((8, HGRN_DV), F32)
        o_nat.append(jnp.concatenate(pieces, axis=0))

    for u in units:
        for b in vrange:
            cv_ref[u, 2, pl.ds(b, 8, stride=8), :] = o_tr[u][b]

    upd = [lax.dot_general(cat16(vnat[u]), cat16(cn[u]),
                           _TN, preferred_element_type=F32) for u in units]
    for h in range(nh):
        st = st_ref[h]
        for u in range(h, len(units), nh):
            qs = cat16(an[u])
            o_nat[u] = o_nat[u] + lax.dot_general(qs, st.astype(BF16), _NT,
                                                  preferred_element_type=F32)
            dec = t32[u][3:4, :] * t32[u][7:8, :]
            st = st * dec + upd[u]
        st_ref[h] = st

    for u in units:
        o_h = o_nat[u] + cv_ref[u, 2]
        ms = jnp.mean(o_h * o_h, axis=-1, keepdims=True)
        xgs = 0.5 * load(3, hd[u], rows[u])
        gate = xgs * jnp.tanh(xgs) + xgs
        store(hd[u], rows[u], o_h * lax.rsqrt(ms + EPS) * og * gate)


def _swa_slab(q_slab, k_lo, k_hi, v_lo, v_hi, ones_pad, bias_first, bias_any,
              sink_a, sink_b, qgain, ones_bd, lo128, store):
    blk = ATTN_BLOCK
    nblk = MIX_ROWS // blk
    state = {}

    def prep():
        state["qs"] = (_pair_rms(q_slab(), ones_bd) * qgain).astype(BF16)
        state["sink_slab"] = jnp.where(lo128, sink_a, sink_b)

    def scores(n):
        keys = slice(n * blk, (n + 2) * blk)
        k_cat = jnp.concatenate([k_lo()[keys], k_hi()[keys]], axis=0)
        s = lax.dot_general(state["qs"][n * blk:(n + 1) * blk], k_cat, _NT,
                            preferred_element_type=F32)
        s = s + (bias_first() if n == 0 else bias_any())
        m_a = jnp.max(s[:, :2 * blk], axis=-1, keepdims=True)
        m_b = jnp.max(s[:, 2 * blk:], axis=-1, keepdims=True)
        m_a = jnp.maximum(jnp.broadcast_to(m_a, (blk, LANES)), sink_a)
        m_b = jnp.maximum(jnp.broadcast_to(m_b, (blk, LANES)), sink_b)
        cols = [s[:, c * LANES:(c + 1) * LANES] - (m_a if c < 2 * blk // LANES else m_b)
                for c in range(4 * blk // LANES)]
        state["p", n] = jnp.exp2(jnp.concatenate(cols, axis=1)).astype(BF16)
        state["m", n] = jnp.where(lo128, m_a, m_b)

    def attend(n):
        keys = slice(n * blk, (n + 2) * blk)
        v_cat = jnp.concatenate([v_lo()[keys], v_hi()[keys]], axis=0)
        rhs = jnp.concatenate([v_cat, ones_pad], axis=1)
        nd = jnp.dot(state.pop(("p", n)), rhs, preferred_element_type=F32)
        den = nd[:, LANES:] + jnp.exp2(state["sink_slab"] - state.pop(("m", n)))
        store(n, nd[:, :LANES] / den)

    stages = [prep]
    for n in range(nblk):
        stages += [lambda n=n: scores(n), lambda n=n: attend(n)]
    return stages


_CAST_SHAPES = ((D_MODEL, D_MODEL, 32), (D_MODEL, D_FF, 32), (D_MODEL, D_FF, 32), (D_FF, D_MODEL, 96))
N_CAST = len(_CAST_SHAPES)


def _cast_chunk(k, step):
    rows, _, per = _CAST_SHAPES[k]
    n = -(-rows // per)
    start = jnp.minimum(step * per, rows - per)
    return n, pl.multiple_of(start, 16)


def _cast_copies(k, step, src_refs, dst_refs, fbuf_refs, bbuf_refs, sem_in, sem_out):
    per = _CAST_SHAPES[k][2]
    _, start = _cast_chunk(k, step)
    rows = pl.ds(start, per)
    return (pltpu.make_async_copy(src_refs[k].at[rows, :], fbuf_refs[k], sem_in.at[k]),
            pltpu.make_async_copy(bbuf_refs[k], dst_refs[k].at[rows, :], sem_out.at[k]))


def _mix_kernel(x_ref, gain_ref, w_ref, lbl_ref, og_ref, qg_ref, kg_ref, sink_ref,
                *refs):
    cast_src = refs[:N_CAST]
    rec_ref, att_ref = refs[N_CAST:N_CAST + 2]
    cast_dst = refs[N_CAST + 2:2 * N_CAST + 2]
    (u_ref, pa_ref, pb_ref, st_ref, cv_ref, kvp_ref, kpad_ref, vpad_ref, bias_ref,
     lev_ref) = refs[2 * N_CAST + 2:2 * N_CAST + 12]
    cast_f = refs[2 * N_CAST + 12:3 * N_CAST + 12]
    cast_b = refs[3 * N_CAST + 12:4 * N_CAST + 12]
    sem_in, sem_out = refs[4 * N_CAST + 12:]
    g = pl.program_id(0)
    blk = ATTN_BLOCK

    def cast_copies(k, step):
        return _cast_copies(k, step, cast_src, cast_dst, cast_f, cast_b, sem_in, sem_out)

    cast_groups = {}
    for k in range(N_CAST):
        cast_groups.setdefault(_cast_chunk(k, g)[0], []).append(k)

    for n_chunks, members in cast_groups.items():
        @pl.when(g < n_chunks)
        def _(members=members):
            for k in members:
                cast_copies(k, g)[0].start()

    lo128 = lax.broadcasted_iota(jnp.int32, (1, LANES), 1) < ATTN_HD

    @pl.when(g == 0)
    def _():
        kvp_ref[...] = jnp.zeros_like(kvp_ref)
        lev_ref[...] = _level_mask()
        qi = lax.broadcasted_iota(jnp.int32, (blk, 2 * blk), 0)
        kj = lax.broadcasted_iota(jnp.int32, (blk, 2 * blk), 1)
        rel = qi + blk - kj
        in_window = (rel >= 0) & (rel < WINDOW)
        b_any = jnp.where(in_window, 0.0, NEG_INF)
        b_first = jnp.where(in_window & (kj >= blk), 0.0, NEG_INF)
        bias_ref[0] = jnp.concatenate([b_any, b_any], axis=1)
        bias_ref[1] = jnp.concatenate([b_first, b_first], axis=1)

    first_blk = lax.rem(g + SEQ_BLOCKS - 1, SEQ_BLOCKS) == 0

    @pl.when(first_blk)
    def _():
        st_ref[...] = jnp.zeros_like(st_ref)

    hr = lax.broadcasted_iota(jnp.int32, (LANES, LANES), 0) < ATTN_HD
    hc = lax.broadcasted_iota(jnp.int32, (LANES, LANES), 1) < ATTN_HD
    ones_bd = jnp.where(hr == hc, 1.0, 0.0).astype(BF16)
    pair = lambda r: jnp.concatenate([r, r], axis=1)
    qgain = pair(qg_ref[...]) * (ATTN_HD ** -0.5 * LOG2E)
    kgain = pair(kg_ref[...])
    ones_lo = jnp.broadcast_to(jnp.where(lo128, 1.0, 0.0), (2 * blk, LANES))
    ones_pad = jnp.concatenate([ones_lo, 1.0 - ones_lo], axis=0).astype(BF16)

    l0 = lbl_ref[0:1, :]
    l1 = lbl_ref[1:2, :]
    lmax = jnp.maximum(l0, l1)
    e0 = jnp.exp(l0 - lmax)
    e1 = jnp.exp(l1 - lmax)
    lb = e0 / (e0 + e1)
    c0 = 0.5 * (1.0 + lb)
    c1 = 0.5 * (1.0 - lb)
    og = og_ref[...]
    first_idx = first_blk.astype(jnp.int32)

    def step(pc_ref, pp_ref):
        _mix_step(pc_ref, pp_ref, x_ref, gain_ref, u_ref, w_ref, rec_ref, att_ref,
                  st_ref, cv_ref, kvp_ref, kpad_ref, vpad_ref, bias_ref, lev_ref, sink_ref,
                  (lo128, ones_bd, qgain, kgain, ones_pad, c0, c1, og, first_idx))

    @pl.when(g == 0)
    def _():
        _project_only(pa_ref, x_ref, gain_ref, u_ref, w_ref)

    @pl.when((lax.rem(g, 2) == 0) & (g > 0))
    def _():
        step(pa_ref, pb_ref)

    @pl.when(lax.rem(g, 2) == 1)
    def _():
        step(pb_ref, pa_ref)

    for n_chunks, members in cast_groups.items():
        @pl.when((g >= 1) & (g <= n_chunks))
        def _(members=members):
            for k in members:
                cast_copies(k, g - 1)[1].wait()

        @pl.when(g < n_chunks)
        def _(members=members):
            for k in members:
                copy_in, copy_out = cast_copies(k, g)
                copy_in.wait()
                cast_b[k][...] = cast_f[k][...].astype(BF16)
                copy_out.start()


def _normalize(x_ref, gain_ref, u_ref):
    x = x_ref[...]
    ms = jnp.mean(x * x, axis=-1, keepdims=True)
    u_ref[...] = (x * lax.rsqrt(ms + EPS) * gain_ref[...]).astype(BF16)


def _project_slabs(pc_ref, u_ref, w_ref, first, count):
    res = jnp.dot(u_ref[...], w_ref[:, first * LANES:(first + count) * LANES],
                  preferred_element_type=F32)
    for t in range(count):
        pc_ref[first + t] = res[:, t * LANES:(t + 1) * LANES]


def _project_tail(pc_ref, u_ref, w_ref):
    _project_slabs(pc_ref, u_ref, w_ref, MIX_ITERS * LOOP_SLABS, TAIL_SLABS)


def _project_main(pc_ref, u_ref, w_ref, i):
    _project_slabs(pc_ref, u_ref, w_ref, i * LOOP_SLABS, LOOP_SLABS)


def _project_only(pc_ref, x_ref, gain_ref, u_ref, w_ref):
    _normalize(x_ref, gain_ref, u_ref)
    for i in range(MIX_ITERS):
        _project_main(pc_ref, u_ref, w_ref, i)
    _project_tail(pc_ref, u_ref, w_ref)


def _mix_step(pc_ref, pp_ref, x_ref, gain_ref, u_ref, w_ref, rec_ref, att_ref, st_ref,
              cv_ref, kvp_ref, kpad_ref, vpad_ref, bias_ref, lev_ref, sink_ref, consts):
    (lo128, ones_bd, qgain, kgain, ones_pad, c0, c1, og, first_idx) = consts
    blk = ATTN_BLOCK
    _normalize(x_ref, gain_ref, u_ref)

    k_all = jnp.concatenate([kvp_ref[0], pp_ref[K_SLAB]], axis=0)
    v_all = jnp.concatenate([kvp_ref[1], pp_ref[V_SLAB]], axis=0)
    kvp_ref[0] = pp_ref[K_SLAB, MIX_ROWS - blk:, :]
    kvp_ref[1] = pp_ref[V_SLAB, MIX_ROWS - blk:, :]
    kn = _pair_rms(k_all, ones_bd) * kgain
    kn_sw = pltpu.roll(kn, ATTN_HD, 1)
    v_sw = pltpu.roll(v_all, ATTN_HD, 1)
    zero = jnp.zeros_like(kn)
    kpad_ref[0, 0] = jnp.where(lo128, kn, zero).astype(BF16)
    kpad_ref[0, 1] = jnp.where(lo128, zero, kn_sw).astype(BF16)
    kpad_ref[1, 0] = jnp.where(lo128, kn_sw, zero).astype(BF16)
    kpad_ref[1, 1] = jnp.where(lo128, zero, kn).astype(BF16)
    vpad_ref[0, 0] = jnp.where(lo128, v_all, zero).astype(BF16)
    vpad_ref[0, 1] = jnp.where(lo128, zero, v_sw).astype(BF16)
    vpad_ref[1, 0] = jnp.where(lo128, v_sw, zero).astype(BF16)
    vpad_ref[1, 1] = jnp.where(lo128, zero, v_all).astype(BF16)

    def load(part, h, rows):
        return pp_ref[part * HGRN_HEADS + h, rows, :]

    def store(h, rows, tile):
        rec_ref[0, rows, h * HGRN_DV:(h + 1) * HGRN_DV] = tile.astype(rec_ref.dtype)

    def hgrn(i):
        r0s = [(i * HGRN_UNROLL + cc) * HGRN_CHUNK for cc in range(HGRN_UNROLL)]
        _hgrn_chunks(load, store, r0s, c0, c1, og, lev_ref[...], st_ref, cv_ref)

    def slab_stages(c):
        kvh = c // (ATTN_GROUP // 2)

        def store_att(n, tile):
            att_ref[0, c, n * blk:(n + 1) * blk, :] = tile.astype(att_ref.dtype)

        return _swa_slab(lambda: pp_ref[Q_SLAB + c],
                         lambda: kpad_ref[kvh, 0], lambda: kpad_ref[kvh, 1],
                         lambda: vpad_ref[kvh, 0], lambda: vpad_ref[kvh, 1], ones_pad,
                         lambda: bias_ref[first_idx], lambda: bias_ref[0],
                         sink_ref[2 * c] * LOG2E, sink_ref[2 * c + 1] * LOG2E,
                         qgain, ones_bd, lo128, store_att)

    def swa(i):
        slabs = [slab_stages(i * SLABS_PER_ITER + s) for s in range(SLABS_PER_ITER)]
        for stages in zip(*slabs):
            for stage in stages:
                stage()

    for i in range(MIX_ITERS):
        _project_main(pc_ref, u_ref, w_ref, i)
        hgrn(i)
        swa(i)
    _project_tail(pc_ref, u_ref, w_ref)


def _mix(x2, seq_len, gain, w_bf16, lb_logits, out_gain, q_gain, k_gain, sinks,
         ffn_weights):
    t = x2.shape[0]
    nblk = t // MIX_ROWS
    assert seq_len // MIX_ROWS == SEQ_BLOCKS
    batch = t // seq_len
    assert tuple(w.shape for w in ffn_weights) == tuple(s[:2] for s in _CAST_SHAPES)
    assert all(-(-rows // per) <= nblk for rows, _, per in _CAST_SHAPES)
    hbm = pl.BlockSpec(memory_space=pl.ANY)

    def in_rows(gi):
        return (jnp.minimum(gi, nblk - 1), 0)

    def out_block(gi):
        blk_i = jnp.maximum(gi - 1, 0)
        return blk_i // SEQ_BLOCKS, lax.rem(blk_i, SEQ_BLOCKS)

    full = lambda shape: pl.BlockSpec(shape, lambda gi: (0,) * len(shape))
    return pl.pallas_call(
        _mix_kernel,
        grid=(nblk + 1,),
        in_specs=[
            pl.BlockSpec((MIX_ROWS, D_MODEL), in_rows),
            full((1, D_MODEL)),
            full((D_MODEL, PROJ_WIDTH)),
            full((2, HGRN_WIDTH)),
            full((1, HGRN_DV)),
            full((1, ATTN_HD)),
            full((1, ATTN_HD)),
            pl.BlockSpec(memory_space=pltpu.SMEM),
        ] + [hbm] * N_CAST,
        out_specs=[
            pl.BlockSpec((1, MIX_ROWS, HGRN_WIDTH),
                         lambda gi: (*out_block(gi), 0)),
            pl.BlockSpec((1, ATT_SLABS, MIX_ROWS, LANES),
                         lambda gi: (out_block(gi)[0], 0, out_block(gi)[1], 0)),
        ] + [hbm] * N_CAST,
        out_shape=[
            jax.ShapeDtypeStruct((batch, seq_len, HGRN_WIDTH), BF16),
            jax.ShapeDtypeStruct((batch, ATT_SLABS, seq_len, LANES), BF16),
        ] + [jax.ShapeDtypeStruct(w.shape, BF16) for w in ffn_weights],
        scratch_shapes=[
            pltpu.VMEM((MIX_ROWS, D_MODEL), BF16),
            pltpu.VMEM((N_SLABS, MIX_ROWS, LANES), F32),
            pltpu.VMEM((N_SLABS, MIX_ROWS, LANES), F32),
            pltpu.VMEM((HGRN_HEADS, HGRN_DV, HGRN_DK), F32),
            pltpu.VMEM((HGRN_HEADS * HGRN_UNROLL, 3, HGRN_CHUNK, HGRN_DK), F32),
            pltpu.VMEM((2, ATTN_BLOCK, LANES), F32),
            pltpu.VMEM((ATTN_KV_HEADS, 2, ATTN_BLOCK + MIX_ROWS, LANES), BF16),
            pltpu.VMEM((ATTN_KV_HEADS, 2, ATTN_BLOCK + MIX_ROWS, LANES), BF16),
            pltpu.VMEM((2, ATTN_BLOCK, 4 * ATTN_BLOCK), F32),
            pltpu.VMEM((LEVEL_ROWS, LEVEL_ROWS), jnp.int32),
        ] + [pltpu.VMEM((per, cols), F32) for _, cols, per in _CAST_SHAPES]
        + [pltpu.VMEM((per, cols), BF16) for _, cols, per in _CAST_SHAPES]
        + [pltpu.SemaphoreType.DMA((N_CAST,)), pltpu.SemaphoreType.DMA((N_CAST,))],
        compiler_params=pltpu.CompilerParams(
            dimension_semantics=("arbitrary",), vmem_limit_bytes=VMEM_LIMIT_BYTES),
        name="mixer",
    )(x2, gain, w_bf16, lb_logits, out_gain, q_gain, k_gain, sinks, *ffn_weights)


def _ffn_kernel(x_ref, rec_ref, att_ref, wo_ref, gain_ref, wg_ref, wu_ref, wd_ref, o_ref):
    mixed = jnp.concatenate([rec_ref[0]] + [att_ref[0, c] for c in range(ATT_SLABS)],
                            axis=1)
    h = x_ref[...] + jnp.dot(mixed, wo_ref[...], preferred_element_type=F32)
    ms = jnp.mean(h * h, axis=-1, keepdims=True)
    u = (h * lax.rsqrt(ms + EPS) * gain_ref[...]).astype(BF16)
    gate = jnp.dot(u, wg_ref[...], preferred_element_type=F32)
    up = jnp.dot(u, wu_ref[...], preferred_element_type=F32)
    gs = 0.5 * gate
    act = ((gs * jnp.tanh(gs) + gs) * up).astype(BF16)
    o_ref[...] = h + jnp.dot(act, wd_ref[...], preferred_element_type=F32)


def _ffn(x2, rec, att, wo, gain, wg, wu, wd):
    t = x2.shape[0]
    seq_blocks = rec.shape[1] // FFN_ROWS

    def resident(shape):
        return pl.BlockSpec(shape, lambda i: (0, 0), pipeline_mode=pl.Buffered(1))

    return pl.pallas_call(
        _ffn_kernel,
        grid=(t // FFN_ROWS,),
        in_specs=[
            pl.BlockSpec((FFN_ROWS, D_MODEL), lambda i: (i, 0)),
            pl.BlockSpec((1, FFN_ROWS, HGRN_WIDTH),
                         lambda i: (i // seq_blocks, lax.rem(i, seq_blocks), 0)),
            pl.BlockSpec((1, ATT_SLABS, FFN_ROWS, LANES),
                         lambda i: (i // seq_blocks, 0, lax.rem(i, seq_blocks), 0)),
            resident((D_MODEL, D_MODEL)),
            resident((1, D_MODEL)),
            resident((D_MODEL, D_FF)),
            resident((D_MODEL, D_FF)),
            resident((D_FF, D_MODEL)),
        ],
        out_specs=pl.BlockSpec((FFN_ROWS, D_MODEL), lambda i: (i, 0)),
        out_shape=jax.ShapeDtypeStruct((t, D_MODEL), F32),
        compiler_params=pltpu.CompilerParams(
            dimension_semantics=("arbitrary",), vmem_limit_bytes=VMEM_LIMIT_BYTES),
        name="outproj_ffn",
    )(x2, rec, att, wo, gain, wg, wu, wd)


def kernel(x, norm1_gain, w_in, hgrn_lb_logits, hgrn_out_gain, q_norm_gain, k_norm_gain,
           attn_sinks, w_out, norm2_gain, w_ffn_gate, w_ffn_up, w_ffn_down):
    b, s, d = x.shape
    assert (d, w_in.shape[0]) == (D_MODEL, 1), "single-layer kernel"
    assert s == SEQ_BLOCKS * MIX_ROWS and MIX_ROWS == FFN_ROWS
    t = b * s
    x2 = x.reshape(t, d)

    rec, att, wo, wg, wu, wd = _mix(
        x2, s, norm1_gain[0][None, :], w_in[0].astype(BF16), hgrn_lb_logits,
        hgrn_out_gain[0][None, :], q_norm_gain[0][None, :], k_norm_gain[0][None, :],
        attn_sinks[0], (w_out[0], w_ffn_gate[0], w_ffn_up[0], w_ffn_down[0]))
    out = _ffn(x2, rec, att, wo, norm2_gain[0][None, :], wg, wu, wd)
    return out.reshape(b, s, d)
```

```python
import jax
import jax.numpy as jnp
from jax import lax
from jax.experimental import pallas as pl
from jax.experimental.pallas import tpu as pltpu

D_MODEL = 1024
HGRN_HEADS = 4
HGRN_DK = 128
HGRN_DV = 128
HGRN_WIDTH = HGRN_HEADS * HGRN_DK
ATTN_HEADS = 8
ATTN_KV_HEADS = 2
ATTN_GROUP = ATTN_HEADS // ATTN_KV_HEADS
ATTN_HD = 64
ATTN_WIDTH = ATTN_HEADS * ATTN_HD
KV_WIDTH = ATTN_KV_HEADS * ATTN_HD
WINDOW = 128
PROJ_WIDTH = 4 * HGRN_WIDTH + ATTN_WIDTH + 2 * KV_WIDTH
D_FF = 2816
EPS = 1e-6
NEG_INF = -1e30
LOG2E = 1.4426950408889634

F32 = jnp.float32
BF16 = jnp.bfloat16

LANES = 128
VMEM_LIMIT_BYTES = 56 * 1024 * 1024

HGRN_CHUNK = 64
HGRN_LEVELS = 6
HGRN_UNROLL = 2
ATTN_BLOCK = 128
MIX_ROWS = 512
MIX_ITERS = 4
SEQ_BLOCKS = 4
FFN_ROWS = 512

N_SLABS = PROJ_WIDTH // LANES
ATT_SLABS = ATTN_WIDTH // LANES
SLABS_PER_ITER = ATT_SLABS // MIX_ITERS
Q_SLAB = 4 * HGRN_HEADS
K_SLAB = Q_SLAB + ATT_SLABS
V_SLAB = K_SLAB + 1
LOOP_SLABS = Q_SLAB // MIX_ITERS
TAIL_SLABS = N_SLABS - MIX_ITERS * LOOP_SLABS
assert MIX_ITERS * LOOP_SLABS == Q_SLAB and V_SLAB == N_SLABS - 1 and LOOP_SLABS % 2 == 0
W_PIECE_COLS = 2 * LANES
assert PROJ_WIDTH % W_PIECE_COLS == 0 and (LOOP_SLABS * LANES) % W_PIECE_COLS == 0
assert MIX_ROWS == MIX_ITERS * HGRN_UNROLL * HGRN_CHUNK
assert ATT_SLABS == MIX_ITERS * SLABS_PER_ITER

_NT = (((1,), (1,)), ((), ()))
_TN = (((0,), (0,)), ((), ()))


_LEVEL_Q = ((1, 3, 5, 7), (2, 3, 6, 7), (4, 5, 6, 7))
_LEVEL_K = ((0, 2, 4, 6), (0, 1, 4, 5), (0, 1, 2, 3))
LEVEL_ROWS = 8 * sum(len(q) for q in _LEVEL_Q)


def _level_mask():
    r = lax.broadcasted_iota(jnp.int32, (LEVEL_ROWS, LEVEL_ROWS), 0)
    c = lax.broadcasted_iota(jnp.int32, (LEVEL_ROWS, LEVEL_ROWS), 1)
    lev_r, lev_c = r >> 5, c >> 5
    j_r, j_c = (r >> 3) & 3, (c >> 3) & 3
    same_block = (((lev_r == 0) & (j_r == j_c))
                  | ((lev_r == 1) & ((j_r >> 1) == (j_c >> 1)))
                  | (lev_r == 2))
    return ((lev_r == lev_c) & same_block).astype(jnp.int32)


def _mul1(a, b):
    if a is None:
        return b
    if b is None:
        return a
    return a * b


def _pair_rms(x, ones_bd):
    ss = jnp.dot((x * x).astype(BF16), ones_bd, preferred_element_type=F32)
    return x * lax.rsqrt(ss * (1.0 / ATTN_HD) + EPS)


def _hgrn_chunks(load, store, r0s, c0, c1, og, mask_nat, st_ref, cv_ref):
    nh = HGRN_HEADS
    c_rows = HGRN_CHUNK
    nv = c_rows // 8
    assert nv == 8 and HGRN_LEVELS == 6

    def bc(t, r):
        return jnp.broadcast_to(t[r:r + 1, :], (8, HGRN_DK))

    def cat16(pieces):
        return jnp.concatenate(pieces, axis=0).astype(BF16)

    q_order = _LEVEL_Q[0] + _LEVEL_Q[1] + _LEVEL_Q[2]
    k_order = _LEVEL_K[0] + _LEVEL_K[1] + _LEVEL_K[2]

    def stacked_attend(q_pieces, k_pieces, v_pieces, mask):
        s = [lax.dot_general(cat16(q_pieces[u]), cat16(k_pieces[u]), _NT,
                             preferred_element_type=F32) for u in units]
        s = [jnp.where(mask, s[u], 0.0).astype(BF16) for u in units]
        o = [jnp.dot(s[u], cat16([v_pieces[u][j] for j in k_order]),
                     preferred_element_type=F32) for u in units]
        return [[o[u][8 * n:8 * n + 8, :] for n in range(len(q_order))] for u in units]

    def scatter_add(acc, pieces):
        for n, j in enumerate(q_order):
            acc[j] = pieces[n] if acc[j] is None else acc[j] + pieces[n]
        return acc

    m_nat = mask_nat != 0

    units = range(nh * len(r0s))
    hd = [u % nh for u in units]
    r0 = [r0s[u // nh] for u in units]
    rows = [pl.ds(r0[u], c_rows) for u in units]
    lanes = [slice(hd[u] * HGRN_DK, (hd[u] + 1) * HGRN_DK) for u in units]
    vrange = range(nv)

    fp, kp, qp, vtr = [], [], [], []
    for u in units:
        c0h, c1h = c0[:, lanes[u]], c1[:, lanes[u]]
        fh, kh, qh, vh = [], [], [], []
        for b in vrange:
            srows = pl.ds(r0[u] + b, 8, stride=8)
            ct = c1h * jnp.tanh(0.5 * load(1, hd[u], srows))
            fh.append(c0h + ct)
            kh.append(c1h - ct)
            xq = load(0, hd[u], srows)
            xs = xq * (0.5 * HGRN_DK ** -0.5)
            qh.append(xs * jnp.tanh(0.5 * xq) + xs)
            vh.append(load(2, hd[u], srows))
        fp.append(fh)
        kp.append(kh)
        qp.append(qh)
        vtr.append(vh)

    o_tr = [[jnp.sum(qp[u][b] * kp[u][b], axis=-1, keepdims=True) * vtr[u][b] for b in vrange]
            for u in units]

    p2 = [[fp[u][b] * fp[u][b - 1] if b & 1 else fp[u][b] for b in vrange] for u in units]
    x2 = [[None if b & 1 else fp[u][b + 1] for b in vrange] for u in units]
    p4 = [[p2[u][b] * p2[u][(b & ~3) + 1] if b & 2 else p2[u][b] for b in vrange] for u in units]
    x4 = [[x2[u][b] if b & 2 else _mul1(x2[u][b], p2[u][(b & ~3) + 3]) for b in vrange]
          for u in units]
    p8 = [[p4[u][b] * p4[u][3] if b & 4 else p4[u][b] for b in vrange] for u in units]
    x8 = [[x4[u][b] if b & 4 else _mul1(x4[u][b], p4[u][7]) for b in vrange] for u in units]
    tot8 = [p8[u][7] for u in units]

    for u in units:
        for b in vrange:
            srows = pl.ds(b, 8, stride=8)
            cv_ref[u, 0, srows, :] = qp[u][b] * p8[u][b]
            cv_ref[u, 1, srows, :] = _mul1(kp[u][b], x8[u][b])

    q_pieces = [([qp[u][b] * fp[u][b] for b in _LEVEL_Q[0]]
                 + [qp[u][b] * p2[u][b] for b in _LEVEL_Q[1]]
                 + [qp[u][b] * p4[u][b] for b in _LEVEL_Q[2]]) for u in units]
    k_pieces = [([kp[u][b] for b in _LEVEL_K[0]]
                 + [_mul1(kp[u][b], x2[u][b]) for b in _LEVEL_K[1]]
                 + [_mul1(kp[u][b], x4[u][b]) for b in _LEVEL_K[2]]) for u in units]
    for lvl in range(3):
        group = 1 << lvl
        for jq, b_q in enumerate(_LEVEL_Q[lvl]):
            for jk, b_k in enumerate(_LEVEL_K[lvl]):
                if jq // group != jk // group:
                    continue
                nq, nk = 4 * lvl + jq, 4 * lvl + jk
                for u in units:
                    d = jnp.sum(q_pieces[u][nq] * k_pieces[u][nk], axis=-1, keepdims=True)
                    o_tr[u][b_q] = o_tr[u][b_q] + d * vtr[u][b_k]

    an = [[cv_ref[u, 0, 8 * a:8 * a + 8, :] for a in vrange] for u in units]
    cn = [[cv_ref[u, 1, 8 * a:8 * a + 8, :] for a in vrange] for u in units]
    vnat = [[load(2, hd[u], pl.ds(r0[u] + 8 * a, 8)) for a in vrange] for u in units]

    q4 = [[an[u][a] for a in _LEVEL_Q[0]] for u in units]
    k4 = [[cn[u][a] for a in _LEVEL_K[0]] for u in units]
    an = [[an[u][a] * bc(tot8[u], a - 1) if a & 1 else an[u][a] for a in vrange] for u in units]
    cn = [[cn[u][a] if a & 1 else cn[u][a] * bc(tot8[u], a + 1) for a in vrange] for u in units]
    t16 = [tot8[u] * pltpu.roll(tot8[u], 1, 0) for u in units]
    q5 = [[an[u][a] for a in _LEVEL_Q[1]] for u in units]
    k5 = [[cn[u][a] for a in _LEVEL_K[1]] for u in units]
    an = [[an[u][a] * bc(t16[u], (a & ~3) + 1) if a & 2 else an[u][a] for a in vrange]
          for u in units]
    cn = [[cn[u][a] if a & 2 else cn[u][a] * bc(t16[u], (a & ~3) + 3) for a in vrange]
          for u in units]
    t32 = [t16[u] * pltpu.roll(t16[u], 2, 0) for u in units]
    q6 = [[an[u][a] for a in _LEVEL_Q[2]] for u in units]
    k6 = [[cn[u][a] for a in _LEVEL_K[2]] for u in units]
    an = [[an[u][a] * bc(t32[u], 3) if a & 4 else an[u][a] for a in vrange] for u in units]
    cn = [[cn[u][a] if a & 4 else cn[u][a] * bc(t32[u], 7) for a in vrange] for u in units]
    att_nat = stacked_attend([q4[u] + q5[u] + q6[u] for u in units],
                             [k4[u] + k5[u] + k6[u] for u in units], vnat, m_nat)
    o_nat = []
    for u in units:
        pieces = scatter_add([None] * nv, att_nat[u])
        pieces[0] = jnp.zeros((8, HGRN_DV), F32)
        o_nat.append(jnp.concatenate(pieces, axis=0))

    for u in units:
        for b in vrange:
            cv_ref[u, 2, pl.ds(b, 8, stride=8), :] = o_tr[u][b]

    upd = [lax.dot_general(cat16(vnat[u]), cat16(cn[u]),
                           _TN, preferred_element_type=F32) for u in units]
    for h in range(nh):
        st = st_ref[h]
        for u in range(h, len(units), nh):
            qs = cat16(an[u])
            o_nat[u] = o_nat[u] + lax.dot_general(qs, st.astype(BF16), _NT,
                                                  preferred_element_type=F32)
            dec = t32[u][3:4, :] * t32[u][7:8, :]
            st = st * dec + upd[u]
        st_ref[h] = st

    for u in units:
        o_h = o_nat[u] + cv_ref[u, 2]
        ms = jnp.mean(o_h * o_h, axis=-1, keepdims=True)
        xgs = 0.5 * load(3, hd[u], rows[u])
        gate = xgs * jnp.tanh(xgs) + xgs
        store(hd[u], rows[u], o_h * lax.rsqrt(ms + EPS) * og * gate)


def _swa_slab(q_slab, k_lo, k_hi, v_lo, v_hi, ones_pad, bias_first, bias_any,
              sink_a, sink_b, qgain, ones_bd, lo128, store):
    blk = ATTN_BLOCK
    nblk = MIX_ROWS // blk
    state = {}

    def prep():
        state["qs"] = (_pair_rms(q_slab(), ones_bd) * qgain).astype(BF16)
        state["sink_slab"] = jnp.where(lo128, sink_a, sink_b)

    def scores(n):
        keys = slice(n * blk, (n + 2) * blk)
        k_cat = jnp.concatenate([k_lo()[keys], k_hi()[keys]], axis=0)
        s = lax.dot_general(state["qs"][n * blk:(n + 1) * blk], k_cat, _NT,
                            preferred_element_type=F32)
        s = s + (bias_first() if n == 0 else bias_any())
        m_a = jnp.max(s[:, :2 * blk], axis=-1, keepdims=True)
        m_b = jnp.max(s[:, 2 * blk:], axis=-1, keepdims=True)
        m_a = jnp.maximum(jnp.broadcast_to(m_a, (blk, LANES)), sink_a)
        m_b = jnp.maximum(jnp.broadcast_to(m_b, (blk, LANES)), sink_b)
        cols = [s[:, c * LANES:(c + 1) * LANES] - (m_a if c < 2 * blk // LANES else m_b)
                for c in range(4 * blk // LANES)]
        state["p", n] = jnp.exp2(jnp.concatenate(cols, axis=1)).astype(BF16)
        state["m", n] = jnp.where(lo128, m_a, m_b)

    def attend(n):
        keys = slice(n * blk, (n + 2) * blk)
        v_cat = jnp.concatenate([v_lo()[keys], v_hi()[keys]], axis=0)
        rhs = jnp.concatenate([v_cat, ones_pad], axis=1)
        nd = jnp.dot(state.pop(("p", n)), rhs, preferred_element_type=F32)
        den = nd[:, LANES:] + jnp.exp2(state["sink_slab"] - state.pop(("m", n)))
        store(n, nd[:, :LANES] / den)

    stages = [prep]
    for n in range(nblk):
        stages += [lambda n=n: scores(n), lambda n=n: attend(n)]
    return stages


_CAST_SHAPES = ((D_MODEL, D_MODEL, 32), (D_MODEL, D_FF, 32), (D_MODEL, D_FF, 32), (D_FF, D_MODEL, 96))
N_CAST = len(_CAST_SHAPES)


def _cast_chunk(k, step):
    rows, _, per = _CAST_SHAPES[k]
    n = -(-rows // per)
    start = jnp.minimum(step * per, rows - per)
    return n, pl.multiple_of(start, 16)


def _cast_copies(k, step, src_refs, dst_refs, fbuf_refs, bbuf_refs, sem_in, sem_out):
    per = _CAST_SHAPES[k][2]
    _, start = _cast_chunk(k, step)
    rows = pl.ds(start, per)
    return (pltpu.make_async_copy(src_refs[k].at[rows, :], fbuf_refs[k], sem_in.at[k]),
            pltpu.make_async_copy(bbuf_refs[k], dst_refs[k].at[rows, :], sem_out.at[k]))


def _mix_kernel(x_ref, gain_ref, w_hbm, lbl_ref, og_ref, qg_ref, kg_ref, sink_ref,
                *refs):
    cast_src = refs[:N_CAST]
    rec_ref, att_ref = refs[N_CAST:N_CAST + 2]
    cast_dst = refs[N_CAST + 2:2 * N_CAST + 2]
    (u_ref, pa_ref, pb_ref, st_ref, cv_ref, kvp_ref, kpad_ref, vpad_ref, bias_ref,
     lev_ref) = refs[2 * N_CAST + 2:2 * N_CAST + 12]
    cast_f = refs[2 * N_CAST + 12:3 * N_CAST + 12]
    cast_b = refs[3 * N_CAST + 12:4 * N_CAST + 12]
    sem_in, sem_out, w_ref, w_stage, sem_w = refs[4 * N_CAST + 12:]
    g = pl.program_id(0)
    blk = ATTN_BLOCK

    def cast_copies(k, step):
        return _cast_copies(k, step, cast_src, cast_dst, cast_f, cast_b, sem_in, sem_out)

    cast_groups = {}
    for k in range(N_CAST):
        cast_groups.setdefault(_cast_chunk(k, g)[0], []).append(k)

    for n_chunks, members in cast_groups.items():
        @pl.when(g < n_chunks)
        def _(members=members):
            for k in members:
                cast_copies(k, g)[0].start()

    lo128 = lax.broadcasted_iota(jnp.int32, (1, LANES), 1) < ATTN_HD

    @pl.when(g == 0)
    def _():
        kvp_ref[...] = jnp.zeros_like(kvp_ref)
        lev_ref[...] = _level_mask()
        qi = lax.broadcasted_iota(jnp.int32, (blk, 2 * blk), 0)
        kj = lax.broadcasted_iota(jnp.int32, (blk, 2 * blk), 1)
        rel = qi + blk - kj
        in_window = (rel >= 0) & (rel < WINDOW)
        b_any = jnp.where(in_window, 0.0, NEG_INF)
        b_first = jnp.where(in_window & (kj >= blk), 0.0, NEG_INF)
        bias_ref[0] = jnp.concatenate([b_any, b_any], axis=1)
        bias_ref[1] = jnp.concatenate([b_first, b_first], axis=1)

    first_blk = lax.rem(g + SEQ_BLOCKS - 1, SEQ_BLOCKS) == 0

    @pl.when(first_blk)
    def _():
        st_ref[...] = jnp.zeros_like(st_ref)

    hr = lax.broadcasted_iota(jnp.int32, (LANES, LANES), 0) < ATTN_HD
    hc = lax.broadcasted_iota(jnp.int32, (LANES, LANES), 1) < ATTN_HD
    ones_bd = jnp.where(hr == hc, 1.0, 0.0).astype(BF16)
    pair = lambda r: jnp.concatenate([r, r], axis=1)
    qgain = pair(qg_ref[...]) * (ATTN_HD ** -0.5 * LOG2E)
    kgain = pair(kg_ref[...])
    ones_lo = jnp.broadcast_to(jnp.where(lo128, 1.0, 0.0), (2 * blk, LANES))
    ones_pad = jnp.concatenate([ones_lo, 1.0 - ones_lo], axis=0).astype(BF16)

    l0 = lbl_ref[0:1, :]
    l1 = lbl_ref[1:2, :]
    lmax = jnp.maximum(l0, l1)
    e0 = jnp.exp(l0 - lmax)
    e1 = jnp.exp(l1 - lmax)
    lb = e0 / (e0 + e1)
    c0 = 0.5 * (1.0 + lb)
    c1 = 0.5 * (1.0 - lb)
    og = og_ref[...]
    first_idx = first_blk.astype(jnp.int32)

    def step(pc_ref, pp_ref):
        _mix_step(pc_ref, pp_ref, x_ref, gain_ref, u_ref, w_ref, rec_ref, att_ref,
                  st_ref, cv_ref, kvp_ref, kpad_ref, vpad_ref, bias_ref, lev_ref, sink_ref,
                  (lo128, ones_bd, qgain, kgain, ones_pad, c0, c1, og, first_idx))

    @pl.when(g == 0)
    def _():
        _load_weights_and_project(pa_ref, x_ref, gain_ref, u_ref, w_hbm, w_ref, w_stage, sem_w)

    @pl.when((lax.rem(g, 2) == 0) & (g > 0))
    def _():
        step(pa_ref, pb_ref)

    @pl.when(lax.rem(g, 2) == 1)
    def _():
        step(pb_ref, pa_ref)

    for n_chunks, members in cast_groups.items():
        @pl.when((g >= 1) & (g <= n_chunks))
        def _(members=members):
            for k in members:
                cast_copies(k, g - 1)[1].wait()

        @pl.when(g < n_chunks)
        def _(members=members):
            for k in members:
                copy_in, copy_out = cast_copies(k, g)
                copy_in.wait()
                cast_b[k][...] = cast_f[k][...].astype(BF16)
                copy_out.start()


def _normalize(x_ref, gain_ref, u_ref):
    x = x_ref[...]
    ms = jnp.mean(x * x, axis=-1, keepdims=True)
    u_ref[...] = (x * lax.rsqrt(ms + EPS) * gain_ref[...]).astype(BF16)


def _project_slabs(pc_ref, u_ref, w_ref, first, count):
    res = jnp.dot(u_ref[...], w_ref[:, first * LANES:(first + count) * LANES],
                  preferred_element_type=F32)
    for t in range(count):
        pc_ref[first + t] = res[:, t * LANES:(t + 1) * LANES]


def _project_tail(pc_ref, u_ref, w_ref):
    _project_slabs(pc_ref, u_ref, w_ref, MIX_ITERS * LOOP_SLABS, TAIL_SLABS)


def _project_main(pc_ref, u_ref, w_ref, i):
    _project_slabs(pc_ref, u_ref, w_ref, i * LOOP_SLABS, LOOP_SLABS)


def _load_weights_and_project(pc_ref, x_ref, gain_ref, u_ref, w_hbm, w_ref, stage_ref, sem):
    n = PROJ_WIDTH // W_PIECE_COLS

    def copy(k):
        cols = pl.ds(k * W_PIECE_COLS, W_PIECE_COLS)
        return pltpu.make_async_copy(w_hbm.at[0, :, cols], stage_ref.at[k % 2], sem.at[k % 2])

    copy(0).start()
    _normalize(x_ref, gain_ref, u_ref)
    per_main = LOOP_SLABS * LANES // W_PIECE_COLS
    for k in range(n):
        if k + 1 < n:
            copy(k + 1).start()
        copy(k).wait()
        w_ref[:, k * W_PIECE_COLS:(k + 1) * W_PIECE_COLS] = stage_ref[k % 2].astype(BF16)
        if (k + 1) % per_main == 0 and (k + 1) // per_main <= MIX_ITERS:
            _project_main(pc_ref, u_ref, w_ref, (k + 1) // per_main - 1)
    _project_tail(pc_ref, u_ref, w_ref)


def _mix_step(pc_ref, pp_ref, x_ref, gain_ref, u_ref, w_ref, rec_ref, att_ref, st_ref,
              cv_ref, kvp_ref, kpad_ref, vpad_ref, bias_ref, lev_ref, sink_ref, consts):
    (lo128, ones_bd, qgain, kgain, ones_pad, c0, c1, og, first_idx) = consts
    blk = ATTN_BLOCK
    _normalize(x_ref, gain_ref, u_ref)

    k_all = jnp.concatenate([kvp_ref[0], pp_ref[K_SLAB]], axis=0)
    v_all = jnp.concatenate([kvp_ref[1], pp_ref[V_SLAB]], axis=0)
    kvp_ref[0] = pp_ref[K_SLAB, MIX_ROWS - blk:, :]
    kvp_ref[1] = pp_ref[V_SLAB, MIX_ROWS - blk:, :]
    kn = _pair_rms(k_all, ones_bd) * kgain
    kn_sw = pltpu.roll(kn, ATTN_HD, 1)
    v_sw = pltpu.roll(v_all, ATTN_HD, 1)
    zero = jnp.zeros_like(kn)
    kpad_ref[0, 0] = jnp.where(lo128, kn, zero).astype(BF16)
    kpad_ref[0, 1] = jnp.where(lo128, zero, kn_sw).astype(BF16)
    kpad_ref[1, 0] = jnp.where(lo128, kn_sw, zero).astype(BF16)
    kpad_ref[1, 1] = jnp.where(lo128, zero, kn).astype(BF16)
    vpad_ref[0, 0] = jnp.where(lo128, v_all, zero).astype(BF16)
    vpad_ref[0, 1] = jnp.where(lo128, zero, v_sw).astype(BF16)
    vpad_ref[1, 0] = jnp.where(lo128, v_sw, zero).astype(BF16)
    vpad_ref[1, 1] = jnp.where(lo128, zero, v_all).astype(BF16)

    def load(part, h, rows):
        return pp_ref[part * HGRN_HEADS + h, rows, :]

    def store(h, rows, tile):
        rec_ref[0, rows, h * HGRN_DV:(h + 1) * HGRN_DV] = tile.astype(rec_ref.dtype)

    def hgrn(i):
        r0s = [(i * HGRN_UNROLL + cc) * HGRN_CHUNK for cc in range(HGRN_UNROLL)]
        _hgrn_chunks(load, store, r0s, c0, c1, og, lev_ref[...], st_ref, cv_ref)

    def slab_stages(c):
        kvh = c // (ATTN_GROUP // 2)

        def store_att(n, tile):
            att_ref[0, c, n * blk:(n + 1) * blk, :] = tile.astype(att_ref.dtype)

        return _swa_slab(lambda: pp_ref[Q_SLAB + c],
                         lambda: kpad_ref[kvh, 0], lambda: kpad_ref[kvh, 1],
                         lambda: vpad_ref[kvh, 0], lambda: vpad_ref[kvh, 1], ones_pad,
                         lambda: bias_ref[first_idx], lambda: bias_ref[0],
                         sink_ref[2 * c] * LOG2E, sink_ref[2 * c + 1] * LOG2E,
                         qgain, ones_bd, lo128, store_att)

    def swa(i):
        slabs = [slab_stages(i * SLABS_PER_ITER + s) for s in range(SLABS_PER_ITER)]
        for stages in zip(*slabs):
            for stage in stages:
                stage()

    for i in range(MIX_ITERS):
        _project_main(pc_ref, u_ref, w_ref, i)
        hgrn(i)
        swa(i)
    _project_tail(pc_ref, u_ref, w_ref)


def _mix(x2, seq_len, gain, w_in, lb_logits, out_gain, q_gain, k_gain, sinks,
         ffn_weights):
    assert w_in.shape == (1, D_MODEL, PROJ_WIDTH) and w_in.dtype == F32
    t = x2.shape[0]
    nblk = t // MIX_ROWS
    assert seq_len // MIX_ROWS == SEQ_BLOCKS
    batch = t // seq_len
    assert tuple(w.shape for w in ffn_weights) == tuple(s[:2] for s in _CAST_SHAPES)
    assert all(-(-rows // per) <= nblk for rows, _, per in _CAST_SHAPES)
    hbm = pl.BlockSpec(memory_space=pl.ANY)

    def in_rows(gi):
        return (jnp.minimum(gi, nblk - 1), 0)

    def out_block(gi):
        blk_i = jnp.maximum(gi - 1, 0)
        return blk_i // SEQ_BLOCKS, lax.rem(blk_i, SEQ_BLOCKS)

    full = lambda shape: pl.BlockSpec(shape, lambda gi: (0,) * len(shape))
    return pl.pallas_call(
        _mix_kernel,
        grid=(nblk + 1,),
        in_specs=[
            pl.BlockSpec((MIX_ROWS, D_MODEL), in_rows),
            full((1, D_MODEL)),
            hbm,
            full((2, HGRN_WIDTH)),
            full((1, HGRN_DV)),
            full((1, ATTN_HD)),
            full((1, ATTN_HD)),
            pl.BlockSpec(memory_space=pltpu.SMEM),
        ] + [hbm] * N_CAST,
        out_specs=[
            pl.BlockSpec((1, MIX_ROWS, HGRN_WIDTH),
                         lambda gi: (*out_block(gi), 0)),
            pl.BlockSpec((1, ATT_SLABS, MIX_ROWS, LANES),
                         lambda gi: (out_block(gi)[0], 0, out_block(gi)[1], 0)),
        ] + [hbm] * N_CAST,
        out_shape=[
            jax.ShapeDtypeStruct((batch, seq_len, HGRN_WIDTH), BF16),
            jax.ShapeDtypeStruct((batch, ATT_SLABS, seq_len, LANES), BF16),
        ] + [jax.ShapeDtypeStruct(w.shape, BF16) for w in ffn_weights],
        scratch_shapes=[
            pltpu.VMEM((MIX_ROWS, D_MODEL), BF16),
            pltpu.VMEM((N_SLABS, MIX_ROWS, LANES), F32),
            pltpu.VMEM((N_SLABS, MIX_ROWS, LANES), F32),
            pltpu.VMEM((HGRN_HEADS, HGRN_DV, HGRN_DK), F32),
            pltpu.VMEM((HGRN_HEADS * HGRN_UNROLL, 3, HGRN_CHUNK, HGRN_DK), F32),
            pltpu.VMEM((2, ATTN_BLOCK, LANES), F32),
            pltpu.VMEM((ATTN_KV_HEADS, 2, ATTN_BLOCK + MIX_ROWS, LANES), BF16),
            pltpu.VMEM((ATTN_KV_HEADS, 2, ATTN_BLOCK + MIX_ROWS, LANES), BF16),
            pltpu.VMEM((2, ATTN_BLOCK, 4 * ATTN_BLOCK), F32),
            pltpu.VMEM((LEVEL_ROWS, LEVEL_ROWS), jnp.int32),
        ] + [pltpu.VMEM((per, cols), F32) for _, cols, per in _CAST_SHAPES]
        + [pltpu.VMEM((per, cols), BF16) for _, cols, per in _CAST_SHAPES]
        + [pltpu.SemaphoreType.DMA((N_CAST,)), pltpu.SemaphoreType.DMA((N_CAST,)),
           pltpu.VMEM((D_MODEL, PROJ_WIDTH), BF16),
           pltpu.VMEM((2, D_MODEL, W_PIECE_COLS), F32),
           pltpu.SemaphoreType.DMA((2,))],
        compiler_params=pltpu.CompilerParams(
            dimension_semantics=("arbitrary",), vmem_limit_bytes=VMEM_LIMIT_BYTES),
        name="mixer",
    )(x2, gain, w_in, lb_logits, out_gain, q_gain, k_gain, sinks, *ffn_weights)


def _ffn_kernel(x_ref, rec_ref, att_ref, wo_ref, gain_ref, wg_ref, wu_ref, wd_ref, o_ref):
    mixed = jnp.concatenate([rec_ref[0]] + [att_ref[0, c] for c in range(ATT_SLABS)],
                            axis=1)
    h = x_ref[...] + jnp.dot(mixed, wo_ref[...], preferred_element_type=F32)
    ms = jnp.mean(h * h, axis=-1, keepdims=True)
    u = (h * lax.rsqrt(ms + EPS) * gain_ref[...]).astype(BF16)
    gate = jnp.dot(u, wg_ref[...], preferred_element_type=F32)
    up = jnp.dot(u, wu_ref[...], preferred_element_type=F32)
    gs = 0.5 * gate
    act = ((gs * jnp.tanh(gs) + gs) * up).astype(BF16)
    o_ref[...] = h + jnp.dot(act, wd_ref[...], preferred_element_type=F32)


def _ffn(x2, rec, att, wo, gain, wg, wu, wd):
    t = x2.shape[0]
    seq_blocks = rec.shape[1] // FFN_ROWS

    def resident(shape):
        return pl.BlockSpec(shape, lambda i: (0, 0), pipeline_mode=pl.Buffered(1))

    return pl.pallas_call(
        _ffn_kernel,
        grid=(t // FFN_ROWS,),
        in_specs=[
            pl.BlockSpec((FFN_ROWS, D_MODEL), lambda i: (i, 0)),
            pl.BlockSpec((1, FFN_ROWS, HGRN_WIDTH),
                         lambda i: (i // seq_blocks, lax.rem(i, seq_blocks), 0)),
            pl.BlockSpec((1, ATT_SLABS, FFN_ROWS, LANES),
                         lambda i: (i // seq_blocks, 0, lax.rem(i, seq_blocks), 0)),
            resident((D_MODEL, D_MODEL)),
            resident((1, D_MODEL)),
            resident((D_MODEL, D_FF)),
            resident((D_MODEL, D_FF)),
            resident((D_FF, D_MODEL)),
        ],
        out_specs=pl.BlockSpec((FFN_ROWS, D_MODEL), lambda i: (i, 0)),
        out_shape=jax.ShapeDtypeStruct((t, D_MODEL), F32),
        compiler_params=pltpu.CompilerParams(
            dimension_semantics=("arbitrary",), vmem_limit_bytes=VMEM_LIMIT_BYTES),
        name="outproj_ffn",
    )(x2, rec, att, wo, gain, wg, wu, wd)


def kernel(x, norm1_gain, w_in, hgrn_lb_logits, hgrn_out_gain, q_norm_gain, k_norm_gain,
           attn_sinks, w_out, norm2_gain, w_ffn_gate, w_ffn_up, w_ffn_down):
    b, s, d = x.shape
    assert (d, w_in.shape[0]) == (D_MODEL, 1), "single-layer kernel"
    assert s == SEQ_BLOCKS * MIX_ROWS and MIX_ROWS == FFN_ROWS
    t = b * s
    x2 = x.reshape(t, d)

    rec, att, wo, wg, wu, wd = _mix(
        x2, s, norm1_gain[0][None, :], w_in, hgrn_lb_logits,
        hgrn_out_gain[0][None, :], q_norm_gain[0][None, :], k_norm_gain[0][None, :],
        attn_sinks[0], (w_out[0], w_ffn_gate[0], w_ffn_up[0], w_ffn_down[0]))
    out = _ffn(x2, rec, att, wo, norm2_gain[0][None, :], wg, wu, wd)
    return out.reshape(b, s, d)
```

```python
import jax
import jax.numpy as jnp
from jax import lax
from jax.experimental import pallas as pl
from jax.experimental.pallas import tpu as pltpu

D_MODEL = 1024
HGRN_HEADS = 4
HGRN_DK = 128
HGRN_DV = 128
HGRN_WIDTH = HGRN_HEADS * HGRN_DK
ATTN_HEADS = 8
ATTN_KV_HEADS = 2
ATTN_GROUP = ATTN_HEADS // ATTN_KV_HEADS
ATTN_HD = 64
ATTN_WIDTH = ATTN_HEADS * ATTN_HD
KV_WIDTH = ATTN_KV_HEADS * ATTN_HD
WINDOW = 128
PROJ_WIDTH = 4 * HGRN_WIDTH + ATTN_WIDTH + 2 * KV_WIDTH
D_FF = 2816
EPS = 1e-6
NEG_INF = -1e30
LOG2E = 1.4426950408889634

F32 = jnp.float32
BF16 = jnp.bfloat16

LANES = 128
VMEM_LIMIT_BYTES = 56 * 1024 * 1024

HGRN_CHUNK = 64
HGRN_LEVELS = 6
HGRN_UNROLL = 2
ATTN_BLOCK = 128
MIX_ROWS = 512
MIX_ITERS = 4
SEQ_BLOCKS = 4
FFN_ROWS = 512

N_SLABS = PROJ_WIDTH // LANES
ATT_SLABS = ATTN_WIDTH // LANES
SLABS_PER_ITER = ATT_SLABS // MIX_ITERS
Q_SLAB = 4 * HGRN_HEADS
K_SLAB = Q_SLAB + ATT_SLABS
V_SLAB = K_SLAB + 1
LOOP_SLABS = Q_SLAB // MIX_ITERS
TAIL_SLABS = N_SLABS - MIX_ITERS * LOOP_SLABS
assert MIX_ITERS * LOOP_SLABS == Q_SLAB and V_SLAB == N_SLABS - 1 and LOOP_SLABS % 2 == 0
W_PIECE_COLS = 2 * LANES
W_STAGE_SLOTS = 4
assert PROJ_WIDTH % W_PIECE_COLS == 0 and (LOOP_SLABS * LANES) % W_PIECE_COLS == 0
assert MIX_ROWS == MIX_ITERS * HGRN_UNROLL * HGRN_CHUNK
assert ATT_SLABS == MIX_ITERS * SLABS_PER_ITER

_NT = (((1,), (1,)), ((), ()))
_TN = (((0,), (0,)), ((), ()))


_LEVEL_Q = ((1, 3, 5, 7), (2, 3, 6, 7), (4, 5, 6, 7))
_LEVEL_K = ((0, 2, 4, 6), (0, 1, 4, 5), (0, 1, 2, 3))
LEVEL_ROWS = 8 * sum(len(q) for q in _LEVEL_Q)


def _level_mask():
    r = lax.broadcasted_iota(jnp.int32, (LEVEL_ROWS, LEVEL_ROWS), 0)
    c = lax.broadcasted_iota(jnp.int32, (LEVEL_ROWS, LEVEL_ROWS), 1)
    lev_r, lev_c = r >> 5, c >> 5
    j_r, j_c = (r >> 3) & 3, (c >> 3) & 3
    same_block = (((lev_r == 0) & (j_r == j_c))
                  | ((lev_r == 1) & ((j_r >> 1) == (j_c >> 1)))
                  | (lev_r == 2))
    return ((lev_r == lev_c) & same_block).astype(jnp.int32)


def _mul1(a, b):
    if a is None:
        return b
    if b is None:
        return a
    return a * b


def _pair_rms(x, ones_bd):
    ss = jnp.dot((x * x).astype(BF16), ones_bd, preferred_element_type=F32)
    return x * lax.rsqrt(ss * (1.0 / ATTN_HD) + EPS)


def _hgrn_chunks(load, store, r0s, c0, c1, og, mask_nat, st_ref, cv_ref):
    nh = HGRN_HEADS
    c_rows = HGRN_CHUNK
    nv = c_rows // 8
    assert nv == 8 and HGRN_LEVELS == 6

    def bc(t, r):
        return jnp.broadcast_to(t[r:r + 1, :], (8, HGRN_DK))

    def cat16(pieces):
        return jnp.concatenate(pieces, axis=0).astype(BF16)

    q_order = _LEVEL_Q[0] + _LEVEL_Q[1] + _LEVEL_Q[2]
    k_order = _LEVEL_K[0] + _LEVEL_K[1] + _LEVEL_K[2]

    def stacked_attend(q_pieces, k_pieces, v_pieces, mask):
        s = [lax.dot_general(cat16(q_pieces[u]), cat16(k_pieces[u]), _NT,
                             preferred_element_type=F32) for u in units]
        s = [jnp.where(mask, s[u], 0.0).astype(BF16) for u in units]
        o = [jnp.dot(s[u], cat16([v_pieces[u][j] for j in k_order]),
                     preferred_element_type=F32) for u in units]
        return [[o[u][8 * n:8 * n + 8, :] for n in range(len(q_order))] for u in units]

    def scatter_add(acc, pieces):
        for n, j in enumerate(q_order):
            acc[j] = pieces[n] if acc[j] is None else acc[j] + pieces[n]
        return acc

    m_nat = mask_nat != 0

    units = range(nh * len(r0s))
    hd = [u % nh for u in units]
    r0 = [r0s[u // nh] for u in units]
    rows = [pl.ds(r0[u], c_rows) for u in units]
    lanes = [slice(hd[u] * HGRN_DK, (hd[u] + 1) * HGRN_DK) for u in units]
    vrange = range(nv)

    fp, kp, qp, vtr = [], [], [], []
    for u in units:
        c0h, c1h = c0[:, lanes[u]], c1[:, lanes[u]]
        fh, kh, qh, vh = [], [], [], []
        for b in vrange:
            srows = pl.ds(r0[u] + b, 8, stride=8)
            ct = c1h * jnp.tanh(0.5 * load(1, hd[u], srows))
            fh.append(c0h + ct)
            kh.append(c1h - ct)
            xq = load(0, hd[u], srows)
            xs = xq * (0.5 * HGRN_DK ** -0.5)
            qh.append(xs * jnp.tanh(0.5 * xq) + xs)
            vh.append(load(2, hd[u], srows))
        fp.append(fh)
        kp.append(kh)
        qp.append(qh)
        vtr.append(vh)

    o_tr = [[jnp.sum(qp[u][b] * kp[u][b], axis=-1, keepdims=True) * vtr[u][b] for b in vrange]
            for u in units]

    p2 = [[fp[u][b] * fp[u][b - 1] if b & 1 else fp[u][b] for b in vrange] for u in units]
    x2 = [[None if b & 1 else fp[u][b + 1] for b in vrange] for u in units]
    p4 = [[p2[u][b] * p2[u][(b & ~3) + 1] if b & 2 else p2[u][b] for b in vrange] for u in units]
    x4 = [[x2[u][b] if b & 2 else _mul1(x2[u][b], p2[u][(b & ~3) + 3]) for b in vrange]
          for u in units]
    p8 = [[p4[u][b] * p4[u][3] if b & 4 else p4[u][b] for b in vrange] for u in units]
    x8 = [[x4[u][b] if b & 4 else _mul1(x4[u][b], p4[u][7]) for b in vrange] for u in units]
    tot8 = [p8[u][7] for u in units]

    for u in units:
        for b in vrange:
            srows = pl.ds(b, 8, stride=8)
            cv_ref[u, 0, srows, :] = qp[u][b] * p8[u][b]
            cv_ref[u, 1, srows, :] = _mul1(kp[u][b], x8[u][b])

    q_pieces = [([qp[u][b] * fp[u][b] for b in _LEVEL_Q[0]]
                 + [qp[u][b] * p2[u][b] for b in _LEVEL_Q[1]]
                 + [qp[u][b] * p4[u][b] for b in _LEVEL_Q[2]]) for u in units]
    k_pieces = [([kp[u][b] for b in _LEVEL_K[0]]
                 + [_mul1(kp[u][b], x2[u][b]) for b in _LEVEL_K[1]]
                 + [_mul1(kp[u][b], x4[u][b]) for b in _LEVEL_K[2]]) for u in units]
    for lvl in range(3):
        group = 1 << lvl
        for jq, b_q in enumerate(_LEVEL_Q[lvl]):
            for jk, b_k in enumerate(_LEVEL_K[lvl]):
                if jq // group != jk // group:
                    continue
                nq, nk = 4 * lvl + jq, 4 * lvl + jk
                for u in units:
                    d = jnp.sum(q_pieces[u][nq] * k_pieces[u][nk], axis=-1, keepdims=True)
                    o_tr[u][b_q] = o_tr[u][b_q] + d * vtr[u][b_k]

    an = [[cv_ref[u, 0, 8 * a:8 * a + 8, :] for a in vrange] for u in units]
    cn = [[cv_ref[u, 1, 8 * a:8 * a + 8, :] for a in vrange] for u in units]
    vnat = [[load(2, hd[u], pl.ds(r0[u] + 8 * a, 8)) for a in vrange] for u in units]

    q4 = [[an[u][a] for a in _LEVEL_Q[0]] for u in units]
    k4 = [[cn[u][a] for a in _LEVEL_K[0]] for u in units]
    an = [[an[u][a] * bc(tot8[u], a - 1) if a & 1 else an[u][a] for a in vrange] for u in units]
    cn = [[cn[u][a] if a & 1 else cn[u][a] * bc(tot8[u], a + 1) for a in vrange] for u in units]
    t16 = [tot8[u] * pltpu.roll(tot8[u], 1, 0) for u in units]
    q5 = [[an[u][a] for a in _LEVEL_Q[1]] for u in units]
    k5 = [[cn[u][a] for a in _LEVEL_K[1]] for u in units]
    an = [[an[u][a] * bc(t16[u], (a & ~3) + 1) if a & 2 else an[u][a] for a in vrange]
          for u in units]
    cn = [[cn[u][a] if a & 2 else cn[u][a] * bc(t16[u], (a & ~3) + 3) for a in vrange]
          for u in units]
    t32 = [t16[u] * pltpu.roll(t16[u], 2, 0) for u in units]
    q6 = [[an[u][a] for a in _LEVEL_Q[2]] for u in units]
    k6 = [[cn[u][a] for a in _LEVEL_K[2]] for u in units]
    an = [[an[u][a] * bc(t32[u], 3) if a & 4 else an[u][a] for a in vrange] for u in units]
    cn = [[cn[u][a] if a & 4 else cn[u][a] * bc(t32[u], 7) for a in vrange] for u in units]
    att_nat = stacked_attend([q4[u] + q5[u] + q6[u] for u in units],
                             [k4[u] + k5[u] + k6[u] for u in units], vnat, m_nat)
    o_nat = []
    for u in units:
        pieces = scatter_add([None] * nv, att_nat[u])
        pieces[0] = jnp.zeros((8, HGRN_DV), F32)
        o_nat.append(jnp.concatenate(pieces, axis=0))

    for u in units:
        for b in vrange:
            cv_ref[u, 2, pl.ds(b, 8, stride=8), :] = o_tr[u][b]

    upd = [lax.dot_general(cat16(vnat[u]), cat16(cn[u]),
                           _TN, preferred_element_type=F32) for u in units]
    for h in range(nh):
        st = st_ref[h]
        for u in range(h, len(units), nh):
            qs = cat16(an[u])
            o_nat[u] = o_nat[u] + lax.dot_general(qs, st.astype(BF16), _NT,
                                                  preferred_element_type=F32)
            dec = t32[u][3:4, :] * t32[u][7:8, :]
            st = st * dec + upd[u]
        st_ref[h] = st

    for u in units:
        o_h = o_nat[u] + cv_ref[u, 2]
        ms = jnp.mean(o_h * o_h, axis=-1, keepdims=True)
        xgs = 0.5 * load(3, hd[u], rows[u])
        gate = xgs * jnp.tanh(xgs) + xgs
        store(hd[u], rows[u], o_h * lax.rsqrt(ms + EPS) * og * gate)


def _swa_slab(q_slab, k_lo, k_hi, v_lo, v_hi, ones_pad, bias_first, bias_any,
              sink_a, sink_b, qgain, ones_bd, lo128, store):
    blk = ATTN_BLOCK
    nblk = MIX_ROWS // blk
    state = {}

    def prep():
        state["qs"] = (_pair_rms(q_slab(), ones_bd) * qgain).astype(BF16)
        state["sink_slab"] = jnp.where(lo128, sink_a, sink_b)

    def scores(n):
        keys = slice(n * blk, (n + 2) * blk)
        k_cat = jnp.concatenate([k_lo()[keys], k_hi()[keys]], axis=0)
        s = lax.dot_general(state["qs"][n * blk:(n + 1) * blk], k_cat, _NT,
                            preferred_element_type=F32)
        s = s + (bias_first() if n == 0 else bias_any())
        m_a = jnp.max(s[:, :2 * blk], axis=-1, keepdims=True)
        m_b = jnp.max(s[:, 2 * blk:], axis=-1, keepdims=True)
        m_a = jnp.maximum(jnp.broadcast_to(m_a, (blk, LANES)), sink_a)
        m_b = jnp.maximum(jnp.broadcast_to(m_b, (blk, LANES)), sink_b)
        cols = [s[:, c * LANES:(c + 1) * LANES] - (m_a if c < 2 * blk // LANES else m_b)
                for c in range(4 * blk // LANES)]
        state["p", n] = jnp.exp2(jnp.concatenate(cols, axis=1)).astype(BF16)
        state["m", n] = jnp.where(lo128, m_a, m_b)

    def attend(n):
        keys = slice(n * blk, (n + 2) * blk)
        v_cat = jnp.concatenate([v_lo()[keys], v_hi()[keys]], axis=0)
        rhs = jnp.concatenate([v_cat, ones_pad], axis=1)
        nd = jnp.dot(state.pop(("p", n)), rhs, preferred_element_type=F32)
        den = nd[:, LANES:] + jnp.exp2(state["sink_slab"] - state.pop(("m", n)))
        store(n, nd[:, :LANES] / den)

    stages = [prep]
    for n in range(nblk):
        stages += [lambda n=n: scores(n), lambda n=n: attend(n)]
    return stages


_CAST_SHAPES = ((D_MODEL, D_MODEL, 32), (D_MODEL, D_FF, 32), (D_MODEL, D_FF, 32), (D_FF, D_MODEL, 96))
N_CAST = len(_CAST_SHAPES)


def _cast_chunk(k, step):
    rows, _, per = _CAST_SHAPES[k]
    n = -(-rows // per)
    start = jnp.minimum(step * per, rows - per)
    return n, pl.multiple_of(start, 16)


def _cast_copies(k, step, src_refs, dst_refs, fbuf_refs, bbuf_refs, sem_in, sem_out):
    per = _CAST_SHAPES[k][2]
    _, start = _cast_chunk(k, step)
    rows = pl.ds(start, per)
    return (pltpu.make_async_copy(src_refs[k].at[rows, :], fbuf_refs[k], sem_in.at[k]),
            pltpu.make_async_copy(bbuf_refs[k], dst_refs[k].at[rows, :], sem_out.at[k]))


def _mix_kernel(x_ref, gain_ref, w_hbm, lbl_ref, og_ref, qg_ref, kg_ref, sink_ref,
                *refs):
    cast_src = refs[:N_CAST]
    rec_ref, att_ref = refs[N_CAST:N_CAST + 2]
    cast_dst = refs[N_CAST + 2:2 * N_CAST + 2]
    (u_ref, pa_ref, pb_ref, st_ref, cv_ref, kvp_ref, kpad_ref, vpad_ref, bias_ref,
     lev_ref) = refs[2 * N_CAST + 2:2 * N_CAST + 12]
    cast_f = refs[2 * N_CAST + 12:3 * N_CAST + 12]
    cast_b = refs[3 * N_CAST + 12:4 * N_CAST + 12]
    sem_in, sem_out, w_ref, w_stage, sem_w = refs[4 * N_CAST + 12:]
    g = pl.program_id(0)
    blk = ATTN_BLOCK

    def cast_copies(k, step):
        return _cast_copies(k, step, cast_src, cast_dst, cast_f, cast_b, sem_in, sem_out)

    cast_groups = {}
    for k in range(N_CAST):
        cast_groups.setdefault(_cast_chunk(k, g)[0], []).append(k)

    for n_chunks, members in cast_groups.items():
        @pl.when(g < n_chunks)
        def _(members=members):
            for k in members:
                cast_copies(k, g)[0].start()

    lo128 = lax.broadcasted_iota(jnp.int32, (1, LANES), 1) < ATTN_HD

    @pl.when(g == 0)
    def _():
        kvp_ref[...] = jnp.zeros_like(kvp_ref)
        lev_ref[...] = _level_mask()
        qi = lax.broadcasted_iota(jnp.int32, (blk, 2 * blk), 0)
        kj = lax.broadcasted_iota(jnp.int32, (blk, 2 * blk), 1)
        rel = qi + blk - kj
        in_window = (rel >= 0) & (rel < WINDOW)
        b_any = jnp.where(in_window, 0.0, NEG_INF)
        b_first = jnp.where(in_window & (kj >= blk), 0.0, NEG_INF)
        bias_ref[0] = jnp.concatenate([b_any, b_any], axis=1)
        bias_ref[1] = jnp.concatenate([b_first, b_first], axis=1)

    first_blk = lax.rem(g + SEQ_BLOCKS - 1, SEQ_BLOCKS) == 0

    @pl.when(first_blk)
    def _():
        st_ref[...] = jnp.zeros_like(st_ref)

    hr = lax.broadcasted_iota(jnp.int32, (LANES, LANES), 0) < ATTN_HD
    hc = lax.broadcasted_iota(jnp.int32, (LANES, LANES), 1) < ATTN_HD
    ones_bd = jnp.where(hr == hc, 1.0, 0.0).astype(BF16)
    pair = lambda r: jnp.concatenate([r, r], axis=1)
    qgain = pair(qg_ref[...]) * (ATTN_HD ** -0.5 * LOG2E)
    kgain = pair(kg_ref[...])
    ones_lo = jnp.broadcast_to(jnp.where(lo128, 1.0, 0.0), (2 * blk, LANES))
    ones_pad = jnp.concatenate([ones_lo, 1.0 - ones_lo], axis=0).astype(BF16)

    l0 = lbl_ref[0:1, :]
    l1 = lbl_ref[1:2, :]
    lmax = jnp.maximum(l0, l1)
    e0 = jnp.exp(l0 - lmax)
    e1 = jnp.exp(l1 - lmax)
    lb = e0 / (e0 + e1)
    c0 = 0.5 * (1.0 + lb)
    c1 = 0.5 * (1.0 - lb)
    og = og_ref[...]
    first_idx = first_blk.astype(jnp.int32)

    def step(pc_ref, pp_ref):
        _mix_step(pc_ref, pp_ref, x_ref, gain_ref, u_ref, w_ref, rec_ref, att_ref,
                  st_ref, cv_ref, kvp_ref, kpad_ref, vpad_ref, bias_ref, lev_ref, sink_ref,
                  (lo128, ones_bd, qgain, kgain, ones_pad, c0, c1, og, first_idx))

    @pl.when(g == 0)
    def _():
        _load_weights_and_project(pa_ref, x_ref, gain_ref, u_ref, w_hbm, w_ref, w_stage, sem_w)

    @pl.when((lax.rem(g, 2) == 0) & (g > 0))
    def _():
        step(pa_ref, pb_ref)

    @pl.when(lax.rem(g, 2) == 1)
    def _():
        step(pb_ref, pa_ref)

    for n_chunks, members in cast_groups.items():
        @pl.when((g >= 1) & (g <= n_chunks))
        def _(members=members):
            for k in members:
                cast_copies(k, g - 1)[1].wait()

        @pl.when(g < n_chunks)
        def _(members=members):
            for k in members:
                copy_in, copy_out = cast_copies(k, g)
                copy_in.wait()
                cast_b[k][...] = cast_f[k][...].astype(BF16)
                copy_out.start()


def _normalize(x_ref, gain_ref, u_ref):
    x = x_ref[...]
    ms = jnp.mean(x * x, axis=-1, keepdims=True)
    u_ref[...] = (x * lax.rsqrt(ms + EPS) * gain_ref[...]).astype(BF16)


def _project_slabs(pc_ref, u_ref, w_ref, first, count):
    res = jnp.dot(u_ref[...], w_ref[:, first * LANES:(first + count) * LANES],
                  preferred_element_type=F32)
    for t in range(count):
        pc_ref[first + t] = res[:, t * LANES:(t + 1) * LANES]


def _project_tail(pc_ref, u_ref, w_ref):
    _project_slabs(pc_ref, u_ref, w_ref, MIX_ITERS * LOOP_SLABS, TAIL_SLABS)


def _project_main(pc_ref, u_ref, w_ref, i):
    _project_slabs(pc_ref, u_ref, w_ref, i * LOOP_SLABS, LOOP_SLABS)


def _load_weights_and_project(pc_ref, x_ref, gain_ref, u_ref, w_hbm, w_ref, stage_ref, sem):
    n = PROJ_WIDTH // W_PIECE_COLS

    slots = stage_ref.shape[0]

    def copy(k):
        cols = pl.ds(k * W_PIECE_COLS, W_PIECE_COLS)
        return pltpu.make_async_copy(w_hbm.at[0, :, cols], stage_ref.at[k % slots],
                                     sem.at[k % slots])

    for k in range(min(slots - 1, n)):
        copy(k).start()
    _normalize(x_ref, gain_ref, u_ref)
    per_main = LOOP_SLABS * LANES // W_PIECE_COLS
    for k in range(n):
        if k + slots - 1 < n:
            copy(k + slots - 1).start()
        copy(k).wait()
        w_ref[:, k * W_PIECE_COLS:(k + 1) * W_PIECE_COLS] = stage_ref[k % slots].astype(BF16)
        if (k + 1) % per_main == 0 and (k + 1) // per_main <= MIX_ITERS:
            _project_main(pc_ref, u_ref, w_ref, (k + 1) // per_main - 1)
    _project_tail(pc_ref, u_ref, w_ref)


def _mix_step(pc_ref, pp_ref, x_ref, gain_ref, u_ref, w_ref, rec_ref, att_ref, st_ref,
              cv_ref, kvp_ref, kpad_ref, vpad_ref, bias_ref, lev_ref, sink_ref, consts):
    (lo128, ones_bd, qgain, kgain, ones_pad, c0, c1, og, first_idx) = consts
    blk = ATTN_BLOCK
    _normalize(x_ref, gain_ref, u_ref)

    k_all = jnp.concatenate([kvp_ref[0], pp_ref[K_SLAB]], axis=0)
    v_all = jnp.concatenate([kvp_ref[1], pp_ref[V_SLAB]], axis=0)
    kvp_ref[0] = pp_ref[K_SLAB, MIX_ROWS - blk:, :]
    kvp_ref[1] = pp_ref[V_SLAB, MIX_ROWS - blk:, :]
    kn = _pair_rms(k_all, ones_bd) * kgain
    kn_sw = pltpu.roll(kn, ATTN_HD, 1)
    v_sw = pltpu.roll(v_all, ATTN_HD, 1)
    zero = jnp.zeros_like(kn)
    kpad_ref[0, 0] = jnp.where(lo128, kn, zero).astype(BF16)
    kpad_ref[0, 1] = jnp.where(lo128, zero, kn_sw).astype(BF16)
    kpad_ref[1, 0] = jnp.where(lo128, kn_sw, zero).astype(BF16)
    kpad_ref[1, 1] = jnp.where(lo128, zero, kn).astype(BF16)
    vpad_ref[0, 0] = jnp.where(lo128, v_all, zero).astype(BF16)
    vpad_ref[0, 1] = jnp.where(lo128, zero, v_sw).astype(BF16)
    vpad_ref[1, 0] = jnp.where(lo128, v_sw, zero).astype(BF16)
    vpad_ref[1, 1] = jnp.where(lo128, zero, v_all).astype(BF16)

    def load(part, h, rows):
        return pp_ref[part * HGRN_HEADS + h, rows, :]

    def store(h, rows, tile):
        rec_ref[0, rows, h * HGRN_DV:(h + 1) * HGRN_DV] = tile.astype(rec_ref.dtype)

    def hgrn(i):
        r0s = [(i * HGRN_UNROLL + cc) * HGRN_CHUNK for cc in range(HGRN_UNROLL)]
        _hgrn_chunks(load, store, r0s, c0, c1, og, lev_ref[...], st_ref, cv_ref)

    def slab_stages(c):
        kvh = c // (ATTN_GROUP // 2)

        def store_att(n, tile):
            att_ref[0, c, n * blk:(n + 1) * blk, :] = tile.astype(att_ref.dtype)

        return _swa_slab(lambda: pp_ref[Q_SLAB + c],
                         lambda: kpad_ref[kvh, 0], lambda: kpad_ref[kvh, 1],
                         lambda: vpad_ref[kvh, 0], lambda: vpad_ref[kvh, 1], ones_pad,
                         lambda: bias_ref[first_idx], lambda: bias_ref[0],
                         sink_ref[2 * c] * LOG2E, sink_ref[2 * c + 1] * LOG2E,
                         qgain, ones_bd, lo128, store_att)

    def swa(i):
        slabs = [slab_stages(i * SLABS_PER_ITER + s) for s in range(SLABS_PER_ITER)]
        for stages in zip(*slabs):
            for stage in stages:
                stage()

    for i in range(MIX_ITERS):
        _project_main(pc_ref, u_ref, w_ref, i)
        hgrn(i)
        swa(i)
    _project_tail(pc_ref, u_ref, w_ref)


def _mix(x2, seq_len, gain, w_in, lb_logits, out_gain, q_gain, k_gain, sinks,
         ffn_weights):
    assert w_in.shape == (1, D_MODEL, PROJ_WIDTH) and w_in.dtype == F32
    t = x2.shape[0]
    nblk = t // MIX_ROWS
    assert seq_len // MIX_ROWS == SEQ_BLOCKS
    batch = t // seq_len
    assert tuple(w.shape for w in ffn_weights) == tuple(s[:2] for s in _CAST_SHAPES)
    assert all(-(-rows // per) <= nblk for rows, _, per in _CAST_SHAPES)
    hbm = pl.BlockSpec(memory_space=pl.ANY)

    def in_rows(gi):
        return (jnp.minimum(gi, nblk - 1), 0)

    def out_block(gi):
        blk_i = jnp.maximum(gi - 1, 0)
        return blk_i // SEQ_BLOCKS, lax.rem(blk_i, SEQ_BLOCKS)

    full = lambda shape: pl.BlockSpec(shape, lambda gi: (0,) * len(shape))
    return pl.pallas_call(
        _mix_kernel,
        grid=(nblk + 1,),
        in_specs=[
            pl.BlockSpec((MIX_ROWS, D_MODEL), in_rows),
            full((1, D_MODEL)),
            hbm,
            full((2, HGRN_WIDTH)),
            full((1, HGRN_DV)),
            full((1, ATTN_HD)),
            full((1, ATTN_HD)),
            pl.BlockSpec(memory_space=pltpu.SMEM),
        ] + [hbm] * N_CAST,
        out_specs=[
            pl.BlockSpec((1, MIX_ROWS, HGRN_WIDTH),
                         lambda gi: (*out_block(gi), 0)),
            pl.BlockSpec((1, ATT_SLABS, MIX_ROWS, LANES),
                         lambda gi: (out_block(gi)[0], 0, out_block(gi)[1], 0)),
        ] + [hbm] * N_CAST,
        out_shape=[
            jax.ShapeDtypeStruct((batch, seq_len, HGRN_WIDTH), BF16),
            jax.ShapeDtypeStruct((batch, ATT_SLABS, seq_len, LANES), BF16),
        ] + [jax.ShapeDtypeStruct(w.shape, BF16) for w in ffn_weights],
        scratch_shapes=[
            pltpu.VMEM((MIX_ROWS, D_MODEL), BF16),
            pltpu.VMEM((N_SLABS, MIX_ROWS, LANES), F32),
            pltpu.VMEM((N_SLABS, MIX_ROWS, LANES), F32),
            pltpu.VMEM((HGRN_HEADS, HGRN_DV, HGRN_DK), F32),
            pltpu.VMEM((HGRN_HEADS * HGRN_UNROLL, 3, HGRN_CHUNK, HGRN_DK), F32),
            pltpu.VMEM((2, ATTN_BLOCK, LANES), F32),
            pltpu.VMEM((ATTN_KV_HEADS, 2, ATTN_BLOCK + MIX_ROWS, LANES), BF16),
            pltpu.VMEM((ATTN_KV_HEADS, 2, ATTN_BLOCK + MIX_ROWS, LANES), BF16),
            pltpu.VMEM((2, ATTN_BLOCK, 4 * ATTN_BLOCK), F32),
            pltpu.VMEM((LEVEL_ROWS, LEVEL_ROWS), jnp.int32),
        ] + [pltpu.VMEM((per, cols), F32) for _, cols, per in _CAST_SHAPES]
        + [pltpu.VMEM((per, cols), BF16) for _, cols, per in _CAST_SHAPES]
        + [pltpu.SemaphoreType.DMA((N_CAST,)), pltpu.SemaphoreType.DMA((N_CAST,)),
           pltpu.VMEM((D_MODEL, PROJ_WIDTH), BF16),
           pltpu.VMEM((W_STAGE_SLOTS, D_MODEL, W_PIECE_COLS), F32),
           pltpu.SemaphoreType.DMA((W_STAGE_SLOTS,))],
        compiler_params=pltpu.CompilerParams(
            dimension_semantics=("arbitrary",), vmem_limit_bytes=VMEM_LIMIT_BYTES),
        name="mixer",
    )(x2, gain, w_in, lb_logits, out_gain, q_gain, k_gain, sinks, *ffn_weights)


def _ffn_kernel(x_ref, rec_ref, att_ref, wo_ref, gain_ref, wg_ref, wu_ref, wd_ref, o_ref):
    mixed = jnp.concatenate([rec_ref[0]] + [att_ref[0, c] for c in range(ATT_SLABS)],
                            axis=1)
    h = x_ref[...] + jnp.dot(mixed, wo_ref[...], preferred_element_type=F32)
    ms = jnp.mean(h * h, axis=-1, keepdims=True)
    u = (h * lax.rsqrt(ms + EPS) * gain_ref[...]).astype(BF16)
    gate = jnp.dot(u, wg_ref[...], preferred_element_type=F32)
    up = jnp.dot(u, wu_ref[...], preferred_element_type=F32)
    gs = 0.5 * gate
    act = ((gs * jnp.tanh(gs) + gs) * up).astype(BF16)
    o_ref[...] = h + jnp.dot(act, wd_ref[...], preferred_element_type=F32)


def _ffn(x2, rec, att, wo, gain, wg, wu, wd):
    t = x2.shape[0]
    seq_blocks = rec.shape[1] // FFN_ROWS

    def resident(shape):
        return pl.BlockSpec(shape, lambda i: (0, 0), pipeline_mode=pl.Buffered(1))

    return pl.pallas_call(
        _ffn_kernel,
        grid=(t // FFN_ROWS,),
        in_specs=[
            pl.BlockSpec((FFN_ROWS, D_MODEL), lambda i: (i, 0)),
            pl.BlockSpec((1, FFN_ROWS, HGRN_WIDTH),
                         lambda i: (i // seq_blocks, lax.rem(i, seq_blocks), 0)),
            pl.BlockSpec((1, ATT_SLABS, FFN_ROWS, LANES),
                         lambda i: (i // seq_blocks, 0, lax.rem(i, seq_blocks), 0)),
            resident((D_MODEL, D_MODEL)),
            resident((1, D_MODEL)),
            resident((D_MODEL, D_FF)),
            resident((D_MODEL, D_FF)),
            resident((D_FF, D_MODEL)),
        ],
        out_specs=pl.BlockSpec((FFN_ROWS, D_MODEL), lambda i: (i, 0)),
        out_shape=jax.ShapeDtypeStruct((t, D_MODEL), F32),
        compiler_params=pltpu.CompilerParams(
            dimension_semantics=("arbitrary",), vmem_limit_bytes=VMEM_LIMIT_BYTES),
        name="outproj_ffn",
    )(x2, rec, att, wo, gain, wg, wu, wd)


def kernel(x, norm1_gain, w_in, hgrn_lb_logits, hgrn_out_gain, q_norm_gain, k_norm_gain,
           attn_sinks, w_out, norm2_gain, w_ffn_gate, w_ffn_up, w_ffn_down):
    b, s, d = x.shape
    assert (d, w_in.shape[0]) == (D_MODEL, 1), "single-layer kernel"
    assert s == SEQ_BLOCKS * MIX_ROWS and MIX_ROWS == FFN_ROWS
    t = b * s
    x2 = x.reshape(t, d)

    rec, att, wo, wg, wu, wd = _mix(
        x2, s, norm1_gain[0][None, :], w_in, hgrn_lb_logits,
        hgrn_out_gain[0][None, :], q_norm_gain[0][None, :], k_norm_gain[0][None, :],
        attn_sinks[0], (w_out[0], w_ffn_gate[0], w_ffn_up[0], w_ffn_down[0]))
    out = _ffn(x2, rec, att, wo, norm2_gain[0][None, :], wg, wu, wd)
    return out.reshape(b, s, d)
```

```python
import jax
import jax.numpy as jnp
from jax import lax
from jax.experimental import pallas as pl
from jax.experimental.pallas import tpu as pltpu

D_MODEL = 1024
HGRN_HEADS = 4
HGRN_DK = 128
HGRN_DV = 128
HGRN_WIDTH = HGRN_HEADS * HGRN_DK
ATTN_HEADS = 8
ATTN_KV_HEADS = 2
ATTN_GROUP = ATTN_HEADS // ATTN_KV_HEADS
ATTN_HD = 64
ATTN_WIDTH = ATTN_HEADS * ATTN_HD
KV_WIDTH = ATTN_KV_HEADS * ATTN_HD
WINDOW = 128
PROJ_WIDTH = 4 * HGRN_WIDTH + ATTN_WIDTH + 2 * KV_WIDTH
D_FF = 2816
EPS = 1e-6
NEG_INF = -1e30
LOG2E = 1.4426950408889634

F32 = jnp.float32
BF16 = jnp.bfloat16

LANES = 128
VMEM_LIMIT_BYTES = 56 * 1024 * 1024

HGRN_CHUNK = 64
HGRN_LEVELS = 6
HGRN_UNROLL = 2
ATTN_BLOCK = 128
MIX_ROWS = 512
MIX_ITERS = 4
SEQ_BLOCKS = 4
FFN_ROWS = 512

N_SLABS = PROJ_WIDTH // LANES
ATT_SLABS = ATTN_WIDTH // LANES
SLABS_PER_ITER = ATT_SLABS // MIX_ITERS
Q_SLAB = 4 * HGRN_HEADS
K_SLAB = Q_SLAB + ATT_SLABS
V_SLAB = K_SLAB + 1
LOOP_SLABS = Q_SLAB // MIX_ITERS
TAIL_SLABS = N_SLABS - MIX_ITERS * LOOP_SLABS
assert MIX_ITERS * LOOP_SLABS == Q_SLAB and V_SLAB == N_SLABS - 1 and LOOP_SLABS % 2 == 0
W_PIECE_COLS = 2 * LANES
W_STAGE_SLOTS = 4
assert PROJ_WIDTH % W_PIECE_COLS == 0 and (LOOP_SLABS * LANES) % W_PIECE_COLS == 0
assert MIX_ROWS == MIX_ITERS * HGRN_UNROLL * HGRN_CHUNK
assert ATT_SLABS == MIX_ITERS * SLABS_PER_ITER

_NT = (((1,), (1,)), ((), ()))
_TN = (((0,), (0,)), ((), ()))


_LEVEL_Q = ((1, 3, 5, 7), (2, 3, 6, 7), (4, 5, 6, 7))
_LEVEL_K = ((0, 2, 4, 6), (0, 1, 4, 5), (0, 1, 2, 3))
LEVEL_ROWS = 8 * sum(len(q) for q in _LEVEL_Q)


def _level_mask():
    r = lax.broadcasted_iota(jnp.int32, (LEVEL_ROWS, LEVEL_ROWS), 0)
    c = lax.broadcasted_iota(jnp.int32, (LEVEL_ROWS, LEVEL_ROWS), 1)
    lev_r, lev_c = r >> 5, c >> 5
    j_r, j_c = (r >> 3) & 3, (c >> 3) & 3
    same_block = (((lev_r == 0) & (j_r == j_c))
                  | ((lev_r == 1) & ((j_r >> 1) == (j_c >> 1)))
                  | (lev_r == 2))
    return ((lev_r == lev_c) & same_block).astype(jnp.int32)


def _mul1(a, b):
    if a is None:
        return b
    if b is None:
        return a
    return a * b


def _pair_rms(x, ones_bd):
    ss = jnp.dot((x * x).astype(BF16), ones_bd, preferred_element_type=F32)
    return x * lax.rsqrt(ss * (1.0 / ATTN_HD) + EPS)


def _hgrn_chunks(load, store, r0s, c0, c1, og, mask_nat, st_ref, cv_ref):
    nh = HGRN_HEADS
    c_rows = HGRN_CHUNK
    nv = c_rows // 8
    assert nv == 8 and HGRN_LEVELS == 6

    def bc(t, r):
        return jnp.broadcast_to(t[r:r + 1, :], (8, HGRN_DK))

    def cat16(pieces):
        return jnp.concatenate(pieces, axis=0).astype(BF16)

    q_order = _LEVEL_Q[0] + _LEVEL_Q[1] + _LEVEL_Q[2]
    k_order = _LEVEL_K[0] + _LEVEL_K[1] + _LEVEL_K[2]

    def stacked_attend(q_pieces, k_pieces, v_pieces, mask):
        s = [lax.dot_general(cat16(q_pieces[u]), cat16(k_pieces[u]), _NT,
                             preferred_element_type=F32) for u in units]
        s = [jnp.where(mask, s[u], 0.0).astype(BF16) for u in units]
        o = [jnp.dot(s[u], cat16([v_pieces[u][j] for j in k_order]),
                     preferred_element_type=F32) for u in units]
        return [[o[u][8 * n:8 * n + 8, :] for n in range(len(q_order))] for u in units]

    def scatter_add(acc, pieces):
        for n, j in enumerate(q_order):
            acc[j] = pieces[n] if acc[j] is None else acc[j] + pieces[n]
        return acc

    m_nat = mask_nat != 0

    units = range(nh * len(r0s))
    hd = [u % nh for u in units]
    r0 = [r0s[u // nh] for u in units]
    rows = [pl.ds(r0[u], c_rows) for u in units]
    lanes = [slice(hd[u] * HGRN_DK, (hd[u] + 1) * HGRN_DK) for u in units]
    vrange = range(nv)

    fp, kp, qp, vtr = [], [], [], []
    for u in units:
        c0h, c1h = c0[:, lanes[u]], c1[:, lanes[u]]
        fh, kh, qh, vh = [], [], [], []
        for b in vrange:
            srows = pl.ds(r0[u] + b, 8, stride=8)
            ct = c1h * jnp.tanh(load(1, hd[u], srows))
            fh.append(c0h + ct)
            kh.append(c1h - ct)
            xq = load(0, hd[u], srows)
            xs = xq * (HGRN_DK ** -0.5)
            qh.append(xs * jnp.tanh(xq) + xs)
            vh.append(load(2, hd[u], srows))
        fp.append(fh)
        kp.append(kh)
        qp.append(qh)
        vtr.append(vh)

    o_tr = [[jnp.sum(qp[u][b] * kp[u][b], axis=-1, keepdims=True) * vtr[u][b] for b in vrange]
            for u in units]

    p2 = [[fp[u][b] * fp[u][b - 1] if b & 1 else fp[u][b] for b in vrange] for u in units]
    x2 = [[None if b & 1 else fp[u][b + 1] for b in vrange] for u in units]
    p4 = [[p2[u][b] * p2[u][(b & ~3) + 1] if b & 2 else p2[u][b] for b in vrange] for u in units]
    x4 = [[x2[u][b] if b & 2 else _mul1(x2[u][b], p2[u][(b & ~3) + 3]) for b in vrange]
          for u in units]
    p8 = [[p4[u][b] * p4[u][3] if b & 4 else p4[u][b] for b in vrange] for u in units]
    x8 = [[x4[u][b] if b & 4 else _mul1(x4[u][b], p4[u][7]) for b in vrange] for u in units]
    tot8 = [p8[u][7] for u in units]

    for u in units:
        for b in vrange:
            srows = pl.ds(b, 8, stride=8)
            cv_ref[u, 0, srows, :] = qp[u][b] * p8[u][b]
            cv_ref[u, 1, srows, :] = _mul1(kp[u][b], x8[u][b])

    q_pieces = [([qp[u][b] * fp[u][b] for b in _LEVEL_Q[0]]
                 + [qp[u][b] * p2[u][b] for b in _LEVEL_Q[1]]
                 + [qp[u][b] * p4[u][b] for b in _LEVEL_Q[2]]) for u in units]
    k_pieces = [([kp[u][b] for b in _LEVEL_K[0]]
                 + [_mul1(kp[u][b], x2[u][b]) for b in _LEVEL_K[1]]
                 + [_mul1(kp[u][b], x4[u][b]) for b in _LEVEL_K[2]]) for u in units]
    for lvl in range(3):
        group = 1 << lvl
        for jq, b_q in enumerate(_LEVEL_Q[lvl]):
            for jk, b_k in enumerate(_LEVEL_K[lvl]):
                if jq // group != jk // group:
                    continue
                nq, nk = 4 * lvl + jq, 4 * lvl + jk
                for u in units:
                    d = jnp.sum(q_pieces[u][nq] * k_pieces[u][nk], axis=-1, keepdims=True)
                    o_tr[u][b_q] = o_tr[u][b_q] + d * vtr[u][b_k]

    an = [[cv_ref[u, 0, 8 * a:8 * a + 8, :] for a in vrange] for u in units]
    cn = [[cv_ref[u, 1, 8 * a:8 * a + 8, :] for a in vrange] for u in units]
    vnat = [[load(2, hd[u], pl.ds(r0[u] + 8 * a, 8)) for a in vrange] for u in units]

    q4 = [[an[u][a] for a in _LEVEL_Q[0]] for u in units]
    k4 = [[cn[u][a] for a in _LEVEL_K[0]] for u in units]
    an = [[an[u][a] * bc(tot8[u], a - 1) if a & 1 else an[u][a] for a in vrange] for u in units]
    cn = [[cn[u][a] if a & 1 else cn[u][a] * bc(tot8[u], a + 1) for a in vrange] for u in units]
    t16 = [tot8[u] * pltpu.roll(tot8[u], 1, 0) for u in units]
    q5 = [[an[u][a] for a in _LEVEL_Q[1]] for u in units]
    k5 = [[cn[u][a] for a in _LEVEL_K[1]] for u in units]
    an = [[an[u][a] * bc(t16[u], (a & ~3) + 1) if a & 2 else an[u][a] for a in vrange]
          for u in units]
    cn = [[cn[u][a] if a & 2 else cn[u][a] * bc(t16[u], (a & ~3) + 3) for a in vrange]
          for u in units]
    t32 = [t16[u] * pltpu.roll(t16[u], 2, 0) for u in units]
    q6 = [[an[u][a] for a in _LEVEL_Q[2]] for u in units]
    k6 = [[cn[u][a] for a in _LEVEL_K[2]] for u in units]
    an = [[an[u][a] * bc(t32[u], 3) if a & 4 else an[u][a] for a in vrange] for u in units]
    cn = [[cn[u][a] if a & 4 else cn[u][a] * bc(t32[u], 7) for a in vrange] for u in units]
    att_nat = stacked_attend([q4[u] + q5[u] + q6[u] for u in units],
                             [k4[u] + k5[u] + k6[u] for u in units], vnat, m_nat)
    o_nat = []
    for u in units:
        pieces = scatter_add([None] * nv, att_nat[u])
        pieces[0] = jnp.zeros((8, HGRN_DV), F32)
        o_nat.append(jnp.concatenate(pieces, axis=0))

    for u in units:
        for b in vrange:
            cv_ref[u, 2, pl.ds(b, 8, stride=8), :] = o_tr[u][b]

    upd = [lax.dot_general(cat16(vnat[u]), cat16(cn[u]),
                           _TN, preferred_element_type=F32) for u in units]
    for h in range(nh):
        st = st_ref[h]
        for u in range(h, len(units), nh):
            qs = cat16(an[u])
            o_nat[u] = o_nat[u] + lax.dot_general(qs, st.astype(BF16), _NT,
                                                  preferred_element_type=F32)
            dec = t32[u][3:4, :] * t32[u][7:8, :]
            st = st * dec + upd[u]
        st_ref[h] = st

    for u in units:
        o_h = o_nat[u] + cv_ref[u, 2]
        ms = jnp.mean(o_h * o_h, axis=-1, keepdims=True)
        xgs = load(3, hd[u], rows[u])
        gate = xgs * jnp.tanh(xgs) + xgs
        store(hd[u], rows[u], o_h * lax.rsqrt(ms + EPS) * og * gate)


def _swa_slab(q_slab, k_lo, k_hi, v_lo, v_hi, ones_pad, bias_first, bias_any,
              sink_a, sink_b, qgain, ones_bd, lo128, store):
    blk = ATTN_BLOCK
    nblk = MIX_ROWS // blk
    state = {}

    def prep():
        state["qs"] = (_pair_rms(q_slab(), ones_bd) * qgain).astype(BF16)
        state["sink_slab"] = jnp.where(lo128, sink_a, sink_b)

    def scores(n):
        keys = slice(n * blk, (n + 2) * blk)
        k_cat = jnp.concatenate([k_lo()[keys], k_hi()[keys]], axis=0)
        s = lax.dot_general(state["qs"][n * blk:(n + 1) * blk], k_cat, _NT,
                            preferred_element_type=F32)
        s = s + (bias_first() if n == 0 else bias_any())
        m_a = jnp.max(s[:, :2 * blk], axis=-1, keepdims=True)
        m_b = jnp.max(s[:, 2 * blk:], axis=-1, keepdims=True)
        m_a = jnp.maximum(jnp.broadcast_to(m_a, (blk, LANES)), sink_a)
        m_b = jnp.maximum(jnp.broadcast_to(m_b, (blk, LANES)), sink_b)
        cols = [s[:, c * LANES:(c + 1) * LANES] - (m_a if c < 2 * blk // LANES else m_b)
                for c in range(4 * blk // LANES)]
        state["p", n] = jnp.exp2(jnp.concatenate(cols, axis=1)).astype(BF16)
        state["m", n] = jnp.where(lo128, m_a, m_b)

    def attend(n):
        keys = slice(n * blk, (n + 2) * blk)
        v_cat = jnp.concatenate([v_lo()[keys], v_hi()[keys]], axis=0)
        rhs = jnp.concatenate([v_cat, ones_pad], axis=1)
        nd = jnp.dot(state.pop(("p", n)), rhs, preferred_element_type=F32)
        den = nd[:, LANES:] + jnp.exp2(state["sink_slab"] - state.pop(("m", n)))
        store(n, nd[:, :LANES] / den)

    stages = [prep]
    for n in range(nblk):
        stages += [lambda n=n: scores(n), lambda n=n: attend(n)]
    return stages


_CAST_SHAPES = ((D_MODEL, D_MODEL, 32), (D_MODEL, D_FF, 32), (D_MODEL, D_FF, 32), (D_FF, D_MODEL, 96))
N_CAST = len(_CAST_SHAPES)


def _cast_chunk(k, step):
    rows, _, per = _CAST_SHAPES[k]
    n = -(-rows // per)
    start = jnp.minimum(step * per, rows - per)
    return n, pl.multiple_of(start, 16)


def _cast_copies(k, step, src_refs, dst_refs, fbuf_refs, bbuf_refs, sem_in, sem_out):
    per = _CAST_SHAPES[k][2]
    _, start = _cast_chunk(k, step)
    rows = pl.ds(start, per)
    return (pltpu.make_async_copy(src_refs[k].at[rows, :], fbuf_refs[k], sem_in.at[k]),
            pltpu.make_async_copy(bbuf_refs[k], dst_refs[k].at[rows, :], sem_out.at[k]))


def _mix_kernel(x_ref, gain_ref, w_hbm, lbl_ref, og_ref, qg_ref, kg_ref, sink_ref,
                *refs):
    cast_src = refs[:N_CAST]
    rec_ref, att_ref = refs[N_CAST:N_CAST + 2]
    cast_dst = refs[N_CAST + 2:2 * N_CAST + 2]
    (u_ref, pa_ref, pb_ref, st_ref, cv_ref, kvp_ref, kpad_ref, vpad_ref, bias_ref,
     lev_ref) = refs[2 * N_CAST + 2:2 * N_CAST + 12]
    cast_f = refs[2 * N_CAST + 12:3 * N_CAST + 12]
    cast_b = refs[3 * N_CAST + 12:4 * N_CAST + 12]
    sem_in, sem_out, w_ref, w_stage, sem_w = refs[4 * N_CAST + 12:]
    g = pl.program_id(0)
    blk = ATTN_BLOCK

    def cast_copies(k, step):
        return _cast_copies(k, step, cast_src, cast_dst, cast_f, cast_b, sem_in, sem_out)

    cast_groups = {}
    for k in range(N_CAST):
        cast_groups.setdefault(_cast_chunk(k, g)[0], []).append(k)

    for n_chunks, members in cast_groups.items():
        @pl.when(g < n_chunks)
        def _(members=members):
            for k in members:
                cast_copies(k, g)[0].start()

    lo128 = lax.broadcasted_iota(jnp.int32, (1, LANES), 1) < ATTN_HD

    @pl.when(g == 0)
    def _():
        kvp_ref[...] = jnp.zeros_like(kvp_ref)
        lev_ref[...] = _level_mask()
        qi = lax.broadcasted_iota(jnp.int32, (blk, 2 * blk), 0)
        kj = lax.broadcasted_iota(jnp.int32, (blk, 2 * blk), 1)
        rel = qi + blk - kj
        in_window = (rel >= 0) & (rel < WINDOW)
        b_any = jnp.where(in_window, 0.0, NEG_INF)
        b_first = jnp.where(in_window & (kj >= blk), 0.0, NEG_INF)
        bias_ref[0] = jnp.concatenate([b_any, b_any], axis=1)
        bias_ref[1] = jnp.concatenate([b_first, b_first], axis=1)

    first_blk = lax.rem(g + SEQ_BLOCKS - 1, SEQ_BLOCKS) == 0

    @pl.when(first_blk)
    def _():
        st_ref[...] = jnp.zeros_like(st_ref)

    hr = lax.broadcasted_iota(jnp.int32, (LANES, LANES), 0) < ATTN_HD
    hc = lax.broadcasted_iota(jnp.int32, (LANES, LANES), 1) < ATTN_HD
    ones_bd = jnp.where(hr == hc, 1.0, 0.0).astype(BF16)
    pair = lambda r: jnp.concatenate([r, r], axis=1)
    qgain = pair(qg_ref[...]) * (ATTN_HD ** -0.5 * LOG2E)
    kgain = pair(kg_ref[...])
    ones_lo = jnp.broadcast_to(jnp.where(lo128, 1.0, 0.0), (2 * blk, LANES))
    ones_pad = jnp.concatenate([ones_lo, 1.0 - ones_lo], axis=0).astype(BF16)

    l0 = lbl_ref[0:1, :]
    l1 = lbl_ref[1:2, :]
    lmax = jnp.maximum(l0, l1)
    e0 = jnp.exp(l0 - lmax)
    e1 = jnp.exp(l1 - lmax)
    lb = e0 / (e0 + e1)
    c0 = 0.5 * (1.0 + lb)
    c1 = 0.5 * (1.0 - lb)
    og = og_ref[...]
    first_idx = first_blk.astype(jnp.int32)

    def step(pc_ref, pp_ref):
        _mix_step(pc_ref, pp_ref, x_ref, gain_ref, u_ref, w_ref, rec_ref, att_ref,
                  st_ref, cv_ref, kvp_ref, kpad_ref, vpad_ref, bias_ref, lev_ref, sink_ref,
                  (lo128, ones_bd, qgain, kgain, ones_pad, c0, c1, og, first_idx))

    @pl.when(g == 0)
    def _():
        _load_weights_and_project(pa_ref, x_ref, gain_ref, u_ref, w_hbm, w_ref, w_stage, sem_w)

    @pl.when((lax.rem(g, 2) == 0) & (g > 0))
    def _():
        step(pa_ref, pb_ref)

    @pl.when(lax.rem(g, 2) == 1)
    def _():
        step(pb_ref, pa_ref)

    for n_chunks, members in cast_groups.items():
        @pl.when((g >= 1) & (g <= n_chunks))
        def _(members=members):
            for k in members:
                cast_copies(k, g - 1)[1].wait()

        @pl.when(g < n_chunks)
        def _(members=members):
            for k in members:
                copy_in, copy_out = cast_copies(k, g)
                copy_in.wait()
                cast_b[k][...] = cast_f[k][...].astype(BF16)
                copy_out.start()


def _normalize(x_ref, gain_ref, u_ref):
    x = x_ref[...]
    ms = jnp.mean(x * x, axis=-1, keepdims=True)
    u_ref[...] = (x * lax.rsqrt(ms + EPS) * gain_ref[...]).astype(BF16)


def _project_slabs(pc_ref, u_ref, w_ref, first, count):
    res = jnp.dot(u_ref[...], w_ref[:, first * LANES:(first + count) * LANES],
                  preferred_element_type=F32)
    for t in range(count):
        pc_ref[first + t] = res[:, t * LANES:(t + 1) * LANES]


def _project_tail(pc_ref, u_ref, w_ref):
    _project_slabs(pc_ref, u_ref, w_ref, MIX_ITERS * LOOP_SLABS, TAIL_SLABS)


def _project_main(pc_ref, u_ref, w_ref, i):
    _project_slabs(pc_ref, u_ref, w_ref, i * LOOP_SLABS, LOOP_SLABS)


def _halved_part(col):
    part = col // HGRN_WIDTH
    return part in (0, 1, 3)


def _load_weights_and_project(pc_ref, x_ref, gain_ref, u_ref, w_hbm, w_ref, stage_ref, sem):
    n = PROJ_WIDTH // W_PIECE_COLS

    slots = stage_ref.shape[0]

    def copy(k):
        cols = pl.ds(k * W_PIECE_COLS, W_PIECE_COLS)
        return pltpu.make_async_copy(w_hbm.at[0, :, cols], stage_ref.at[k % slots],
                                     sem.at[k % slots])

    for k in range(min(slots - 1, n)):
        copy(k).start()
    _normalize(x_ref, gain_ref, u_ref)
    per_main = LOOP_SLABS * LANES // W_PIECE_COLS
    for k in range(n):
        if k + slots - 1 < n:
            copy(k + slots - 1).start()
        copy(k).wait()
        piece = stage_ref[k % slots]
        if _halved_part(k * W_PIECE_COLS):
            piece = piece * 0.5
        w_ref[:, k * W_PIECE_COLS:(k + 1) * W_PIECE_COLS] = piece.astype(BF16)
        if (k + 1) % per_main == 0 and (k + 1) // per_main <= MIX_ITERS:
            _project_main(pc_ref, u_ref, w_ref, (k + 1) // per_main - 1)
    _project_tail(pc_ref, u_ref, w_ref)


def _mix_step(pc_ref, pp_ref, x_ref, gain_ref, u_ref, w_ref, rec_ref, att_ref, st_ref,
              cv_ref, kvp_ref, kpad_ref, vpad_ref, bias_ref, lev_ref, sink_ref, consts):
    (lo128, ones_bd, qgain, kgain, ones_pad, c0, c1, og, first_idx) = consts
    blk = ATTN_BLOCK
    _normalize(x_ref, gain_ref, u_ref)

    k_all = jnp.concatenate([kvp_ref[0], pp_ref[K_SLAB]], axis=0)
    v_all = jnp.concatenate([kvp_ref[1], pp_ref[V_SLAB]], axis=0)
    kvp_ref[0] = pp_ref[K_SLAB, MIX_ROWS - blk:, :]
    kvp_ref[1] = pp_ref[V_SLAB, MIX_ROWS - blk:, :]
    kn = _pair_rms(k_all, ones_bd) * kgain
    kn_sw = pltpu.roll(kn, ATTN_HD, 1)
    v_sw = pltpu.roll(v_all, ATTN_HD, 1)
    zero = jnp.zeros_like(kn)
    kpad_ref[0, 0] = jnp.where(lo128, kn, zero).astype(BF16)
    kpad_ref[0, 1] = jnp.where(lo128, zero, kn_sw).astype(BF16)
    kpad_ref[1, 0] = jnp.where(lo128, kn_sw, zero).astype(BF16)
    kpad_ref[1, 1] = jnp.where(lo128, zero, kn).astype(BF16)
    vpad_ref[0, 0] = jnp.where(lo128, v_all, zero).astype(BF16)
    vpad_ref[0, 1] = jnp.where(lo128, zero, v_sw).astype(BF16)
    vpad_ref[1, 0] = jnp.where(lo128, v_sw, zero).astype(BF16)
    vpad_ref[1, 1] = jnp.where(lo128, zero, v_all).astype(BF16)

    def load(part, h, rows):
        return pp_ref[part * HGRN_HEADS + h, rows, :]

    def store(h, rows, tile):
        rec_ref[0, rows, h * HGRN_DV:(h + 1) * HGRN_DV] = tile.astype(rec_ref.dtype)

    def hgrn(i):
        r0s = [(i * HGRN_UNROLL + cc) * HGRN_CHUNK for cc in range(HGRN_UNROLL)]
        _hgrn_chunks(load, store, r0s, c0, c1, og, lev_ref[...], st_ref, cv_ref)

    def slab_stages(c):
        kvh = c // (ATTN_GROUP // 2)

        def store_att(n, tile):
            att_ref[0, c, n * blk:(n + 1) * blk, :] = tile.astype(att_ref.dtype)

        return _swa_slab(lambda: pp_ref[Q_SLAB + c],
                         lambda: kpad_ref[kvh, 0], lambda: kpad_ref[kvh, 1],
                         lambda: vpad_ref[kvh, 0], lambda: vpad_ref[kvh, 1], ones_pad,
                         lambda: bias_ref[first_idx], lambda: bias_ref[0],
                         sink_ref[2 * c] * LOG2E, sink_ref[2 * c + 1] * LOG2E,
                         qgain, ones_bd, lo128, store_att)

    def swa(i):
        slabs = [slab_stages(i * SLABS_PER_ITER + s) for s in range(SLABS_PER_ITER)]
        for stages in zip(*slabs):
            for stage in stages:
                stage()

    for i in range(MIX_ITERS):
        _project_main(pc_ref, u_ref, w_ref, i)
        hgrn(i)
        swa(i)
    _project_tail(pc_ref, u_ref, w_ref)


def _mix(x2, seq_len, gain, w_in, lb_logits, out_gain, q_gain, k_gain, sinks,
         ffn_weights):
    assert w_in.shape == (1, D_MODEL, PROJ_WIDTH) and w_in.dtype == F32
    t = x2.shape[0]
    nblk = t // MIX_ROWS
    assert seq_len // MIX_ROWS == SEQ_BLOCKS
    batch = t // seq_len
    assert tuple(w.shape for w in ffn_weights) == tuple(s[:2] for s in _CAST_SHAPES)
    assert all(-(-rows // per) <= nblk for rows, _, per in _CAST_SHAPES)
    hbm = pl.BlockSpec(memory_space=pl.ANY)

    def in_rows(gi):
        return (jnp.minimum(gi, nblk - 1), 0)

    def out_block(gi):
        blk_i = jnp.maximum(gi - 1, 0)
        return blk_i // SEQ_BLOCKS, lax.rem(blk_i, SEQ_BLOCKS)

    full = lambda shape: pl.BlockSpec(shape, lambda gi: (0,) * len(shape))
    return pl.pallas_call(
        _mix_kernel,
        grid=(nblk + 1,),
        in_specs=[
            pl.BlockSpec((MIX_ROWS, D_MODEL), in_rows),
            full((1, D_MODEL)),
            hbm,
            full((2, HGRN_WIDTH)),
            full((1, HGRN_DV)),
            full((1, ATTN_HD)),
            full((1, ATTN_HD)),
            pl.BlockSpec(memory_space=pltpu.SMEM),
        ] + [hbm] * N_CAST,
        out_specs=[
            pl.BlockSpec((1, MIX_ROWS, HGRN_WIDTH),
                         lambda gi: (*out_block(gi), 0)),
            pl.BlockSpec((1, ATT_SLABS, MIX_ROWS, LANES),
                         lambda gi: (out_block(gi)[0], 0, out_block(gi)[1], 0)),
        ] + [hbm] * N_CAST,
        out_shape=[
            jax.ShapeDtypeStruct((batch, seq_len, HGRN_WIDTH), BF16),
            jax.ShapeDtypeStruct((batch, ATT_SLABS, seq_len, LANES), BF16),
        ] + [jax.ShapeDtypeStruct(w.shape, BF16) for w in ffn_weights],
        scratch_shapes=[
            pltpu.VMEM((MIX_ROWS, D_MODEL), BF16),
            pltpu.VMEM((N_SLABS, MIX_ROWS, LANES), F32),
            pltpu.VMEM((N_SLABS, MIX_ROWS, LANES), F32),
            pltpu.VMEM((HGRN_HEADS, HGRN_DV, HGRN_DK), F32),
            pltpu.VMEM((HGRN_HEADS * HGRN_UNROLL, 3, HGRN_CHUNK, HGRN_DK), F32),
            pltpu.VMEM((2, ATTN_BLOCK, LANES), F32),
            pltpu.VMEM((ATTN_KV_HEADS, 2, ATTN_BLOCK + MIX_ROWS, LANES), BF16),
            pltpu.VMEM((ATTN_KV_HEADS, 2, ATTN_BLOCK + MIX_ROWS, LANES), BF16),
            pltpu.VMEM((2, ATTN_BLOCK, 4 * ATTN_BLOCK), F32),
            pltpu.VMEM((LEVEL_ROWS, LEVEL_ROWS), jnp.int32),
        ] + [pltpu.VMEM((per, cols), F32) for _, cols, per in _CAST_SHAPES]
        + [pltpu.VMEM((per, cols), BF16) for _, cols, per in _CAST_SHAPES]
        + [pltpu.SemaphoreType.DMA((N_CAST,)), pltpu.SemaphoreType.DMA((N_CAST,)),
           pltpu.VMEM((D_MODEL, PROJ_WIDTH), BF16),
           pltpu.VMEM((W_STAGE_SLOTS, D_MODEL, W_PIECE_COLS), F32),
           pltpu.SemaphoreType.DMA((W_STAGE_SLOTS,))],
        compiler_params=pltpu.CompilerParams(
            dimension_semantics=("arbitrary",), vmem_limit_bytes=VMEM_LIMIT_BYTES),
        name="mixer",
    )(x2, gain, w_in, lb_logits, out_gain, q_gain, k_gain, sinks, *ffn_weights)


def _ffn_kernel(x_ref, rec_ref, att_ref, wo_ref, gain_ref, wg_ref, wu_ref, wd_ref, o_ref):
    mixed = jnp.concatenate([rec_ref[0]] + [att_ref[0, c] for c in range(ATT_SLABS)],
                            axis=1)
    h = x_ref[...] + jnp.dot(mixed, wo_ref[...], preferred_element_type=F32)
    ms = jnp.mean(h * h, axis=-1, keepdims=True)
    u = (h * lax.rsqrt(ms + EPS) * gain_ref[...]).astype(BF16)
    gate = jnp.dot(u, wg_ref[...], preferred_element_type=F32)
    up = jnp.dot(u, wu_ref[...], preferred_element_type=F32)
    gs = 0.5 * gate
    act = ((gs * jnp.tanh(gs) + gs) * up).astype(BF16)
    o_ref[...] = h + jnp.dot(act, wd_ref[...], preferred_element_type=F32)


def _ffn(x2, rec, att, wo, gain, wg, wu, wd):
    t = x2.shape[0]
    seq_blocks = rec.shape[1] // FFN_ROWS

    def resident(shape):
        return pl.BlockSpec(shape, lambda i: (0, 0), pipeline_mode=pl.Buffered(1))

    return pl.pallas_call(
        _ffn_kernel,
        grid=(t // FFN_ROWS,),
        in_specs=[
            pl.BlockSpec((FFN_ROWS, D_MODEL), lambda i: (i, 0)),
            pl.BlockSpec((1, FFN_ROWS, HGRN_WIDTH),
                         lambda i: (i // seq_blocks, lax.rem(i, seq_blocks), 0)),
            pl.BlockSpec((1, ATT_SLABS, FFN_ROWS, LANES),
                         lambda i: (i // seq_blocks, 0, lax.rem(i, seq_blocks), 0)),
            resident((D_MODEL, D_MODEL)),
            resident((1, D_MODEL)),
            resident((D_MODEL, D_FF)),
            resident((D_MODEL, D_FF)),
            resident((D_FF, D_MODEL)),
        ],
        out_specs=pl.BlockSpec((FFN_ROWS, D_MODEL), lambda i: (i, 0)),
        out_shape=jax.ShapeDtypeStruct((t, D_MODEL), F32),
        compiler_params=pltpu.CompilerParams(
            dimension_semantics=("arbitrary",), vmem_limit_bytes=VMEM_LIMIT_BYTES),
        name="outproj_ffn",
    )(x2, rec, att, wo, gain, wg, wu, wd)


def kernel(x, norm1_gain, w_in, hgrn_lb_logits, hgrn_out_gain, q_norm_gain, k_norm_gain,
           attn_sinks, w_out, norm2_gain, w_ffn_gate, w_ffn_up, w_ffn_down):
    b, s, d = x.shape
    assert (d, w_in.shape[0]) == (D_MODEL, 1), "single-layer kernel"
    assert s == SEQ_BLOCKS * MIX_ROWS and MIX_ROWS == FFN_ROWS
    t = b * s
    x2 = x.reshape(t, d)

    rec, att, wo, wg, wu, wd = _mix(
        x2, s, norm1_gain[0][None, :], w_in, hgrn_lb_logits,
        hgrn_out_gain[0][None, :], q_norm_gain[0][None, :], k_norm_gain[0][None, :],
        attn_sinks[0], (w_out[0], w_ffn_gate[0], w_ffn_up[0], w_ffn_down[0]))
    out = _ffn(x2, rec, att, wo, norm2_gain[0][None, :], wg, wu, wd)
    return out.reshape(b, s, d)
```

```python
import jax
import jax.numpy as jnp
from jax import lax
from jax.experimental import pallas as pl
from jax.experimental.pallas import tpu as pltpu

D_MODEL = 1024
HGRN_HEADS = 4
HGRN_DK = 128
HGRN_DV = 128
HGRN_WIDTH = HGRN_HEADS * HGRN_DK
ATTN_HEADS = 8
ATTN_KV_HEADS = 2
ATTN_GROUP = ATTN_HEADS // ATTN_KV_HEADS
ATTN_HD = 64
ATTN_WIDTH = ATTN_HEADS * ATTN_HD
KV_WIDTH = ATTN_KV_HEADS * ATTN_HD
WINDOW = 128
PROJ_WIDTH = 4 * HGRN_WIDTH + ATTN_WIDTH + 2 * KV_WIDTH
D_FF = 2816
EPS = 1e-6
NEG_INF = -1e30
LOG2E = 1.4426950408889634

F32 = jnp.float32
BF16 = jnp.bfloat16

LANES = 128
VMEM_LIMIT_BYTES = 56 * 1024 * 1024

HGRN_CHUNK = 64
HGRN_LEVELS = 6
HGRN_UNROLL = 2
ATTN_BLOCK = 128
MIX_ROWS = 512
MIX_ITERS = 4
SEQ_BLOCKS = 4
FFN_ROWS = 512

N_SLABS = PROJ_WIDTH // LANES
ATT_SLABS = ATTN_WIDTH // LANES
SLABS_PER_ITER = ATT_SLABS // MIX_ITERS
Q_SLAB = 4 * HGRN_HEADS
K_SLAB = Q_SLAB + ATT_SLABS
V_SLAB = K_SLAB + 1
LOOP_SLABS = Q_SLAB // MIX_ITERS
TAIL_SLABS = N_SLABS - MIX_ITERS * LOOP_SLABS
assert MIX_ITERS * LOOP_SLABS == Q_SLAB and V_SLAB == N_SLABS - 1 and LOOP_SLABS % 2 == 0
W_PIECE_COLS = 2 * LANES
W_STAGE_SLOTS = 4
assert PROJ_WIDTH % W_PIECE_COLS == 0 and (LOOP_SLABS * LANES) % W_PIECE_COLS == 0
assert MIX_ROWS == MIX_ITERS * HGRN_UNROLL * HGRN_CHUNK
assert ATT_SLABS == MIX_ITERS * SLABS_PER_ITER

_NT = (((1,), (1,)), ((), ()))
_TN = (((0,), (0,)), ((), ()))


_LEVEL_Q = ((1, 3, 5, 7), (2, 3, 6, 7), (4, 5, 6, 7))
_LEVEL_K = ((0, 2, 4, 6), (0, 1, 4, 5), (0, 1, 2, 3))
LEVEL_ROWS = 8 * sum(len(q) for q in _LEVEL_Q)


def _level_mask():
    r = lax.broadcasted_iota(jnp.int32, (LEVEL_ROWS, LEVEL_ROWS), 0)
    c = lax.broadcasted_iota(jnp.int32, (LEVEL_ROWS, LEVEL_ROWS), 1)
    lev_r, lev_c = r >> 5, c >> 5
    j_r, j_c = (r >> 3) & 3, (c >> 3) & 3
    same_block = (((lev_r == 0) & (j_r == j_c))
                  | ((lev_r == 1) & ((j_r >> 1) == (j_c >> 1)))
                  | (lev_r == 2))
    return ((lev_r == lev_c) & same_block).astype(jnp.int32)


def _mul1(a, b):
    if a is None:
        return b
    if b is None:
        return a
    return a * b


def _pair_rms(x, ones_bd):
    ss = jnp.dot((x * x).astype(BF16), ones_bd, preferred_element_type=F32)
    return x * lax.rsqrt(ss * (1.0 / ATTN_HD) + EPS)


def _hgrn_chunks(load, store, r0s, c0, c1, og, mask_nat, st_ref, cv_ref):
    nh = HGRN_HEADS
    c_rows = HGRN_CHUNK
    nv = c_rows // 8
    assert nv == 8 and HGRN_LEVELS == 6

    def bc(t, r):
        return jnp.broadcast_to(t[r:r + 1, :], (8, HGRN_DK))

    def cat16(pieces):
        return jnp.concatenate(pieces, axis=0).astype(BF16)

    q_order = _LEVEL_Q[0] + _LEVEL_Q[1] + _LEVEL_Q[2]
    k_order = _LEVEL_K[0] + _LEVEL_K[1] + _LEVEL_K[2]

    def stacked_attend(q_pieces, k_pieces, v_pieces, mask):
        s = [lax.dot_general(cat16(q_pieces[u]), cat16(k_pieces[u]), _NT,
                             preferred_element_type=F32) for u in units]
        s = [jnp.where(mask, s[u], 0.0).astype(BF16) for u in units]
        o = [jnp.dot(s[u], cat16([v_pieces[u][j] for j in k_order]),
                     preferred_element_type=F32) for u in units]
        return [[o[u][8 * n:8 * n + 8, :] for n in range(len(q_order))] for u in units]

    def scatter_add(acc, pieces):
        for n, j in enumerate(q_order):
            acc[j] = pieces[n] if acc[j] is None else acc[j] + pieces[n]
        return acc

    m_nat = mask_nat != 0

    units = range(nh * len(r0s))
    hd = [u % nh for u in units]
    r0 = [r0s[u // nh] for u in units]
    rows = [pl.ds(r0[u], c_rows) for u in units]
    lanes = [slice(hd[u] * HGRN_DK, (hd[u] + 1) * HGRN_DK) for u in units]
    vrange = range(nv)

    tot8 = []
    for u in units:
        c0h, c1h = c0[:, lanes[u]], c1[:, lanes[u]]
        fp, kp, qp, vtr = [], [], [], []
        for b in vrange:
            srows = pl.ds(r0[u] + b, 8, stride=8)
            ct = c1h * jnp.tanh(load(1, hd[u], srows))
            fp.append(c0h + ct)
            kp.append(c1h - ct)
            xq = load(0, hd[u], srows)
            xs = xq * (HGRN_DK ** -0.5)
            qp.append(xs * jnp.tanh(xq) + xs)
            vtr.append(load(2, hd[u], srows))

        o_tr = [jnp.sum(qp[b] * kp[b], axis=-1, keepdims=True) * vtr[b] for b in vrange]

        p2 = [fp[b] * fp[b - 1] if b & 1 else fp[b] for b in vrange]
        x2 = [None if b & 1 else fp[b + 1] for b in vrange]
        p4 = [p2[b] * p2[(b & ~3) + 1] if b & 2 else p2[b] for b in vrange]
        x4 = [x2[b] if b & 2 else _mul1(x2[b], p2[(b & ~3) + 3]) for b in vrange]
        p8 = [p4[b] * p4[3] if b & 4 else p4[b] for b in vrange]
        x8 = [x4[b] if b & 4 else _mul1(x4[b], p4[7]) for b in vrange]
        tot8.append(p8[7])

        for b in vrange:
            srows = pl.ds(b, 8, stride=8)
            cv_ref[u, 0, srows, :] = qp[b] * p8[b]
            cv_ref[u, 1, srows, :] = _mul1(kp[b], x8[b])

        q_pieces = ([qp[b] * fp[b] for b in _LEVEL_Q[0]] + [qp[b] * p2[b] for b in _LEVEL_Q[1]]
                    + [qp[b] * p4[b] for b in _LEVEL_Q[2]])
        k_pieces = ([kp[b] for b in _LEVEL_K[0]] + [_mul1(kp[b], x2[b]) for b in _LEVEL_K[1]]
                    + [_mul1(kp[b], x4[b]) for b in _LEVEL_K[2]])
        for lvl in range(3):
            group = 1 << lvl
            for jq, b_q in enumerate(_LEVEL_Q[lvl]):
                for jk, b_k in enumerate(_LEVEL_K[lvl]):
                    if jq // group == jk // group:
                        d = jnp.sum(q_pieces[4 * lvl + jq] * k_pieces[4 * lvl + jk],
                                    axis=-1, keepdims=True)
                        o_tr[b_q] = o_tr[b_q] + d * vtr[b_k]
        for b in vrange:
            cv_ref[u, 2, pl.ds(b, 8, stride=8), :] = o_tr[b]

    an = [[cv_ref[u, 0, 8 * a:8 * a + 8, :] for a in vrange] for u in units]
    cn = [[cv_ref[u, 1, 8 * a:8 * a + 8, :] for a in vrange] for u in units]
    vnat = [[load(2, hd[u], pl.ds(r0[u] + 8 * a, 8)) for a in vrange] for u in units]

    q4 = [[an[u][a] for a in _LEVEL_Q[0]] for u in units]
    k4 = [[cn[u][a] for a in _LEVEL_K[0]] for u in units]
    an = [[an[u][a] * bc(tot8[u], a - 1) if a & 1 else an[u][a] for a in vrange] for u in units]
    cn = [[cn[u][a] if a & 1 else cn[u][a] * bc(tot8[u], a + 1) for a in vrange] for u in units]
    t16 = [tot8[u] * pltpu.roll(tot8[u], 1, 0) for u in units]
    q5 = [[an[u][a] for a in _LEVEL_Q[1]] for u in units]
    k5 = [[cn[u][a] for a in _LEVEL_K[1]] for u in units]
    an = [[an[u][a] * bc(t16[u], (a & ~3) + 1) if a & 2 else an[u][a] for a in vrange]
          for u in units]
    cn = [[cn[u][a] if a & 2 else cn[u][a] * bc(t16[u], (a & ~3) + 3) for a in vrange]
          for u in units]
    t32 = [t16[u] * pltpu.roll(t16[u], 2, 0) for u in units]
    q6 = [[an[u][a] for a in _LEVEL_Q[2]] for u in units]
    k6 = [[cn[u][a] for a in _LEVEL_K[2]] for u in units]
    an = [[an[u][a] * bc(t32[u], 3) if a & 4 else an[u][a] for a in vrange] for u in units]
    cn = [[cn[u][a] if a & 4 else cn[u][a] * bc(t32[u], 7) for a in vrange] for u in units]
    att_nat = stacked_attend([q4[u] + q5[u] + q6[u] for u in units],
                             [k4[u] + k5[u] + k6[u] for u in units], vnat, m_nat)
    o_nat = []
    for u in units:
        pieces = scatter_add([None] * nv, att_nat[u])
        pieces[0] = jnp.zeros((8, HGRN_DV), F32)
        o_nat.append(jnp.concatenate(pieces, axis=0))

    upd = [lax.dot_general(cat16(vnat[u]), cat16(cn[u]),
                           _TN, preferred_element_type=F32) for u in units]
    for h in range(nh):
        st = st_ref[h]
        for u in range(h, len(units), nh):
            qs = cat16(an[u])
            o_nat[u] = o_nat[u] + lax.dot_general(qs, st.astype(BF16), _NT,
                                                  preferred_element_type=F32)
            dec = t32[u][3:4, :] * t32[u][7:8, :]
            st = st * dec + upd[u]
        st_ref[h] = st

    for u in units:
        o_h = o_nat[u] + cv_ref[u, 2]
        ms = jnp.mean(o_h * o_h, axis=-1, keepdims=True)
        xgs = load(3, hd[u], rows[u])
        gate = xgs * jnp.tanh(xgs) + xgs
        store(hd[u], rows[u], o_h * lax.rsqrt(ms + EPS) * og * gate)


def _swa_slab(q_slab, k_lo, k_hi, v_lo, v_hi, ones_pad, bias_first, bias_any,
              sink_a, sink_b, qgain, ones_bd, lo128, store):
    blk = ATTN_BLOCK
    nblk = MIX_ROWS // blk
    state = {}

    def prep():
        state["qs"] = (_pair_rms(q_slab(), ones_bd) * qgain).astype(BF16)
        state["sink_slab"] = jnp.where(lo128, sink_a, sink_b)

    def scores(n):
        keys = slice(n * blk, (n + 2) * blk)
        k_cat = jnp.concatenate([k_lo()[keys], k_hi()[keys]], axis=0)
        s = lax.dot_general(state["qs"][n * blk:(n + 1) * blk], k_cat, _NT,
                            preferred_element_type=F32)
        s = s + (bias_first() if n == 0 else bias_any())
        m_a = jnp.max(s[:, :2 * blk], axis=-1, keepdims=True)
        m_b = jnp.max(s[:, 2 * blk:], axis=-1, keepdims=True)
        m_a = jnp.maximum(jnp.broadcast_to(m_a, (blk, LANES)), sink_a)
        m_b = jnp.maximum(jnp.broadcast_to(m_b, (blk, LANES)), sink_b)
        cols = [s[:, c * LANES:(c + 1) * LANES] - (m_a if c < 2 * blk // LANES else m_b)
                for c in range(4 * blk // LANES)]
        state["p", n] = jnp.exp2(jnp.concatenate(cols, axis=1)).astype(BF16)
        state["m", n] = jnp.where(lo128, m_a, m_b)

    def attend(n):
        keys = slice(n * blk, (n + 2) * blk)
        v_cat = jnp.concatenate([v_lo()[keys], v_hi()[keys]], axis=0)
        rhs = jnp.concatenate([v_cat, ones_pad], axis=1)
        nd = jnp.dot(state.pop(("p", n)), rhs, preferred_element_type=F32)
        den = nd[:, LANES:] + jnp.exp2(state["sink_slab"] - state.pop(("m", n)))
        store(n, nd[:, :LANES] / den)

    stages = [prep]
    for n in range(nblk):
        stages += [lambda n=n: scores(n), lambda n=n: attend(n)]
    return stages


_CAST_SHAPES = ((D_MODEL, D_MODEL, 32), (D_MODEL, D_FF, 32), (D_MODEL, D_FF, 32), (D_FF, D_MODEL, 96))
N_CAST = len(_CAST_SHAPES)


def _cast_chunk(k, step):
    rows, _, per = _CAST_SHAPES[k]
    n = -(-rows // per)
    start = jnp.minimum(step * per, rows - per)
    return n, pl.multiple_of(start, 16)


def _cast_copies(k, step, src_refs, dst_refs, fbuf_refs, bbuf_refs, sem_in, sem_out):
    per = _CAST_SHAPES[k][2]
    _, start = _cast_chunk(k, step)
    rows = pl.ds(start, per)
    return (pltpu.make_async_copy(src_refs[k].at[rows, :], fbuf_refs[k], sem_in.at[k]),
            pltpu.make_async_copy(bbuf_refs[k], dst_refs[k].at[rows, :], sem_out.at[k]))


def _mix_kernel(x_ref, gain_ref, w_hbm, lbl_ref, og_ref, qg_ref, kg_ref, sink_ref,
                *refs):
    cast_src = refs[:N_CAST]
    rec_ref, att_ref = refs[N_CAST:N_CAST + 2]
    cast_dst = refs[N_CAST + 2:2 * N_CAST + 2]
    (u_ref, pa_ref, pb_ref, st_ref, cv_ref, kvp_ref, kpad_ref, vpad_ref, bias_ref,
     lev_ref) = refs[2 * N_CAST + 2:2 * N_CAST + 12]
    cast_f = refs[2 * N_CAST + 12:3 * N_CAST + 12]
    cast_b = refs[3 * N_CAST + 12:4 * N_CAST + 12]
    sem_in, sem_out, w_ref, w_stage, sem_w = refs[4 * N_CAST + 12:]
    g = pl.program_id(0)
    blk = ATTN_BLOCK

    def cast_copies(k, step):
        return _cast_copies(k, step, cast_src, cast_dst, cast_f, cast_b, sem_in, sem_out)

    cast_groups = {}
    for k in range(N_CAST):
        cast_groups.setdefault(_cast_chunk(k, g)[0], []).append(k)

    for n_chunks, members in cast_groups.items():
        @pl.when(g < n_chunks)
        def _(members=members):
            for k in members:
                cast_copies(k, g)[0].start()

    lo128 = lax.broadcasted_iota(jnp.int32, (1, LANES), 1) < ATTN_HD

    @pl.when(g == 0)
    def _():
        kvp_ref[...] = jnp.zeros_like(kvp_ref)
        lev_ref[...] = _level_mask()
        qi = lax.broadcasted_iota(jnp.int32, (blk, 2 * blk), 0)
        kj = lax.broadcasted_iota(jnp.int32, (blk, 2 * blk), 1)
        rel = qi + blk - kj
        in_window = (rel >= 0) & (rel < WINDOW)
        b_any = jnp.where(in_window, 0.0, NEG_INF)
        b_first = jnp.where(in_window & (kj >= blk), 0.0, NEG_INF)
        bias_ref[0] = jnp.concatenate([b_any, b_any], axis=1)
        bias_ref[1] = jnp.concatenate([b_first, b_first], axis=1)

    first_blk = lax.rem(g + SEQ_BLOCKS - 1, SEQ_BLOCKS) == 0

    @pl.when(first_blk)
    def _():
        st_ref[...] = jnp.zeros_like(st_ref)

    hr = lax.broadcasted_iota(jnp.int32, (LANES, LANES), 0) < ATTN_HD
    hc = lax.broadcasted_iota(jnp.int32, (LANES, LANES), 1) < ATTN_HD
    ones_bd = jnp.where(hr == hc, 1.0, 0.0).astype(BF16)
    pair = lambda r: jnp.concatenate([r, r], axis=1)
    qgain = pair(qg_ref[...]) * (ATTN_HD ** -0.5 * LOG2E)
    kgain = pair(kg_ref[...])
    ones_lo = jnp.broadcast_to(jnp.where(lo128, 1.0, 0.0), (2 * blk, LANES))
    ones_pad = jnp.concatenate([ones_lo, 1.0 - ones_lo], axis=0).astype(BF16)

    l0 = lbl_ref[0:1, :]
    l1 = lbl_ref[1:2, :]
    lmax = jnp.maximum(l0, l1)
    e0 = jnp.exp(l0 - lmax)
    e1 = jnp.exp(l1 - lmax)
    lb = e0 / (e0 + e1)
    c0 = 0.5 * (1.0 + lb)
    c1 = 0.5 * (1.0 - lb)
    og = og_ref[...]
    first_idx = first_blk.astype(jnp.int32)

    def step(pc_ref, pp_ref):
        _mix_step(pc_ref, pp_ref, x_ref, gain_ref, u_ref, w_ref, rec_ref, att_ref,
                  st_ref, cv_ref, kvp_ref, kpad_ref, vpad_ref, bias_ref, lev_ref, sink_ref,
                  (lo128, ones_bd, qgain, kgain, ones_pad, c0, c1, og, first_idx))

    @pl.when(g == 0)
    def _():
        _load_weights_and_project(pa_ref, x_ref, gain_ref, u_ref, w_hbm, w_ref, w_stage, sem_w)

    @pl.when((lax.rem(g, 2) == 0) & (g > 0))
    def _():
        step(pa_ref, pb_ref)

    @pl.when(lax.rem(g, 2) == 1)
    def _():
        step(pb_ref, pa_ref)

    for n_chunks, members in cast_groups.items():
        @pl.when((g >= 1) & (g <= n_chunks))
        def _(members=members):
            for k in members:
                cast_copies(k, g - 1)[1].wait()

        @pl.when(g < n_chunks)
        def _(members=members):
            for k in members:
                copy_in, copy_out = cast_copies(k, g)
                copy_in.wait()
                cast_b[k][...] = cast_f[k][...].astype(BF16)
                copy_out.start()


def _normalize(x_ref, gain_ref, u_ref):
    x = x_ref[...]
    ms = jnp.mean(x * x, axis=-1, keepdims=True)
    u_ref[...] = (x * lax.rsqrt(ms + EPS) * gain_ref[...]).astype(BF16)


def _project_slabs(pc_ref, u_ref, w_ref, first, count):
    res = jnp.dot(u_ref[...], w_ref[:, first * LANES:(first + count) * LANES],
                  preferred_element_type=F32)
    for t in range(count):
        pc_ref[first + t] = res[:, t * LANES:(t + 1) * LANES]


def _project_tail(pc_ref, u_ref, w_ref):
    _project_slabs(pc_ref, u_ref, w_ref, MIX_ITERS * LOOP_SLABS, TAIL_SLABS)


def _project_main(pc_ref, u_ref, w_ref, i):
    _project_slabs(pc_ref, u_ref, w_ref, i * LOOP_SLABS, LOOP_SLABS)


def _halved_part(col):
    part = col // HGRN_WIDTH
    return part in (0, 1, 3)


def _load_weights_and_project(pc_ref, x_ref, gain_ref, u_ref, w_hbm, w_ref, stage_ref, sem):
    n = PROJ_WIDTH // W_PIECE_COLS

    slots = stage_ref.shape[0]

    def copy(k):
        cols = pl.ds(k * W_PIECE_COLS, W_PIECE_COLS)
        return pltpu.make_async_copy(w_hbm.at[0, :, cols], stage_ref.at[k % slots],
                                     sem.at[k % slots])

    for k in range(min(slots - 1, n)):
        copy(k).start()
    _normalize(x_ref, gain_ref, u_ref)
    per_main = LOOP_SLABS * LANES // W_PIECE_COLS
    for k in range(n):
        if k + slots - 1 < n:
            copy(k + slots - 1).start()
        copy(k).wait()
        piece = stage_ref[k % slots]
        if _halved_part(k * W_PIECE_COLS):
            piece = piece * 0.5
        w_ref[:, k * W_PIECE_COLS:(k + 1) * W_PIECE_COLS] = piece.astype(BF16)
        if (k + 1) % per_main == 0 and (k + 1) // per_main <= MIX_ITERS:
            _project_main(pc_ref, u_ref, w_ref, (k + 1) // per_main - 1)
    _project_tail(pc_ref, u_ref, w_ref)


def _mix_step(pc_ref, pp_ref, x_ref, gain_ref, u_ref, w_ref, rec_ref, att_ref, st_ref,
              cv_ref, kvp_ref, kpad_ref, vpad_ref, bias_ref, lev_ref, sink_ref, consts):
    (lo128, ones_bd, qgain, kgain, ones_pad, c0, c1, og, first_idx) = consts
    blk = ATTN_BLOCK
    _normalize(x_ref, gain_ref, u_ref)

    k_all = jnp.concatenate([kvp_ref[0], pp_ref[K_SLAB]], axis=0)
    v_all = jnp.concatenate([kvp_ref[1], pp_ref[V_SLAB]], axis=0)
    kvp_ref[0] = pp_ref[K_SLAB, MIX_ROWS - blk:, :]
    kvp_ref[1] = pp_ref[V_SLAB, MIX_ROWS - blk:, :]
    kn = _pair_rms(k_all, ones_bd) * kgain
    kn_sw = pltpu.roll(kn, ATTN_HD, 1)
    v_sw = pltpu.roll(v_all, ATTN_HD, 1)
    zero = jnp.zeros_like(kn)
    kpad_ref[0, 0] = jnp.where(lo128, kn, zero).astype(BF16)
    kpad_ref[0, 1] = jnp.where(lo128, zero, kn_sw).astype(BF16)
    kpad_ref[1, 0] = jnp.where(lo128, kn_sw, zero).astype(BF16)
    kpad_ref[1, 1] = jnp.where(lo128, zero, kn).astype(BF16)
    vpad_ref[0, 0] = jnp.where(lo128, v_all, zero).astype(BF16)
    vpad_ref[0, 1] = jnp.where(lo128, zero, v_sw).astype(BF16)
    vpad_ref[1, 0] = jnp.where(lo128, v_sw, zero).astype(BF16)
    vpad_ref[1, 1] = jnp.where(lo128, zero, v_all).astype(BF16)

    def load(part, h, rows):
        return pp_ref[part * HGRN_HEADS + h, rows, :]

    def store(h, rows, tile):
        rec_ref[0, rows, h * HGRN_DV:(h + 1) * HGRN_DV] = tile.astype(rec_ref.dtype)

    def hgrn(i):
        r0s = [(i * HGRN_UNROLL + cc) * HGRN_CHUNK for cc in range(HGRN_UNROLL)]
        _hgrn_chunks(load, store, r0s, c0, c1, og, lev_ref[...], st_ref, cv_ref)

    def slab_stages(c):
        kvh = c // (ATTN_GROUP // 2)

        def store_att(n, tile):
            att_ref[0, c, n * blk:(n + 1) * blk, :] = tile.astype(att_ref.dtype)

        return _swa_slab(lambda: pp_ref[Q_SLAB + c],
                         lambda: kpad_ref[kvh, 0], lambda: kpad_ref[kvh, 1],
                         lambda: vpad_ref[kvh, 0], lambda: vpad_ref[kvh, 1], ones_pad,
                         lambda: bias_ref[first_idx], lambda: bias_ref[0],
                         sink_ref[2 * c] * LOG2E, sink_ref[2 * c + 1] * LOG2E,
                         qgain, ones_bd, lo128, store_att)

    def swa(i):
        slabs = [slab_stages(i * SLABS_PER_ITER + s) for s in range(SLABS_PER_ITER)]
        for stages in zip(*slabs):
            for stage in stages:
                stage()

    for i in range(MIX_ITERS):
        _project_main(pc_ref, u_ref, w_ref, i)
        hgrn(i)
        swa(i)
    _project_tail(pc_ref, u_ref, w_ref)


def _mix(x2, seq_len, gain, w_in, lb_logits, out_gain, q_gain, k_gain, sinks,
         ffn_weights):
    assert w_in.shape == (1, D_MODEL, PROJ_WIDTH) and w_in.dtype == F32
    t = x2.shape[0]
    nblk = t // MIX_ROWS
    assert seq_len // MIX_ROWS == SEQ_BLOCKS
    batch = t // seq_len
    assert tuple(w.shape for w in ffn_weights) == tuple(s[:2] for s in _CAST_SHAPES)
    assert all(-(-rows // per) <= nblk for rows, _, per in _CAST_SHAPES)
    hbm = pl.BlockSpec(memory_space=pl.ANY)

    def in_rows(gi):
        return (jnp.minimum(gi, nblk - 1), 0)

    def out_block(gi):
        blk_i = jnp.maximum(gi - 1, 0)
        return blk_i // SEQ_BLOCKS, lax.rem(blk_i, SEQ_BLOCKS)

    full = lambda shape: pl.BlockSpec(shape, lambda gi: (0,) * len(shape))
    return pl.pallas_call(
        _mix_kernel,
        grid=(nblk + 1,),
        in_specs=[
            pl.BlockSpec((MIX_ROWS, D_MODEL), in_rows),
            full((1, D_MODEL)),
            hbm,
            full((2, HGRN_WIDTH)),
            full((1, HGRN_DV)),
            full((1, ATTN_HD)),
            full((1, ATTN_HD)),
            pl.BlockSpec(memory_space=pltpu.SMEM),
        ] + [hbm] * N_CAST,
        out_specs=[
            pl.BlockSpec((1, MIX_ROWS, HGRN_WIDTH),
                         lambda gi: (*out_block(gi), 0)),
            pl.BlockSpec((1, ATT_SLABS, MIX_ROWS, LANES),
                         lambda gi: (out_block(gi)[0], 0, out_block(gi)[1], 0)),
        ] + [hbm] * N_CAST,
        out_shape=[
            jax.ShapeDtypeStruct((batch, seq_len, HGRN_WIDTH), BF16),
            jax.ShapeDtypeStruct((batch, ATT_SLABS, seq_len, LANES), BF16),
        ] + [jax.ShapeDtypeStruct(w.shape, BF16) for w in ffn_weights],
        scratch_shapes=[
            pltpu.VMEM((MIX_ROWS, D_MODEL), BF16),
            pltpu.VMEM((N_SLABS, MIX_ROWS, LANES), F32),
            pltpu.VMEM((N_SLABS, MIX_ROWS, LANES), F32),
            pltpu.VMEM((HGRN_HEADS, HGRN_DV, HGRN_DK), F32),
            pltpu.VMEM((HGRN_HEADS * HGRN_UNROLL, 3, HGRN_CHUNK, HGRN_DK), F32),
            pltpu.VMEM((2, ATTN_BLOCK, LANES), F32),
            pltpu.VMEM((ATTN_KV_HEADS, 2, ATTN_BLOCK + MIX_ROWS, LANES), BF16),
            pltpu.VMEM((ATTN_KV_HEADS, 2, ATTN_BLOCK + MIX_ROWS, LANES), BF16),
            pltpu.VMEM((2, ATTN_BLOCK, 4 * ATTN_BLOCK), F32),
            pltpu.VMEM((LEVEL_ROWS, LEVEL_ROWS), jnp.int32),
        ] + [pltpu.VMEM((per, cols), F32) for _, cols, per in _CAST_SHAPES]
        + [pltpu.VMEM((per, cols), BF16) for _, cols, per in _CAST_SHAPES]
        + [pltpu.SemaphoreType.DMA((N_CAST,)), pltpu.SemaphoreType.DMA((N_CAST,)),
           pltpu.VMEM((D_MODEL, PROJ_WIDTH), BF16),
           pltpu.VMEM((W_STAGE_SLOTS, D_MODEL, W_PIECE_COLS), F32),
           pltpu.SemaphoreType.DMA((W_STAGE_SLOTS,))],
        compiler_params=pltpu.CompilerParams(
            dimension_semantics=("arbitrary",), vmem_limit_bytes=VMEM_LIMIT_BYTES),
        name="mixer",
    )(x2, gain, w_in, lb_logits, out_gain, q_gain, k_gain, sinks, *ffn_weights)


def _ffn_kernel(x_ref, rec_ref, att_ref, wo_ref, gain_ref, wg_ref, wu_ref, wd_ref, o_ref):
    mixed = jnp.concatenate([rec_ref[0]] + [att_ref[0, c] for c in range(ATT_SLABS)],
                            axis=1)
    h = x_ref[...] + jnp.dot(mixed, wo_ref[...], preferred_element_type=F32)
    ms = jnp.mean(h * h, axis=-1, keepdims=True)
    u = (h * lax.rsqrt(ms + EPS) * gain_ref[...]).astype(BF16)
    gate = jnp.dot(u, wg_ref[...], preferred_element_type=F32)
    up = jnp.dot(u, wu_ref[...], preferred_element_type=F32)
    gs = 0.5 * gate
    act = ((gs * jnp.tanh(gs) + gs) * up).astype(BF16)
    o_ref[...] = h + jnp.dot(act, wd_ref[...], preferred_element_type=F32)


def _ffn(x2, rec, att, wo, gain, wg, wu, wd):
    t = x2.shape[0]
    seq_blocks = rec.shape[1] // FFN_ROWS

    def resident(shape):
        return pl.BlockSpec(shape, lambda i: (0, 0), pipeline_mode=pl.Buffered(1))

    return pl.pallas_call(
        _ffn_kernel,
        grid=(t // FFN_ROWS,),
        in_specs=[
            pl.BlockSpec((FFN_ROWS, D_MODEL), lambda i: (i, 0)),
            pl.BlockSpec((1, FFN_ROWS, HGRN_WIDTH),
                         lambda i: (i // seq_blocks, lax.rem(i, seq_blocks), 0)),
            pl.BlockSpec((1, ATT_SLABS, FFN_ROWS, LANES),
                         lambda i: (i // seq_blocks, 0, lax.rem(i, seq_blocks), 0)),
            resident((D_MODEL, D_MODEL)),
            resident((1, D_MODEL)),
            resident((D_MODEL, D_FF)),
            resident((D_MODEL, D_FF)),
            resident((D_FF, D_MODEL)),
        ],
        out_specs=pl.BlockSpec((FFN_ROWS, D_MODEL), lambda i: (i, 0)),
        out_shape=jax.ShapeDtypeStruct((t, D_MODEL), F32),
        compiler_params=pltpu.CompilerParams(
            dimension_semantics=("arbitrary",), vmem_limit_bytes=VMEM_LIMIT_BYTES),
        name="outproj_ffn",
    )(x2, rec, att, wo, gain, wg, wu, wd)


def kernel(x, norm1_gain, w_in, hgrn_lb_logits, hgrn_out_gain, q_norm_gain, k_norm_gain,
           attn_sinks, w_out, norm2_gain, w_ffn_gate, w_ffn_up, w_ffn_down):
    b, s, d = x.shape
    assert (d, w_in.shape[0]) == (D_MODEL, 1), "single-layer kernel"
    assert s == SEQ_BLOCKS * MIX_ROWS and MIX_ROWS == FFN_ROWS
    t = b * s
    x2 = x.reshape(t, d)

    rec, att, wo, wg, wu, wd = _mix(
        x2, s, norm1_gain[0][None, :], w_in, hgrn_lb_logits,
        hgrn_out_gain[0][None, :], q_norm_gain[0][None, :], k_norm_gain[0][None, :],
        attn_sinks[0], (w_out[0], w_ffn_gate[0], w_ffn_up[0], w_ffn_down[0]))
    out = _ffn(x2, rec, att, wo, norm2_gain[0][None, :], wg, wu, wd)
    return out.reshape(b, s, d)
```

```python
import jax
import jax.numpy as jnp
from jax import lax
from jax.experimental import pallas as pl
from jax.experimental.pallas import tpu as pltpu

D_MODEL = 1024
HGRN_HEADS = 4
HGRN_DK = 128
HGRN_DV = 128
HGRN_WIDTH = HGRN_HEADS * HGRN_DK
ATTN_HEADS = 8
ATTN_KV_HEADS = 2
ATTN_GROUP = ATTN_HEADS // ATTN_KV_HEADS
ATTN_HD = 64
ATTN_WIDTH = ATTN_HEADS * ATTN_HD
KV_WIDTH = ATTN_KV_HEADS * ATTN_HD
WINDOW = 128
PROJ_WIDTH = 4 * HGRN_WIDTH + ATTN_WIDTH + 2 * KV_WIDTH
D_FF = 2816
EPS = 1e-6
NEG_INF = -1e30
LOG2E = 1.4426950408889634

F32 = jnp.float32
BF16 = jnp.bfloat16

LANES = 128
VMEM_LIMIT_BYTES = 56 * 1024 * 1024

HGRN_CHUNK = 64
HGRN_LEVELS = 6
HGRN_UNROLL = 2
ATTN_BLOCK = 128
MIX_ROWS = 512
MIX_ITERS = 4
SEQ_BLOCKS = 4
FFN_ROWS = 512

N_SLABS = PROJ_WIDTH // LANES
ATT_SLABS = ATTN_WIDTH // LANES
SLABS_PER_ITER = ATT_SLABS // MIX_ITERS
Q_SLAB = 4 * HGRN_HEADS
K_SLAB = Q_SLAB + ATT_SLABS
V_SLAB = K_SLAB + 1
LOOP_SLABS = Q_SLAB // MIX_ITERS
TAIL_SLABS = N_SLABS - MIX_ITERS * LOOP_SLABS
assert MIX_ITERS * LOOP_SLABS == Q_SLAB and V_SLAB == N_SLABS - 1 and LOOP_SLABS % 2 == 0
W_PIECE_COLS = 2 * LANES
W_STAGE_SLOTS = 4
assert PROJ_WIDTH % W_PIECE_COLS == 0 and (LOOP_SLABS * LANES) % W_PIECE_COLS == 0
assert MIX_ROWS == MIX_ITERS * HGRN_UNROLL * HGRN_CHUNK
assert ATT_SLABS == MIX_ITERS * SLABS_PER_ITER

_NT = (((1,), (1,)), ((), ()))
_TN = (((0,), (0,)), ((), ()))


_LEVEL_Q = ((1, 3, 5, 7), (2, 3, 6, 7), (4, 5, 6, 7))
_LEVEL_K = ((0, 2, 4, 6), (0, 1, 4, 5), (0, 1, 2, 3))
LEVEL_ROWS = 8 * sum(len(q) for q in _LEVEL_Q)


def _level_mask():
    r = lax.broadcasted_iota(jnp.int32, (LEVEL_ROWS, LEVEL_ROWS), 0)
    c = lax.broadcasted_iota(jnp.int32, (LEVEL_ROWS, LEVEL_ROWS), 1)
    lev_r, lev_c = r >> 5, c >> 5
    j_r, j_c = (r >> 3) & 3, (c >> 3) & 3
    same_block = (((lev_r == 0) & (j_r == j_c))
                  | ((lev_r == 1) & ((j_r >> 1) == (j_c >> 1)))
                  | (lev_r == 2))
    return ((lev_r == lev_c) & same_block).astype(jnp.int32)


def _mul1(a, b):
    if a is None:
        return b
    if b is None:
        return a
    return a * b


def _pair_rms(x, ones_bd):
    ss = jnp.dot((x * x).astype(BF16), ones_bd, preferred_element_type=F32)
    return x * lax.rsqrt(ss * (1.0 / ATTN_HD) + EPS)


def _hgrn_chunks(load, store, r0s, c0, c1, og, mask_nat, st_ref, cv_ref):
    nh = HGRN_HEADS
    c_rows = HGRN_CHUNK
    nv = c_rows // 8
    assert nv == 8 and HGRN_LEVELS == 6

    def bc(t, r):
        return jnp.broadcast_to(t[r:r + 1, :], (8, HGRN_DK))

    def cat16(pieces):
        return jnp.concatenate(pieces, axis=0).astype(BF16)

    q_order = _LEVEL_Q[0] + _LEVEL_Q[1] + _LEVEL_Q[2]
    k_order = _LEVEL_K[0] + _LEVEL_K[1] + _LEVEL_K[2]

    def stacked_attend(q_stack, k_stack, v_stack, mask):
        s = [lax.dot_general(q_stack[u], k_stack[u], _NT, preferred_element_type=F32)
             for u in units]
        s = [jnp.where(mask, s[u], 0.0).astype(BF16) for u in units]
        o = [jnp.dot(s[u], v_stack[u], preferred_element_type=F32) for u in units]
        return [[o[u][8 * n:8 * n + 8, :] for n in range(len(q_order))] for u in units]

    def scatter_add(acc, pieces):
        for n, j in enumerate(q_order):
            acc[j] = pieces[n] if acc[j] is None else acc[j] + pieces[n]
        return acc

    m_nat = mask_nat != 0

    units = range(nh * len(r0s))
    hd = [u % nh for u in units]
    r0 = [r0s[u // nh] for u in units]
    rows = [pl.ds(r0[u], c_rows) for u in units]
    lanes = [slice(hd[u] * HGRN_DK, (hd[u] + 1) * HGRN_DK) for u in units]
    vrange = range(nv)

    tot8 = []
    for u in units:
        c0h, c1h = c0[:, lanes[u]], c1[:, lanes[u]]
        fp, kp, qp, vtr = [], [], [], []
        for b in vrange:
            srows = pl.ds(r0[u] + b, 8, stride=8)
            ct = c1h * jnp.tanh(load(1, hd[u], srows))
            fp.append(c0h + ct)
            kp.append(c1h - ct)
            xq = load(0, hd[u], srows)
            xs = xq * (HGRN_DK ** -0.5)
            qp.append(xs * jnp.tanh(xq) + xs)
            vtr.append(load(2, hd[u], srows))

        o_tr = [jnp.sum(qp[b] * kp[b], axis=-1, keepdims=True) * vtr[b] for b in vrange]

        p2 = [fp[b] * fp[b - 1] if b & 1 else fp[b] for b in vrange]
        x2 = [None if b & 1 else fp[b + 1] for b in vrange]
        p4 = [p2[b] * p2[(b & ~3) + 1] if b & 2 else p2[b] for b in vrange]
        x4 = [x2[b] if b & 2 else _mul1(x2[b], p2[(b & ~3) + 3]) for b in vrange]
        p8 = [p4[b] * p4[3] if b & 4 else p4[b] for b in vrange]
        x8 = [x4[b] if b & 4 else _mul1(x4[b], p4[7]) for b in vrange]
        tot8.append(p8[7])

        for b in vrange:
            srows = pl.ds(b, 8, stride=8)
            cv_ref[u, 0, srows, :] = qp[b] * p8[b]
            cv_ref[u, 1, srows, :] = _mul1(kp[b], x8[b])

        q_pieces = ([qp[b] * fp[b] for b in _LEVEL_Q[0]] + [qp[b] * p2[b] for b in _LEVEL_Q[1]]
                    + [qp[b] * p4[b] for b in _LEVEL_Q[2]])
        k_pieces = ([kp[b] for b in _LEVEL_K[0]] + [_mul1(kp[b], x2[b]) for b in _LEVEL_K[1]]
                    + [_mul1(kp[b], x4[b]) for b in _LEVEL_K[2]])
        for lvl in range(3):
            group = 1 << lvl
            for jq, b_q in enumerate(_LEVEL_Q[lvl]):
                for jk, b_k in enumerate(_LEVEL_K[lvl]):
                    if jq // group == jk // group:
                        d = jnp.sum(q_pieces[4 * lvl + jq] * k_pieces[4 * lvl + jk],
                                    axis=-1, keepdims=True)
                        o_tr[b_q] = o_tr[b_q] + d * vtr[b_k]
        for b in vrange:
            cv_ref[u, 2, pl.ds(b, 8, stride=8), :] = o_tr[b]

    q_nat, k_nat, v_nat, qs, kd, vb, dec = [], [], [], [], [], [], []
    for u in units:
        an = [cv_ref[u, 0, 8 * a:8 * a + 8, :] for a in vrange]
        cn = [cv_ref[u, 1, 8 * a:8 * a + 8, :] for a in vrange]
        vnat = [load(2, hd[u], pl.ds(r0[u] + 8 * a, 8)) for a in vrange]
        q4 = [an[a] for a in _LEVEL_Q[0]]
        k4 = [cn[a] for a in _LEVEL_K[0]]
        an = [an[a] * bc(tot8[u], a - 1) if a & 1 else an[a] for a in vrange]
        cn = [cn[a] if a & 1 else cn[a] * bc(tot8[u], a + 1) for a in vrange]
        t16 = tot8[u] * pltpu.roll(tot8[u], 1, 0)
        q5 = [an[a] for a in _LEVEL_Q[1]]
        k5 = [cn[a] for a in _LEVEL_K[1]]
        an = [an[a] * bc(t16, (a & ~3) + 1) if a & 2 else an[a] for a in vrange]
        cn = [cn[a] if a & 2 else cn[a] * bc(t16, (a & ~3) + 3) for a in vrange]
        t32 = t16 * pltpu.roll(t16, 2, 0)
        q6 = [an[a] for a in _LEVEL_Q[2]]
        k6 = [cn[a] for a in _LEVEL_K[2]]
        an = [an[a] * bc(t32, 3) if a & 4 else an[a] for a in vrange]
        cn = [cn[a] if a & 4 else cn[a] * bc(t32, 7) for a in vrange]
        q_nat.append(cat16(q4 + q5 + q6))
        k_nat.append(cat16(k4 + k5 + k6))
        v_nat.append(cat16([vnat[j] for j in k_order]))
        qs.append(cat16(an))
        kd.append(cat16(cn))
        vb.append(cat16(vnat))
        dec.append(t32[3:4, :] * t32[7:8, :])

    att_nat = stacked_attend(q_nat, k_nat, v_nat, m_nat)
    o_nat = []
    for u in units:
        pieces = scatter_add([None] * nv, att_nat[u])
        pieces[0] = jnp.zeros((8, HGRN_DV), F32)
        o_nat.append(jnp.concatenate(pieces, axis=0))

    upd = [lax.dot_general(vb[u], kd[u], _TN, preferred_element_type=F32)
           for u in units]
    for h in range(nh):
        st = st_ref[h]
        for u in range(h, len(units), nh):
            o_nat[u] = o_nat[u] + lax.dot_general(qs[u], st.astype(BF16), _NT,
                                                  preferred_element_type=F32)
            st = st * dec[u] + upd[u]
        st_ref[h] = st

    for u in units:
        o_h = o_nat[u] + cv_ref[u, 2]
        ms = jnp.mean(o_h * o_h, axis=-1, keepdims=True)
        xgs = load(3, hd[u], rows[u])
        gate = xgs * jnp.tanh(xgs) + xgs
        store(hd[u], rows[u], o_h * lax.rsqrt(ms + EPS) * og * gate)


def _swa_slab(q_slab, k_lo, k_hi, v_lo, v_hi, ones_pad, bias_first, bias_any,
              sink_a, sink_b, qgain, ones_bd, lo128, store):
    blk = ATTN_BLOCK
    nblk = MIX_ROWS // blk
    state = {}

    def prep():
        state["qs"] = (_pair_rms(q_slab(), ones_bd) * qgain).astype(BF16)
        state["sink_slab"] = jnp.where(lo128, sink_a, sink_b)

    def scores(n):
        keys = slice(n * blk, (n + 2) * blk)
        k_cat = jnp.concatenate([k_lo()[keys], k_hi()[keys]], axis=0)
        s = lax.dot_general(state["qs"][n * blk:(n + 1) * blk], k_cat, _NT,
                            preferred_element_type=F32)
        s = s + (bias_first() if n == 0 else bias_any())
        m_a = jnp.max(s[:, :2 * blk], axis=-1, keepdims=True)
        m_b = jnp.max(s[:, 2 * blk:], axis=-1, keepdims=True)
        m_a = jnp.maximum(jnp.broadcast_to(m_a, (blk, LANES)), sink_a)
        m_b = jnp.maximum(jnp.broadcast_to(m_b, (blk, LANES)), sink_b)
        cols = [s[:, c * LANES:(c + 1) * LANES] - (m_a if c < 2 * blk // LANES else m_b)
                for c in range(4 * blk // LANES)]
        state["p", n] = jnp.exp2(jnp.concatenate(cols, axis=1)).astype(BF16)
        state["m", n] = jnp.where(lo128, m_a, m_b)

    def attend(n):
        keys = slice(n * blk, (n + 2) * blk)
        v_cat = jnp.concatenate([v_lo()[keys], v_hi()[keys]], axis=0)
        rhs = jnp.concatenate([v_cat, ones_pad], axis=1)
        nd = jnp.dot(state.pop(("p", n)), rhs, preferred_element_type=F32)
        den = nd[:, LANES:] + jnp.exp2(state["sink_slab"] - state.pop(("m", n)))
        store(n, nd[:, :LANES] / den)

    stages = [prep]
    for n in range(nblk):
        stages += [lambda n=n: scores(n), lambda n=n: attend(n)]
    return stages


_CAST_SHAPES = ((D_MODEL, D_MODEL, 32), (D_MODEL, D_FF, 32), (D_MODEL, D_FF, 32), (D_FF, D_MODEL, 96))
N_CAST = len(_CAST_SHAPES)


def _cast_chunk(k, step):
    rows, _, per = _CAST_SHAPES[k]
    n = -(-rows // per)
    start = jnp.minimum(step * per, rows - per)
    return n, pl.multiple_of(start, 16)


def _cast_copies(k, step, src_refs, dst_refs, fbuf_refs, bbuf_refs, sem_in, sem_out):
    per = _CAST_SHAPES[k][2]
    _, start = _cast_chunk(k, step)
    rows = pl.ds(start, per)
    return (pltpu.make_async_copy(src_refs[k].at[rows, :], fbuf_refs[k], sem_in.at[k]),
            pltpu.make_async_copy(bbuf_refs[k], dst_refs[k].at[rows, :], sem_out.at[k]))


def _mix_kernel(x_ref, gain_ref, w_hbm, lbl_ref, og_ref, qg_ref, kg_ref, sink_ref,
                *refs):
    cast_src = refs[:N_CAST]
    rec_ref, att_ref = refs[N_CAST:N_CAST + 2]
    cast_dst = refs[N_CAST + 2:2 * N_CAST + 2]
    (u_ref, pa_ref, pb_ref, st_ref, cv_ref, kvp_ref, kpad_ref, vpad_ref, bias_ref,
     lev_ref) = refs[2 * N_CAST + 2:2 * N_CAST + 12]
    cast_f = refs[2 * N_CAST + 12:3 * N_CAST + 12]
    cast_b = refs[3 * N_CAST + 12:4 * N_CAST + 12]
    sem_in, sem_out, w_ref, w_stage, sem_w = refs[4 * N_CAST + 12:]
    g = pl.program_id(0)
    blk = ATTN_BLOCK

    def cast_copies(k, step):
        return _cast_copies(k, step, cast_src, cast_dst, cast_f, cast_b, sem_in, sem_out)

    cast_groups = {}
    for k in range(N_CAST):
        cast_groups.setdefault(_cast_chunk(k, g)[0], []).append(k)

    for n_chunks, members in cast_groups.items():
        @pl.when(g < n_chunks)
        def _(members=members):
            for k in members:
                cast_copies(k, g)[0].start()

    lo128 = lax.broadcasted_iota(jnp.int32, (1, LANES), 1) < ATTN_HD

    @pl.when(g == 0)
    def _():
        kvp_ref[...] = jnp.zeros_like(kvp_ref)
        lev_ref[...] = _level_mask()
        qi = lax.broadcasted_iota(jnp.int32, (blk, 2 * blk), 0)
        kj = lax.broadcasted_iota(jnp.int32, (blk, 2 * blk), 1)
        rel = qi + blk - kj
        in_window = (rel >= 0) & (rel < WINDOW)
        b_any = jnp.where(in_window, 0.0, NEG_INF)
        b_first = jnp.where(in_window & (kj >= blk), 0.0, NEG_INF)
        bias_ref[0] = jnp.concatenate([b_any, b_any], axis=1)
        bias_ref[1] = jnp.concatenate([b_first, b_first], axis=1)

    first_blk = lax.rem(g + SEQ_BLOCKS - 1, SEQ_BLOCKS) == 0

    @pl.when(first_blk)
    def _():
        st_ref[...] = jnp.zeros_like(st_ref)

    hr = lax.broadcasted_iota(jnp.int32, (LANES, LANES), 0) < ATTN_HD
    hc = lax.broadcasted_iota(jnp.int32, (LANES, LANES), 1) < ATTN_HD
    ones_bd = jnp.where(hr == hc, 1.0, 0.0).astype(BF16)
    pair = lambda r: jnp.concatenate([r, r], axis=1)
    qgain = pair(qg_ref[...]) * (ATTN_HD ** -0.5 * LOG2E)
    kgain = pair(kg_ref[...])
    ones_lo = jnp.broadcast_to(jnp.where(lo128, 1.0, 0.0), (2 * blk, LANES))
    ones_pad = jnp.concatenate([ones_lo, 1.0 - ones_lo], axis=0).astype(BF16)

    l0 = lbl_ref[0:1, :]
    l1 = lbl_ref[1:2, :]
    lmax = jnp.maximum(l0, l1)
    e0 = jnp.exp(l0 - lmax)
    e1 = jnp.exp(l1 - lmax)
    lb = e0 / (e0 + e1)
    c0 = 0.5 * (1.0 + lb)
    c1 = 0.5 * (1.0 - lb)
    og = og_ref[...]
    first_idx = first_blk.astype(jnp.int32)

    def step(pc_ref, pp_ref):
        _mix_step(pc_ref, pp_ref, x_ref, gain_ref, u_ref, w_ref, rec_ref, att_ref,
                  st_ref, cv_ref, kvp_ref, kpad_ref, vpad_ref, bias_ref, lev_ref, sink_ref,
                  (lo128, ones_bd, qgain, kgain, ones_pad, c0, c1, og, first_idx))

    @pl.when(g == 0)
    def _():
        _load_weights_and_project(pa_ref, x_ref, gain_ref, u_ref, w_hbm, w_ref, w_stage, sem_w)

    @pl.when((lax.rem(g, 2) == 0) & (g > 0))
    def _():
        step(pa_ref, pb_ref)

    @pl.when(lax.rem(g, 2) == 1)
    def _():
        step(pb_ref, pa_ref)

    for n_chunks, members in cast_groups.items():
        @pl.when((g >= 1) & (g <= n_chunks))
        def _(members=members):
            for k in members:
                cast_copies(k, g - 1)[1].wait()

        @pl.when(g < n_chunks)
        def _(members=members):
            for k in members:
                copy_in, copy_out = cast_copies(k, g)
                copy_in.wait()
                cast_b[k][...] = cast_f[k][...].astype(BF16)
                copy_out.start()


def _normalize(x_ref, gain_ref, u_ref):
    x = x_ref[...]
    ms = jnp.mean(x * x, axis=-1, keepdims=True)
    u_ref[...] = (x * lax.rsqrt(ms + EPS) * gain_ref[...]).astype(BF16)


def _project_slabs(pc_ref, u_ref, w_ref, first, count):
    res = jnp.dot(u_ref[...], w_ref[:, first * LANES:(first + count) * LANES],
                  preferred_element_type=F32)
    for t in range(count):
        pc_ref[first + t] = res[:, t * LANES:(t + 1) * LANES]


def _project_tail(pc_ref, u_ref, w_ref):
    _project_slabs(pc_ref, u_ref, w_ref, MIX_ITERS * LOOP_SLABS, TAIL_SLABS)


def _project_main(pc_ref, u_ref, w_ref, i):
    _project_slabs(pc_ref, u_ref, w_ref, i * LOOP_SLABS, LOOP_SLABS)


def _halved_part(col):
    part = col // HGRN_WIDTH
    return part in (0, 1, 3)


def _load_weights_and_project(pc_ref, x_ref, gain_ref, u_ref, w_hbm, w_ref, stage_ref, sem):
    n = PROJ_WIDTH // W_PIECE_COLS

    slots = stage_ref.shape[0]

    def copy(k):
        cols = pl.ds(k * W_PIECE_COLS, W_PIECE_COLS)
        return pltpu.make_async_copy(w_hbm.at[0, :, cols], stage_ref.at[k % slots],
                                     sem.at[k % slots])

    for k in range(min(slots - 1, n)):
        copy(k).start()
    _normalize(x_ref, gain_ref, u_ref)
    per_main = LOOP_SLABS * LANES // W_PIECE_COLS
    for k in range(n):
        if k + slots - 1 < n:
            copy(k + slots - 1).start()
        copy(k).wait()
        piece = stage_ref[k % slots]
        if _halved_part(k * W_PIECE_COLS):
            piece = piece * 0.5
        w_ref[:, k * W_PIECE_COLS:(k + 1) * W_PIECE_COLS] = piece.astype(BF16)
        if (k + 1) % per_main == 0 and (k + 1) // per_main <= MIX_ITERS:
            _project_main(pc_ref, u_ref, w_ref, (k + 1) // per_main - 1)
    _project_tail(pc_ref, u_ref, w_ref)


def _mix_step(pc_ref, pp_ref, x_ref, gain_ref, u_ref, w_ref, rec_ref, att_ref, st_ref,
              cv_ref, kvp_ref, kpad_ref, vpad_ref, bias_ref, lev_ref, sink_ref, consts):
    (lo128, ones_bd, qgain, kgain, ones_pad, c0, c1, og, first_idx) = consts
    blk = ATTN_BLOCK
    _normalize(x_ref, gain_ref, u_ref)

    k_all = jnp.concatenate([kvp_ref[0], pp_ref[K_SLAB]], axis=0)
    v_all = jnp.concatenate([kvp_ref[1], pp_ref[V_SLAB]], axis=0)
    kvp_ref[0] = pp_ref[K_SLAB, MIX_ROWS - blk:, :]
    kvp_ref[1] = pp_ref[V_SLAB, MIX_ROWS - blk:, :]
    kn = _pair_rms(k_all, ones_bd) * kgain
    kn_sw = pltpu.roll(kn, ATTN_HD, 1)
    v_sw = pltpu.roll(v_all, ATTN_HD, 1)
    zero = jnp.zeros_like(kn)
    kpad_ref[0, 0] = jnp.where(lo128, kn, zero).astype(BF16)
    kpad_ref[0, 1] = jnp.where(lo128, zero, kn_sw).astype(BF16)
    kpad_ref[1, 0] = jnp.where(lo128, kn_sw, zero).astype(BF16)
    kpad_ref[1, 1] = jnp.where(lo128, zero, kn).astype(BF16)
    vpad_ref[0, 0] = jnp.where(lo128, v_all, zero).astype(BF16)
    vpad_ref[0, 1] = jnp.where(lo128, zero, v_sw).astype(BF16)
    vpad_ref[1, 0] = jnp.where(lo128, v_sw, zero).astype(BF16)
    vpad_ref[1, 1] = jnp.where(lo128, zero, v_all).astype(BF16)

    def load(part, h, rows):
        return pp_ref[part * HGRN_HEADS + h, rows, :]

    def store(h, rows, tile):
        rec_ref[0, rows, h * HGRN_DV:(h + 1) * HGRN_DV] = tile.astype(rec_ref.dtype)

    def hgrn(i):
        r0s = [(i * HGRN_UNROLL + cc) * HGRN_CHUNK for cc in range(HGRN_UNROLL)]
        _hgrn_chunks(load, store, r0s, c0, c1, og, lev_ref[...], st_ref, cv_ref)

    def slab_stages(c):
        kvh = c // (ATTN_GROUP // 2)

        def store_att(n, tile):
            att_ref[0, c, n * blk:(n + 1) * blk, :] = tile.astype(att_ref.dtype)

        return _swa_slab(lambda: pp_ref[Q_SLAB + c],
                         lambda: kpad_ref[kvh, 0], lambda: kpad_ref[kvh, 1],
                         lambda: vpad_ref[kvh, 0], lambda: vpad_ref[kvh, 1], ones_pad,
                         lambda: bias_ref[first_idx], lambda: bias_ref[0],
                         sink_ref[2 * c] * LOG2E, sink_ref[2 * c + 1] * LOG2E,
                         qgain, ones_bd, lo128, store_att)

    def swa(i):
        slabs = [slab_stages(i * SLABS_PER_ITER + s) for s in range(SLABS_PER_ITER)]
        for stages in zip(*slabs):
            for stage in stages:
                stage()

    for i in range(MIX_ITERS):
        _project_main(pc_ref, u_ref, w_ref, i)
        hgrn(i)
        swa(i)
    _project_tail(pc_ref, u_ref, w_ref)


def _mix(x2, seq_len, gain, w_in, lb_logits, out_gain, q_gain, k_gain, sinks,
         ffn_weights):
    assert w_in.shape == (1, D_MODEL, PROJ_WIDTH) and w_in.dtype == F32
    t = x2.shape[0]
    nblk = t // MIX_ROWS
    assert seq_len // MIX_ROWS == SEQ_BLOCKS
    batch = t // seq_len
    assert tuple(w.shape for w in ffn_weights) == tuple(s[:2] for s in _CAST_SHAPES)
    assert all(-(-rows // per) <= nblk for rows, _, per in _CAST_SHAPES)
    hbm = pl.BlockSpec(memory_space=pl.ANY)

    def in_rows(gi):
        return (jnp.minimum(gi, nblk - 1), 0)

    def out_block(gi):
        blk_i = jnp.maximum(gi - 1, 0)
        return blk_i // SEQ_BLOCKS, lax.rem(blk_i, SEQ_BLOCKS)

    full = lambda shape: pl.BlockSpec(shape, lambda gi: (0,) * len(shape))
    return pl.pallas_call(
        _mix_kernel,
        grid=(nblk + 1,),
        in_specs=[
            pl.BlockSpec((MIX_ROWS, D_MODEL), in_rows),
            full((1, D_MODEL)),
            hbm,
            full((2, HGRN_WIDTH)),
            full((1, HGRN_DV)),
            full((1, ATTN_HD)),
            full((1, ATTN_HD)),
            pl.BlockSpec(memory_space=pltpu.SMEM),
        ] + [hbm] * N_CAST,
        out_specs=[
            pl.BlockSpec((1, MIX_ROWS, HGRN_WIDTH),
                         lambda gi: (*out_block(gi), 0)),
            pl.BlockSpec((1, ATT_SLABS, MIX_ROWS, LANES),
                         lambda gi: (out_block(gi)[0], 0, out_block(gi)[1], 0)),
        ] + [hbm] * N_CAST,
        out_shape=[
            jax.ShapeDtypeStruct((batch, seq_len, HGRN_WIDTH), BF16),
            jax.ShapeDtypeStruct((batch, ATT_SLABS, seq_len, LANES), BF16),
        ] + [jax.ShapeDtypeStruct(w.shape, BF16) for w in ffn_weights],
        scratch_shapes=[
            pltpu.VMEM((MIX_ROWS, D_MODEL), BF16),
            pltpu.VMEM((N_SLABS, MIX_ROWS, LANES), F32),
            pltpu.VMEM((N_SLABS, MIX_ROWS, LANES), F32),
            pltpu.VMEM((HGRN_HEADS, HGRN_DV, HGRN_DK), F32),
            pltpu.VMEM((HGRN_HEADS * HGRN_UNROLL, 3, HGRN_CHUNK, HGRN_DK), F32),
            pltpu.VMEM((2, ATTN_BLOCK, LANES), F32),
            pltpu.VMEM((ATTN_KV_HEADS, 2, ATTN_BLOCK + MIX_ROWS, LANES), BF16),
            pltpu.VMEM((ATTN_KV_HEADS, 2, ATTN_BLOCK + MIX_ROWS, LANES), BF16),
            pltpu.VMEM((2, ATTN_BLOCK, 4 * ATTN_BLOCK), F32),
            pltpu.VMEM((LEVEL_ROWS, LEVEL_ROWS), jnp.int32),
        ] + [pltpu.VMEM((per, cols), F32) for _, cols, per in _CAST_SHAPES]
        + [pltpu.VMEM((per, cols), BF16) for _, cols, per in _CAST_SHAPES]
        + [pltpu.SemaphoreType.DMA((N_CAST,)), pltpu.SemaphoreType.DMA((N_CAST,)),
           pltpu.VMEM((D_MODEL, PROJ_WIDTH), BF16),
           pltpu.VMEM((W_STAGE_SLOTS, D_MODEL, W_PIECE_COLS), F32),
           pltpu.SemaphoreType.DMA((W_STAGE_SLOTS,))],
        compiler_params=pltpu.CompilerParams(
            dimension_semantics=("arbitrary",), vmem_limit_bytes=VMEM_LIMIT_BYTES),
        name="mixer",
    )(x2, gain, w_in, lb_logits, out_gain, q_gain, k_gain, sinks, *ffn_weights)


def _ffn_kernel(x_ref, rec_ref, att_ref, wo_ref, gain_ref, wg_ref, wu_ref, wd_ref, o_ref):
    mixed = jnp.concatenate([rec_ref[0]] + [att_ref[0, c] for c in range(ATT_SLABS)],
                            axis=1)
    h = x_ref[...] + jnp.dot(mixed, wo_ref[...], preferred_element_type=F32)
    ms = jnp.mean(h * h, axis=-1, keepdims=True)
    u = (h * lax.rsqrt(ms + EPS) * gain_ref[...]).astype(BF16)
    gate = jnp.dot(u, wg_ref[...], preferred_element_type=F32)
    up = jnp.dot(u, wu_ref[...], preferred_element_type=F32)
    gs = 0.5 * gate
    act = ((gs * jnp.tanh(gs) + gs) * up).astype(BF16)
    o_ref[...] = h + jnp.dot(act, wd_ref[...], preferred_element_type=F32)


def _ffn(x2, rec, att, wo, gain, wg, wu, wd):
    t = x2.shape[0]
    seq_blocks = rec.shape[1] // FFN_ROWS

    def resident(shape):
        return pl.BlockSpec(shape, lambda i: (0, 0), pipeline_mode=pl.Buffered(1))

    return pl.pallas_call(
        _ffn_kernel,
        grid=(t // FFN_ROWS,),
        in_specs=[
            pl.BlockSpec((FFN_ROWS, D_MODEL), lambda i: (i, 0)),
            pl.BlockSpec((1, FFN_ROWS, HGRN_WIDTH),
                         lambda i: (i // seq_blocks, lax.rem(i, seq_blocks), 0)),
            pl.BlockSpec((1, ATT_SLABS, FFN_ROWS, LANES),
                         lambda i: (i // seq_blocks, 0, lax.rem(i, seq_blocks), 0)),
            resident((D_MODEL, D_MODEL)),
            resident((1, D_MODEL)),
            resident((D_MODEL, D_FF)),
            resident((D_MODEL, D_FF)),
            resident((D_FF, D_MODEL)),
        ],
        out_specs=pl.BlockSpec((FFN_ROWS, D_MODEL), lambda i: (i, 0)),
        out_shape=jax.ShapeDtypeStruct((t, D_MODEL), F32),
        compiler_params=pltpu.CompilerParams(
            dimension_semantics=("arbitrary",), vmem_limit_bytes=VMEM_LIMIT_BYTES),
        name="outproj_ffn",
    )(x2, rec, att, wo, gain, wg, wu, wd)


def kernel(x, norm1_gain, w_in, hgrn_lb_logits, hgrn_out_gain, q_norm_gain, k_norm_gain,
           attn_sinks, w_out, norm2_gain, w_ffn_gate, w_ffn_up, w_ffn_down):
    b, s, d = x.shape
    assert (d, w_in.shape[0]) == (D_MODEL, 1), "single-layer kernel"
    assert s == SEQ_BLOCKS * MIX_ROWS and MIX_ROWS == FFN_ROWS
    t = b * s
    x2 = x.reshape(t, d)

    rec, att, wo, wg, wu, wd = _mix(
        x2, s, norm1_gain[0][None, :], w_in, hgrn_lb_logits,
        hgrn_out_gain[0][None, :], q_norm_gain[0][None, :], k_norm_gain[0][None, :],
        attn_sinks[0], (w_out[0], w_ffn_gate[0], w_ffn_up[0], w_ffn_down[0]))
    out = _ffn(x2, rec, att, wo, norm2_gain[0][None, :], wg, wu, wd)
    return out.reshape(b, s, d)
```

```python
import jax
import jax.numpy as jnp
from jax import lax
from jax.experimental import pallas as pl
from jax.experimental.pallas import tpu as pltpu

D_MODEL = 1024
HGRN_HEADS = 4
HGRN_DK = 128
HGRN_DV = 128
HGRN_WIDTH = HGRN_HEADS * HGRN_DK
ATTN_HEADS = 8
ATTN_KV_HEADS = 2
ATTN_GROUP = ATTN_HEADS // ATTN_KV_HEADS
ATTN_HD = 64
ATTN_WIDTH = ATTN_HEADS * ATTN_HD
KV_WIDTH = ATTN_KV_HEADS * ATTN_HD
WINDOW = 128
PROJ_WIDTH = 4 * HGRN_WIDTH + ATTN_WIDTH + 2 * KV_WIDTH
D_FF = 2816
EPS = 1e-6
NEG_INF = -1e30
LOG2E = 1.4426950408889634

F32 = jnp.float32
BF16 = jnp.bfloat16

LANES = 128
VMEM_LIMIT_BYTES = 56 * 1024 * 1024

HGRN_CHUNK = 64
HGRN_LEVELS = 6
HGRN_UNROLL = 2
ATTN_BLOCK = 128
MIX_ROWS = 512
MIX_ITERS = 4
SEQ_BLOCKS = 4
FFN_ROWS = 512

N_SLABS = PROJ_WIDTH // LANES
ATT_SLABS = ATTN_WIDTH // LANES
SLABS_PER_ITER = ATT_SLABS // MIX_ITERS
Q_SLAB = 4 * HGRN_HEADS
K_SLAB = Q_SLAB + ATT_SLABS
V_SLAB = K_SLAB + 1
LOOP_SLABS = Q_SLAB // MIX_ITERS
TAIL_SLABS = N_SLABS - MIX_ITERS * LOOP_SLABS
assert MIX_ITERS * LOOP_SLABS == Q_SLAB and V_SLAB == N_SLABS - 1 and LOOP_SLABS % 2 == 0
W_PIECE_COLS = 2 * LANES
W_STAGE_SLOTS = 4
assert PROJ_WIDTH % W_PIECE_COLS == 0 and (LOOP_SLABS * LANES) % W_PIECE_COLS == 0
assert MIX_ROWS == MIX_ITERS * HGRN_UNROLL * HGRN_CHUNK
assert ATT_SLABS == MIX_ITERS * SLABS_PER_ITER

_NT = (((1,), (1,)), ((), ()))
_TN = (((0,), (0,)), ((), ()))


_LEVEL_Q = ((1, 3, 5, 7), (2, 3, 6, 7), (4, 5, 6, 7))
_LEVEL_K = ((0, 2, 4, 6), (0, 1, 4, 5), (0, 1, 2, 3))
LEVEL_ROWS = 8 * sum(len(q) for q in _LEVEL_Q)


def _level_mask():
    r = lax.broadcasted_iota(jnp.int32, (LEVEL_ROWS, LEVEL_ROWS), 0)
    c = lax.broadcasted_iota(jnp.int32, (LEVEL_ROWS, LEVEL_ROWS), 1)
    lev_r, lev_c = r >> 5, c >> 5
    j_r, j_c = (r >> 3) & 3, (c >> 3) & 3
    same_block = (((lev_r == 0) & (j_r == j_c))
                  | ((lev_r == 1) & ((j_r >> 1) == (j_c >> 1)))
                  | (lev_r == 2))
    return ((lev_r == lev_c) & same_block).astype(jnp.int32)


def _mul1(a, b):
    if a is None:
        return b
    if b is None:
        return a
    return a * b


def _pair_rms(x, ones_bd):
    ss = jnp.dot((x * x).astype(BF16), ones_bd, preferred_element_type=F32)
    return x * lax.rsqrt(ss * (1.0 / ATTN_HD) + EPS)


def _hgrn_chunks(load, store, r0s, c0, c1, og, mask_nat, st_ref, cv_ref):
    nh = HGRN_HEADS
    c_rows = HGRN_CHUNK
    nv = c_rows // 8
    assert nv == 8 and HGRN_LEVELS == 6

    def bc(t, r):
        return jnp.broadcast_to(t[r:r + 1, :], (8, HGRN_DK))

    def cat16(pieces):
        return jnp.concatenate(pieces, axis=0).astype(BF16)

    q_order = _LEVEL_Q[0] + _LEVEL_Q[1] + _LEVEL_Q[2]
    k_order = _LEVEL_K[0] + _LEVEL_K[1] + _LEVEL_K[2]

    def stacked_attend(q_stack, k_stack, v_stack, mask):
        s = [lax.dot_general(q_stack[u], k_stack[u], _NT, preferred_element_type=F32)
             for u in units]
        s = [jnp.where(mask, s[u], 0.0).astype(BF16) for u in units]
        o = [jnp.dot(s[u], v_stack[u], preferred_element_type=F32) for u in units]
        return [[o[u][8 * n:8 * n + 8, :] for n in range(len(q_order))] for u in units]

    def scatter_add(acc, pieces):
        for n, j in enumerate(q_order):
            acc[j] = pieces[n] if acc[j] is None else acc[j] + pieces[n]
        return acc

    m_nat = mask_nat != 0

    units = range(nh * len(r0s))
    hd = [u % nh for u in units]
    r0 = [r0s[u // nh] for u in units]
    rows = [pl.ds(r0[u], c_rows) for u in units]
    lanes = [slice(hd[u] * HGRN_DK, (hd[u] + 1) * HGRN_DK) for u in units]
    vrange = range(nv)

    tot8 = []
    for u in units:
        c0h, c1h = c0[:, lanes[u]], c1[:, lanes[u]]
        fp, kp, qp, vtr = [], [], [], []
        for b in vrange:
            srows = pl.ds(r0[u] + b, 8, stride=8)
            ct = c1h * jnp.tanh(load(1, hd[u], srows))
            fp.append(c0h + ct)
            kp.append(c1h - ct)
            xq = load(0, hd[u], srows)
            xs = xq * (HGRN_DK ** -0.5)
            qp.append(xs * jnp.tanh(xq) + xs)
            vtr.append(load(2, hd[u], srows))

        o_tr = [jnp.sum(qp[b] * kp[b], axis=-1, keepdims=True) * vtr[b] for b in vrange]

        p2 = [fp[b] * fp[b - 1] if b & 1 else fp[b] for b in vrange]
        x2 = [None if b & 1 else fp[b + 1] for b in vrange]
        p4 = [p2[b] * p2[(b & ~3) + 1] if b & 2 else p2[b] for b in vrange]
        x4 = [x2[b] if b & 2 else _mul1(x2[b], p2[(b & ~3) + 3]) for b in vrange]
        p8 = [p4[b] * p4[3] if b & 4 else p4[b] for b in vrange]
        x8 = [x4[b] if b & 4 else _mul1(x4[b], p4[7]) for b in vrange]
        tot8.append(p8[7])

        for b in vrange:
            srows = pl.ds(b, 8, stride=8)
            cv_ref[u, 0, srows, :] = qp[b] * p8[b]
            cv_ref[u, 1, srows, :] = _mul1(kp[b], x8[b])

        q_pieces = ([qp[b] * fp[b] for b in _LEVEL_Q[0]] + [qp[b] * p2[b] for b in _LEVEL_Q[1]]
                    + [qp[b] * p4[b] for b in _LEVEL_Q[2]])
        k_pieces = ([kp[b] for b in _LEVEL_K[0]] + [_mul1(kp[b], x2[b]) for b in _LEVEL_K[1]]
                    + [_mul1(kp[b], x4[b]) for b in _LEVEL_K[2]])
        for lvl in range(3):
            group = 1 << lvl
            for jq, b_q in enumerate(_LEVEL_Q[lvl]):
                for jk, b_k in enumerate(_LEVEL_K[lvl]):
                    if jq // group == jk // group:
                        d = jnp.sum(q_pieces[4 * lvl + jq] * k_pieces[4 * lvl + jk],
                                    axis=-1, keepdims=True)
                        o_tr[b_q] = o_tr[b_q] + d * vtr[b_k]
        for b in vrange:
            cv_ref[u, 2, pl.ds(b, 8, stride=8), :] = o_tr[b]

    q_nat, k_nat, v_nat, qs, kd, vb, dec = [], [], [], [], [], [], []
    for u in units:
        an = [cv_ref[u, 0, 8 * a:8 * a + 8, :] for a in vrange]
        cn = [cv_ref[u, 1, 8 * a:8 * a + 8, :] for a in vrange]
        vnat = [load(2, hd[u], pl.ds(r0[u] + 8 * a, 8)) for a in vrange]
        q4 = [an[a] for a in _LEVEL_Q[0]]
        k4 = [cn[a] for a in _LEVEL_K[0]]
        an = [an[a] * bc(tot8[u], a - 1) if a & 1 else an[a] for a in vrange]
        cn = [cn[a] if a & 1 else cn[a] * bc(tot8[u], a + 1) for a in vrange]
        t16 = tot8[u] * pltpu.roll(tot8[u], 1, 0)
        q5 = [an[a] for a in _LEVEL_Q[1]]
        k5 = [cn[a] for a in _LEVEL_K[1]]
        an = [an[a] * bc(t16, (a & ~3) + 1) if a & 2 else an[a] for a in vrange]
        cn = [cn[a] if a & 2 else cn[a] * bc(t16, (a & ~3) + 3) for a in vrange]
        t32 = t16 * pltpu.roll(t16, 2, 0)
        q6 = [an[a] for a in _LEVEL_Q[2]]
        k6 = [cn[a] for a in _LEVEL_K[2]]
        an = [an[a] * bc(t32, 3) if a & 4 else an[a] for a in vrange]
        cn = [cn[a] if a & 4 else cn[a] * bc(t32, 7) for a in vrange]
        q_nat.append(cat16(q4 + q5 + q6))
        k_nat.append(cat16(k4 + k5 + k6))
        v_nat.append(cat16([vnat[j] for j in k_order]))
        qs.append(cat16(an))
        kd.append(cat16(cn))
        vb.append(cat16(vnat))
        dec.append(t32[3:4, :] * t32[7:8, :])

    att_nat = stacked_attend(q_nat, k_nat, v_nat, m_nat)
    o_nat = []
    for u in units:
        pieces = scatter_add([None] * nv, att_nat[u])
        pieces[0] = jnp.zeros((8, HGRN_DV), F32)
        o_nat.append(jnp.concatenate(pieces, axis=0))

    upd = [lax.dot_general(vb[u], kd[u], _TN, preferred_element_type=F32)
           for u in units]
    for h in range(nh):
        st = st_ref[h]
        for u in range(h, len(units), nh):
            o_nat[u] = o_nat[u] + lax.dot_general(qs[u], st.astype(BF16), _NT,
                                                  preferred_element_type=F32)
            st = st * dec[u] + upd[u]
        st_ref[h] = st

    for u in units:
        o_h = o_nat[u] + cv_ref[u, 2]
        ms = jnp.mean(o_h * o_h, axis=-1, keepdims=True)
        xgs = load(3, hd[u], rows[u])
        gate = xgs * jnp.tanh(xgs) + xgs
        store(hd[u], rows[u], o_h * lax.rsqrt(ms + EPS) * og * gate)


def _swa_slab(q_slab, k_lo, k_hi, v_lo, v_hi, ones_pad, bias_first, bias_any,
              sink_a, sink_b, qgain, ones_bd, lo128, store):
    blk = ATTN_BLOCK
    nblk = MIX_ROWS // blk
    state = {}

    def prep():
        state["qs"] = (_pair_rms(q_slab(), ones_bd) * qgain).astype(BF16)
        state["sink_slab"] = jnp.where(lo128, sink_a, sink_b)

    def scores(n):
        keys = slice(n * blk, (n + 2) * blk)
        k_cat = jnp.concatenate([k_lo()[keys], k_hi()[keys]], axis=0)
        s = lax.dot_general(state["qs"][n * blk:(n + 1) * blk], k_cat, _NT,
                            preferred_element_type=F32)
        s = s + (bias_first() if n == 0 else bias_any())
        m_a = jnp.max(s[:, :2 * blk], axis=-1, keepdims=True)
        m_b = jnp.max(s[:, 2 * blk:], axis=-1, keepdims=True)
        m_a = jnp.maximum(jnp.broadcast_to(m_a, (blk, LANES)), sink_a)
        m_b = jnp.maximum(jnp.broadcast_to(m_b, (blk, LANES)), sink_b)
        cols = [s[:, c * LANES:(c + 1) * LANES] - (m_a if c < 2 * blk // LANES else m_b)
                for c in range(4 * blk // LANES)]
        state["p", n] = jnp.exp2(jnp.concatenate(cols, axis=1)).astype(BF16)
        state["m", n] = jnp.where(lo128, m_a, m_b)

    def attend(n):
        keys = slice(n * blk, (n + 2) * blk)
        v_cat = jnp.concatenate([v_lo()[keys], v_hi()[keys]], axis=0)
        rhs = jnp.concatenate([v_cat, ones_pad], axis=1)
        nd = jnp.dot(state.pop(("p", n)), rhs, preferred_element_type=F32)
        den = nd[:, LANES:] + jnp.exp2(state["sink_slab"] - state.pop(("m", n)))
        store(n, nd[:, :LANES] / den)

    stages = [prep, lambda: scores(0)]
    for n in range(nblk):
        if n + 1 < nblk:
            stages.append(lambda n=n: scores(n + 1))
        stages.append(lambda n=n: attend(n))
    return stages


_CAST_SHAPES = ((D_MODEL, D_MODEL, 32), (D_MODEL, D_FF, 32), (D_MODEL, D_FF, 32), (D_FF, D_MODEL, 96))
N_CAST = len(_CAST_SHAPES)


def _cast_chunk(k, step):
    rows, _, per = _CAST_SHAPES[k]
    n = -(-rows // per)
    start = jnp.minimum(step * per, rows - per)
    return n, pl.multiple_of(start, 16)


def _cast_copies(k, step, src_refs, dst_refs, fbuf_refs, bbuf_refs, sem_in, sem_out):
    per = _CAST_SHAPES[k][2]
    _, start = _cast_chunk(k, step)
    rows = pl.ds(start, per)
    return (pltpu.make_async_copy(src_refs[k].at[rows, :], fbuf_refs[k], sem_in.at[k]),
            pltpu.make_async_copy(bbuf_refs[k], dst_refs[k].at[rows, :], sem_out.at[k]))


def _mix_kernel(x_ref, gain_ref, w_hbm, lbl_ref, og_ref, qg_ref, kg_ref, sink_ref,
                *refs):
    cast_src = refs[:N_CAST]
    rec_ref, att_ref = refs[N_CAST:N_CAST + 2]
    cast_dst = refs[N_CAST + 2:2 * N_CAST + 2]
    (u_ref, pa_ref, pb_ref, st_ref, cv_ref, kvp_ref, kpad_ref, vpad_ref, bias_ref,
     lev_ref) = refs[2 * N_CAST + 2:2 * N_CAST + 12]
    cast_f = refs[2 * N_CAST + 12:3 * N_CAST + 12]
    cast_b = refs[3 * N_CAST + 12:4 * N_CAST + 12]
    sem_in, sem_out, w_ref, w_stage, sem_w = refs[4 * N_CAST + 12:]
    g = pl.program_id(0)
    blk = ATTN_BLOCK

    def cast_copies(k, step):
        return _cast_copies(k, step, cast_src, cast_dst, cast_f, cast_b, sem_in, sem_out)

    cast_groups = {}
    for k in range(N_CAST):
        cast_groups.setdefault(_cast_chunk(k, g)[0], []).append(k)

    for n_chunks, members in cast_groups.items():
        @pl.when(g < n_chunks)
        def _(members=members):
            for k in members:
                cast_copies(k, g)[0].start()

    lo128 = lax.broadcasted_iota(jnp.int32, (1, LANES), 1) < ATTN_HD

    @pl.when(g == 0)
    def _():
        kvp_ref[...] = jnp.zeros_like(kvp_ref)
        lev_ref[...] = _level_mask()
        qi = lax.broadcasted_iota(jnp.int32, (blk, 2 * blk), 0)
        kj = lax.broadcasted_iota(jnp.int32, (blk, 2 * blk), 1)
        rel = qi + blk - kj
        in_window = (rel >= 0) & (rel < WINDOW)
        b_any = jnp.where(in_window, 0.0, NEG_INF)
        b_first = jnp.where(in_window & (kj >= blk), 0.0, NEG_INF)
        bias_ref[0] = jnp.concatenate([b_any, b_any], axis=1)
        bias_ref[1] = jnp.concatenate([b_first, b_first], axis=1)

    first_blk = lax.rem(g + SEQ_BLOCKS - 1, SEQ_BLOCKS) == 0

    @pl.when(first_blk)
    def _():
        st_ref[...] = jnp.zeros_like(st_ref)

    hr = lax.broadcasted_iota(jnp.int32, (LANES, LANES), 0) < ATTN_HD
    hc = lax.broadcasted_iota(jnp.int32, (LANES, LANES), 1) < ATTN_HD
    ones_bd = jnp.where(hr == hc, 1.0, 0.0).astype(BF16)
    pair = lambda r: jnp.concatenate([r, r], axis=1)
    qgain = pair(qg_ref[...]) * (ATTN_HD ** -0.5 * LOG2E)
    kgain = pair(kg_ref[...])
    ones_lo = jnp.broadcast_to(jnp.where(lo128, 1.0, 0.0), (2 * blk, LANES))
    ones_pad = jnp.concatenate([ones_lo, 1.0 - ones_lo], axis=0).astype(BF16)

    l0 = lbl_ref[0:1, :]
    l1 = lbl_ref[1:2, :]
    lmax = jnp.maximum(l0, l1)
    e0 = jnp.exp(l0 - lmax)
    e1 = jnp.exp(l1 - lmax)
    lb = e0 / (e0 + e1)
    c0 = 0.5 * (1.0 + lb)
    c1 = 0.5 * (1.0 - lb)
    og = og_ref[...]
    first_idx = first_blk.astype(jnp.int32)

    def step(pc_ref, pp_ref):
        _mix_step(pc_ref, pp_ref, x_ref, gain_ref, u_ref, w_ref, rec_ref, att_ref,
                  st_ref, cv_ref, kvp_ref, kpad_ref, vpad_ref, bias_ref, lev_ref, sink_ref,
                  (lo128, ones_bd, qgain, kgain, ones_pad, c0, c1, og, first_idx))

    @pl.when(g == 0)
    def _():
        _load_weights_and_project(pa_ref, x_ref, gain_ref, u_ref, w_hbm, w_ref, w_stage, sem_w)

    @pl.when((lax.rem(g, 2) == 0) & (g > 0))
    def _():
        step(pa_ref, pb_ref)

    @pl.when(lax.rem(g, 2) == 1)
    def _():
        step(pb_ref, pa_ref)

    for n_chunks, members in cast_groups.items():
        @pl.when((g >= 1) & (g <= n_chunks))
        def _(members=members):
            for k in members:
                cast_copies(k, g - 1)[1].wait()

        @pl.when(g < n_chunks)
        def _(members=members):
            for k in members:
                copy_in, copy_out = cast_copies(k, g)
                copy_in.wait()
                cast_b[k][...] = cast_f[k][...].astype(BF16)
                copy_out.start()


def _normalize(x_ref, gain_ref, u_ref):
    x = x_ref[...]
    ms = jnp.mean(x * x, axis=-1, keepdims=True)
    u_ref[...] = (x * lax.rsqrt(ms + EPS) * gain_ref[...]).astype(BF16)


def _project_slabs(pc_ref, u_ref, w_ref, first, count):
    res = jnp.dot(u_ref[...], w_ref[:, first * LANES:(first + count) * LANES],
                  preferred_element_type=F32)
    for t in range(count):
        pc_ref[first + t] = res[:, t * LANES:(t + 1) * LANES]


def _project_tail(pc_ref, u_ref, w_ref):
    _project_slabs(pc_ref, u_ref, w_ref, MIX_ITERS * LOOP_SLABS, TAIL_SLABS)


def _project_main(pc_ref, u_ref, w_ref, i):
    _project_slabs(pc_ref, u_ref, w_ref, i * LOOP_SLABS, LOOP_SLABS)


def _halved_part(col):
    part = col // HGRN_WIDTH
    return part in (0, 1, 3)


def _load_weights_and_project(pc_ref, x_ref, gain_ref, u_ref, w_hbm, w_ref, stage_ref, sem):
    n = PROJ_WIDTH // W_PIECE_COLS

    slots = stage_ref.shape[0]

    def copy(k):
        cols = pl.ds(k * W_PIECE_COLS, W_PIECE_COLS)
        return pltpu.make_async_copy(w_hbm.at[0, :, cols], stage_ref.at[k % slots],
                                     sem.at[k % slots])

    for k in range(min(slots - 1, n)):
        copy(k).start()
    _normalize(x_ref, gain_ref, u_ref)
    per_main = LOOP_SLABS * LANES // W_PIECE_COLS
    for k in range(n):
        if k + slots - 1 < n:
            copy(k + slots - 1).start()
        copy(k).wait()
        piece = stage_ref[k % slots]
        if _halved_part(k * W_PIECE_COLS):
            piece = piece * 0.5
        w_ref[:, k * W_PIECE_COLS:(k + 1) * W_PIECE_COLS] = piece.astype(BF16)
        if (k + 1) % per_main == 0 and (k + 1) // per_main <= MIX_ITERS:
            _project_main(pc_ref, u_ref, w_ref, (k + 1) // per_main - 1)
    _project_tail(pc_ref, u_ref, w_ref)


def _mix_step(pc_ref, pp_ref, x_ref, gain_ref, u_ref, w_ref, rec_ref, att_ref, st_ref,
              cv_ref, kvp_ref, kpad_ref, vpad_ref, bias_ref, lev_ref, sink_ref, consts):
    (lo128, ones_bd, qgain, kgain, ones_pad, c0, c1, og, first_idx) = consts
    blk = ATTN_BLOCK
    _normalize(x_ref, gain_ref, u_ref)

    k_all = jnp.concatenate([kvp_ref[0], pp_ref[K_SLAB]], axis=0)
    v_all = jnp.concatenate([kvp_ref[1], pp_ref[V_SLAB]], axis=0)
    kvp_ref[0] = pp_ref[K_SLAB, MIX_ROWS - blk:, :]
    kvp_ref[1] = pp_ref[V_SLAB, MIX_ROWS - blk:, :]
    kn = _pair_rms(k_all, ones_bd) * kgain
    kn_sw = pltpu.roll(kn, ATTN_HD, 1)
    v_sw = pltpu.roll(v_all, ATTN_HD, 1)
    zero = jnp.zeros_like(kn)
    kpad_ref[0, 0] = jnp.where(lo128, kn, zero).astype(BF16)
    kpad_ref[0, 1] = jnp.where(lo128, zero, kn_sw).astype(BF16)
    kpad_ref[1, 0] = jnp.where(lo128, kn_sw, zero).astype(BF16)
    kpad_ref[1, 1] = jnp.where(lo128, zero, kn).astype(BF16)
    vpad_ref[0, 0] = jnp.where(lo128, v_all, zero).astype(BF16)
    vpad_ref[0, 1] = jnp.where(lo128, zero, v_sw).astype(BF16)
    vpad_ref[1, 0] = jnp.where(lo128, v_sw, zero).astype(BF16)
    vpad_ref[1, 1] = jnp.where(lo128, zero, v_all).astype(BF16)

    def load(part, h, rows):
        return pp_ref[part * HGRN_HEADS + h, rows, :]

    def store(h, rows, tile):
        rec_ref[0, rows, h * HGRN_DV:(h + 1) * HGRN_DV] = tile.astype(rec_ref.dtype)

    def hgrn(i):
        r0s = [(i * HGRN_UNROLL + cc) * HGRN_CHUNK for cc in range(HGRN_UNROLL)]
        _hgrn_chunks(load, store, r0s, c0, c1, og, lev_ref[...], st_ref, cv_ref)

    def slab_stages(c):
        kvh = c // (ATTN_GROUP // 2)

        def store_att(n, tile):
            att_ref[0, c, n * blk:(n + 1) * blk, :] = tile.astype(att_ref.dtype)

        return _swa_slab(lambda: pp_ref[Q_SLAB + c],
                         lambda: kpad_ref[kvh, 0], lambda: kpad_ref[kvh, 1],
                         lambda: vpad_ref[kvh, 0], lambda: vpad_ref[kvh, 1], ones_pad,
                         lambda: bias_ref[first_idx], lambda: bias_ref[0],
                         sink_ref[2 * c] * LOG2E, sink_ref[2 * c + 1] * LOG2E,
                         qgain, ones_bd, lo128, store_att)

    def swa(i):
        slabs = [slab_stages(i * SLABS_PER_ITER + s) for s in range(SLABS_PER_ITER)]
        for stages in zip(*slabs):
            for stage in stages:
                stage()

    for i in range(MIX_ITERS):
        _project_main(pc_ref, u_ref, w_ref, i)
        hgrn(i)
        swa(i)
    _project_tail(pc_ref, u_ref, w_ref)


def _mix(x2, seq_len, gain, w_in, lb_logits, out_gain, q_gain, k_gain, sinks,
         ffn_weights):
    assert w_in.shape == (1, D_MODEL, PROJ_WIDTH) and w_in.dtype == F32
    t = x2.shape[0]
    nblk = t // MIX_ROWS
    assert seq_len // MIX_ROWS == SEQ_BLOCKS
    batch = t // seq_len
    assert tuple(w.shape for w in ffn_weights) == tuple(s[:2] for s in _CAST_SHAPES)
    assert all(-(-rows // per) <= nblk for rows, _, per in _CAST_SHAPES)
    hbm = pl.BlockSpec(memory_space=pl.ANY)

    def in_rows(gi):
        return (jnp.minimum(gi, nblk - 1), 0)

    def out_block(gi):
        blk_i = jnp.maximum(gi - 1, 0)
        return blk_i // SEQ_BLOCKS, lax.rem(blk_i, SEQ_BLOCKS)

    full = lambda shape: pl.BlockSpec(shape, lambda gi: (0,) * len(shape))
    return pl.pallas_call(
        _mix_kernel,
        grid=(nblk + 1,),
        in_specs=[
            pl.BlockSpec((MIX_ROWS, D_MODEL), in_rows),
            full((1, D_MODEL)),
            hbm,
            full((2, HGRN_WIDTH)),
            full((1, HGRN_DV)),
            full((1, ATTN_HD)),
            full((1, ATTN_HD)),
            pl.BlockSpec(memory_space=pltpu.SMEM),
        ] + [hbm] * N_CAST,
        out_specs=[
            pl.BlockSpec((1, MIX_ROWS, HGRN_WIDTH),
                         lambda gi: (*out_block(gi), 0)),
            pl.BlockSpec((1, ATT_SLABS, MIX_ROWS, LANES),
                         lambda gi: (out_block(gi)[0], 0, out_block(gi)[1], 0)),
        ] + [hbm] * N_CAST,
        out_shape=[
            jax.ShapeDtypeStruct((batch, seq_len, HGRN_WIDTH), BF16),
            jax.ShapeDtypeStruct((batch, ATT_SLABS, seq_len, LANES), BF16),
        ] + [jax.ShapeDtypeStruct(w.shape, BF16) for w in ffn_weights],
        scratch_shapes=[
            pltpu.VMEM((MIX_ROWS, D_MODEL), BF16),
            pltpu.VMEM((N_SLABS, MIX_ROWS, LANES), F32),
            pltpu.VMEM((N_SLABS, MIX_ROWS, LANES), F32),
            pltpu.VMEM((HGRN_HEADS, HGRN_DV, HGRN_DK), F32),
            pltpu.VMEM((HGRN_HEADS * HGRN_UNROLL, 3, HGRN_CHUNK, HGRN_DK), F32),
            pltpu.VMEM((2, ATTN_BLOCK, LANES), F32),
            pltpu.VMEM((ATTN_KV_HEADS, 2, ATTN_BLOCK + MIX_ROWS, LANES), BF16),
            pltpu.VMEM((ATTN_KV_HEADS, 2, ATTN_BLOCK + MIX_ROWS, LANES), BF16),
            pltpu.VMEM((2, ATTN_BLOCK, 4 * ATTN_BLOCK), F32),
            pltpu.VMEM((LEVEL_ROWS, LEVEL_ROWS), jnp.int32),
        ] + [pltpu.VMEM((per, cols), F32) for _, cols, per in _CAST_SHAPES]
        + [pltpu.VMEM((per, cols), BF16) for _, cols, per in _CAST_SHAPES]
        + [pltpu.SemaphoreType.DMA((N_CAST,)), pltpu.SemaphoreType.DMA((N_CAST,)),
           pltpu.VMEM((D_MODEL, PROJ_WIDTH), BF16),
           pltpu.VMEM((W_STAGE_SLOTS, D_MODEL, W_PIECE_COLS), F32),
           pltpu.SemaphoreType.DMA((W_STAGE_SLOTS,))],
        compiler_params=pltpu.CompilerParams(
            dimension_semantics=("arbitrary",), vmem_limit_bytes=VMEM_LIMIT_BYTES),
        name="mixer",
    )(x2, gain, w_in, lb_logits, out_gain, q_gain, k_gain, sinks, *ffn_weights)


def _ffn_kernel(x_ref, rec_ref, att_ref, wo_ref, gain_ref, wg_ref, wu_ref, wd_ref, o_ref):
    mixed = jnp.concatenate([rec_ref[0]] + [att_ref[0, c] for c in range(ATT_SLABS)],
                            axis=1)
    h = x_ref[...] + jnp.dot(mixed, wo_ref[...], preferred_element_type=F32)
    ms = jnp.mean(h * h, axis=-1, keepdims=True)
    u = (h * lax.rsqrt(ms + EPS) * gain_ref[...]).astype(BF16)
    gate = jnp.dot(u, wg_ref[...], preferred_element_type=F32)
    up = jnp.dot(u, wu_ref[...], preferred_element_type=F32)
    gs = 0.5 * gate
    act = ((gs * jnp.tanh(gs) + gs) * up).astype(BF16)
    o_ref[...] = h + jnp.dot(act, wd_ref[...], preferred_element_type=F32)


def _ffn(x2, rec, att, wo, gain, wg, wu, wd):
    t = x2.shape[0]
    seq_blocks = rec.shape[1] // FFN_ROWS

    def resident(shape):
        return pl.BlockSpec(shape, lambda i: (0, 0), pipeline_mode=pl.Buffered(1))

    return pl.pallas_call(
        _ffn_kernel,
        grid=(t // FFN_ROWS,),
        in_specs=[
            pl.BlockSpec((FFN_ROWS, D_MODEL), lambda i: (i, 0)),
            pl.BlockSpec((1, FFN_ROWS, HGRN_WIDTH),
                         lambda i: (i // seq_blocks, lax.rem(i, seq_blocks), 0)),
            pl.BlockSpec((1, ATT_SLABS, FFN_ROWS, LANES),
                         lambda i: (i // seq_blocks, 0, lax.rem(i, seq_blocks), 0)),
            resident((D_MODEL, D_MODEL)),
            resident((1, D_MODEL)),
            resident((D_MODEL, D_FF)),
            resident((D_MODEL, D_FF)),
            resident((D_FF, D_MODEL)),
        ],
        out_specs=pl.BlockSpec((FFN_ROWS, D_MODEL), lambda i: (i, 0)),
        out_shape=jax.ShapeDtypeStruct((t, D_MODEL), F32),
        compiler_params=pltpu.CompilerParams(
            dimension_semantics=("arbitrary",), vmem_limit_bytes=VMEM_LIMIT_BYTES),
        name="outproj_ffn",
    )(x2, rec, att, wo, gain, wg, wu, wd)


def kernel(x, norm1_gain, w_in, hgrn_lb_logits, hgrn_out_gain, q_norm_gain, k_norm_gain,
           attn_sinks, w_out, norm2_gain, w_ffn_gate, w_ffn_up, w_ffn_down):
    b, s, d = x.shape
    assert (d, w_in.shape[0]) == (D_MODEL, 1), "single-layer kernel"
    assert s == SEQ_BLOCKS * MIX_ROWS and MIX_ROWS == FFN_ROWS
    t = b * s
    x2 = x.reshape(t, d)

    rec, att, wo, wg, wu, wd = _mix(
        x2, s, norm1_gain[0][None, :], w_in, hgrn_lb_logits,
        hgrn_out_gain[0][None, :], q_norm_gain[0][None, :], k_norm_gain[0][None, :],
        attn_sinks[0], (w_out[0], w_ffn_gate[0], w_ffn_up[0], w_ffn_down[0]))
    out = _ffn(x2, rec, att, wo, norm2_gain[0][None, :], wg, wu, wd)
    return out.reshape(b, s, d)
```

```python
import jax
import jax.numpy as jnp
from jax import lax
from jax.experimental import pallas as pl
from jax.experimental.pallas import tpu as pltpu

D_MODEL = 1024
HGRN_HEADS = 4
HGRN_DK = 128
HGRN_DV = 128
HGRN_WIDTH = HGRN_HEADS * HGRN_DK
ATTN_HEADS = 8
ATTN_KV_HEADS = 2
ATTN_GROUP = ATTN_HEADS // ATTN_KV_HEADS
ATTN_HD = 64
ATTN_WIDTH = ATTN_HEADS * ATTN_HD
KV_WIDTH = ATTN_KV_HEADS * ATTN_HD
WINDOW = 128
PROJ_WIDTH = 4 * HGRN_WIDTH + ATTN_WIDTH + 2 * KV_WIDTH
D_FF = 2816
EPS = 1e-6
NEG_INF = -1e30
LOG2E = 1.4426950408889634

F32 = jnp.float32
BF16 = jnp.bfloat16

LANES = 128
VMEM_LIMIT_BYTES = 56 * 1024 * 1024

HGRN_CHUNK = 64
HGRN_LEVELS = 6
HGRN_UNROLL = 2
ATTN_BLOCK = 128
MIX_ROWS = 512
FFN_CHUNK = 1024
MIX_ITERS = 4
SEQ_BLOCKS = 4
FFN_ROWS = 512

N_SLABS = PROJ_WIDTH // LANES
ATT_SLABS = ATTN_WIDTH // LANES
SLABS_PER_ITER = ATT_SLABS // MIX_ITERS
Q_SLAB = 4 * HGRN_HEADS
K_SLAB = Q_SLAB + ATT_SLABS
V_SLAB = K_SLAB + 1
LOOP_SLABS = Q_SLAB // MIX_ITERS
TAIL_SLABS = N_SLABS - MIX_ITERS * LOOP_SLABS
assert MIX_ITERS * LOOP_SLABS == Q_SLAB and V_SLAB == N_SLABS - 1 and LOOP_SLABS % 2 == 0
W_PIECE_COLS = 2 * LANES
W_STAGE_SLOTS = 4
assert PROJ_WIDTH % W_PIECE_COLS == 0 and (LOOP_SLABS * LANES) % W_PIECE_COLS == 0
assert MIX_ROWS == MIX_ITERS * HGRN_UNROLL * HGRN_CHUNK
assert ATT_SLABS == MIX_ITERS * SLABS_PER_ITER

_NT = (((1,), (1,)), ((), ()))
_TN = (((0,), (0,)), ((), ()))


_LEVEL_Q = ((1, 3, 5, 7), (2, 3, 6, 7), (4, 5, 6, 7))
_LEVEL_K = ((0, 2, 4, 6), (0, 1, 4, 5), (0, 1, 2, 3))
LEVEL_ROWS = 8 * sum(len(q) for q in _LEVEL_Q)


def _level_mask():
    r = lax.broadcasted_iota(jnp.int32, (LEVEL_ROWS, LEVEL_ROWS), 0)
    c = lax.broadcasted_iota(jnp.int32, (LEVEL_ROWS, LEVEL_ROWS), 1)
    lev_r, lev_c = r >> 5, c >> 5
    j_r, j_c = (r >> 3) & 3, (c >> 3) & 3
    same_block = (((lev_r == 0) & (j_r == j_c))
                  | ((lev_r == 1) & ((j_r >> 1) == (j_c >> 1)))
                  | (lev_r == 2))
    return ((lev_r == lev_c) & same_block).astype(jnp.int32)


def _mul1(a, b):
    if a is None:
        return b
    if b is None:
        return a
    return a * b


def _pair_rms(x, ones_bd):
    ss = jnp.dot((x * x).astype(BF16), ones_bd, preferred_element_type=F32)
    return x * lax.rsqrt(ss * (1.0 / ATTN_HD) + EPS)


def _hgrn_chunks(load, store, r0s, c0, c1, og, mask_nat, st_ref, cv_ref):
    nh = HGRN_HEADS
    c_rows = HGRN_CHUNK
    nv = c_rows // 8
    assert nv == 8 and HGRN_LEVELS == 6

    def bc(t, r):
        return jnp.broadcast_to(t[r:r + 1, :], (8, HGRN_DK))

    def cat16(pieces):
        return jnp.concatenate(pieces, axis=0).astype(BF16)

    q_order = _LEVEL_Q[0] + _LEVEL_Q[1] + _LEVEL_Q[2]
    k_order = _LEVEL_K[0] + _LEVEL_K[1] + _LEVEL_K[2]

    def stacked_attend(q_stack, k_stack, v_stack, mask):
        s = [lax.dot_general(q_stack[u], k_stack[u], _NT, preferred_element_type=F32)
             for u in units]
        s = [jnp.where(mask, s[u], 0.0).astype(BF16) for u in units]
        o = [jnp.dot(s[u], v_stack[u], preferred_element_type=F32) for u in units]
        return [[o[u][8 * n:8 * n + 8, :] for n in range(len(q_order))] for u in units]

    def scatter_add(acc, pieces):
        for n, j in enumerate(q_order):
            acc[j] = pieces[n] if acc[j] is None else acc[j] + pieces[n]
        return acc

    m_nat = mask_nat != 0

    units = range(nh * len(r0s))
    hd = [u % nh for u in units]
    r0 = [r0s[u // nh] for u in units]
    rows = [pl.ds(r0[u], c_rows) for u in units]
    lanes = [slice(hd[u] * HGRN_DK, (hd[u] + 1) * HGRN_DK) for u in units]
    vrange = range(nv)

    tot8 = []
    for u in units:
        c0h, c1h = c0[:, lanes[u]], c1[:, lanes[u]]
        fp, kp, qp, vtr = [], [], [], []
        for b in vrange:
            srows = pl.ds(r0[u] + b, 8, stride=8)
            ct = c1h * jnp.tanh(load(1, hd[u], srows))
            fp.append(c0h + ct)
            kp.append(c1h - ct)
            xq = load(0, hd[u], srows)
            xs = xq * (HGRN_DK ** -0.5)
            qp.append(xs * jnp.tanh(xq) + xs)
            vtr.append(load(2, hd[u], srows))

        o_tr = [jnp.sum(qp[b] * kp[b], axis=-1, keepdims=True) * vtr[b] for b in vrange]

        p2 = [fp[b] * fp[b - 1] if b & 1 else fp[b] for b in vrange]
        x2 = [None if b & 1 else fp[b + 1] for b in vrange]
        p4 = [p2[b] * p2[(b & ~3) + 1] if b & 2 else p2[b] for b in vrange]
        x4 = [x2[b] if b & 2 else _mul1(x2[b], p2[(b & ~3) + 3]) for b in vrange]
        p8 = [p4[b] * p4[3] if b & 4 else p4[b] for b in vrange]
        x8 = [x4[b] if b & 4 else _mul1(x4[b], p4[7]) for b in vrange]
        tot8.append(p8[7])

        for b in vrange:
            srows = pl.ds(b, 8, stride=8)
            cv_ref[u, 0, srows, :] = qp[b] * p8[b]
            cv_ref[u, 1, srows, :] = _mul1(kp[b], x8[b])

        q_pieces = ([qp[b] * fp[b] for b in _LEVEL_Q[0]] + [qp[b] * p2[b] for b in _LEVEL_Q[1]]
                    + [qp[b] * p4[b] for b in _LEVEL_Q[2]])
        k_pieces = ([kp[b] for b in _LEVEL_K[0]] + [_mul1(kp[b], x2[b]) for b in _LEVEL_K[1]]
                    + [_mul1(kp[b], x4[b]) for b in _LEVEL_K[2]])
        for lvl in range(3):
            group = 1 << lvl
            for jq, b_q in enumerate(_LEVEL_Q[lvl]):
                for jk, b_k in enumerate(_LEVEL_K[lvl]):
                    if jq // group == jk // group:
                        d = jnp.sum(q_pieces[4 * lvl + jq] * k_pieces[4 * lvl + jk],
                                    axis=-1, keepdims=True)
                        o_tr[b_q] = o_tr[b_q] + d * vtr[b_k]
        for b in vrange:
            cv_ref[u, 2, pl.ds(b, 8, stride=8), :] = o_tr[b]

    q_nat, k_nat, v_nat, qs, kd, vb, dec = [], [], [], [], [], [], []
    for u in units:
        an = [cv_ref[u, 0, 8 * a:8 * a + 8, :] for a in vrange]
        cn = [cv_ref[u, 1, 8 * a:8 * a + 8, :] for a in vrange]
        vnat = [load(2, hd[u], pl.ds(r0[u] + 8 * a, 8)) for a in vrange]
        q4 = [an[a] for a in _LEVEL_Q[0]]
        k4 = [cn[a] for a in _LEVEL_K[0]]
        an = [an[a] * bc(tot8[u], a - 1) if a & 1 else an[a] for a in vrange]
        cn = [cn[a] if a & 1 else cn[a] * bc(tot8[u], a + 1) for a in vrange]
        t16 = tot8[u] * pltpu.roll(tot8[u], 1, 0)
        q5 = [an[a] for a in _LEVEL_Q[1]]
        k5 = [cn[a] for a in _LEVEL_K[1]]
        an = [an[a] * bc(t16, (a & ~3) + 1) if a & 2 else an[a] for a in vrange]
        cn = [cn[a] if a & 2 else cn[a] * bc(t16, (a & ~3) + 3) for a in vrange]
        t32 = t16 * pltpu.roll(t16, 2, 0)
        q6 = [an[a] for a in _LEVEL_Q[2]]
        k6 = [cn[a] for a in _LEVEL_K[2]]
        an = [an[a] * bc(t32, 3) if a & 4 else an[a] for a in vrange]
        cn = [cn[a] if a & 4 else cn[a] * bc(t32, 7) for a in vrange]
        q_nat.append(cat16(q4 + q5 + q6))
        k_nat.append(cat16(k4 + k5 + k6))
        v_nat.append(cat16([vnat[j] for j in k_order]))
        qs.append(cat16(an))
        kd.append(cat16(cn))
        vb.append(cat16(vnat))
        dec.append(t32[3:4, :] * t32[7:8, :])

    att_nat = stacked_attend(q_nat, k_nat, v_nat, m_nat)
    o_nat = []
    for u in units:
        pieces = scatter_add([None] * nv, att_nat[u])
        pieces[0] = jnp.zeros((8, HGRN_DV), F32)
        o_nat.append(jnp.concatenate(pieces, axis=0))

    upd = [lax.dot_general(vb[u], kd[u], _TN, preferred_element_type=F32)
           for u in units]
    for h in range(nh):
        st = st_ref[h]
        for u in range(h, len(units), nh):
            o_nat[u] = o_nat[u] + lax.dot_general(qs[u], st.astype(BF16), _NT,
                                                  preferred_element_type=F32)
            st = st * dec[u] + upd[u]
        st_ref[h] = st

    for u in units:
        o_h = o_nat[u] + cv_ref[u, 2]
        ms = jnp.mean(o_h * o_h, axis=-1, keepdims=True)
        xgs = load(3, hd[u], rows[u])
        gate = xgs * jnp.tanh(xgs) + xgs
        store(hd[u], rows[u], o_h * lax.rsqrt(ms + EPS) * og * gate)


def _swa_slab(q_slab, k_lo, k_hi, v_lo, v_hi, ones_pad, bias_first, bias_any,
              sink_a, sink_b, qgain, ones_bd, lo128, store):
    blk = ATTN_BLOCK
    nblk = MIX_ROWS // blk
    state = {}

    def prep():
        state["qs"] = (_pair_rms(q_slab(), ones_bd) * qgain).astype(BF16)
        state["sink_slab"] = jnp.where(lo128, sink_a, sink_b)

    def scores(n):
        keys = slice(n * blk, (n + 2) * blk)
        k_cat = jnp.concatenate([k_lo()[keys], k_hi()[keys]], axis=0)
        s = lax.dot_general(state["qs"][n * blk:(n + 1) * blk], k_cat, _NT,
                            preferred_element_type=F32)
        s = s + (bias_first() if n == 0 else bias_any())
        m_a = jnp.max(s[:, :2 * blk], axis=-1, keepdims=True)
        m_b = jnp.max(s[:, 2 * blk:], axis=-1, keepdims=True)
        m_a = jnp.maximum(jnp.broadcast_to(m_a, (blk, LANES)), sink_a)
        m_b = jnp.maximum(jnp.broadcast_to(m_b, (blk, LANES)), sink_b)
        cols = [s[:, c * LANES:(c + 1) * LANES] - (m_a if c < 2 * blk // LANES else m_b)
                for c in range(4 * blk // LANES)]
        state["p", n] = jnp.exp2(jnp.concatenate(cols, axis=1)).astype(BF16)
        state["m", n] = jnp.where(lo128, m_a, m_b)

    def attend(n):
        keys = slice(n * blk, (n + 2) * blk)
        v_cat = jnp.concatenate([v_lo()[keys], v_hi()[keys]], axis=0)
        rhs = jnp.concatenate([v_cat, ones_pad], axis=1)
        nd = jnp.dot(state.pop(("p", n)), rhs, preferred_element_type=F32)
        den = nd[:, LANES:] + jnp.exp2(state["sink_slab"] - state.pop(("m", n)))
        store(n, nd[:, :LANES] / den)

    stages = [prep]
    for n in range(nblk):
        stages += [lambda n=n: scores(n), lambda n=n: attend(n)]
    return stages


_CAST_SHAPES = ((D_MODEL, D_MODEL, 32), (D_MODEL, D_FF, 32), (D_MODEL, D_FF, 32), (D_FF, D_MODEL, 96))
N_CAST = len(_CAST_SHAPES)


def _cast_chunk(k, step):
    rows, _, per = _CAST_SHAPES[k]
    n = -(-rows // per)
    start = jnp.minimum(step * per, rows - per)
    return n, pl.multiple_of(start, 16)


def _cast_copies(k, step, src_refs, dst_refs, fbuf_refs, bbuf_refs, sem_in, sem_out):
    per = _CAST_SHAPES[k][2]
    _, start = _cast_chunk(k, step)
    rows = pl.ds(start, per)
    return (pltpu.make_async_copy(src_refs[k].at[rows, :], fbuf_refs[k], sem_in.at[k]),
            pltpu.make_async_copy(bbuf_refs[k], dst_refs[k].at[rows, :], sem_out.at[k]))


def _mix_kernel(x_ref, gain_ref, w_hbm, lbl_ref, og_ref, qg_ref, kg_ref, sink_ref,
                *refs):
    cast_src = refs[:N_CAST]
    rec_ref, att_ref = refs[N_CAST:N_CAST + 2]
    cast_dst = refs[N_CAST + 2:2 * N_CAST + 2]
    (u_ref, pa_ref, pb_ref, st_ref, cv_ref, kvp_ref, kpad_ref, vpad_ref, bias_ref,
     lev_ref) = refs[2 * N_CAST + 2:2 * N_CAST + 12]
    cast_f = refs[2 * N_CAST + 12:3 * N_CAST + 12]
    cast_b = refs[3 * N_CAST + 12:4 * N_CAST + 12]
    sem_in, sem_out, w_ref, w_stage, sem_w = refs[4 * N_CAST + 12:]
    g = pl.program_id(0)
    blk = ATTN_BLOCK

    def cast_copies(k, step):
        return _cast_copies(k, step, cast_src, cast_dst, cast_f, cast_b, sem_in, sem_out)

    cast_groups = {}
    for k in range(N_CAST):
        cast_groups.setdefault(_cast_chunk(k, g)[0], []).append(k)

    for n_chunks, members in cast_groups.items():
        @pl.when(g < n_chunks)
        def _(members=members):
            for k in members:
                cast_copies(k, g)[0].start()

    lo128 = lax.broadcasted_iota(jnp.int32, (1, LANES), 1) < ATTN_HD

    @pl.when(g == 0)
    def _():
        kvp_ref[...] = jnp.zeros_like(kvp_ref)
        lev_ref[...] = _level_mask()
        qi = lax.broadcasted_iota(jnp.int32, (blk, 2 * blk), 0)
        kj = lax.broadcasted_iota(jnp.int32, (blk, 2 * blk), 1)
        rel = qi + blk - kj
        in_window = (rel >= 0) & (rel < WINDOW)
        b_any = jnp.where(in_window, 0.0, NEG_INF)
        b_first = jnp.where(in_window & (kj >= blk), 0.0, NEG_INF)
        bias_ref[0] = jnp.concatenate([b_any, b_any], axis=1)
        bias_ref[1] = jnp.concatenate([b_first, b_first], axis=1)

    first_blk = lax.rem(g + SEQ_BLOCKS - 1, SEQ_BLOCKS) == 0

    @pl.when(first_blk)
    def _():
        st_ref[...] = jnp.zeros_like(st_ref)

    hr = lax.broadcasted_iota(jnp.int32, (LANES, LANES), 0) < ATTN_HD
    hc = lax.broadcasted_iota(jnp.int32, (LANES, LANES), 1) < ATTN_HD
    ones_bd = jnp.where(hr == hc, 1.0, 0.0).astype(BF16)
    pair = lambda r: jnp.concatenate([r, r], axis=1)
    qgain = pair(qg_ref[...]) * (ATTN_HD ** -0.5 * LOG2E)
    kgain = pair(kg_ref[...])
    ones_lo = jnp.broadcast_to(jnp.where(lo128, 1.0, 0.0), (2 * blk, LANES))
    ones_pad = jnp.concatenate([ones_lo, 1.0 - ones_lo], axis=0).astype(BF16)

    l0 = lbl_ref[0:1, :]
    l1 = lbl_ref[1:2, :]
    lmax = jnp.maximum(l0, l1)
    e0 = jnp.exp(l0 - lmax)
    e1 = jnp.exp(l1 - lmax)
    lb = e0 / (e0 + e1)
    c0 = 0.5 * (1.0 + lb)
    c1 = 0.5 * (1.0 - lb)
    og = og_ref[...]
    first_idx = first_blk.astype(jnp.int32)

    def step(pc_ref, pp_ref):
        _mix_step(pc_ref, pp_ref, x_ref, gain_ref, u_ref, w_ref, rec_ref, att_ref,
                  st_ref, cv_ref, kvp_ref, kpad_ref, vpad_ref, bias_ref, lev_ref, sink_ref,
                  (lo128, ones_bd, qgain, kgain, ones_pad, c0, c1, og, first_idx))

    @pl.when(g == 0)
    def _():
        _load_weights_and_project(pa_ref, x_ref, gain_ref, u_ref, w_hbm, w_ref, w_stage, sem_w)

    @pl.when((lax.rem(g, 2) == 0) & (g > 0))
    def _():
        step(pa_ref, pb_ref)

    @pl.when(lax.rem(g, 2) == 1)
    def _():
        step(pb_ref, pa_ref)

    for n_chunks, members in cast_groups.items():
        @pl.when((g >= 1) & (g <= n_chunks))
        def _(members=members):
            for k in members:
                cast_copies(k, g - 1)[1].wait()

        @pl.when(g < n_chunks)
        def _(members=members):
            for k in members:
                copy_in, copy_out = cast_copies(k, g)
                copy_in.wait()
                cast_b[k][...] = cast_f[k][...].astype(BF16)
                copy_out.start()


def _normalize(x_ref, gain_ref, u_ref):
    x = x_ref[...]
    ms = jnp.mean(x * x, axis=-1, keepdims=True)
    u_ref[...] = (x * lax.rsqrt(ms + EPS) * gain_ref[...]).astype(BF16)


def _project_slabs(pc_ref, u_ref, w_ref, first, count):
    res = jnp.dot(u_ref[...], w_ref[:, first * LANES:(first + count) * LANES],
                  preferred_element_type=F32)
    for t in range(count):
        pc_ref[first + t] = res[:, t * LANES:(t + 1) * LANES]


def _project_tail(pc_ref, u_ref, w_ref):
    _project_slabs(pc_ref, u_ref, w_ref, MIX_ITERS * LOOP_SLABS, TAIL_SLABS)


def _project_main(pc_ref, u_ref, w_ref, i):
    _project_slabs(pc_ref, u_ref, w_ref, i * LOOP_SLABS, LOOP_SLABS)


def _halved_part(col):
    part = col // HGRN_WIDTH
    return part in (0, 1, 3)


def _load_weights_and_project(pc_ref, x_ref, gain_ref, u_ref, w_hbm, w_ref, stage_ref, sem):
    n = PROJ_WIDTH // W_PIECE_COLS

    slots = stage_ref.shape[0]

    def copy(k):
        cols = pl.ds(k * W_PIECE_COLS, W_PIECE_COLS)
        return pltpu.make_async_copy(w_hbm.at[0, :, cols], stage_ref.at[k % slots],
                                     sem.at[k % slots])

    for k in range(min(slots - 1, n)):
        copy(k).start()
    _normalize(x_ref, gain_ref, u_ref)
    per_main = LOOP_SLABS * LANES // W_PIECE_COLS
    for k in range(n):
        if k + slots - 1 < n:
            copy(k + slots - 1).start()
        copy(k).wait()
        piece = stage_ref[k % slots]
        if _halved_part(k * W_PIECE_COLS):
            piece = piece * 0.5
        w_ref[:, k * W_PIECE_COLS:(k + 1) * W_PIECE_COLS] = piece.astype(BF16)
        if (k + 1) % per_main == 0 and (k + 1) // per_main <= MIX_ITERS:
            _project_main(pc_ref, u_ref, w_ref, (k + 1) // per_main - 1)
    _project_tail(pc_ref, u_ref, w_ref)


def _mix_step(pc_ref, pp_ref, x_ref, gain_ref, u_ref, w_ref, rec_ref, att_ref, st_ref,
              cv_ref, kvp_ref, kpad_ref, vpad_ref, bias_ref, lev_ref, sink_ref, consts):
    (lo128, ones_bd, qgain, kgain, ones_pad, c0, c1, og, first_idx) = consts
    blk = ATTN_BLOCK
    _normalize(x_ref, gain_ref, u_ref)

    k_all = jnp.concatenate([kvp_ref[0], pp_ref[K_SLAB]], axis=0)
    v_all = jnp.concatenate([kvp_ref[1], pp_ref[V_SLAB]], axis=0)
    kvp_ref[0] = pp_ref[K_SLAB, MIX_ROWS - blk:, :]
    kvp_ref[1] = pp_ref[V_SLAB, MIX_ROWS - blk:, :]
    kn = _pair_rms(k_all, ones_bd) * kgain
    kn_sw = pltpu.roll(kn, ATTN_HD, 1)
    v_sw = pltpu.roll(v_all, ATTN_HD, 1)
    zero = jnp.zeros_like(kn)
    kpad_ref[0, 0] = jnp.where(lo128, kn, zero).astype(BF16)
    kpad_ref[0, 1] = jnp.where(lo128, zero, kn_sw).astype(BF16)
    kpad_ref[1, 0] = jnp.where(lo128, kn_sw, zero).astype(BF16)
    kpad_ref[1, 1] = jnp.where(lo128, zero, kn).astype(BF16)
    vpad_ref[0, 0] = jnp.where(lo128, v_all, zero).astype(BF16)
    vpad_ref[0, 1] = jnp.where(lo128, zero, v_sw).astype(BF16)
    vpad_ref[1, 0] = jnp.where(lo128, v_sw, zero).astype(BF16)
    vpad_ref[1, 1] = jnp.where(lo128, zero, v_all).astype(BF16)

    def load(part, h, rows):
        return pp_ref[part * HGRN_HEADS + h, rows, :]

    def store(h, rows, tile):
        rec_ref[0, rows, h * HGRN_DV:(h + 1) * HGRN_DV] = tile.astype(rec_ref.dtype)

    def hgrn(i):
        r0s = [(i * HGRN_UNROLL + cc) * HGRN_CHUNK for cc in range(HGRN_UNROLL)]
        _hgrn_chunks(load, store, r0s, c0, c1, og, lev_ref[...], st_ref, cv_ref)

    def slab_stages(c):
        kvh = c // (ATTN_GROUP // 2)

        def store_att(n, tile):
            att_ref[0, c, n * blk:(n + 1) * blk, :] = tile.astype(att_ref.dtype)

        return _swa_slab(lambda: pp_ref[Q_SLAB + c],
                         lambda: kpad_ref[kvh, 0], lambda: kpad_ref[kvh, 1],
                         lambda: vpad_ref[kvh, 0], lambda: vpad_ref[kvh, 1], ones_pad,
                         lambda: bias_ref[first_idx], lambda: bias_ref[0],
                         sink_ref[2 * c] * LOG2E, sink_ref[2 * c + 1] * LOG2E,
                         qgain, ones_bd, lo128, store_att)

    def swa(i):
        slabs = [slab_stages(i * SLABS_PER_ITER + s) for s in range(SLABS_PER_ITER)]
        for stages in zip(*slabs):
            for stage in stages:
                stage()

    for i in range(MIX_ITERS):
        _project_main(pc_ref, u_ref, w_ref, i)
        hgrn(i)
        swa(i)
    _project_tail(pc_ref, u_ref, w_ref)


def _mix(x2, seq_len, gain, w_in, lb_logits, out_gain, q_gain, k_gain, sinks,
         ffn_weights):
    assert w_in.shape == (1, D_MODEL, PROJ_WIDTH) and w_in.dtype == F32
    t = x2.shape[0]
    nblk = t // MIX_ROWS
    assert seq_len // MIX_ROWS == SEQ_BLOCKS
    batch = t // seq_len
    assert tuple(w.shape for w in ffn_weights) == tuple(s[:2] for s in _CAST_SHAPES)
    assert all(-(-rows // per) <= nblk for rows, _, per in _CAST_SHAPES)
    hbm = pl.BlockSpec(memory_space=pl.ANY)

    def in_rows(gi):
        return (jnp.minimum(gi, nblk - 1), 0)

    def out_block(gi):
        blk_i = jnp.maximum(gi - 1, 0)
        return blk_i // SEQ_BLOCKS, lax.rem(blk_i, SEQ_BLOCKS)

    full = lambda shape: pl.BlockSpec(shape, lambda gi: (0,) * len(shape))
    return pl.pallas_call(
        _mix_kernel,
        grid=(nblk + 1,),
        in_specs=[
            pl.BlockSpec((MIX_ROWS, D_MODEL), in_rows),
            full((1, D_MODEL)),
            hbm,
            full((2, HGRN_WIDTH)),
            full((1, HGRN_DV)),
            full((1, ATTN_HD)),
            full((1, ATTN_HD)),
            pl.BlockSpec(memory_space=pltpu.SMEM),
        ] + [hbm] * N_CAST,
        out_specs=[
            pl.BlockSpec((1, MIX_ROWS, HGRN_WIDTH),
                         lambda gi: (*out_block(gi), 0)),
            pl.BlockSpec((1, ATT_SLABS, MIX_ROWS, LANES),
                         lambda gi: (out_block(gi)[0], 0, out_block(gi)[1], 0)),
        ] + [hbm] * N_CAST,
        out_shape=[
            jax.ShapeDtypeStruct((batch, seq_len, HGRN_WIDTH), BF16),
            jax.ShapeDtypeStruct((batch, ATT_SLABS, seq_len, LANES), BF16),
        ] + [jax.ShapeDtypeStruct(w.shape, BF16) for w in ffn_weights],
        scratch_shapes=[
            pltpu.VMEM((MIX_ROWS, D_MODEL), BF16),
            pltpu.VMEM((N_SLABS, MIX_ROWS, LANES), F32),
            pltpu.VMEM((N_SLABS, MIX_ROWS, LANES), F32),
            pltpu.VMEM((HGRN_HEADS, HGRN_DV, HGRN_DK), F32),
            pltpu.VMEM((HGRN_HEADS * HGRN_UNROLL, 3, HGRN_CHUNK, HGRN_DK), F32),
            pltpu.VMEM((2, ATTN_BLOCK, LANES), F32),
            pltpu.VMEM((ATTN_KV_HEADS, 2, ATTN_BLOCK + MIX_ROWS, LANES), BF16),
            pltpu.VMEM((ATTN_KV_HEADS, 2, ATTN_BLOCK + MIX_ROWS, LANES), BF16),
            pltpu.VMEM((2, ATTN_BLOCK, 4 * ATTN_BLOCK), F32),
            pltpu.VMEM((LEVEL_ROWS, LEVEL_ROWS), jnp.int32),
        ] + [pltpu.VMEM((per, cols), F32) for _, cols, per in _CAST_SHAPES]
        + [pltpu.VMEM((per, cols), BF16) for _, cols, per in _CAST_SHAPES]
        + [pltpu.SemaphoreType.DMA((N_CAST,)), pltpu.SemaphoreType.DMA((N_CAST,)),
           pltpu.VMEM((D_MODEL, PROJ_WIDTH), BF16),
           pltpu.VMEM((W_STAGE_SLOTS, D_MODEL, W_PIECE_COLS), F32),
           pltpu.SemaphoreType.DMA((W_STAGE_SLOTS,))],
        compiler_params=pltpu.CompilerParams(
            dimension_semantics=("arbitrary",), vmem_limit_bytes=VMEM_LIMIT_BYTES),
        name="mixer",
    )(x2, gain, w_in, lb_logits, out_gain, q_gain, k_gain, sinks, *ffn_weights)


def _ffn_kernel(x_ref, rec_ref, att_ref, wo_ref, gain_ref, wg_ref, wu_ref, wd_ref, o_ref):
    mixed = jnp.concatenate([rec_ref[0]] + [att_ref[0, c] for c in range(ATT_SLABS)],
                            axis=1)
    h = x_ref[...] + jnp.dot(mixed, wo_ref[...], preferred_element_type=F32)
    ms = jnp.mean(h * h, axis=-1, keepdims=True)
    u = (h * lax.rsqrt(ms + EPS) * gain_ref[...]).astype(BF16)
    out = h
    for c0 in range(0, D_FF, FFN_CHUNK):
        cols = slice(c0, min(c0 + FFN_CHUNK, D_FF))
        gate = jnp.dot(u, wg_ref[:, cols], preferred_element_type=F32)
        up = jnp.dot(u, wu_ref[:, cols], preferred_element_type=F32)
        gs = 0.5 * gate
        act = ((gs * jnp.tanh(gs) + gs) * up).astype(BF16)
        out = out + jnp.dot(act, wd_ref[cols, :], preferred_element_type=F32)
    o_ref[...] = out


def _ffn(x2, rec, att, wo, gain, wg, wu, wd):
    t = x2.shape[0]
    seq_blocks = rec.shape[1] // FFN_ROWS

    def resident(shape):
        return pl.BlockSpec(shape, lambda i: (0, 0), pipeline_mode=pl.Buffered(1))

    return pl.pallas_call(
        _ffn_kernel,
        grid=(t // FFN_ROWS,),
        in_specs=[
            pl.BlockSpec((FFN_ROWS, D_MODEL), lambda i: (i, 0)),
            pl.BlockSpec((1, FFN_ROWS, HGRN_WIDTH),
                         lambda i: (i // seq_blocks, lax.rem(i, seq_blocks), 0)),
            pl.BlockSpec((1, ATT_SLABS, FFN_ROWS, LANES),
                         lambda i: (i // seq_blocks, 0, lax.rem(i, seq_blocks), 0)),
            resident((D_MODEL, D_MODEL)),
            resident((1, D_MODEL)),
            resident((D_MODEL, D_FF)),
            resident((D_MODEL, D_FF)),
            resident((D_FF, D_MODEL)),
        ],
        out_specs=pl.BlockSpec((FFN_ROWS, D_MODEL), lambda i: (i, 0)),
        out_shape=jax.ShapeDtypeStruct((t, D_MODEL), F32),
        compiler_params=pltpu.CompilerParams(
            dimension_semantics=("arbitrary",), vmem_limit_bytes=VMEM_LIMIT_BYTES),
        name="outproj_ffn",
    )(x2, rec, att, wo, gain, wg, wu, wd)


def kernel(x, norm1_gain, w_in, hgrn_lb_logits, hgrn_out_gain, q_norm_gain, k_norm_gain,
           attn_sinks, w_out, norm2_gain, w_ffn_gate, w_ffn_up, w_ffn_down):
    b, s, d = x.shape
    assert (d, w_in.shape[0]) == (D_MODEL, 1), "single-layer kernel"
    assert s == SEQ_BLOCKS * MIX_ROWS and MIX_ROWS == FFN_ROWS
    t = b * s
    x2 = x.reshape(t, d)

    rec, att, wo, wg, wu, wd = _mix(
        x2, s, norm1_gain[0][None, :], w_in, hgrn_lb_logits,
        hgrn_out_gain[0][None, :], q_norm_gain[0][None, :], k_norm_gain[0][None, :],
        attn_sinks[0], (w_out[0], w_ffn_gate[0], w_ffn_up[0], w_ffn_down[0]))
    out = _ffn(x2, rec, att, wo, norm2_gain[0][None, :], wg, wu, wd)
    return out.reshape(b, s, d)
```

```python
import jax
import jax.numpy as jnp
from jax import lax
from jax.experimental import pallas as pl
from jax.experimental.pallas import tpu as pltpu

D_MODEL = 1024
HGRN_HEADS = 4
HGRN_DK = 128
HGRN_DV = 128
HGRN_WIDTH = HGRN_HEADS * HGRN_DK
ATTN_HEADS = 8
ATTN_KV_HEADS = 2
ATTN_GROUP = ATTN_HEADS // ATTN_KV_HEADS
ATTN_HD = 64
ATTN_WIDTH = ATTN_HEADS * ATTN_HD
KV_WIDTH = ATTN_KV_HEADS * ATTN_HD
WINDOW = 128
PROJ_WIDTH = 4 * HGRN_WIDTH + ATTN_WIDTH + 2 * KV_WIDTH
D_FF = 2816
EPS = 1e-6
NEG_INF = -1e30
LOG2E = 1.4426950408889634

F32 = jnp.float32
BF16 = jnp.bfloat16

LANES = 128
VMEM_LIMIT_BYTES = 56 * 1024 * 1024

HGRN_CHUNK = 64
HGRN_LEVELS = 6
HGRN_UNROLL = 2
ATTN_BLOCK = 128
MIX_ROWS = 512
CV_PITCH = 12
MIX_ITERS = 4
SEQ_BLOCKS = 4
FFN_ROWS = 512

N_SLABS = PROJ_WIDTH // LANES
ATT_SLABS = ATTN_WIDTH // LANES
SLABS_PER_ITER = ATT_SLABS // MIX_ITERS
Q_SLAB = 4 * HGRN_HEADS
K_SLAB = Q_SLAB + ATT_SLABS
V_SLAB = K_SLAB + 1
LOOP_SLABS = Q_SLAB // MIX_ITERS
TAIL_SLABS = N_SLABS - MIX_ITERS * LOOP_SLABS
assert MIX_ITERS * LOOP_SLABS == Q_SLAB and V_SLAB == N_SLABS - 1 and LOOP_SLABS % 2 == 0
W_PIECE_COLS = 2 * LANES
W_STAGE_SLOTS = 4
assert PROJ_WIDTH % W_PIECE_COLS == 0 and (LOOP_SLABS * LANES) % W_PIECE_COLS == 0
assert MIX_ROWS == MIX_ITERS * HGRN_UNROLL * HGRN_CHUNK
assert ATT_SLABS == MIX_ITERS * SLABS_PER_ITER

_NT = (((1,), (1,)), ((), ()))
_TN = (((0,), (0,)), ((), ()))


_LEVEL_Q = ((1, 3, 5, 7), (2, 3, 6, 7), (4, 5, 6, 7))
_LEVEL_K = ((0, 2, 4, 6), (0, 1, 4, 5), (0, 1, 2, 3))
LEVEL_ROWS = 8 * sum(len(q) for q in _LEVEL_Q)


def _level_mask():
    r = lax.broadcasted_iota(jnp.int32, (LEVEL_ROWS, LEVEL_ROWS), 0)
    c = lax.broadcasted_iota(jnp.int32, (LEVEL_ROWS, LEVEL_ROWS), 1)
    lev_r, lev_c = r >> 5, c >> 5
    j_r, j_c = (r >> 3) & 3, (c >> 3) & 3
    same_block = (((lev_r == 0) & (j_r == j_c))
                  | ((lev_r == 1) & ((j_r >> 1) == (j_c >> 1)))
                  | (lev_r == 2))
    return ((lev_r == lev_c) & same_block).astype(jnp.int32)


def _mul1(a, b):
    if a is None:
        return b
    if b is None:
        return a
    return a * b


def _pair_rms(x, ones_bd):
    ss = jnp.dot((x * x).astype(BF16), ones_bd, preferred_element_type=F32)
    return x * lax.rsqrt(ss * (1.0 / ATTN_HD) + EPS)


def _hgrn_chunks(load, store, r0s, c0, c1, og, mask_nat, st_ref, cv_ref):
    nh = HGRN_HEADS
    c_rows = HGRN_CHUNK
    nv = c_rows // 8
    assert nv == 8 and HGRN_LEVELS == 6

    def bc(t, r):
        return jnp.broadcast_to(t[r:r + 1, :], (8, HGRN_DK))

    def cat16(pieces):
        return jnp.concatenate(pieces, axis=0).astype(BF16)

    q_order = _LEVEL_Q[0] + _LEVEL_Q[1] + _LEVEL_Q[2]
    k_order = _LEVEL_K[0] + _LEVEL_K[1] + _LEVEL_K[2]

    def stacked_attend(q_stack, k_stack, v_stack, mask):
        s = [lax.dot_general(q_stack[u], k_stack[u], _NT, preferred_element_type=F32)
             for u in units]
        s = [jnp.where(mask, s[u], 0.0).astype(BF16) for u in units]
        o = [jnp.dot(s[u], v_stack[u], preferred_element_type=F32) for u in units]
        return [[o[u][8 * n:8 * n + 8, :] for n in range(len(q_order))] for u in units]

    def scatter_add(acc, pieces):
        for n, j in enumerate(q_order):
            acc[j] = pieces[n] if acc[j] is None else acc[j] + pieces[n]
        return acc

    m_nat = mask_nat != 0

    units = range(nh * len(r0s))
    hd = [u % nh for u in units]
    r0 = [r0s[u // nh] for u in units]
    rows = [pl.ds(r0[u], c_rows) for u in units]
    lanes = [slice(hd[u] * HGRN_DK, (hd[u] + 1) * HGRN_DK) for u in units]
    vrange = range(nv)

    tot8 = []
    for u in units:
        c0h, c1h = c0[:, lanes[u]], c1[:, lanes[u]]
        fp, kp, qp, vtr = [], [], [], []
        for b in vrange:
            srows = pl.ds(r0[u] + b, 8, stride=8)
            ct = c1h * jnp.tanh(load(1, hd[u], srows))
            fp.append(c0h + ct)
            kp.append(c1h - ct)
            xq = load(0, hd[u], srows)
            xs = xq * (HGRN_DK ** -0.5)
            qp.append(xs * jnp.tanh(xq) + xs)
            vtr.append(load(2, hd[u], srows))

        o_tr = [jnp.sum(qp[b] * kp[b], axis=-1, keepdims=True) * vtr[b] for b in vrange]

        p2 = [fp[b] * fp[b - 1] if b & 1 else fp[b] for b in vrange]
        x2 = [None if b & 1 else fp[b + 1] for b in vrange]
        p4 = [p2[b] * p2[(b & ~3) + 1] if b & 2 else p2[b] for b in vrange]
        x4 = [x2[b] if b & 2 else _mul1(x2[b], p2[(b & ~3) + 3]) for b in vrange]
        p8 = [p4[b] * p4[3] if b & 4 else p4[b] for b in vrange]
        x8 = [x4[b] if b & 4 else _mul1(x4[b], p4[7]) for b in vrange]
        tot8.append(p8[7])

        for b in vrange:
            srows = pl.ds(b, 8, stride=CV_PITCH)
            cv_ref[u, 0, srows, :] = qp[b] * p8[b]
            cv_ref[u, 1, srows, :] = _mul1(kp[b], x8[b])

        q_pieces = ([qp[b] * fp[b] for b in _LEVEL_Q[0]] + [qp[b] * p2[b] for b in _LEVEL_Q[1]]
                    + [qp[b] * p4[b] for b in _LEVEL_Q[2]])
        k_pieces = ([kp[b] for b in _LEVEL_K[0]] + [_mul1(kp[b], x2[b]) for b in _LEVEL_K[1]]
                    + [_mul1(kp[b], x4[b]) for b in _LEVEL_K[2]])
        for lvl in range(3):
            group = 1 << lvl
            for jq, b_q in enumerate(_LEVEL_Q[lvl]):
                for jk, b_k in enumerate(_LEVEL_K[lvl]):
                    if jq // group == jk // group:
                        d = jnp.sum(q_pieces[4 * lvl + jq] * k_pieces[4 * lvl + jk],
                                    axis=-1, keepdims=True)
                        o_tr[b_q] = o_tr[b_q] + d * vtr[b_k]
        for b in vrange:
            cv_ref[u, 2, pl.ds(b, 8, stride=CV_PITCH), :] = o_tr[b]

    q_nat, k_nat, v_nat, qs, kd, vb, dec = [], [], [], [], [], [], []
    for u in units:
        an = [cv_ref[u, 0, CV_PITCH * a:CV_PITCH * a + 8, :] for a in vrange]
        cn = [cv_ref[u, 1, CV_PITCH * a:CV_PITCH * a + 8, :] for a in vrange]
        vnat = [load(2, hd[u], pl.ds(r0[u] + 8 * a, 8)) for a in vrange]
        q4 = [an[a] for a in _LEVEL_Q[0]]
        k4 = [cn[a] for a in _LEVEL_K[0]]
        an = [an[a] * bc(tot8[u], a - 1) if a & 1 else an[a] for a in vrange]
        cn = [cn[a] if a & 1 else cn[a] * bc(tot8[u], a + 1) for a in vrange]
        t16 = tot8[u] * pltpu.roll(tot8[u], 1, 0)
        q5 = [an[a] for a in _LEVEL_Q[1]]
        k5 = [cn[a] for a in _LEVEL_K[1]]
        an = [an[a] * bc(t16, (a & ~3) + 1) if a & 2 else an[a] for a in vrange]
        cn = [cn[a] if a & 2 else cn[a] * bc(t16, (a & ~3) + 3) for a in vrange]
        t32 = t16 * pltpu.roll(t16, 2, 0)
        q6 = [an[a] for a in _LEVEL_Q[2]]
        k6 = [cn[a] for a in _LEVEL_K[2]]
        an = [an[a] * bc(t32, 3) if a & 4 else an[a] for a in vrange]
        cn = [cn[a] if a & 4 else cn[a] * bc(t32, 7) for a in vrange]
        q_nat.append(cat16(q4 + q5 + q6))
        k_nat.append(cat16(k4 + k5 + k6))
        v_nat.append(cat16([vnat[j] for j in k_order]))
        qs.append(cat16(an))
        kd.append(cat16(cn))
        vb.append(cat16(vnat))
        dec.append(t32[3:4, :] * t32[7:8, :])

    att_nat = stacked_attend(q_nat, k_nat, v_nat, m_nat)
    o_nat = []
    for u in units:
        pieces = scatter_add([None] * nv, att_nat[u])
        pieces[0] = jnp.zeros((8, HGRN_DV), F32)
        o_nat.append(jnp.concatenate(pieces, axis=0))

    upd = [lax.dot_general(vb[u], kd[u], _TN, preferred_element_type=F32)
           for u in units]
    for h in range(nh):
        st = st_ref[h]
        for u in range(h, len(units), nh):
            o_nat[u] = o_nat[u] + lax.dot_general(qs[u], st.astype(BF16), _NT,
                                                  preferred_element_type=F32)
            st = st * dec[u] + upd[u]
        st_ref[h] = st

    for u in units:
        o_h = o_nat[u] + jnp.concatenate(
            [cv_ref[u, 2, CV_PITCH * a:CV_PITCH * a + 8, :] for a in vrange], axis=0)
        ms = jnp.mean(o_h * o_h, axis=-1, keepdims=True)
        xgs = load(3, hd[u], rows[u])
        gate = xgs * jnp.tanh(xgs) + xgs
        store(hd[u], rows[u], o_h * lax.rsqrt(ms + EPS) * og * gate)


def _swa_slab(q_slab, k_lo, k_hi, v_lo, v_hi, ones_pad, bias_first, bias_any,
              sink_a, sink_b, qgain, ones_bd, lo128, store):
    blk = ATTN_BLOCK
    nblk = MIX_ROWS // blk
    state = {}

    def prep():
        state["qs"] = (_pair_rms(q_slab(), ones_bd) * qgain).astype(BF16)
        state["sink_slab"] = jnp.where(lo128, sink_a, sink_b)

    def scores(n):
        keys = slice(n * blk, (n + 2) * blk)
        k_cat = jnp.concatenate([k_lo()[keys], k_hi()[keys]], axis=0)
        s = lax.dot_general(state["qs"][n * blk:(n + 1) * blk], k_cat, _NT,
                            preferred_element_type=F32)
        s = s + (bias_first() if n == 0 else bias_any())
        m_a = jnp.max(s[:, :2 * blk], axis=-1, keepdims=True)
        m_b = jnp.max(s[:, 2 * blk:], axis=-1, keepdims=True)
        m_a = jnp.maximum(jnp.broadcast_to(m_a, (blk, LANES)), sink_a)
        m_b = jnp.maximum(jnp.broadcast_to(m_b, (blk, LANES)), sink_b)
        cols = [s[:, c * LANES:(c + 1) * LANES] - (m_a if c < 2 * blk // LANES else m_b)
                for c in range(4 * blk // LANES)]
        state["p", n] = jnp.exp2(jnp.concatenate(cols, axis=1)).astype(BF16)
        state["m", n] = jnp.where(lo128, m_a, m_b)

    def attend(n):
        keys = slice(n * blk, (n + 2) * blk)
        v_cat = jnp.concatenate([v_lo()[keys], v_hi()[keys]], axis=0)
        rhs = jnp.concatenate([v_cat, ones_pad], axis=1)
        nd = jnp.dot(state.pop(("p", n)), rhs, preferred_element_type=F32)
        den = nd[:, LANES:] + jnp.exp2(state["sink_slab"] - state.pop(("m", n)))
        store(n, nd[:, :LANES] / den)

    stages = [prep]
    for n in range(nblk):
        stages += [lambda n=n: scores(n), lambda n=n: attend(n)]
    return stages


_CAST_SHAPES = ((D_MODEL, D_MODEL, 32), (D_MODEL, D_FF, 32), (D_MODEL, D_FF, 32), (D_FF, D_MODEL, 96))
N_CAST = len(_CAST_SHAPES)


def _cast_chunk(k, step):
    rows, _, per = _CAST_SHAPES[k]
    n = -(-rows // per)
    start = jnp.minimum(step * per, rows - per)
    return n, pl.multiple_of(start, 16)


def _cast_copies(k, step, src_refs, dst_refs, fbuf_refs, bbuf_refs, sem_in, sem_out):
    per = _CAST_SHAPES[k][2]
    _, start = _cast_chunk(k, step)
    rows = pl.ds(start, per)
    return (pltpu.make_async_copy(src_refs[k].at[rows, :], fbuf_refs[k], sem_in.at[k]),
            pltpu.make_async_copy(bbuf_refs[k], dst_refs[k].at[rows, :], sem_out.at[k]))


def _mix_kernel(x_ref, gain_ref, w_hbm, lbl_ref, og_ref, qg_ref, kg_ref, sink_ref,
                *refs):
    cast_src = refs[:N_CAST]
    rec_ref, att_ref = refs[N_CAST:N_CAST + 2]
    cast_dst = refs[N_CAST + 2:2 * N_CAST + 2]
    (u_ref, pa_ref, pb_ref, st_ref, cv_ref, kvp_ref, kpad_ref, vpad_ref, bias_ref,
     lev_ref) = refs[2 * N_CAST + 2:2 * N_CAST + 12]
    cast_f = refs[2 * N_CAST + 12:3 * N_CAST + 12]
    cast_b = refs[3 * N_CAST + 12:4 * N_CAST + 12]
    sem_in, sem_out, w_ref, w_stage, sem_w = refs[4 * N_CAST + 12:]
    g = pl.program_id(0)
    blk = ATTN_BLOCK

    def cast_copies(k, step):
        return _cast_copies(k, step, cast_src, cast_dst, cast_f, cast_b, sem_in, sem_out)

    cast_groups = {}
    for k in range(N_CAST):
        cast_groups.setdefault(_cast_chunk(k, g)[0], []).append(k)

    for n_chunks, members in cast_groups.items():
        @pl.when(g < n_chunks)
        def _(members=members):
            for k in members:
                cast_copies(k, g)[0].start()

    lo128 = lax.broadcasted_iota(jnp.int32, (1, LANES), 1) < ATTN_HD

    @pl.when(g == 0)
    def _():
        kvp_ref[...] = jnp.zeros_like(kvp_ref)
        lev_ref[...] = _level_mask()
        qi = lax.broadcasted_iota(jnp.int32, (blk, 2 * blk), 0)
        kj = lax.broadcasted_iota(jnp.int32, (blk, 2 * blk), 1)
        rel = qi + blk - kj
        in_window = (rel >= 0) & (rel < WINDOW)
        b_any = jnp.where(in_window, 0.0, NEG_INF)
        b_first = jnp.where(in_window & (kj >= blk), 0.0, NEG_INF)
        bias_ref[0] = jnp.concatenate([b_any, b_any], axis=1)
        bias_ref[1] = jnp.concatenate([b_first, b_first], axis=1)

    first_blk = lax.rem(g + SEQ_BLOCKS - 1, SEQ_BLOCKS) == 0

    @pl.when(first_blk)
    def _():
        st_ref[...] = jnp.zeros_like(st_ref)

    hr = lax.broadcasted_iota(jnp.int32, (LANES, LANES), 0) < ATTN_HD
    hc = lax.broadcasted_iota(jnp.int32, (LANES, LANES), 1) < ATTN_HD
    ones_bd = jnp.where(hr == hc, 1.0, 0.0).astype(BF16)
    pair = lambda r: jnp.concatenate([r, r], axis=1)
    qgain = pair(qg_ref[...]) * (ATTN_HD ** -0.5 * LOG2E)
    kgain = pair(kg_ref[...])
    ones_lo = jnp.broadcast_to(jnp.where(lo128, 1.0, 0.0), (2 * blk, LANES))
    ones_pad = jnp.concatenate([ones_lo, 1.0 - ones_lo], axis=0).astype(BF16)

    l0 = lbl_ref[0:1, :]
    l1 = lbl_ref[1:2, :]
    lmax = jnp.maximum(l0, l1)
    e0 = jnp.exp(l0 - lmax)
    e1 = jnp.exp(l1 - lmax)
    lb = e0 / (e0 + e1)
    c0 = 0.5 * (1.0 + lb)
    c1 = 0.5 * (1.0 - lb)
    og = og_ref[...]
    first_idx = first_blk.astype(jnp.int32)

    def step(pc_ref, pp_ref):
        _mix_step(pc_ref, pp_ref, x_ref, gain_ref, u_ref, w_ref, rec_ref, att_ref,
                  st_ref, cv_ref, kvp_ref, kpad_ref, vpad_ref, bias_ref, lev_ref, sink_ref,
                  (lo128, ones_bd, qgain, kgain, ones_pad, c0, c1, og, first_idx))

    @pl.when(g == 0)
    def _():
        _load_weights_and_project(pa_ref, x_ref, gain_ref, u_ref, w_hbm, w_ref, w_stage, sem_w)

    @pl.when((lax.rem(g, 2) == 0) & (g > 0))
    def _():
        step(pa_ref, pb_ref)

    @pl.when(lax.rem(g, 2) == 1)
    def _():
        step(pb_ref, pa_ref)

    for n_chunks, members in cast_groups.items():
        @pl.when((g >= 1) & (g <= n_chunks))
        def _(members=members):
            for k in members:
                cast_copies(k, g - 1)[1].wait()

        @pl.when(g < n_chunks)
        def _(members=members):
            for k in members:
                copy_in, copy_out = cast_copies(k, g)
                copy_in.wait()
                cast_b[k][...] = cast_f[k][...].astype(BF16)
                copy_out.start()


def _normalize(x_ref, gain_ref, u_ref):
    x = x_ref[...]
    ms = jnp.mean(x * x, axis=-1, keepdims=True)
    u_ref[...] = (x * lax.rsqrt(ms + EPS) * gain_ref[...]).astype(BF16)


def _project_slabs(pc_ref, u_ref, w_ref, first, count):
    res = jnp.dot(u_ref[...], w_ref[:, first * LANES:(first + count) * LANES],
                  preferred_element_type=F32)
    for t in range(count):
        pc_ref[first + t] = res[:, t * LANES:(t + 1) * LANES]


def _project_tail(pc_ref, u_ref, w_ref):
    _project_slabs(pc_ref, u_ref, w_ref, MIX_ITERS * LOOP_SLABS, TAIL_SLABS)


def _project_main(pc_ref, u_ref, w_ref, i):
    _project_slabs(pc_ref, u_ref, w_ref, i * LOOP_SLABS, LOOP_SLABS)


def _halved_part(col):
    part = col // HGRN_WIDTH
    return part in (0, 1, 3)


def _load_weights_and_project(pc_ref, x_ref, gain_ref, u_ref, w_hbm, w_ref, stage_ref, sem):
    n = PROJ_WIDTH // W_PIECE_COLS

    slots = stage_ref.shape[0]

    def copy(k):
        cols = pl.ds(k * W_PIECE_COLS, W_PIECE_COLS)
        return pltpu.make_async_copy(w_hbm.at[0, :, cols], stage_ref.at[k % slots],
                                     sem.at[k % slots])

    for k in range(min(slots - 1, n)):
        copy(k).start()
    _normalize(x_ref, gain_ref, u_ref)
    per_main = LOOP_SLABS * LANES // W_PIECE_COLS
    for k in range(n):
        if k + slots - 1 < n:
            copy(k + slots - 1).start()
        copy(k).wait()
        piece = stage_ref[k % slots]
        if _halved_part(k * W_PIECE_COLS):
            piece = piece * 0.5
        w_ref[:, k * W_PIECE_COLS:(k + 1) * W_PIECE_COLS] = piece.astype(BF16)
        if (k + 1) % per_main == 0 and (k + 1) // per_main <= MIX_ITERS:
            _project_main(pc_ref, u_ref, w_ref, (k + 1) // per_main - 1)
    _project_tail(pc_ref, u_ref, w_ref)


def _mix_step(pc_ref, pp_ref, x_ref, gain_ref, u_ref, w_ref, rec_ref, att_ref, st_ref,
              cv_ref, kvp_ref, kpad_ref, vpad_ref, bias_ref, lev_ref, sink_ref, consts):
    (lo128, ones_bd, qgain, kgain, ones_pad, c0, c1, og, first_idx) = consts
    blk = ATTN_BLOCK
    _normalize(x_ref, gain_ref, u_ref)

    k_all = jnp.concatenate([kvp_ref[0], pp_ref[K_SLAB]], axis=0)
    v_all = jnp.concatenate([kvp_ref[1], pp_ref[V_SLAB]], axis=0)
    kvp_ref[0] = pp_ref[K_SLAB, MIX_ROWS - blk:, :]
    kvp_ref[1] = pp_ref[V_SLAB, MIX_ROWS - blk:, :]
    kn = _pair_rms(k_all, ones_bd) * kgain
    kn_sw = pltpu.roll(kn, ATTN_HD, 1)
    v_sw = pltpu.roll(v_all, ATTN_HD, 1)
    zero = jnp.zeros_like(kn)
    kpad_ref[0, 0] = jnp.where(lo128, kn, zero).astype(BF16)
    kpad_ref[0, 1] = jnp.where(lo128, zero, kn_sw).astype(BF16)
    kpad_ref[1, 0] = jnp.where(lo128, kn_sw, zero).astype(BF16)
    kpad_ref[1, 1] = jnp.where(lo128, zero, kn).astype(BF16)
    vpad_ref[0, 0] = jnp.where(lo128, v_all, zero).astype(BF16)
    vpad_ref[0, 1] = jnp.where(lo128, zero, v_sw).astype(BF16)
    vpad_ref[1, 0] = jnp.where(lo128, v_sw, zero).astype(BF16)
    vpad_ref[1, 1] = jnp.where(lo128, zero, v_all).astype(BF16)

    def load(part, h, rows):
        return pp_ref[part * HGRN_HEADS + h, rows, :]

    def store(h, rows, tile):
        rec_ref[0, rows, h * HGRN_DV:(h + 1) * HGRN_DV] = tile.astype(rec_ref.dtype)

    def hgrn(i):
        r0s = [(i * HGRN_UNROLL + cc) * HGRN_CHUNK for cc in range(HGRN_UNROLL)]
        _hgrn_chunks(load, store, r0s, c0, c1, og, lev_ref[...], st_ref, cv_ref)

    def slab_stages(c):
        kvh = c // (ATTN_GROUP // 2)

        def store_att(n, tile):
            att_ref[0, c, n * blk:(n + 1) * blk, :] = tile.astype(att_ref.dtype)

        return _swa_slab(lambda: pp_ref[Q_SLAB + c],
                         lambda: kpad_ref[kvh, 0], lambda: kpad_ref[kvh, 1],
                         lambda: vpad_ref[kvh, 0], lambda: vpad_ref[kvh, 1], ones_pad,
                         lambda: bias_ref[first_idx], lambda: bias_ref[0],
                         sink_ref[2 * c] * LOG2E, sink_ref[2 * c + 1] * LOG2E,
                         qgain, ones_bd, lo128, store_att)

    def swa(i):
        slabs = [slab_stages(i * SLABS_PER_ITER + s) for s in range(SLABS_PER_ITER)]
        for stages in zip(*slabs):
            for stage in stages:
                stage()

    for i in range(MIX_ITERS):
        _project_main(pc_ref, u_ref, w_ref, i)
        hgrn(i)
        swa(i)
    _project_tail(pc_ref, u_ref, w_ref)


def _mix(x2, seq_len, gain, w_in, lb_logits, out_gain, q_gain, k_gain, sinks,
         ffn_weights):
    assert w_in.shape == (1, D_MODEL, PROJ_WIDTH) and w_in.dtype == F32
    t = x2.shape[0]
    nblk = t // MIX_ROWS
    assert seq_len // MIX_ROWS == SEQ_BLOCKS
    batch = t // seq_len
    assert tuple(w.shape for w in ffn_weights) == tuple(s[:2] for s in _CAST_SHAPES)
    assert all(-(-rows // per) <= nblk for rows, _, per in _CAST_SHAPES)
    hbm = pl.BlockSpec(memory_space=pl.ANY)

    def in_rows(gi):
        return (jnp.minimum(gi, nblk - 1), 0)

    def out_block(gi):
        blk_i = jnp.maximum(gi - 1, 0)
        return blk_i // SEQ_BLOCKS, lax.rem(blk_i, SEQ_BLOCKS)

    full = lambda shape: pl.BlockSpec(shape, lambda gi: (0,) * len(shape))
    return pl.pallas_call(
        _mix_kernel,
        grid=(nblk + 1,),
        in_specs=[
            pl.BlockSpec((MIX_ROWS, D_MODEL), in_rows),
            full((1, D_MODEL)),
            hbm,
            full((2, HGRN_WIDTH)),
            full((1, HGRN_DV)),
            full((1, ATTN_HD)),
            full((1, ATTN_HD)),
            pl.BlockSpec(memory_space=pltpu.SMEM),
        ] + [hbm] * N_CAST,
        out_specs=[
            pl.BlockSpec((1, MIX_ROWS, HGRN_WIDTH),
                         lambda gi: (*out_block(gi), 0)),
            pl.BlockSpec((1, ATT_SLABS, MIX_ROWS, LANES),
                         lambda gi: (out_block(gi)[0], 0, out_block(gi)[1], 0)),
        ] + [hbm] * N_CAST,
        out_shape=[
            jax.ShapeDtypeStruct((batch, seq_len, HGRN_WIDTH), BF16),
            jax.ShapeDtypeStruct((batch, ATT_SLABS, seq_len, LANES), BF16),
        ] + [jax.ShapeDtypeStruct(w.shape, BF16) for w in ffn_weights],
        scratch_shapes=[
            pltpu.VMEM((MIX_ROWS, D_MODEL), BF16),
            pltpu.VMEM((N_SLABS, MIX_ROWS, LANES), F32),
            pltpu.VMEM((N_SLABS, MIX_ROWS, LANES), F32),
            pltpu.VMEM((HGRN_HEADS, HGRN_DV, HGRN_DK), F32),
            pltpu.VMEM((HGRN_HEADS * HGRN_UNROLL, 3, CV_PITCH * 8, HGRN_DK), F32),
            pltpu.VMEM((2, ATTN_BLOCK, LANES), F32),
            pltpu.VMEM((ATTN_KV_HEADS, 2, ATTN_BLOCK + MIX_ROWS, LANES), BF16),
            pltpu.VMEM((ATTN_KV_HEADS, 2, ATTN_BLOCK + MIX_ROWS, LANES), BF16),
            pltpu.VMEM((2, ATTN_BLOCK, 4 * ATTN_BLOCK), F32),
            pltpu.VMEM((LEVEL_ROWS, LEVEL_ROWS), jnp.int32),
        ] + [pltpu.VMEM((per, cols), F32) for _, cols, per in _CAST_SHAPES]
        + [pltpu.VMEM((per, cols), BF16) for _, cols, per in _CAST_SHAPES]
        + [pltpu.SemaphoreType.DMA((N_CAST,)), pltpu.SemaphoreType.DMA((N_CAST,)),
           pltpu.VMEM((D_MODEL, PROJ_WIDTH), BF16),
           pltpu.VMEM((W_STAGE_SLOTS, D_MODEL, W_PIECE_COLS), F32),
           pltpu.SemaphoreType.DMA((W_STAGE_SLOTS,))],
        compiler_params=pltpu.CompilerParams(
            dimension_semantics=("arbitrary",), vmem_limit_bytes=VMEM_LIMIT_BYTES),
        name="mixer",
    )(x2, gain, w_in, lb_logits, out_gain, q_gain, k_gain, sinks, *ffn_weights)


def _ffn_kernel(x_ref, rec_ref, att_ref, wo_ref, gain_ref, wg_ref, wu_ref, wd_ref, o_ref):
    mixed = jnp.concatenate([rec_ref[0]] + [att_ref[0, c] for c in range(ATT_SLABS)],
                            axis=1)
    h = x_ref[...] + jnp.dot(mixed, wo_ref[...], preferred_element_type=F32)
    ms = jnp.mean(h * h, axis=-1, keepdims=True)
    u = (h * lax.rsqrt(ms + EPS) * gain_ref[...]).astype(BF16)
    gate = jnp.dot(u, wg_ref[...], preferred_element_type=F32)
    up = jnp.dot(u, wu_ref[...], preferred_element_type=F32)
    gs = 0.5 * gate
    act = ((gs * jnp.tanh(gs) + gs) * up).astype(BF16)
    o_ref[...] = h + jnp.dot(act, wd_ref[...], preferred_element_type=F32)


def _ffn(x2, rec, att, wo, gain, wg, wu, wd):
    t = x2.shape[0]
    seq_blocks = rec.shape[1] // FFN_ROWS

    def resident(shape):
        return pl.BlockSpec(shape, lambda i: (0, 0), pipeline_mode=pl.Buffered(1))

    return pl.pallas_call(
        _ffn_kernel,
        grid=(t // FFN_ROWS,),
        in_specs=[
            pl.BlockSpec((FFN_ROWS, D_MODEL), lambda i: (i, 0)),
            pl.BlockSpec((1, FFN_ROWS, HGRN_WIDTH),
                         lambda i: (i // seq_blocks, lax.rem(i, seq_blocks), 0)),
            pl.BlockSpec((1, ATT_SLABS, FFN_ROWS, LANES),
                         lambda i: (i // seq_blocks, 0, lax.rem(i, seq_blocks), 0)),
            resident((D_MODEL, D_MODEL)),
            resident((1, D_MODEL)),
            resident((D_MODEL, D_FF)),
            resident((D_MODEL, D_FF)),
            resident((D_FF, D_MODEL)),
        ],
        out_specs=pl.BlockSpec((FFN_ROWS, D_MODEL), lambda i: (i, 0)),
        out_shape=jax.ShapeDtypeStruct((t, D_MODEL), F32),
        compiler_params=pltpu.CompilerParams(
            dimension_semantics=("arbitrary",), vmem_limit_bytes=VMEM_LIMIT_BYTES),
        name="outproj_ffn",
    )(x2, rec, att, wo, gain, wg, wu, wd)


def kernel(x, norm1_gain, w_in, hgrn_lb_logits, hgrn_out_gain, q_norm_gain, k_norm_gain,
           attn_sinks, w_out, norm2_gain, w_ffn_gate, w_ffn_up, w_ffn_down):
    b, s, d = x.shape
    assert (d, w_in.shape[0]) == (D_MODEL, 1), "single-layer kernel"
    assert s == SEQ_BLOCKS * MIX_ROWS and MIX_ROWS == FFN_ROWS
    t = b * s
    x2 = x.reshape(t, d)

    rec, att, wo, wg, wu, wd = _mix(
        x2, s, norm1_gain[0][None, :], w_in, hgrn_lb_logits,
        hgrn_out_gain[0][None, :], q_norm_gain[0][None, :], k_norm_gain[0][None, :],
        attn_sinks[0], (w_out[0], w_ffn_gate[0], w_ffn_up[0], w_ffn_down[0]))
    out = _ffn(x2, rec, att, wo, norm2_gain[0][None, :], wg, wu, wd)
    return out.reshape(b, s, d)
```
